```python
import jax
import jax.numpy as jnp
from jax import lax
import numpy as np

D_MODEL = 1024
BATCH = 4
SEQ = 4096
DEPTH = 1

GRID_W = 64
CTX_LEN = 256
HEAD_DIM = 64
A_HEADS = 8
A_KV_HEADS = 2
A_GROUP = A_HEADS // A_KV_HEADS
WINDOW = 128
BLOCK = 128
A_SCALE = HEAD_DIM ** -0.5
MLA_HEADS = 8
MLA_NOPE = 64
MLA_ROPE = 32
MLA_V = 64
MLA_Q_LORA = 384
MLA_KV_LORA = 256
MLA_SCALE = (MLA_NOPE + MLA_ROPE) ** -0.5
A_Q_W = A_HEADS * HEAD_DIM
A_KV_W = A_KV_HEADS * HEAD_DIM
PROJ_SPLITS = (A_Q_W, A_Q_W + A_KV_W, A_Q_W + 2 * A_KV_W, A_Q_W + 2 * A_KV_W + MLA_Q_LORA,
               A_Q_W + 2 * A_KV_W + MLA_Q_LORA + MLA_KV_LORA)
PROJ_WIDTH = A_Q_W + 2 * A_KV_W + MLA_Q_LORA + MLA_KV_LORA + MLA_ROPE
MIX_WIDTH = A_HEADS * HEAD_DIM + MLA_HEADS * MLA_V
N_EXPERTS = 32
TOP_K = 4
D_FF = 1024
SWIGLU_LIMIT = 7.0
SWIGLU_ALPHA = 1.702
MOE_BLOCK = 256
ROPE_BASE = 10000.0
EPS = 1e-6
NEG_INF = -1e30

kernel_name = 'hybrid_swa_mla_moe_dit_layer'


def rms_norm(x, g):
    xf = x.astype(jnp.float32)
    y = xf * lax.rsqrt(jnp.mean(xf * xf, axis=-1, keepdims=True) + EPS)
    return (y * g.astype(jnp.float32)).astype(x.dtype)


def modulate(h, shift, scale):
    return h * (1 + scale) + shift


def axial_rope_tables(rows, rot_dim):
    row = jnp.broadcast_to(jnp.arange(rows, dtype=jnp.float32)[:, None], (rows, GRID_W)).reshape(-1)
    col = jnp.broadcast_to(jnp.arange(GRID_W, dtype=jnp.float32)[None, :], (rows, GRID_W)).reshape(-1)
    quarter = rot_dim // 4
    inv_freq = ROPE_BASE ** (-jnp.arange(quarter, dtype=jnp.float32) / quarter)
    ang = jnp.concatenate([row[:, None] * inv_freq, col[:, None] * inv_freq], axis=-1)
    return jnp.cos(ang), jnp.sin(ang)


def apply_rope(x, cos, sin):
    xf = x.astype(jnp.float32)
    half = xf.shape[-1] // 2
    x1, x2 = xf[..., :half], xf[..., half:]
    c, s = cos[:, None, :], sin[:, None, :]
    return jnp.concatenate([x1 * c - x2 * s, x1 * s + x2 * c], axis=-1).astype(x.dtype)


def windowed_gqa(q, k, v, kc, vc, sink):
    B, S = q.shape[:2]
    C = kc.shape[1]
    nb = S // BLOCK
    qb = q.reshape(B, nb, BLOCK, A_KV_HEADS, A_GROUP, HEAD_DIM)
    pad = ((0, 0), (BLOCK, BLOCK), (0, 0), (0, 0))
    kp = jnp.pad(k, pad).reshape(B, nb + 2, BLOCK, A_KV_HEADS, HEAD_DIM)
    vp = jnp.pad(v, pad).reshape(B, nb + 2, BLOCK, A_KV_HEADS, HEAD_DIM)
    kb = jnp.concatenate([kp[:, :-2], kp[:, 1:-1], kp[:, 2:]], axis=2)
    vb = jnp.concatenate([vp[:, :-2], vp[:, 1:-1], vp[:, 2:]], axis=2)
    n_loc = 3 * BLOCK
    s_loc = jnp.einsum('bnqhgd,bnkhd->bnhgqk', qb, kb, preferred_element_type=jnp.float32) * A_SCALE
    qpos = jnp.arange(nb)[:, None, None] * BLOCK + jnp.arange(BLOCK)[None, :, None]
    kpos = jnp.arange(nb)[:, None, None] * BLOCK - BLOCK + jnp.arange(n_loc)[None, None, :]
    valid = (jnp.abs(qpos - kpos) <= WINDOW) & (kpos >= 0) & (kpos < S)
    s_loc = jnp.where(valid[None, :, None, None], s_loc, NEG_INF)
    s_ctx = jnp.einsum('bnqhgd,bchd->bnhgqc', qb, kc, preferred_element_type=jnp.float32) * A_SCALE
    s_sink = jnp.broadcast_to(sink.astype(jnp.float32).reshape(1, 1, A_KV_HEADS, A_GROUP, 1, 1),
                              s_loc.shape[:-1] + (1,))
    p = jax.nn.softmax(jnp.concatenate([s_loc, s_ctx, s_sink], axis=-1), axis=-1).astype(v.dtype)
    o = (jnp.einsum('bnhgqk,bnkhd->bnqhgd', p[..., :n_loc], vb)
         + jnp.einsum('bnhgqc,bchd->bnqhgd', p[..., n_loc:n_loc + C], vc))
    return o.reshape(B, S, A_HEADS * HEAD_DIM)


def context_gqa(q, k, v, sink):
    B, C = q.shape[:2]
    qg = q.reshape(B, C, A_KV_HEADS, A_GROUP, HEAD_DIM)
    s = jnp.einsum('bqhgd,bkhd->bhgqk', qg, k, preferred_element_type=jnp.float32) * A_SCALE
    s_sink = jnp.broadcast_to(sink.astype(jnp.float32).reshape(1, A_KV_HEADS, A_GROUP, 1, 1), s.shape[:-1] + (1,))
    p = jax.nn.softmax(jnp.concatenate([s, s_sink], axis=-1), axis=-1)[..., :-1].astype(v.dtype)
    return jnp.einsum('bhgqk,bkhd->bqhgd', p, v).reshape(B, C, A_HEADS * HEAD_DIM)


def mla_project(cq, ckv, g_q_a, w_uq, g_kv_a, w_ukv):
    B, N = cq.shape[:2]
    q = (rms_norm(cq, g_q_a) @ w_uq).reshape(B, N, MLA_HEADS, MLA_NOPE + MLA_ROPE)
    kv = (rms_norm(ckv, g_kv_a) @ w_ukv).reshape(B, N, MLA_HEADS, MLA_NOPE + MLA_V)
    return q[..., :MLA_NOPE], q[..., MLA_NOPE:], kv[..., :MLA_NOPE], kv[..., MLA_NOPE:]


def mla_attend(qn, qr, kn, kr, v):
    s = (jnp.einsum('bqhd,bkhd->bhqk', qn, kn, preferred_element_type=jnp.float32)
         + jnp.einsum('bqhd,bkd->bhqk', qr, kr, preferred_element_type=jnp.float32)) * MLA_SCALE
    p = jax.nn.softmax(s, axis=-1).astype(v.dtype)
    return jnp.einsum('bhqk,bkhd->bqhd', p, v)


def mla_latent(qn, qr, kn, kr, v, kn_c, kr_c, v_c):
    B, S = qn.shape[:2]
    nb = S // BLOCK
    kn_all = jnp.concatenate([kn_c, kn], axis=1)
    kr_all = jnp.concatenate([kr_c, kr], axis=1)
    v_all = jnp.concatenate([v_c, v], axis=1)
    qn_b = qn.reshape(B, nb, BLOCK, MLA_HEADS, MLA_NOPE).transpose(1, 0, 2, 3, 4)
    qr_b = qr.reshape(B, nb, BLOCK, MLA_HEADS, MLA_ROPE).transpose(1, 0, 2, 3, 4)
    o = lax.map(lambda qs: mla_attend(qs[0], qs[1], kn_all, kr_all, v_all), (qn_b, qr_b))
    return o.transpose(1, 0, 2, 3, 4).reshape(B, S, MLA_HEADS * MLA_V)


def token_mixing(h, hc, w_in, sink, g_q_a, w_uq, g_kv_a, w_ukv, w_o, rope_a, rope_b, ctx_out):
    B, S, _ = h.shape
    C = hc.shape[1]
    qa, ka, va, cq, ckv, kr = jnp.split(h @ w_in, PROJ_SPLITS, axis=-1)
    qa_c, ka_c, va_c, cq_c, ckv_c, kr_c = jnp.split(hc @ w_in, PROJ_SPLITS, axis=-1)
    qa = apply_rope(qa.reshape(B, S, A_HEADS, HEAD_DIM), *rope_a)
    ka = apply_rope(ka.reshape(B, S, A_KV_HEADS, HEAD_DIM), *rope_a)
    va = va.reshape(B, S, A_KV_HEADS, HEAD_DIM)
    ka_c = ka_c.reshape(B, C, A_KV_HEADS, HEAD_DIM)
    va_c = va_c.reshape(B, C, A_KV_HEADS, HEAD_DIM)
    out_a = windowed_gqa(qa, ka, va, ka_c, va_c, sink)
    qn, qr, kn, vb = mla_project(cq, ckv, g_q_a, w_uq, g_kv_a, w_ukv)
    qr = apply_rope(qr, *rope_b)
    kr = apply_rope(kr[:, :, None, :], *rope_b)[:, :, 0, :]
    qn_c, qr_c, kn_c, vb_c = mla_project(cq_c, ckv_c, g_q_a, w_uq, g_kv_a, w_ukv)
    out_b = mla_latent(qn, qr, kn, kr, vb, kn_c, kr_c, vb_c)
    y = jnp.concatenate([out_a, out_b], axis=-1) @ w_o
    if not ctx_out:
        return y, None
    out_a_c = context_gqa(qa_c.reshape(B, C, A_HEADS, HEAD_DIM), ka_c, va_c, sink)
    out_b_c = mla_attend(qn_c, qr_c, kn_c, kr_c, vb_c).reshape(B, C, MLA_HEADS * MLA_V)
    yc = jnp.concatenate([out_a_c, out_b_c], axis=-1) @ w_o
    return y, yc


def moe_ffn(h, w_router, b_router, w_gate_up, b_gate_up, w_down, b_down):
    shape = h.shape
    d = shape[-1]
    xt = h.reshape(-1, d)
    T = xt.shape[0]
    logits = jnp.einsum('td,de->te', xt, w_router, preferred_element_type=jnp.float32) + b_router.astype(jnp.float32)
    top_logit, top_idx = lax.top_k(logits, TOP_K)
    top_w = jax.nn.softmax(top_logit, axis=-1)
    n_assign = T * TOP_K
    flat_e = top_idx.reshape(-1)
    order = jnp.argsort(flat_e)
    sorted_e = flat_e[order]
    counts = jnp.bincount(flat_e, length=N_EXPERTS)
    padded = (counts + MOE_BLOCK - 1) // MOE_BLOCK * MOE_BLOCK
    pad_end = jnp.cumsum(padded)
    pad_start = pad_end - padded
    start = jnp.cumsum(counts) - counts
    dest = pad_start[sorted_e] + jnp.arange(n_assign) - start[sorted_e]
    n_blocks = -(-n_assign // MOE_BLOCK) + N_EXPERTS
    cap = n_blocks * MOE_BLOCK
    slot_tok = jnp.full((cap,), T, jnp.int32).at[dest].set((order // TOP_K).astype(jnp.int32))
    slot_w = jnp.zeros((cap,), jnp.float32).at[dest].set(top_w.reshape(-1)[order])
    block_e = jnp.minimum(jnp.searchsorted(pad_end, jnp.arange(n_blocks) * MOE_BLOCK, side='right'), N_EXPERTS - 1)
    x_pad = jnp.concatenate([xt, jnp.zeros((1, d), xt.dtype)], axis=0)
    xs = x_pad[slot_tok].reshape(n_blocks, MOE_BLOCK, d)

    def expert_block(args):
        xb, e = args
        gate, lin = jnp.split(xb @ w_gate_up[e] + b_gate_up[e], 2, axis=-1)
        gate = jnp.minimum(gate, SWIGLU_LIMIT)
        lin = jnp.clip(lin, -SWIGLU_LIMIT, SWIGLU_LIMIT)
        act = (lin + 1) * (gate * jax.nn.sigmoid(SWIGLU_ALPHA * gate))
        return act @ w_down[e] + b_down[e]

    ys = lax.map(expert_block, (xs, block_e)).reshape(cap, d)
    ys = ys * slot_w[:, None].astype(ys.dtype)
    out = jnp.zeros((T + 1, d), ys.dtype).at[slot_tok].add(ys)[:T]
    return out.reshape(shape)


def setup_inputs(seed: int = 0) -> dict:
    key = jax.random.key(seed)
    ks = jax.random.split(key, 24)
    f32 = jnp.float32
    L, D = DEPTH, D_MODEL

    def nrm(k, shape, scale):
        return jax.random.normal(k, shape, f32) * scale

    return {
        'x': nrm(ks[0], (BATCH, SEQ, D), 1.0),
        'c': nrm(ks[1], (BATCH, D), 1.0),
        'ctx': nrm(ks[2], (BATCH, CTX_LEN, D), 1.0),
        'c_ctx': nrm(ks[3], (D,), 1.0),
        'w_ada': nrm(ks[4], (L, D, 6 * D), 0.5 * D ** -0.5),
        'b_ada': nrm(ks[5], (L, 6 * D), 0.02),
        'g_mix_pre': 1.0 + nrm(ks[6], (L, D), 0.1),
        'g_mix_post': 1.0 + nrm(ks[7], (L, D), 0.1),
        'w_in': nrm(ks[8], (L, D, PROJ_WIDTH), D ** -0.5),
        'sink': nrm(ks[9], (L, A_HEADS), 1.0),
        'g_q_a': 1.0 + nrm(ks[10], (L, MLA_Q_LORA), 0.1),
        'w_uq': nrm(ks[11], (L, MLA_Q_LORA, MLA_HEADS * (MLA_NOPE + MLA_ROPE)), MLA_Q_LORA ** -0.5),
        'g_kv_a': 1.0 + nrm(ks[12], (L, MLA_KV_LORA), 0.1),
        'w_ukv': nrm(ks[13], (L, MLA_KV_LORA, MLA_HEADS * (MLA_NOPE + MLA_V)), MLA_KV_LORA ** -0.5),
        'w_o': nrm(ks[14], (L, MIX_WIDTH, D), MIX_WIDTH ** -0.5),
        'g_ffn_pre': 1.0 + nrm(ks[15], (L, D), 0.1),
        'g_ffn_post': 1.0 + nrm(ks[16], (L, D), 0.1),
        'w_router': nrm(ks[17], (L, D, N_EXPERTS), D ** -0.5),
        'b_router': nrm(ks[18], (L, N_EXPERTS), 0.01),
        'w_gate_up': nrm(ks[19], (L, N_EXPERTS, D, 2 * D_FF), D ** -0.5),
        'b_gate_up': nrm(ks[20], (L, N_EXPERTS, 2 * D_FF), 0.02),
        'w_down': nrm(ks[21], (L, N_EXPERTS, D_FF, D), D_FF ** -0.5),
        'b_down': nrm(ks[22], (L, N_EXPERTS, D), 0.02),
    }


def reference(x, c, ctx, c_ctx, w_ada, b_ada, g_mix_pre, g_mix_post, w_in, sink, g_q_a, w_uq, g_kv_a, w_ukv,
              w_o, g_ffn_pre, g_ffn_post, w_router, b_router, w_gate_up, b_gate_up, w_down, b_down):
    S = x.shape[1]
    ROWS = S // GRID_W
    rope_a = axial_rope_tables(ROWS, HEAD_DIM)
    rope_b = axial_rope_tables(ROWS, MLA_ROPE)
    silu_c = jax.nn.silu(c)
    silu_cc = jax.nn.silu(c_ctx)
    for l in range(DEPTH):
        ctx_out = l < DEPTH - 1
        mod = jnp.split((silu_c @ w_ada[l] + b_ada[l])[:, None, :], 6, axis=-1)
        mod_c = jnp.split(silu_cc @ w_ada[l] + b_ada[l], 6, axis=-1)
        h = modulate(rms_norm(x, g_mix_pre[l]), mod[0], mod[1])
        hc = modulate(rms_norm(ctx, g_mix_pre[l]), mod_c[0], mod_c[1])
        y, yc = token_mixing(h, hc, w_in[l], sink[l], g_q_a[l], w_uq[l], g_kv_a[l], w_ukv[l], w_o[l],
                             rope_a, rope_b, ctx_out)
        x = x + mod[2] * rms_norm(y, g_mix_post[l])
        h2 = modulate(rms_norm(x, g_ffn_pre[l]), mod[3], mod[4])
        f = moe_ffn(h2, w_router[l], b_router[l], w_gate_up[l], b_gate_up[l], w_down[l], b_down[l])
        x = x + mod[5] * rms_norm(f, g_ffn_post[l])
        if ctx_out:
            ctx = ctx + mod_c[2] * rms_norm(yc, g_mix_post[l])
            hc2 = modulate(rms_norm(ctx, g_ffn_pre[l]), mod_c[3], mod_c[4])
            fc = moe_ffn(hc2, w_router[l], b_router[l], w_gate_up[l], b_gate_up[l], w_down[l], b_down[l])
            ctx = ctx + mod_c[5] * rms_norm(fc, g_ffn_post[l])
    return x
```

```python
import functools

import jax
import jax.numpy as jnp
from jax import lax
from jax.experimental import pallas as pl
from jax.experimental.pallas import tpu as pltpu

F32 = jnp.float32
BF16 = jnp.bfloat16

D_MODEL = 1024
GRID_W = 64
HEAD_DIM = 64
A_HEADS = 8
A_KV_HEADS = 2
BLOCK = 128
A_SCALE = HEAD_DIM ** -0.5
MLA_HEADS = 8
MLA_NOPE = 64
MLA_ROPE = 32
MLA_V = 64
MLA_Q_LORA = 384
MLA_KV_LORA = 256
MLA_SCALE = (MLA_NOPE + MLA_ROPE) ** -0.5
N_EXPERTS = 32
TOP_K = 4
D_FF = 1024
SWIGLU_LIMIT = 7.0
SWIGLU_ALPHA = 1.702
ROPE_BASE = 10000.0
EPS = 1e-6
NEG_INF = -1e30

LANES = 128
VMEM_LIMIT = 56 * 1024 * 1024

TM_IN = 512
TQ_MLA = 256
TK_MLA = 512
TM_OUT = 512
MOE_BLOCK = 256
TM_DISP = 512
TM_COMB = 256


def _dot(a, b):
    return jnp.dot(a, b, preferred_element_type=F32)


def _dot_nt(a, b):
    return lax.dot_general(a, b, (((1,), (1,)), ((), ())), preferred_element_type=F32)


def _split_bf16(x):
    hi = x.astype(BF16)
    lo = (x - hi.astype(F32)).astype(BF16)
    return hi, lo


def _rms(x, g):
    return x * lax.rsqrt(jnp.mean(x * x, axis=-1, keepdims=True) + EPS) * g


def _rope(x, cos, sin_signed, first_half, half):
    n = x.shape[-1]
    partner = jnp.where(first_half, pltpu.roll(x, n - half, 1), pltpu.roll(x, half, 1))
    return x * cos + partner * sin_signed


def _adaln_kernel(c_ref, w_ref, b_ref, o_ref):
    c = c_ref[...]
    s = c * jax.nn.sigmoid(c)
    s_hi, s_lo = _split_bf16(s)
    w_hi, w_lo = _split_bf16(w_ref[...])
    o_ref[...] = _dot(s_hi, w_hi) + _dot(s_hi, w_lo) + _dot(s_lo, w_hi) + b_ref[...]


def _adaln(cc, w, b):
    n = w.shape[1]
    tn = 1024
    return pl.pallas_call(
        _adaln_kernel,
        grid=(n // tn,),
        in_specs=[pl.BlockSpec((8, D_MODEL), lambda j: (0, 0)),
                  pl.BlockSpec((D_MODEL, tn), lambda j: (0, j)),
                  pl.BlockSpec((1, tn), lambda j: (0, j))],
        out_specs=pl.BlockSpec((8, tn), lambda j: (0, j)),
        out_shape=jax.ShapeDtypeStruct((8, n), F32),
        compiler_params=pltpu.CompilerParams(dimension_semantics=("arbitrary",),
                                             vmem_limit_bytes=VMEM_LIMIT),
        name="adaln",
    )(cc, w, b)


def _inproj_kernel(x_ref, shift_ref, scale_ref, gpre_ref, ca_ref, sa_ref, cq_ref, sq_ref, ck_ref, sk_ref,
                   win_ref, gq_ref, wuq_ref, gkv_ref, wk_ref, wv_ref, e_ref,
                   qa_ref, k4_ref, v2_ref, q_ref, k_ref, v_ref):
    x = x_ref[0]
    tm = x.shape[0]
    h = _rms(x, gpre_ref[...]) * (1.0 + scale_ref[0]) + shift_ref[0]
    p = _dot(h.astype(BF16), win_ref[...])
    lane = lax.broadcasted_iota(jnp.int32, (tm, LANES), 1)

    first_a = (lane & 32) == 0
    ca, sa = ca_ref[...], sa_ref[...]
    for j in range(4):
        sl = slice(j * LANES, (j + 1) * LANES)
        qa_ref[0, :, sl] = (_rope(p[:, sl], ca, sa, first_a, 32) * A_SCALE).astype(BF16)
    for j in range(4):
        sl = slice(j * LANES, (j + 1) * LANES)
        k4_ref[0, :, sl] = _rope(p[:, 512 + j * LANES:512 + (j + 1) * LANES], ca, sa, first_a, 32).astype(BF16)
    v2_ref[0] = p[:, 1024:1280].astype(BF16)

    cqn = _rms(p[:, 1280:1664], gq_ref[...]).astype(BF16)
    qf = _dot(cqn, wuq_ref[...])
    first_q = (lane >= 64) & (lane < 80)
    cq, sq = cq_ref[...], sq_ref[...]
    for hh in range(MLA_HEADS):
        sl = slice(hh * LANES, (hh + 1) * LANES)
        q_ref[0, :, sl] = (_rope(qf[:, sl], cq, sq, first_q, 16) * MLA_SCALE).astype(BF16)

    ckvn = _rms(p[:, 1664:1920], gkv_ref[...]).astype(BF16)
    kr = _rope(p[:, 1920:2048], ck_ref[...], sk_ref[...], lane < 16, 16)
    k_ref[0] = (_dot(ckvn, wk_ref[...]) + _dot(kr.astype(BF16), e_ref[...])).astype(BF16)
    v_ref[0] = _dot(ckvn, wv_ref[...]).astype(BF16)


def _inproj(x, shift, scale, gpre, tabs, wts, tm):
    bx, n, d = x.shape
    ca, sa, cq, sq, ck, sk = tabs
    win, gq, wuq, gkv, wk, wv, e = wts
    tok = lambda i, b: (b, i, 0)
    vec = lambda i, b: (b, 0, 0)
    tab = lambda i, b: (i, 0)
    cst = lambda i, b: (0, 0)
    in_specs = [pl.BlockSpec((1, tm, d), tok), pl.BlockSpec((1, 1, d), vec), pl.BlockSpec((1, 1, d), vec),
                pl.BlockSpec((1, d), cst)]
    in_specs += [pl.BlockSpec((tm, LANES), tab)] * 6
    in_specs += [pl.BlockSpec(w.shape, cst) for w in wts]
    widths = (512, 512, 256, 1024, 1024, 512)
    return pl.pallas_call(
        _inproj_kernel,
        grid=(n // tm, bx),
        in_specs=in_specs,
        out_specs=[pl.BlockSpec((1, tm, w), tok) for w in widths],
        out_shape=[jax.ShapeDtypeStruct((bx, n, w), BF16) for w in widths],
        compiler_params=pltpu.CompilerParams(dimension_semantics=("arbitrary", "arbitrary"),
                                             vmem_limit_bytes=VMEM_LIMIT),
        name="inproj",
    )(x, shift, scale, gpre, ca, sa, cq, sq, ck, sk, win, gq, wuq, gkv, wk, wv, e)


def _gqa_kernel(sink_ref, q_ref, kp_ref, kc_ref, kn_ref, kx_ref, vp_ref, vc_ref, vn_ref, vx_ref, o_ref):
    n = pl.program_id(1)
    nb = pl.num_programs(1)
    q = q_ref[0]
    ks = (kp_ref[0], kc_ref[0], kn_ref[0], kx_ref[0])
    vs = (vp_ref[0], vc_ref[0], vn_ref[0], vx_ref[0])
    widths = tuple(k.shape[0] for k in ks)
    row = lax.broadcasted_iota(jnp.int32, (BLOCK, BLOCK), 0)
    col = lax.broadcasted_iota(jnp.int32, (BLOCK, BLOCK), 1)
    off_prev = jnp.where(n > 0, 0, 2 * BLOCK)
    off_next = jnp.where(n < nb - 1, 0, 2 * BLOCK)
    masks = (col >= row + off_prev, None, col <= row - off_next, None)
    lane = lax.broadcasted_iota(jnp.int32, (BLOCK, LANES), 1)
    for pair in range(A_HEADS // 2):
        hk = pair // 2
        qp = q[:, pair * LANES:(pair + 1) * LANES]
        res = []
        for sub in range(2):
            head = 2 * pair + sub
            ksl = slice((2 * hk + sub) * LANES, (2 * hk + sub + 1) * LANES)
            ss = []
            for k, msk in zip(ks, masks):
                s = _dot_nt(qp, k[:, ksl])
                ss.append(s if msk is None else jnp.where(msk, s, NEG_INF))
            sink = sink_ref[head]
            m = jnp.maximum(functools.reduce(jnp.maximum, [jnp.max(s, axis=-1, keepdims=True) for s in ss]), sink)
            ps = [jnp.exp(s - m) for s in ss]
            l = functools.reduce(jnp.add, [jnp.sum(p, axis=-1, keepdims=True) for p in ps]) + jnp.exp(sink - m)
            vsl = slice(0, LANES) if sub == hk else slice(LANES, 2 * LANES)
            acc = functools.reduce(jnp.add, [_dot(p.astype(BF16), v[:, vsl]) for p, v in zip(ps, vs)])
            res.append(acc * (1.0 / l))
        o_ref[0, :, pair * LANES:(pair + 1) * LANES] = jnp.where(lane < HEAD_DIM, res[0], res[1]).astype(BF16)
    del widths


def _gqa(sink, qa, k4, v2, k4c, v2c):
    b, s, _ = qa.shape
    nb = s // BLOCK
    c = k4c.shape[1]
    cur = lambda bb, n: (bb, n, 0)
    prv = lambda bb, n: (bb, jnp.maximum(n - 1, 0), 0)
    nxt = lambda bb, n: (bb, jnp.minimum(n + 1, nb - 1), 0)
    ctx = lambda bb, n: (bb, 0, 0)
    return pl.pallas_call(
        _gqa_kernel,
        grid=(b, nb),
        in_specs=[pl.BlockSpec(memory_space=pltpu.SMEM),
                  pl.BlockSpec((1, BLOCK, 512), cur),
                  pl.BlockSpec((1, BLOCK, 512), prv), pl.BlockSpec((1, BLOCK, 512), cur),
                  pl.BlockSpec((1, BLOCK, 512), nxt), pl.BlockSpec((1, c, 512), ctx),
                  pl.BlockSpec((1, BLOCK, 256), prv), pl.BlockSpec((1, BLOCK, 256), cur),
                  pl.BlockSpec((1, BLOCK, 256), nxt), pl.BlockSpec((1, c, 256), ctx)],
        out_specs=pl.BlockSpec((1, BLOCK, 512), cur),
        out_shape=jax.ShapeDtypeStruct((b, s, 512), BF16),
        compiler_params=pltpu.CompilerParams(dimension_semantics=("arbitrary", "arbitrary"),
                                             vmem_limit_bytes=VMEM_LIMIT),
        name="gqa",
    )(sink, qa, k4, k4, k4, k4c, v2, v2, v2, v2c)


def _mla_kernel(q_ref, k_ref, v_ref, kc_ref, vc_ref, o_ref):
    tq = q_ref.shape[1]
    nk = k_ref.shape[1] // TK_MLA
    lane = lax.broadcasted_iota(jnp.int32, (tq, LANES), 1)
    for pair in range(MLA_HEADS // 2):
        vsl = slice(pair * LANES, (pair + 1) * LANES)
        res = []
        for sub in range(2):
            hsl = slice((2 * pair + sub) * LANES, (2 * pair + sub + 1) * LANES)
            qh = q_ref[0, :, hsl]
            s = _dot_nt(qh, kc_ref[0, :, hsl])
            m0 = jnp.max(s, axis=-1, keepdims=True)
            p = jnp.exp(s - m0)
            l0 = jnp.sum(p, axis=-1, keepdims=True)
            acc0 = _dot(p.astype(BF16), vc_ref[0, :, vsl])

            def body(c, carry, qh=qh, hsl=hsl, vsl=vsl):
                m, l, acc = carry
                off = pl.multiple_of(c * TK_MLA, TK_MLA)
                s = _dot_nt(qh, k_ref[0, pl.ds(off, TK_MLA), hsl])
                m_new = jnp.maximum(m, jnp.max(s, axis=-1, keepdims=True))
                alpha = jnp.exp(m - m_new)
                p = jnp.exp(s - m_new)
                l = alpha * l + jnp.sum(p, axis=-1, keepdims=True)
                acc = alpha * acc + _dot(p.astype(BF16), v_ref[0, pl.ds(off, TK_MLA), vsl])
                return m_new, l, acc

            m, l, acc = lax.fori_loop(0, nk, body, (m0, l0, acc0))
            res.append(acc * (1.0 / l))
        o_ref[0, :, vsl] = jnp.where(lane < MLA_V, res[0], res[1]).astype(BF16)


def _mla(q, k, v, kc, vc):
    b, s, _ = q.shape
    c = kc.shape[1]
    return pl.pallas_call(
        _mla_kernel,
        grid=(b, s // TQ_MLA),
        in_specs=[pl.BlockSpec((1, TQ_MLA, 1024), lambda bb, i: (bb, i, 0)),
                  pl.BlockSpec((1, s, 1024), lambda bb, i: (bb, 0, 0)),
                  pl.BlockSpec((1, s, 512), lambda bb, i: (bb, 0, 0)),
                  pl.BlockSpec((1, c, 1024), lambda bb, i: (bb, 0, 0)),
                  pl.BlockSpec((1, c, 512), lambda bb, i: (bb, 0, 0))],
        out_specs=pl.BlockSpec((1, TQ_MLA, 512), lambda bb, i: (bb, i, 0)),
        out_shape=jax.ShapeDtypeStruct((b, s, 512), BF16),
        compiler_params=pltpu.CompilerParams(dimension_semantics=("arbitrary", "arbitrary"),
                                             vmem_limit_bytes=VMEM_LIMIT),
        name="mla",
    )(q, k, v, kc, vc)


def _outproj_kernel(oa_ref, ob_ref, x_ref, woa_ref, wob_ref, gpost_ref, gate_ref, gffn_ref, shift_ref, scale_ref,
                    wrh_ref, wrl_ref, br_ref, tri_ref,
                    x1_ref, h2p_ref, ri_ref, rw_ref, cnt_ref, carry_ref):
    first = (pl.program_id(0) == 0) & (pl.program_id(1) == 0)

    @pl.when(first)
    def _():
        carry_ref[...] = jnp.zeros_like(carry_ref)

    y = _dot(oa_ref[0], woa_ref[...]) + _dot(ob_ref[0], wob_ref[...])
    x1 = x_ref[0] + gate_ref[0] * _rms(y, gpost_ref[...])
    x1_ref[0] = x1
    h2 = _rms(x1, gffn_ref[...]) * (1.0 + scale_ref[0]) + shift_ref[0]
    tm = h2.shape[0]

    bits = lax.bitcast_convert_type(h2.astype(BF16).astype(F32), jnp.uint32)
    h2p_ref[0] = (bits[:, :512] >> 16) | (bits[:, 512:] & jnp.uint32(0xFFFF0000))

    h_hi, h_lo = _split_bf16(h2)
    logits = _dot(h_hi, wrh_ref[...]) + _dot(h_hi, wrl_ref[...]) + _dot(h_lo, wrh_ref[...]) + br_ref[...]
    lane = lax.broadcasted_iota(jnp.int32, (tm, LANES), 1)
    lane_f = lane.astype(F32)
    tops, idxs = [], []
    cur = logits
    for _ in range(TOP_K):
        mk = jnp.max(cur, axis=-1, keepdims=True)
        ik = jnp.min(jnp.where(cur == mk, lane_f, float(LANES)), axis=-1, keepdims=True)
        tops.append(mk)
        idxs.append(ik)
        cur = jnp.where(lane_f == ik, -jnp.inf, cur)
    es = [jnp.exp(t - tops[0]) for t in tops]
    inv = 1.0 / functools.reduce(jnp.add, es)

    onehots = [(lane_f == ik) for ik in idxs]
    onehot = functools.reduce(jnp.add, [o.astype(F32) for o in onehots])
    base = _dot(tri_ref[...], onehot.astype(BF16)) + carry_ref[...]
    ri = jnp.zeros((tm, LANES), jnp.int32)
    rw = jnp.zeros((tm, LANES), F32)
    for k in range(TOP_K):
        rank = jnp.sum(jnp.where(onehots[k], base, 0.0), axis=-1, keepdims=True).astype(jnp.int32)
        ri = jnp.where(lane == k, idxs[k].astype(jnp.int32), ri)
        ri = jnp.where(lane == TOP_K + k, rank, ri)
        rw = jnp.where(lane == k, es[k] * inv, rw)
    ri_ref[0] = ri
    rw_ref[0] = rw
    carry_ref[...] = carry_ref[...] + jnp.sum(onehot, axis=0, keepdims=True)
    cnt_ref[...] = carry_ref[...]


def _outproj(oa, ob, x, woa, wob, gpost, gate, gffn, shift, scale, wrh, wrl, br, tri):
    b, s, d = x.shape
    tm = TM_OUT
    tok = lambda bb, i: (bb, i, 0)
    vec = lambda bb, i: (bb, 0, 0)
    cst = lambda bb, i: (0, 0)
    return pl.pallas_call(
        _outproj_kernel,
        grid=(b, s // tm),
        in_specs=[pl.BlockSpec((1, tm, 512), tok), pl.BlockSpec((1, tm, 512), tok), pl.BlockSpec((1, tm, d), tok),
                  pl.BlockSpec((512, d), cst), pl.BlockSpec((512, d), cst), pl.BlockSpec((1, d), cst),
                  pl.BlockSpec((1, 1, d), vec), pl.BlockSpec((1, d), cst),
                  pl.BlockSpec((1, 1, d), vec), pl.BlockSpec((1, 1, d), vec),
                  pl.BlockSpec((d, LANES), cst), pl.BlockSpec((d, LANES), cst), pl.BlockSpec((1, LANES), cst),
                  pl.BlockSpec((tm, tm), cst)],
        out_specs=[pl.BlockSpec((1, tm, d), tok), pl.BlockSpec((1, tm, 512), tok),
                   pl.BlockSpec((1, tm, LANES), tok), pl.BlockSpec((1, tm, LANES), tok),
                   pl.BlockSpec((1, LANES), cst)],
        out_shape=[jax.ShapeDtypeStruct((b, s, d), F32), jax.ShapeDtypeStruct((b, s, 512), jnp.uint32),
                   jax.ShapeDtypeStruct((b, s, LANES), jnp.int32), jax.ShapeDtypeStruct((b, s, LANES), F32),
                   jax.ShapeDtypeStruct((1, LANES), F32)],
        scratch_shapes=[pltpu.VMEM((1, LANES), F32)],
        compiler_params=pltpu.CompilerParams(dimension_semantics=("arbitrary", "arbitrary"),
                                             vmem_limit_bytes=VMEM_LIMIT),
        name="outproj",
    )(oa, ob, x, woa, wob, gpost, gate, gffn, shift, scale, wrh, wrl, br, tri)


def _dispatch_kernel(dest_ref, h_ref, xs_in_ref, xs_ref, sem):
    del xs_in_ref
    tm = h_ref.shape[0]

    def row_copy(t, d):
        return pltpu.make_async_copy(h_ref.at[pl.ds(t, 1)], xs_ref.at[pl.ds(d, 1)], sem)

    def issue(t, _):
        for k in range(TOP_K):
            row_copy(t, dest_ref[0, 0, t * TOP_K + k]).start()
        return 0

    lax.fori_loop(0, tm, issue, 0)

    def drain(t, _):
        row_copy(0, 0).wait()
        return 0

    lax.fori_loop(0, tm * TOP_K, drain, 0)


def _dispatch(dest, h2p, xs0):
    t, w = h2p.shape
    tm = TM_DISP
    return pl.pallas_call(
        _dispatch_kernel,
        grid=(t // tm,),
        in_specs=[pl.BlockSpec((1, 1, tm * TOP_K), lambda i: (i, 0, 0), memory_space=pltpu.SMEM),
                  pl.BlockSpec((tm, w), lambda i: (i, 0)),
                  pl.BlockSpec(memory_space=pl.ANY)],
        out_specs=pl.BlockSpec(memory_space=pl.ANY),
        out_shape=jax.ShapeDtypeStruct(xs0.shape, xs0.dtype),
        scratch_shapes=[pltpu.SemaphoreType.DMA(())],
        input_output_aliases={2: 0},
        compiler_params=pltpu.CompilerParams(dimension_semantics=("arbitrary",),
                                             vmem_limit_bytes=VMEM_LIMIT),
        name="dispatch",
    )(dest.reshape(t // tm, 1, tm * TOP_K), h2p, xs0)


def _experts_kernel(be_ref, nu_ref, xs_ref, wgu_ref, bgu_ref, wd_ref, bd_ref, ys_ref):
    i = pl.program_id(0)

    @pl.when(i < nu_ref[0])
    def _():
        w = xs_ref[...]
        x_lo = lax.bitcast_convert_type(w << 16, F32).astype(BF16)
        x_hi = lax.bitcast_convert_type(w & jnp.uint32(0xFFFF0000), F32).astype(BF16)
        gu = (_dot(x_lo, wgu_ref[0, :512, :].astype(BF16)) + _dot(x_hi, wgu_ref[0, 512:, :].astype(BF16))
              + bgu_ref[0])
        gate = jnp.minimum(gu[:, :D_FF], SWIGLU_LIMIT)
        lin = jnp.clip(gu[:, D_FF:], -SWIGLU_LIMIT, SWIGLU_LIMIT)
        act = (lin + 1.0) * (gate * jax.nn.sigmoid(SWIGLU_ALPHA * gate))
        ys_ref[...] = _dot(act.astype(BF16), wd_ref[0].astype(BF16)) + bd_ref[0]

    @pl.when(i >= nu_ref[0])
    def _():
        ys_ref[...] = jnp.zeros_like(ys_ref)


def _experts(block_e, n_used, xs, wgu, bgu, wd, bd):
    cap = xs.shape[0]
    nblk = cap // MOE_BLOCK
    return pl.pallas_call(
        _experts_kernel,
        grid_spec=pltpu.PrefetchScalarGridSpec(
            num_scalar_prefetch=2,
            grid=(nblk,),
            in_specs=[pl.BlockSpec((MOE_BLOCK, 512), lambda i, be, nu: (i, 0)),
                      pl.BlockSpec((1, D_MODEL, 2 * D_FF), lambda i, be, nu: (be[i], 0, 0)),
                      pl.BlockSpec((1, 1, 2 * D_FF), lambda i, be, nu: (be[i], 0, 0)),
                      pl.BlockSpec((1, D_FF, D_MODEL), lambda i, be, nu: (be[i], 0, 0)),
                      pl.BlockSpec((1, 1, D_MODEL), lambda i, be, nu: (be[i], 0, 0))],
            out_specs=pl.BlockSpec((MOE_BLOCK, D_MODEL), lambda i, be, nu: (i, 0))),
        out_shape=jax.ShapeDtypeStruct((cap, D_MODEL), F32),
        compiler_params=pltpu.CompilerParams(dimension_semantics=("arbitrary",),
                                             vmem_limit_bytes=VMEM_LIMIT),
        name="experts",
    )(block_e, n_used, xs, wgu, bgu, wd, bd)


def _combine_kernel(dest_ref, ys_ref, rw_ref, x1_ref, gate_ref, g_ref, o_ref, buf, sem):
    tm = x1_ref.shape[1]

    def row_copy(t, k, d):
        return pltpu.make_async_copy(ys_ref.at[pl.ds(d, 1)], buf.at[k, pl.ds(t, 1)], sem)

    def issue(t, _):
        for k in range(TOP_K):
            row_copy(t, k, dest_ref[0, 0, t * TOP_K + k]).start()
        return 0

    lax.fori_loop(0, tm, issue, 0)

    def drain(t, _):
        row_copy(0, 0, 0).wait()
        return 0

    lax.fori_loop(0, tm * TOP_K, drain, 0)

    rw = rw_ref[0]
    f = functools.reduce(jnp.add, [rw[:, k:k + 1] * buf[k] for k in range(TOP_K)])
    o_ref[0] = x1_ref[0] + gate_ref[0] * _rms(f, g_ref[...])


def _combine(dest, ys, rw, x1, gate, g):
    b, s, d = x1.shape
    tm = TM_COMB
    nt = s // tm
    tok = lambda bb, i: (bb, i, 0)
    return pl.pallas_call(
        _combine_kernel,
        grid=(b, nt),
        in_specs=[pl.BlockSpec((1, 1, tm * TOP_K), lambda bb, i: (bb * nt + i, 0, 0), memory_space=pltpu.SMEM),
                  pl.BlockSpec(memory_space=pl.ANY),
                  pl.BlockSpec((1, tm, LANES), tok), pl.BlockSpec((1, tm, d), tok),
                  pl.BlockSpec((1, 1, d), lambda bb, i: (bb, 0, 0)), pl.BlockSpec((1, d), lambda bb, i: (0, 0))],
        out_specs=pl.BlockSpec((1, tm, d), tok),
        out_shape=jax.ShapeDtypeStruct((b, s, d), F32),
        scratch_shapes=[pltpu.VMEM((TOP_K, tm, d), F32), pltpu.SemaphoreType.DMA(())],
        compiler_params=pltpu.CompilerParams(dimension_semantics=("arbitrary", "arbitrary"),
                                             vmem_limit_bytes=VMEM_LIMIT),
        name="combine",
    )(dest.reshape(b * nt, 1, tm * TOP_K), ys, rw, x1, gate, g)


def _rope_angles(rows, rot_dim):
    row = jnp.broadcast_to(jnp.arange(rows, dtype=F32)[:, None], (rows, GRID_W)).reshape(-1)
    col = jnp.broadcast_to(jnp.arange(GRID_W, dtype=F32)[None, :], (rows, GRID_W)).reshape(-1)
    quarter = rot_dim // 4
    inv_freq = ROPE_BASE ** (-jnp.arange(quarter, dtype=F32) / quarter)
    ang = jnp.concatenate([row[:, None] * inv_freq, col[:, None] * inv_freq], axis=-1)
    return jnp.cos(ang), jnp.sin(ang)


def _rope_tables(s):
    ca, sa = _rope_angles(s // GRID_W, HEAD_DIM)
    cb, sb = _rope_angles(s // GRID_W, MLA_ROPE)
    one = lambda w: jnp.ones((s, w), F32)
    zero = lambda w: jnp.zeros((s, w), F32)
    cos_a = jnp.tile(jnp.concatenate([ca, ca], 1), (1, 2))
    sin_a = jnp.tile(jnp.concatenate([-sa, sa], 1), (1, 2))
    cos_q = jnp.concatenate([one(64), cb, cb, one(32)], 1)
    sin_q = jnp.concatenate([zero(64), -sb, sb, zero(32)], 1)
    cos_k = jnp.concatenate([cb, cb, one(96)], 1)
    sin_k = jnp.concatenate([-sb, sb, zero(96)], 1)
    return cos_a, sin_a, cos_q, sin_q, cos_k, sin_k


def _identity_tables(n):
    one, zero = jnp.ones((n, LANES), F32), jnp.zeros((n, LANES), F32)
    return one, zero, one, zero, one, zero


def _inproj_weights(w_in, g_q_a, w_uq, g_kv_a, w_ukv):
    d = w_in.shape[0]
    wq, wk, wv = w_in[:, :512], w_in[:, 512:640], w_in[:, 640:768]
    wcq, wckv, wkr = w_in[:, 768:1152], w_in[:, 1152:1408], w_in[:, 1408:1440]
    z64 = jnp.zeros((d, 64), F32)
    k4 = jnp.concatenate([wk[:, :64], z64, z64, wk[:, :64], wk[:, 64:], z64, z64, wk[:, 64:]], 1)
    v2 = jnp.concatenate([wv, wv[:, 64:], wv[:, :64]], 1)
    wkr_p = jnp.concatenate([wkr, jnp.zeros((d, 96), F32)], 1)
    win = jnp.concatenate([wq, k4, v2, wcq, wckv, wkr_p], 1).astype(BF16)
    wuq = jnp.pad(w_uq.reshape(MLA_Q_LORA, MLA_HEADS, MLA_NOPE + MLA_ROPE), ((0, 0), (0, 0), (0, 32)))
    wuq = wuq.reshape(MLA_Q_LORA, MLA_HEADS * LANES).astype(BF16)
    wukv = w_ukv.reshape(MLA_KV_LORA, MLA_HEADS, MLA_NOPE + MLA_V)
    wkk = jnp.pad(wukv[:, :, :MLA_NOPE], ((0, 0), (0, 0), (0, 64))).reshape(MLA_KV_LORA, MLA_HEADS * LANES)
    wvv = wukv[:, :, MLA_NOPE:].reshape(MLA_KV_LORA, MLA_HEADS * MLA_V)
    j = jnp.arange(LANES)[:, None]
    cidx = jnp.arange(MLA_HEADS * LANES)[None, :]
    e = ((j < MLA_ROPE) & ((cidx % LANES) == MLA_NOPE + j)).astype(BF16)
    return (win, g_q_a.reshape(1, -1), wuq, g_kv_a.reshape(1, -1), wkk.astype(BF16), wvv.astype(BF16), e)


def kernel(x, c, ctx, c_ctx, w_ada, b_ada, g_mix_pre, g_mix_post, w_in, sink, g_q_a, w_uq, g_kv_a, w_ukv, w_o,
           g_ffn_pre, g_ffn_post, w_router, b_router, w_gate_up, b_gate_up, w_down, b_down):
    b, s, d = x.shape
    cl = ctx.shape[1]
    t = b * s

    cc = jnp.zeros((8, d), F32).at[:b].set(c).at[b].set(c_ctx)
    mod_all = _adaln(cc, w_ada[0], b_ada[0].reshape(1, -1))
    mod = [mod_all[:b, i * d:(i + 1) * d].reshape(b, 1, d) for i in range(6)]
    mod_c = [jnp.broadcast_to(mod_all[b, i * d:(i + 1) * d].reshape(1, 1, d), (b, 1, d)) for i in range(2)]

    wts = _inproj_weights(w_in[0], g_q_a[0], w_uq[0], g_kv_a[0], w_ukv[0])
    gpre = g_mix_pre[0].reshape(1, d)
    qa, k4, v2, q, k, v = _inproj(x, mod[0], mod[1], gpre, _rope_tables(s), wts, TM_IN)
    _, k4c, v2c, _, kc, vc = _inproj(ctx, mod_c[0], mod_c[1], gpre, _identity_tables(cl), wts, cl)

    out_a = _gqa(sink[0], qa, k4, v2, k4c, v2c)
    out_b = _mla(q, k, v, kc, vc)

    wo = w_o[0].astype(BF16)
    wr = jnp.pad(w_router[0], ((0, 0), (0, LANES - N_EXPERTS)))
    wr_hi, wr_lo = _split_bf16(wr)
    br = jnp.concatenate([b_router[0], jnp.full((LANES - N_EXPERTS,), NEG_INF, F32)]).reshape(1, LANES)
    ii = jnp.arange(TM_OUT)
    tri = (ii[:, None] > ii[None, :]).astype(BF16)
    x1, h2p, ri, rw, counts = _outproj(out_a, out_b, x, wo[:512], wo[512:], g_mix_post[0].reshape(1, d), mod[2],
                                       g_ffn_pre[0].reshape(1, d), mod[3], mod[4], wr_hi, wr_lo, br, tri)

    counts = counts[0, :N_EXPERTS].astype(jnp.int32)
    padded = (counts + MOE_BLOCK - 1) // MOE_BLOCK * MOE_BLOCK
    pad_end = jnp.cumsum(padded)
    pad_start = pad_end - padded
    nblk = t * TOP_K // MOE_BLOCK + N_EXPERTS
    cap = nblk * MOE_BLOCK
    block_e = jnp.minimum(jnp.searchsorted(pad_end, jnp.arange(nblk) * MOE_BLOCK, side='right'),
                          N_EXPERTS - 1).astype(jnp.int32)
    n_used = (pad_end[-1] // MOE_BLOCK).astype(jnp.int32).reshape(1)
    ri = ri.reshape(t, LANES)
    idx, rank = ri[:, :TOP_K], ri[:, TOP_K:2 * TOP_K]
    sel = idx[:, :, None] == jnp.arange(N_EXPERTS)[None, None, :]
    dest = (rank + jnp.sum(jnp.where(sel, pad_start[None, None, :], 0), axis=-1)).reshape(-1)

    xs = _dispatch(dest, h2p.reshape(t, 512), jnp.zeros((cap, 512), jnp.uint32))
    ys = _experts(block_e, n_used, xs, w_gate_up[0], b_gate_up[0].reshape(N_EXPERTS, 1, -1),
                  w_down[0], b_down[0].reshape(N_EXPERTS, 1, -1))
    return _combine(dest, ys, rw, x1, mod[5], g_ffn_post[0].reshape(1, d))
```

```python
import functools

import jax
import jax.numpy as jnp
from jax import lax
from jax.experimental import pallas as pl
from jax.experimental.pallas import tpu as pltpu

F32 = jnp.float32
BF16 = jnp.bfloat16

D_MODEL = 1024
GRID_W = 64
HEAD_DIM = 64
A_HEADS = 8
A_KV_HEADS = 2
BLOCK = 128
A_SCALE = HEAD_DIM ** -0.5
MLA_HEADS = 8
MLA_NOPE = 64
MLA_ROPE = 32
MLA_V = 64
MLA_Q_LORA = 384
MLA_KV_LORA = 256
MLA_SCALE = (MLA_NOPE + MLA_ROPE) ** -0.5
LOG2_E = 1.4426950408889634
N_EXPERTS = 32
TOP_K = 4
D_FF = 1024
SWIGLU_LIMIT = 7.0
SWIGLU_ALPHA = 1.702
ROPE_BASE = 10000.0
EPS = 1e-6
NEG_INF = -1e30

LANES = 128
VMEM_LIMIT = 56 * 1024 * 1024

TM_IN = 512
TQ_MLA = 256
TK_MLA = 512
TM_OUT = 512
MOE_BLOCK = 256
RUN_ALIGN = 8
BIG_COPY = 32
SORTED_ROWS = TM_OUT * TOP_K + N_EXPERTS * RUN_ALIGN


def _dot(a, b):
    return jnp.dot(a, b, preferred_element_type=F32)


def _dot_nt(a, b):
    return lax.dot_general(a, b, (((1,), (1,)), ((), ())), preferred_element_type=F32)


def _split_bf16(x):
    hi = x.astype(BF16)
    lo = (x - hi.astype(F32)).astype(BF16)
    return hi, lo


def _rms(x, g):
    return x * lax.rsqrt(jnp.mean(x * x, axis=-1, keepdims=True) + EPS) * g


def _rope(x, cos, sin_signed, first_half, half):
    n = x.shape[-1]
    partner = jnp.where(first_half, pltpu.roll(x, n - half, 1), pltpu.roll(x, half, 1))
    return x * cos + partner * sin_signed


def _adaln_kernel(c_ref, w_ref, b_ref, o_ref):
    c = c_ref[...]
    s = c * jax.nn.sigmoid(c)
    s_hi, s_lo = _split_bf16(s)
    w_hi, w_lo = _split_bf16(w_ref[...])
    o_ref[...] = _dot(s_hi, w_hi) + _dot(s_hi, w_lo) + _dot(s_lo, w_hi) + b_ref[...]


def _adaln(cc, w, b):
    n = w.shape[1]
    tn = 1024
    return pl.pallas_call(
        _adaln_kernel,
        grid=(n // tn,),
        in_specs=[pl.BlockSpec((8, D_MODEL), lambda j: (0, 0)),
                  pl.BlockSpec((D_MODEL, tn), lambda j: (0, j)),
                  pl.BlockSpec((1, tn), lambda j: (0, j))],
        out_specs=pl.BlockSpec((8, tn), lambda j: (0, j)),
        out_shape=jax.ShapeDtypeStruct((8, n), F32),
        compiler_params=pltpu.CompilerParams(dimension_semantics=("arbitrary",),
                                             vmem_limit_bytes=VMEM_LIMIT),
        name="adaln",
    )(cc, w, b)


def _inproj_kernel(x_ref, shift_ref, scale_ref, gpre_ref, ca_ref, sa_ref, cq_ref, sq_ref, ck_ref, sk_ref,
                   win_ref, gq_ref, wuq_ref, gkv_ref, wk_ref, wv_ref, e_ref,
                   qa_ref, k4_ref, v2_ref, q_ref, k_ref, vt_ref):
    x = x_ref[0]
    tm = x.shape[0]
    h = _rms(x, gpre_ref[...]) * (1.0 + scale_ref[0]) + shift_ref[0]
    p = _dot(h.astype(BF16), win_ref[...])
    lane = lax.broadcasted_iota(jnp.int32, (tm, LANES), 1)

    first_a = (lane & 32) == 0
    ca, sa = ca_ref[...], sa_ref[...]
    for j in range(4):
        sl = slice(j * LANES, (j + 1) * LANES)
        qa_ref[0, :, sl] = (_rope(p[:, sl], ca, sa, first_a, 32) * A_SCALE).astype(BF16)
    for j in range(4):
        sl = slice(j * LANES, (j + 1) * LANES)
        k4_ref[0, :, sl] = _rope(p[:, 512 + j * LANES:512 + (j + 1) * LANES], ca, sa, first_a, 32).astype(BF16)
    v2_ref[0] = p[:, 1024:1280].astype(BF16)

    cqn = _rms(p[:, 1280:1664], gq_ref[...]).astype(BF16)
    qf = _dot(cqn, wuq_ref[...])
    first_q = (lane >= 64) & (lane < 80)
    cq, sq = cq_ref[...], sq_ref[...]
    for hh in range(MLA_HEADS):
        sl = slice(hh * LANES, (hh + 1) * LANES)
        q_ref[0, :, sl] = (_rope(qf[:, sl], cq, sq, first_q, 16) * (MLA_SCALE * LOG2_E)).astype(BF16)

    ckvn = _rms(p[:, 1664:1920], gkv_ref[...]).astype(BF16)
    kr = _rope(p[:, 1920:2048], ck_ref[...], sk_ref[...], lane < 16, 16)
    k_ref[0] = (_dot(ckvn, wk_ref[...]) + _dot(kr.astype(BF16), e_ref[...])).astype(BF16)
    vt_ref[0, 0] = _dot_nt(wv_ref[...], ckvn).astype(BF16)


def _inproj(x, shift, scale, gpre, tabs, wts, tm):
    bx, n, d = x.shape
    ca, sa, cq, sq, ck, sk = tabs
    win, gq, wuq, gkv, wk, wv, e = wts
    tok = lambda i, b: (b, i, 0)
    vec = lambda i, b: (b, 0, 0)
    tab = lambda i, b: (i, 0)
    cst = lambda i, b: (0, 0)
    in_specs = [pl.BlockSpec((1, tm, d), tok), pl.BlockSpec((1, 1, d), vec), pl.BlockSpec((1, 1, d), vec),
                pl.BlockSpec((1, d), cst)]
    in_specs += [pl.BlockSpec((tm, LANES), tab)] * 6
    in_specs += [pl.BlockSpec(w.shape, cst) for w in wts]
    widths = (512, 512, 256, 1024, 1024)
    out_specs = [pl.BlockSpec((1, tm, w), tok) for w in widths]
    out_specs.append(pl.BlockSpec((1, 1, 512, tm), lambda i, b: (b, i, 0, 0)))
    out_shape = [jax.ShapeDtypeStruct((bx, n, w), BF16) for w in widths]
    out_shape.append(jax.ShapeDtypeStruct((bx, n // tm, 512, tm), BF16))
    return pl.pallas_call(
        _inproj_kernel,
        grid=(n // tm, bx),
        in_specs=in_specs,
        out_specs=out_specs,
        out_shape=out_shape,
        compiler_params=pltpu.CompilerParams(dimension_semantics=("arbitrary", "arbitrary"),
                                             vmem_limit_bytes=VMEM_LIMIT),
        name="inproj",
    )(x, shift, scale, gpre, ca, sa, cq, sq, ck, sk, win, gq, wuq, gkv, wk, wv, e)


def _gqa_kernel(sink_ref, q_ref, kp_ref, kc_ref, kn_ref, kx_ref, vp_ref, vc_ref, vn_ref, vx_ref, o_ref):
    n = pl.program_id(1)
    nb = pl.num_programs(1)
    q = q_ref[0]
    ks = (kp_ref[0], kc_ref[0], kn_ref[0], kx_ref[0])
    vs = (vp_ref[0], vc_ref[0], vn_ref[0], vx_ref[0])
    row = lax.broadcasted_iota(jnp.int32, (BLOCK, BLOCK), 0)
    col = lax.broadcasted_iota(jnp.int32, (BLOCK, BLOCK), 1)
    off_prev = jnp.where(n > 0, 0, 2 * BLOCK)
    off_next = jnp.where(n < nb - 1, 0, 2 * BLOCK)
    masks = (col >= row + off_prev, None, col <= row - off_next, None)
    lane = lax.broadcasted_iota(jnp.int32, (BLOCK, LANES), 1)
    for pair in range(A_HEADS // 2):
        hk = pair // 2
        qp = q[:, pair * LANES:(pair + 1) * LANES]
        res = []
        for sub in range(2):
            head = 2 * pair + sub
            ksl = slice((2 * hk + sub) * LANES, (2 * hk + sub + 1) * LANES)
            ss = []
            for k, msk in zip(ks, masks):
                s = _dot_nt(qp, k[:, ksl])
                ss.append(s if msk is None else jnp.where(msk, s, NEG_INF))
            sink = sink_ref[head]
            m = jnp.maximum(functools.reduce(jnp.maximum, [jnp.max(s, axis=-1, keepdims=True) for s in ss]), sink)
            ps = [jnp.exp(s - m) for s in ss]
            l = functools.reduce(jnp.add, [jnp.sum(p, axis=-1, keepdims=True) for p in ps]) + jnp.exp(sink - m)
            vsl = slice(0, LANES) if sub == hk else slice(LANES, 2 * LANES)
            acc = functools.reduce(jnp.add, [_dot(p.astype(BF16), v[:, vsl]) for p, v in zip(ps, vs)])
            res.append(acc * (1.0 / l))
        o_ref[0, :, pair * LANES:(pair + 1) * LANES] = jnp.where(lane < HEAD_DIM, res[0], res[1]).astype(BF16)


def _gqa(sink, qa, k4, v2, k4c, v2c):
    b, s, _ = qa.shape
    nb = s // BLOCK
    c = k4c.shape[1]
    cur = lambda bb, n: (bb, n, 0)
    prv = lambda bb, n: (bb, jnp.maximum(n - 1, 0), 0)
    nxt = lambda bb, n: (bb, jnp.minimum(n + 1, nb - 1), 0)
    ctx = lambda bb, n: (bb, 0, 0)
    return pl.pallas_call(
        _gqa_kernel,
        grid=(b, nb),
        in_specs=[pl.BlockSpec(memory_space=pltpu.SMEM),
                  pl.BlockSpec((1, BLOCK, 512), cur),
                  pl.BlockSpec((1, BLOCK, 512), prv), pl.BlockSpec((1, BLOCK, 512), cur),
                  pl.BlockSpec((1, BLOCK, 512), nxt), pl.BlockSpec((1, c, 512), ctx),
                  pl.BlockSpec((1, BLOCK, 256), prv), pl.BlockSpec((1, BLOCK, 256), cur),
                  pl.BlockSpec((1, BLOCK, 256), nxt), pl.BlockSpec((1, c, 256), ctx)],
        out_specs=pl.BlockSpec((1, BLOCK, 512), cur),
        out_shape=jax.ShapeDtypeStruct((b, s, 512), BF16),
        compiler_params=pltpu.CompilerParams(dimension_semantics=("arbitrary", "arbitrary"),
                                             vmem_limit_bytes=VMEM_LIMIT),
        name="gqa",
    )(sink, qa, k4, k4, k4, k4c, v2, v2, v2, v2c)


def _mla_kernel(q_ref, k_ref, vt_ref, kc_ref, vct_ref, o_ref, m_ref, l_ref, acc_ref, sa_ref, sb_ref):
    nk = k_ref.shape[1] // TK_MLA

    def scores_into(buf, k_blk):
        n = k_blk.shape[0]
        for hh in range(MLA_HEADS):
            hsl = slice(hh * LANES, (hh + 1) * LANES)
            buf[hh, 0:n, :] = _dot_nt(k_blk[:, hsl], q_ref[0, :, hsl])

    def consume(buf, vt_blk, first):
        n = vt_blk.shape[1]
        for hh in range(MLA_HEADS):
            dsl = slice(hh * MLA_V, (hh + 1) * MLA_V)
            st = buf[hh, 0:n, :]
            cmax = jnp.max(st, axis=0, keepdims=True)
            if first:
                m_new = cmax
            else:
                m_old = m_ref[hh, 0:1, :]
                m_new = jnp.maximum(m_old, cmax)
                alpha = jnp.exp2(m_old - m_new)
            p = jnp.exp2(st - m_new)
            psum = jnp.sum(p, axis=0, keepdims=True)
            pv = _dot(vt_blk[dsl, :], p.astype(BF16))
            if first:
                l_ref[hh, 0:1, :] = psum
                acc_ref[dsl, :] = pv
            else:
                l_ref[hh, 0:1, :] = alpha * l_ref[hh, 0:1, :] + psum
                acc_ref[dsl, :] = alpha * acc_ref[dsl, :] + pv
            m_ref[hh, 0:1, :] = m_new

    def k_chunk(c):
        return k_ref[0, pl.ds(pl.multiple_of(c * TK_MLA, TK_MLA), TK_MLA), :]

    scores_into(sa_ref, kc_ref[0])
    scores_into(sb_ref, k_chunk(0))
    consume(sa_ref, vct_ref[0, 0], True)

    def body(i, _):
        c = 2 * i
        scores_into(sa_ref, k_chunk(c + 1))
        consume(sb_ref, vt_ref[0, c], False)
        scores_into(sb_ref, k_chunk(c + 2))
        consume(sa_ref, vt_ref[0, c + 1], False)
        return 0

    lax.fori_loop(0, (nk - 2) // 2, body, 0)
    scores_into(sa_ref, k_chunk(nk - 1))
    consume(sb_ref, vt_ref[0, nk - 2], False)
    consume(sa_ref, vt_ref[0, nk - 1], False)

    for hh in range(MLA_HEADS):
        dsl = slice(hh * MLA_V, (hh + 1) * MLA_V)
        acc_ref[dsl, :] = acc_ref[dsl, :] * (1.0 / l_ref[hh, 0:1, :])
    o_ref[0] = acc_ref[...].T.astype(BF16)


def _mla(q, k, vt, kc, vct):
    b, s, _ = q.shape
    c = kc.shape[1]
    nk = s // TK_MLA
    assert vt.shape == (b, nk, MLA_HEADS * MLA_V, TK_MLA) and nk % 2 == 0
    return pl.pallas_call(
        _mla_kernel,
        grid=(b, s // TQ_MLA),
        in_specs=[pl.BlockSpec((1, TQ_MLA, 1024), lambda bb, i: (bb, i, 0)),
                  pl.BlockSpec((1, s, 1024), lambda bb, i: (bb, 0, 0)),
                  pl.BlockSpec((1, nk, 512, TK_MLA), lambda bb, i: (bb, 0, 0, 0)),
                  pl.BlockSpec((1, c, 1024), lambda bb, i: (bb, 0, 0)),
                  pl.BlockSpec((1, 1, 512, c), lambda bb, i: (bb, 0, 0, 0))],
        out_specs=pl.BlockSpec((1, TQ_MLA, 512), lambda bb, i: (bb, i, 0)),
        out_shape=jax.ShapeDtypeStruct((b, s, 512), BF16),
        scratch_shapes=[pltpu.VMEM((MLA_HEADS, 8, TQ_MLA), F32), pltpu.VMEM((MLA_HEADS, 8, TQ_MLA), F32),
                        pltpu.VMEM((MLA_HEADS * MLA_V, TQ_MLA), F32),
                        pltpu.VMEM((MLA_HEADS, TK_MLA, TQ_MLA), F32), pltpu.VMEM((MLA_HEADS, TK_MLA, TQ_MLA), F32)],
        compiler_params=pltpu.CompilerParams(dimension_semantics=("arbitrary", "arbitrary"),
                                             vmem_limit_bytes=VMEM_LIMIT),
        name="mla",
    )(q, k, vt, kc, vct)


def _outproj_kernel(oa_ref, ob_ref, x_ref, woa_ref, wob_ref, gpost_ref, gate_ref, gffn_ref, shift_ref, scale_ref,
                    wrh_ref, wrl_ref, br_ref, tri_ref, upper_ref,
                    x1_ref, h2_ref, ri_ref, rw_ref, cnt_ref):
    y = _dot(oa_ref[0], woa_ref[...]) + _dot(ob_ref[0], wob_ref[...])
    x1 = x_ref[0] + gate_ref[0] * _rms(y, gpost_ref[...])
    x1_ref[0] = x1
    h2 = _rms(x1, gffn_ref[...]) * (1.0 + scale_ref[0]) + shift_ref[0]
    tm = h2.shape[0]
    h2_ref[0] = h2.astype(BF16)

    h_hi, h_lo = _split_bf16(h2)
    logits = _dot(h_hi, wrh_ref[...]) + _dot(h_hi, wrl_ref[...]) + _dot(h_lo, wrh_ref[...]) + br_ref[...]
    lane = lax.broadcasted_iota(jnp.int32, (tm, LANES), 1)
    lane_f = lane.astype(F32)
    tops, idxs = [], []
    cur = logits
    for _ in range(TOP_K):
        mk = jnp.max(cur, axis=-1, keepdims=True)
        ik = jnp.min(jnp.where(cur == mk, lane_f, float(LANES)), axis=-1, keepdims=True)
        tops.append(mk)
        idxs.append(ik)
        cur = jnp.where(lane_f == ik, -jnp.inf, cur)
    es = [jnp.exp(t - tops[0]) for t in tops]
    inv = 1.0 / functools.reduce(jnp.add, es)

    onehots = [(lane_f == ik) for ik in idxs]
    onehot = functools.reduce(jnp.add, [o.astype(F32) for o in onehots])
    cnt = jnp.sum(onehot, axis=0, keepdims=True)
    prefix = _dot(tri_ref[...], onehot.astype(BF16))
    chunks = jnp.floor((cnt + (RUN_ALIGN - 1.0)) * (1.0 / RUN_ALIGN))
    run_start = _dot(jnp.broadcast_to(chunks, (8, LANES)).astype(BF16), upper_ref[...])[0:1, :] * RUN_ALIGN
    base = prefix + run_start
    ri = jnp.zeros((tm, LANES), jnp.int32)
    rw = jnp.zeros((tm, LANES), F32)
    for k in range(TOP_K):
        pos = jnp.sum(jnp.where(onehots[k], base, 0.0), axis=-1, keepdims=True).astype(jnp.int32)
        ri = jnp.where(lane == k, idxs[k].astype(jnp.int32), ri)
        ri = jnp.where(lane == TOP_K + k, pos, ri)
        rw = jnp.where(lane == k, es[k] * inv, rw)
    ri_ref[0] = ri
    rw_ref[0] = rw
    cnt_ref[0] = cnt


def _outproj(oa, ob, x, woa, wob, gpost, gate, gffn, shift, scale, wrh, wrl, br, tri, upper):
    b, s, d = x.shape
    tm = TM_OUT
    nt = s // tm
    tok = lambda bb, i: (bb, i, 0)
    vec = lambda bb, i: (bb, 0, 0)
    cst = lambda bb, i: (0, 0)
    return pl.pallas_call(
        _outproj_kernel,
        grid=(b, nt),
        in_specs=[pl.BlockSpec((1, tm, 512), tok), pl.BlockSpec((1, tm, 512), tok), pl.BlockSpec((1, tm, d), tok),
                  pl.BlockSpec((512, d), cst), pl.BlockSpec((512, d), cst), pl.BlockSpec((1, d), cst),
                  pl.BlockSpec((1, 1, d), vec), pl.BlockSpec((1, d), cst),
                  pl.BlockSpec((1, 1, d), vec), pl.BlockSpec((1, 1, d), vec),
                  pl.BlockSpec((d, LANES), cst), pl.BlockSpec((d, LANES), cst), pl.BlockSpec((1, LANES), cst),
                  pl.BlockSpec((tm, tm), cst), pl.BlockSpec((LANES, LANES), cst)],
        out_specs=[pl.BlockSpec((1, tm, d), tok), pl.BlockSpec((1, tm, d), tok),
                   pl.BlockSpec((1, tm, LANES), tok), pl.BlockSpec((1, tm, LANES), tok),
                   pl.BlockSpec((1, 1, LANES), lambda bb, i: (bb * nt + i, 0, 0))],
        out_shape=[jax.ShapeDtypeStruct((b, s, d), F32), jax.ShapeDtypeStruct((b, s, d), BF16),
                   jax.ShapeDtypeStruct((b, s, LANES), jnp.int32), jax.ShapeDtypeStruct((b, s, LANES), F32),
                   jax.ShapeDtypeStruct((b * nt, 1, LANES), F32)],
        compiler_params=pltpu.CompilerParams(dimension_semantics=("arbitrary", "arbitrary"),
                                             vmem_limit_bytes=VMEM_LIMIT),
        name="outproj",
    )(oa, ob, x, woa, wob, gpost, gate, gffn, shift, scale, wrh, wrl, br, tri, upper)


def _run_copies(gs_ref, to_ref, nr_ref, tile, make_copy):
    def per_expert(e, totals):
        j = tile * N_EXPERTS + e
        g, o, rows = gs_ref[j], to_ref[j], nr_ref[j]
        n_big = rows // BIG_COPY
        n_small = (rows - n_big * BIG_COPY) // RUN_ALIGN

        def big(c, _):
            off = c * BIG_COPY
            make_copy(pl.multiple_of(g + off, RUN_ALIGN), pl.multiple_of(o + off, RUN_ALIGN), BIG_COPY).start()
            return 0

        def small(c, _):
            off = n_big * BIG_COPY + c * RUN_ALIGN
            make_copy(pl.multiple_of(g + off, RUN_ALIGN), pl.multiple_of(o + off, RUN_ALIGN), RUN_ALIGN).start()
            return 0

        lax.fori_loop(0, n_big, big, 0)
        lax.fori_loop(0, n_small, small, 0)
        return totals[0] + n_big, totals[1] + n_small

    n_big, n_small = lax.fori_loop(0, N_EXPERTS, per_expert, (0, 0))

    def wait_big(i, _):
        make_copy(0, 0, BIG_COPY).wait()
        return 0

    def wait_small(i, _):
        make_copy(0, 0, RUN_ALIGN).wait()
        return 0

    lax.fori_loop(0, n_big, wait_big, 0)
    lax.fori_loop(0, n_small, wait_small, 0)


def _dispatch_kernel(gs_ref, to_ref, nc_ref, ri_ref, h_ref, xs_in_ref, xs_ref, sorted_ref, sem):
    del xs_in_ref
    tile = pl.program_id(0)
    tm = h_ref.shape[0]
    post = ri_ref[...].astype(F32).T
    h = h_ref[...]
    rb_rows = 256

    def fill(rb, _):
        r0 = pl.multiple_of(rb * rb_rows, rb_rows)
        row = (lax.broadcasted_iota(jnp.int32, (rb_rows, tm), 0) + r0).astype(F32)
        perm = functools.reduce(jnp.add, [(row == post[TOP_K + k:TOP_K + k + 1, :]).astype(F32)
                                          for k in range(TOP_K)]).astype(BF16)
        xr = _dot(perm, h)
        lo = lax.bitcast_convert_type(xr[:, :512], jnp.uint32)
        hi = lax.bitcast_convert_type(xr[:, 512:], jnp.uint32)
        sorted_ref[pl.ds(r0, rb_rows), :] = (lo >> 16) | (hi & jnp.uint32(0xFFFF0000))
        return 0

    lax.fori_loop(0, SORTED_ROWS // rb_rows, fill, 0)

    def make_copy(g, o, rows):
        return pltpu.make_async_copy(sorted_ref.at[pl.ds(o, rows)], xs_ref.at[pl.ds(g, rows)], sem)

    _run_copies(gs_ref, to_ref, nc_ref, tile, make_copy)


def _dispatch(gstart, toff, nch, ri, h2, xs0):
    t, d = h2.shape
    tm = TM_OUT
    return pl.pallas_call(
        _dispatch_kernel,
        grid_spec=pltpu.PrefetchScalarGridSpec(
            num_scalar_prefetch=3,
            grid=(t // tm,),
            in_specs=[pl.BlockSpec((tm, LANES), lambda i, *_: (i, 0)),
                      pl.BlockSpec((tm, d), lambda i, *_: (i, 0)),
                      pl.BlockSpec(memory_space=pl.ANY)],
            out_specs=pl.BlockSpec(memory_space=pl.ANY),
            scratch_shapes=[pltpu.VMEM((SORTED_ROWS, d // 2), jnp.uint32), pltpu.SemaphoreType.DMA(())]),
        out_shape=jax.ShapeDtypeStruct(xs0.shape, xs0.dtype),
        input_output_aliases={5: 0},
        compiler_params=pltpu.CompilerParams(dimension_semantics=("arbitrary",),
                                             vmem_limit_bytes=VMEM_LIMIT),
        name="dispatch",
    )(gstart, toff, nch, ri, h2, xs0)


def _experts_kernel(be_ref, nu_ref, xs_ref, wgu_ref, bgu_ref, wd_ref, bd_ref, ys_ref):
    i = pl.program_id(0)

    @pl.when(i < nu_ref[0])
    def _():
        w = xs_ref[...]
        x_lo = lax.bitcast_convert_type(w << 16, F32).astype(BF16)
        x_hi = lax.bitcast_convert_type(w & jnp.uint32(0xFFFF0000), F32).astype(BF16)
        gu = (_dot(x_lo, wgu_ref[0, :512, :].astype(BF16)) + _dot(x_hi, wgu_ref[0, 512:, :].astype(BF16))
              + bgu_ref[0])
        gate = jnp.minimum(gu[:, :D_FF], SWIGLU_LIMIT)
        lin = jnp.clip(gu[:, D_FF:], -SWIGLU_LIMIT, SWIGLU_LIMIT)
        act = (lin + 1.0) * (gate * jax.nn.sigmoid(SWIGLU_ALPHA * gate))
        ys_ref[...] = _dot(act.astype(BF16), wd_ref[0].astype(BF16)) + bd_ref[0]

    @pl.when(i >= nu_ref[0])
    def _():
        ys_ref[...] = jnp.zeros_like(ys_ref)


def _experts(block_e, n_used, xs, wgu, bgu, wd, bd):
    cap = xs.shape[0]
    nblk = cap // MOE_BLOCK
    return pl.pallas_call(
        _experts_kernel,
        grid_spec=pltpu.PrefetchScalarGridSpec(
            num_scalar_prefetch=2,
            grid=(nblk,),
            in_specs=[pl.BlockSpec((MOE_BLOCK, 512), lambda i, be, nu: (i, 0)),
                      pl.BlockSpec((1, D_MODEL, 2 * D_FF), lambda i, be, nu: (be[i], 0, 0)),
                      pl.BlockSpec((1, 1, 2 * D_FF), lambda i, be, nu: (be[i], 0, 0)),
                      pl.BlockSpec((1, D_FF, D_MODEL), lambda i, be, nu: (be[i], 0, 0)),
                      pl.BlockSpec((1, 1, D_MODEL), lambda i, be, nu: (be[i], 0, 0))],
            out_specs=pl.BlockSpec((MOE_BLOCK, D_MODEL), lambda i, be, nu: (i, 0))),
        out_shape=jax.ShapeDtypeStruct((cap, D_MODEL), F32),
        compiler_params=pltpu.CompilerParams(dimension_semantics=("arbitrary",),
                                             vmem_limit_bytes=VMEM_LIMIT),
        name="experts",
    )(block_e, n_used, xs, wgu, bgu, wd, bd)


def _combine_kernel(gs_ref, to_ref, nc_ref, ys_ref, ri_ref, rw_ref, x1_ref, gate_ref, g_ref, o_ref, ybuf, sem):
    nt = pl.num_programs(1)
    tile = pl.program_id(0) * nt + pl.program_id(1)
    tm = x1_ref.shape[1]

    @pl.when(tile == 0)
    def _():
        ybuf[...] = jnp.zeros_like(ybuf)

    def make_copy(g, o, rows):
        return pltpu.make_async_copy(ys_ref.at[pl.ds(g, rows)], ybuf.at[pl.ds(o, rows)], sem)

    _run_copies(gs_ref, to_ref, nc_ref, tile, make_copy)

    posf = ri_ref[0].astype(F32)
    rw = rw_ref[0]
    cb_cols = 256
    f = jnp.zeros((tm, D_MODEL), F32)
    for cb in range(SORTED_ROWS // cb_cols):
        col = (lax.broadcasted_iota(jnp.int32, (tm, cb_cols), 1) + cb * cb_cols).astype(F32)
        wp = functools.reduce(jnp.add, [jnp.where(col == posf[:, TOP_K + k:TOP_K + k + 1], rw[:, k:k + 1], 0.0)
                                        for k in range(TOP_K)])
        f = f + _dot(wp.astype(BF16), ybuf[cb * cb_cols:(cb + 1) * cb_cols, :].astype(BF16))
    o_ref[0] = x1_ref[0] + gate_ref[0] * _rms(f, g_ref[...])


def _combine(gstart, toff, nch, ys, ri, rw, x1, gate, g):
    b, s, d = x1.shape
    tm = TM_OUT
    tok = lambda bb, i, *_: (bb, i, 0)
    return pl.pallas_call(
        _combine_kernel,
        grid_spec=pltpu.PrefetchScalarGridSpec(
            num_scalar_prefetch=3,
            grid=(b, s // tm),
            in_specs=[pl.BlockSpec(memory_space=pl.ANY),
                      pl.BlockSpec((1, tm, LANES), tok), pl.BlockSpec((1, tm, LANES), tok),
                      pl.BlockSpec((1, tm, d), tok),
                      pl.BlockSpec((1, 1, d), lambda bb, i, *_: (bb, 0, 0)),
                      pl.BlockSpec((1, d), lambda bb, i, *_: (0, 0))],
            out_specs=pl.BlockSpec((1, tm, d), tok),
            scratch_shapes=[pltpu.VMEM((SORTED_ROWS, d), F32), pltpu.SemaphoreType.DMA(())]),
        out_shape=jax.ShapeDtypeStruct((b, s, d), F32),
        compiler_params=pltpu.CompilerParams(dimension_semantics=("arbitrary", "arbitrary"),
                                             vmem_limit_bytes=VMEM_LIMIT),
        name="combine",
    )(gstart, toff, nch, ys, ri, rw, x1, gate, g)


def _rope_angles(rows, rot_dim):
    row = jnp.broadcast_to(jnp.arange(rows, dtype=F32)[:, None], (rows, GRID_W)).reshape(-1)
    col = jnp.broadcast_to(jnp.arange(GRID_W, dtype=F32)[None, :], (rows, GRID_W)).reshape(-1)
    quarter = rot_dim // 4
    inv_freq = ROPE_BASE ** (-jnp.arange(quarter, dtype=F32) / quarter)
    ang = jnp.concatenate([row[:, None] * inv_freq, col[:, None] * inv_freq], axis=-1)
    return jnp.cos(ang), jnp.sin(ang)


def _rope_tables(s):
    ca, sa = _rope_angles(s // GRID_W, HEAD_DIM)
    cb, sb = _rope_angles(s // GRID_W, MLA_ROPE)
    one = lambda w: jnp.ones((s, w), F32)
    zero = lambda w: jnp.zeros((s, w), F32)
    cos_a = jnp.tile(jnp.concatenate([ca, ca], 1), (1, 2))
    sin_a = jnp.tile(jnp.concatenate([-sa, sa], 1), (1, 2))
    cos_q = jnp.concatenate([one(64), cb, cb, one(32)], 1)
    sin_q = jnp.concatenate([zero(64), -sb, sb, zero(32)], 1)
    cos_k = jnp.concatenate([cb, cb, one(96)], 1)
    sin_k = jnp.concatenate([-sb, sb, zero(96)], 1)
    return cos_a, sin_a, cos_q, sin_q, cos_k, sin_k


def _identity_tables(n):
    one, zero = jnp.ones((n, LANES), F32), jnp.zeros((n, LANES), F32)
    return one, zero, one, zero, one, zero


def _inproj_weights(w_in, g_q_a, w_uq, g_kv_a, w_ukv):
    d = w_in.shape[0]
    wq, wk, wv = w_in[:, :512], w_in[:, 512:640], w_in[:, 640:768]
    wcq, wckv, wkr = w_in[:, 768:1152], w_in[:, 1152:1408], w_in[:, 1408:1440]
    z64 = jnp.zeros((d, 64), F32)
    k4 = jnp.concatenate([wk[:, :64], z64, z64, wk[:, :64], wk[:, 64:], z64, z64, wk[:, 64:]], 1)
    v2 = jnp.concatenate([wv, wv[:, 64:], wv[:, :64]], 1)
    wkr_p = jnp.concatenate([wkr, jnp.zeros((d, 96), F32)], 1)
    win = jnp.concatenate([wq, k4, v2, wcq, wckv, wkr_p], 1).astype(BF16)
    wuq = jnp.pad(w_uq.reshape(MLA_Q_LORA, MLA_HEADS, MLA_NOPE + MLA_ROPE), ((0, 0), (0, 0), (0, 32)))
    wuq = wuq.reshape(MLA_Q_LORA, MLA_HEADS * LANES).astype(BF16)
    wukv = w_ukv.reshape(MLA_KV_LORA, MLA_HEADS, MLA_NOPE + MLA_V)
    wkk = jnp.pad(wukv[:, :, :MLA_NOPE], ((0, 0), (0, 0), (0, 64))).reshape(MLA_KV_LORA, MLA_HEADS * LANES)
    wvv = wukv[:, :, MLA_NOPE:].reshape(MLA_KV_LORA, MLA_HEADS * MLA_V)
    j = jnp.arange(LANES)[:, None]
    cidx = jnp.arange(MLA_HEADS * LANES)[None, :]
    e = ((j < MLA_ROPE) & ((cidx % LANES) == MLA_NOPE + j)).astype(BF16)
    return (win, g_q_a.reshape(1, -1), wuq, g_kv_a.reshape(1, -1), wkk.astype(BF16), wvv.T.astype(BF16), e)


def kernel(x, c, ctx, c_ctx, w_ada, b_ada, g_mix_pre, g_mix_post, w_in, sink, g_q_a, w_uq, g_kv_a, w_ukv, w_o,
           g_ffn_pre, g_ffn_post, w_router, b_router, w_gate_up, b_gate_up, w_down, b_down):
    b, s, d = x.shape
    cl = ctx.shape[1]
    t = b * s

    cc = jnp.zeros((8, d), F32).at[:b].set(c).at[b].set(c_ctx)
    mod_all = _adaln(cc, w_ada[0], b_ada[0].reshape(1, -1))
    mod = [mod_all[:b, i * d:(i + 1) * d].reshape(b, 1, d) for i in range(6)]
    mod_c = [jnp.broadcast_to(mod_all[b, i * d:(i + 1) * d].reshape(1, 1, d), (b, 1, d)) for i in range(2)]

    wts = _inproj_weights(w_in[0], g_q_a[0], w_uq[0], g_kv_a[0], w_ukv[0])
    gpre = g_mix_pre[0].reshape(1, d)
    qa, k4, v2, q, k, vt = _inproj(x, mod[0], mod[1], gpre, _rope_tables(s), wts, TM_IN)
    _, k4c, v2c, _, kc, vct = _inproj(ctx, mod_c[0], mod_c[1], gpre, _identity_tables(cl), wts, cl)

    out_a = _gqa(sink[0], qa, k4, v2, k4c, v2c)
    out_b = _mla(q, k, vt, kc, vct)

    wo = w_o[0].astype(BF16)
    wr = jnp.pad(w_router[0], ((0, 0), (0, LANES - N_EXPERTS)))
    wr_hi, wr_lo = _split_bf16(wr)
    br = jnp.concatenate([b_router[0], jnp.full((LANES - N_EXPERTS,), NEG_INF, F32)]).reshape(1, LANES)
    ii = jnp.arange(TM_OUT)
    tri = (ii[:, None] > ii[None, :]).astype(BF16)
    jj = jnp.arange(LANES)
    upper = (jj[:, None] < jj[None, :]).astype(BF16)
    x1, h2, ri, rw, cnt = _outproj(out_a, out_b, x, wo[:512], wo[512:], g_mix_post[0].reshape(1, d), mod[2],
                                   g_ffn_pre[0].reshape(1, d), mod[3], mod[4], wr_hi, wr_lo, br, tri, upper)

    cnt = cnt[:, 0, :N_EXPERTS].astype(jnp.int32)
    nt = cnt.shape[0]
    rows = (cnt + RUN_ALIGN - 1) // RUN_ALIGN * RUN_ALIGN
    tot = jnp.sum(rows, axis=0)
    carry = jnp.cumsum(rows, axis=0) - rows
    padded = (tot + MOE_BLOCK - 1) // MOE_BLOCK * MOE_BLOCK
    pad_end = jnp.cumsum(padded)
    pad_start = pad_end - padded
    nblk = -(-(t * TOP_K + nt * N_EXPERTS * (RUN_ALIGN - 1)) // MOE_BLOCK) + N_EXPERTS
    cap = nblk * MOE_BLOCK
    blk_row = jnp.arange(nblk, dtype=jnp.int32) * MOE_BLOCK
    block_e = jnp.minimum(jnp.sum((pad_end[None, :] <= blk_row[:, None]).astype(jnp.int32), axis=1), N_EXPERTS - 1)
    n_used = (pad_end[-1] // MOE_BLOCK).astype(jnp.int32).reshape(1)
    toff = (jnp.cumsum(rows, axis=1) - rows).reshape(-1).astype(jnp.int32)
    gstart = (pad_start[None, :] + carry).reshape(-1).astype(jnp.int32)
    nrows = rows.reshape(-1).astype(jnp.int32)

    xs = _dispatch(gstart, toff, nrows, ri.reshape(t, LANES), h2.reshape(t, d),
                   jnp.zeros((cap, d // 2), jnp.uint32))
    ys = _experts(block_e.astype(jnp.int32), n_used, xs, w_gate_up[0], b_gate_up[0].reshape(N_EXPERTS, 1, -1),
                  w_down[0], b_down[0].reshape(N_EXPERTS, 1, -1))
    return _combine(gstart, toff, nrows, ys, ri, rw, x1, mod[5], g_ffn_post[0].reshape(1, d))
```

```python
import functools

import jax
import jax.numpy as jnp
from jax import lax
from jax.experimental import pallas as pl
from jax.experimental.pallas import tpu as pltpu

F32 = jnp.float32
BF16 = jnp.bfloat16

D_MODEL = 1024
GRID_W = 64
HEAD_DIM = 64
A_HEADS = 8
A_KV_HEADS = 2
BLOCK = 128
A_SCALE = HEAD_DIM ** -0.5
MLA_HEADS = 8
MLA_NOPE = 64
MLA_ROPE = 32
MLA_V = 64
MLA_Q_LORA = 384
MLA_KV_LORA = 256
MLA_SCALE = (MLA_NOPE + MLA_ROPE) ** -0.5
LOG2_E = 1.4426950408889634
N_EXPERTS = 32
TOP_K = 4
D_FF = 1024
SWIGLU_LIMIT = 7.0
SWIGLU_ALPHA = 1.702
ROPE_BASE = 10000.0
EPS = 1e-6
NEG_INF = -1e30

LANES = 128
VMEM_LIMIT = 56 * 1024 * 1024

TM_IN = 512
GQA_QB = 4
TQ_MLA = 256
TK_MLA = 512
VT_ROWS = 80
TM_OUT = 512
MOE_BLOCK = 256
RUN_ALIGN = 8
BIG_COPY = 32
SORTED_ROWS = TM_OUT * TOP_K + N_EXPERTS * RUN_ALIGN


def _dot(a, b):
    return jnp.dot(a, b, preferred_element_type=F32)


def _dot_nt(a, b):
    return lax.dot_general(a, b, (((1,), (1,)), ((), ())), preferred_element_type=F32)


def _split_bf16(x):
    hi = x.astype(BF16)
    lo = (x - hi.astype(F32)).astype(BF16)
    return hi, lo


def _rms(x, g):
    return x * lax.rsqrt(jnp.mean(x * x, axis=-1, keepdims=True) + EPS) * g


def _rope(x, cos, sin_signed, first_half, half):
    n = x.shape[-1]
    partner = jnp.where(first_half, pltpu.roll(x, n - half, 1), pltpu.roll(x, half, 1))
    return x * cos + partner * sin_signed


def _adaln_kernel(c_ref, w_ref, b_ref, o_ref):
    c = c_ref[...]
    s = c * jax.nn.sigmoid(c)
    s_hi, s_lo = _split_bf16(s)
    w_hi, w_lo = _split_bf16(w_ref[...])
    o_ref[...] = _dot(s_hi, w_hi) + _dot(s_hi, w_lo) + _dot(s_lo, w_hi) + b_ref[...]


def _adaln(cc, w, b):
    n = w.shape[1]
    tn = 1024
    return pl.pallas_call(
        _adaln_kernel,
        grid=(n // tn,),
        in_specs=[pl.BlockSpec((8, D_MODEL), lambda j: (0, 0)),
                  pl.BlockSpec((D_MODEL, tn), lambda j: (0, j)),
                  pl.BlockSpec((1, tn), lambda j: (0, j))],
        out_specs=pl.BlockSpec((8, tn), lambda j: (0, j)),
        out_shape=jax.ShapeDtypeStruct((8, n), F32),
        compiler_params=pltpu.CompilerParams(dimension_semantics=("arbitrary",),
                                             vmem_limit_bytes=VMEM_LIMIT),
        name="adaln",
    )(cc, w, b)


def _inproj_kernel(x_ref, shift_ref, scale_ref, gpre_ref, ca_ref, sa_ref, cq_ref, sq_ref, ck_ref, sk_ref,
                   win_ref, gq_ref, wuq_ref, gkv_ref, wk_ref, wv_ref, e_ref,
                   qa_ref, ka_ref, vat_ref, q_ref, k_ref, vt_ref):
    x = x_ref[0]
    tm = x.shape[0]
    h = _rms(x, gpre_ref[...]) * (1.0 + scale_ref[0]) + shift_ref[0]
    p = _dot(h.astype(BF16), win_ref[...])
    lane = lax.broadcasted_iota(jnp.int32, (tm, LANES), 1)

    first_a = (lane & 32) == 0
    ca, sa = ca_ref[...], sa_ref[...]
    for hh in range(A_HEADS):
        sl = slice(hh * LANES, (hh + 1) * LANES)
        qa_ref[0, :, sl] = (_rope(p[:, sl], ca, sa, first_a, 32) * (A_SCALE * LOG2_E)).astype(BF16)
    ka_ref[0] = _rope(p[:, 1024:1152], ca, sa, first_a, 32).astype(BF16)
    vat = p[:, 1152:1280].T
    ones_a = (lax.broadcasted_iota(jnp.int32, (VT_ROWS - HEAD_DIM, BLOCK), 0) == 0).astype(BF16)
    for j in range(tm // BLOCK):
        for hk in range(A_KV_HEADS):
            vat_ref[0, j, hk, 0:HEAD_DIM, :] = vat[hk * HEAD_DIM:(hk + 1) * HEAD_DIM,
                                                   j * BLOCK:(j + 1) * BLOCK].astype(BF16)
            vat_ref[0, j, hk, HEAD_DIM:VT_ROWS, :] = ones_a

    cqn = _rms(p[:, 1280:1664], gq_ref[...]).astype(BF16)
    qf = _dot(cqn, wuq_ref[...])
    first_q = (lane >= 64) & (lane < 80)
    cq, sq = cq_ref[...], sq_ref[...]
    for hh in range(MLA_HEADS):
        sl = slice(hh * LANES, (hh + 1) * LANES)
        q_ref[0, :, sl] = (_rope(qf[:, sl], cq, sq, first_q, 16) * (MLA_SCALE * LOG2_E)).astype(BF16)

    ckvn = _rms(p[:, 1664:1920], gkv_ref[...]).astype(BF16)
    kr = _rope(p[:, 1920:2048], ck_ref[...], sk_ref[...], lane < 16, 16)
    k_ref[0] = (_dot(ckvn, wk_ref[...]) + _dot(kr.astype(BF16), e_ref[...])).astype(BF16)
    vt = _dot_nt(wv_ref[...], ckvn).astype(BF16)
    ones_row = (lax.broadcasted_iota(jnp.int32, (VT_ROWS - MLA_V, tm), 0) == 0).astype(BF16)
    for hh in range(MLA_HEADS):
        vt_ref[0, 0, hh, 0:MLA_V, :] = vt[hh * MLA_V:(hh + 1) * MLA_V, :]
        vt_ref[0, 0, hh, MLA_V:VT_ROWS, :] = ones_row


def _inproj(x, shift, scale, gpre, tabs, wts, tm):
    bx, n, d = x.shape
    ca, sa, cq, sq, ck, sk = tabs
    win, gq, wuq, gkv, wk, wv, e = wts
    tok = lambda i, b: (b, i, 0)
    vec = lambda i, b: (b, 0, 0)
    tab = lambda i, b: (i, 0)
    cst = lambda i, b: (0, 0)
    in_specs = [pl.BlockSpec((1, tm, d), tok), pl.BlockSpec((1, 1, d), vec), pl.BlockSpec((1, 1, d), vec),
                pl.BlockSpec((1, d), cst)]
    in_specs += [pl.BlockSpec((tm, LANES), tab)] * 6
    in_specs += [pl.BlockSpec(w.shape, cst) for w in wts]
    row_out = lambda w: (pl.BlockSpec((1, tm, w), tok), jax.ShapeDtypeStruct((bx, n, w), BF16))
    outs = [row_out(A_HEADS * LANES), row_out(LANES),
            (pl.BlockSpec((1, tm // BLOCK, A_KV_HEADS, VT_ROWS, BLOCK), lambda i, b: (b, i, 0, 0, 0)),
             jax.ShapeDtypeStruct((bx, n // BLOCK, A_KV_HEADS, VT_ROWS, BLOCK), BF16)),
            row_out(MLA_HEADS * LANES), row_out(MLA_HEADS * LANES),
            (pl.BlockSpec((1, 1, MLA_HEADS, VT_ROWS, tm), lambda i, b: (b, i, 0, 0, 0)),
             jax.ShapeDtypeStruct((bx, n // tm, MLA_HEADS, VT_ROWS, tm), BF16))]
    out_specs = [o[0] for o in outs]
    out_shape = [o[1] for o in outs]
    return pl.pallas_call(
        _inproj_kernel,
        grid=(n // tm, bx),
        in_specs=in_specs,
        out_specs=out_specs,
        out_shape=out_shape,
        compiler_params=pltpu.CompilerParams(dimension_semantics=("arbitrary", "arbitrary"),
                                             vmem_limit_bytes=VMEM_LIMIT),
        name="inproj",
    )(x, shift, scale, gpre, ca, sa, cq, sq, ck, sk, win, gq, wuq, gkv, wk, wv, e)


def _gqa_kernel(sink_ref, q_ref, *refs):
    nwin = GQA_QB + 2
    k_refs, kx_ref = refs[:nwin], refs[nwin]
    v_refs, vx_ref = refs[nwin + 1:2 * nwin + 1], refs[2 * nwin + 1]
    o_ref = refs[2 * nwin + 2]
    n = pl.program_id(1)
    nsteps = pl.num_programs(1)
    nctx = kx_ref.shape[1] // BLOCK
    group = A_HEADS // A_KV_HEADS
    key = lax.broadcasted_iota(jnp.int32, (BLOCK, group * BLOCK), 0)
    qry = lax.broadcasted_iota(jnp.int32, (BLOCK, group * BLOCK), 1) & (BLOCK - 1)
    sts = {}
    for qb in range(GQA_QB):
        kcat = jnp.concatenate([k_refs[qb + j][0] for j in range(3)] + [kx_ref[0]], axis=0)
        for hk in range(A_KV_HEADS):
            qg = jnp.concatenate([q_ref[0, qb * BLOCK:(qb + 1) * BLOCK, hh * LANES:(hh + 1) * LANES]
                                  for hh in range(hk * group, (hk + 1) * group)], axis=0)
            sts[qb, hk] = _dot_nt(kcat, qg)
    for qb in range(GQA_QB):
        vis_prev = key >= qry
        vis_next = key <= qry
        if qb == 0:
            vis_prev = key >= qry + jnp.where(n > 0, 0, 2 * BLOCK)
        if qb == GQA_QB - 1:
            vis_next = key <= qry - jnp.where(n < nsteps - 1, 0, 2 * BLOCK)
        outs = []
        for hk in range(A_KV_HEADS):
            st = sts[qb, hk]
            pieces = [jnp.where(vis_prev, st[0:BLOCK], NEG_INF), st[BLOCK:2 * BLOCK],
                      jnp.where(vis_next, st[2 * BLOCK:3 * BLOCK], NEG_INF)]
            pieces += [st[(3 + j) * BLOCK:(4 + j) * BLOCK] for j in range(nctx)]
            sink = sink_ref[hk]
            m = jnp.maximum(functools.reduce(jnp.maximum, [jnp.max(p, axis=0, keepdims=True) for p in pieces]),
                            sink)
            ps = [jnp.exp2(p - m).astype(BF16) for p in pieces]
            vts = [v_refs[qb + j][0, 0, hk] for j in range(3)] + [vx_ref[0, j, hk] for j in range(nctx)]
            pv = functools.reduce(jnp.add, [_dot(vt, p) for vt, p in zip(vts, ps)])
            l = pv[HEAD_DIM:HEAD_DIM + 1, :] + jnp.exp2(sink - m)
            o = pv[0:HEAD_DIM, :] * (1.0 / l)
            outs += [o[:, g * BLOCK:(g + 1) * BLOCK] for g in range(group)]
        for pair in range(A_HEADS // 2):
            both = jnp.concatenate([outs[2 * pair], outs[2 * pair + 1]], axis=0)
            o_ref[0, qb * BLOCK:(qb + 1) * BLOCK, pair * LANES:(pair + 1) * LANES] = both.T.astype(BF16)


def _gqa(sinkv, qa, ka, vat, kac, vatc):
    b, s, _ = qa.shape
    nb = s // BLOCK
    c = kac.shape[1]
    tq = GQA_QB * BLOCK
    cur = lambda bb, n: (bb, n, 0)
    win = lambda j: (lambda bb, n: (bb, jnp.clip(n * GQA_QB + j - 1, 0, nb - 1), 0))
    vwin = lambda j: (lambda bb, n: (bb, jnp.clip(n * GQA_QB + j - 1, 0, nb - 1), 0, 0, 0))
    vblk = (1, 1, A_KV_HEADS, VT_ROWS, BLOCK)
    in_specs = [pl.BlockSpec(sinkv.shape, lambda bb, n: (0, 0, 0)), pl.BlockSpec((1, tq, A_HEADS * LANES), cur)]
    in_specs += [pl.BlockSpec((1, BLOCK, LANES), win(j)) for j in range(GQA_QB + 2)]
    in_specs += [pl.BlockSpec((1, c, LANES), lambda bb, n: (bb, 0, 0))]
    in_specs += [pl.BlockSpec(vblk, vwin(j)) for j in range(GQA_QB + 2)]
    in_specs += [pl.BlockSpec((1, c // BLOCK, A_KV_HEADS, VT_ROWS, BLOCK), lambda bb, n: (bb, 0, 0, 0, 0))]
    return pl.pallas_call(
        _gqa_kernel,
        grid=(b, nb // GQA_QB),
        in_specs=in_specs,
        out_specs=pl.BlockSpec((1, tq, 512), cur),
        out_shape=jax.ShapeDtypeStruct((b, s, 512), BF16),
        compiler_params=pltpu.CompilerParams(dimension_semantics=("arbitrary", "arbitrary"),
                                             vmem_limit_bytes=VMEM_LIMIT),
        name="gqa",
    )(sinkv, qa, *([ka] * (GQA_QB + 2)), kac, *([vat] * (GQA_QB + 2)), vatc)


def _mla_kernel(q_ref, k_ref, vt_ref, kc_ref, vct_ref, o_ref, m_ref, acc_ref, sa_ref, sb_ref):
    nk = k_ref.shape[1] // TK_MLA

    def scores_into(buf, k_blk):
        n = k_blk.shape[0]
        for hh in range(MLA_HEADS):
            hsl = slice(hh * LANES, (hh + 1) * LANES)
            buf[hh, 0:n, :] = _dot_nt(k_blk[:, hsl], q_ref[0, :, hsl])

    def consume(buf, vt_blk, first):
        n = vt_blk.shape[2]
        for hh in range(MLA_HEADS):
            st = buf[hh, 0:n, :]
            cmax = jnp.max(st, axis=0, keepdims=True)
            if first:
                m_new = cmax
            else:
                m_old = m_ref[hh, 0:1, :]
                m_new = jnp.maximum(m_old, cmax)
                alpha = jnp.exp2(m_old - m_new)
            p = jnp.exp2(st - m_new).astype(BF16)
            pv = _dot(vt_blk[hh], p)
            acc_ref[hh] = pv if first else alpha * acc_ref[hh] + pv
            m_ref[hh, 0:1, :] = m_new

    def k_chunk(c):
        return k_ref[0, pl.ds(pl.multiple_of(c * TK_MLA, TK_MLA), TK_MLA), :]

    scores_into(sa_ref, k_chunk(0))
    scores_into(sb_ref, k_chunk(1))
    consume(sa_ref, vt_ref[0, 0], True)

    def body(i, _):
        c = 2 * i + 1
        scores_into(sa_ref, k_chunk(c + 1))
        consume(sb_ref, vt_ref[0, c], False)
        scores_into(sb_ref, k_chunk(c + 2))
        consume(sa_ref, vt_ref[0, c + 1], False)
        return 0

    lax.fori_loop(0, (nk - 2) // 2, body, 0)
    scores_into(sa_ref, kc_ref[0])
    consume(sb_ref, vt_ref[0, nk - 1], False)
    consume(sa_ref, vct_ref[0, 0], False)

    outs = [acc_ref[hh, 0:MLA_V, :] * (1.0 / acc_ref[hh, MLA_V:MLA_V + 1, :]) for hh in range(MLA_HEADS)]
    o_ref[0] = jnp.concatenate(outs, axis=0).T.astype(BF16)


def _mla(q, k, vt, kc, vct):
    b, s, _ = q.shape
    c = kc.shape[1]
    nk = s // TK_MLA
    assert vt.shape == (b, nk, MLA_HEADS, VT_ROWS, TK_MLA) and nk % 2 == 0
    return pl.pallas_call(
        _mla_kernel,
        grid=(b, s // TQ_MLA),
        in_specs=[pl.BlockSpec((1, TQ_MLA, 1024), lambda bb, i: (bb, i, 0)),
                  pl.BlockSpec((1, s, 1024), lambda bb, i: (bb, 0, 0)),
                  pl.BlockSpec((1, nk, MLA_HEADS, VT_ROWS, TK_MLA), lambda bb, i: (bb, 0, 0, 0, 0)),
                  pl.BlockSpec((1, c, 1024), lambda bb, i: (bb, 0, 0)),
                  pl.BlockSpec((1, 1, MLA_HEADS, VT_ROWS, c), lambda bb, i: (bb, 0, 0, 0, 0))],
        out_specs=pl.BlockSpec((1, TQ_MLA, 512), lambda bb, i: (bb, i, 0)),
        out_shape=jax.ShapeDtypeStruct((b, s, 512), BF16),
        scratch_shapes=[pltpu.VMEM((MLA_HEADS, 8, TQ_MLA), F32),
                        pltpu.VMEM((MLA_HEADS, VT_ROWS, TQ_MLA), F32),
                        pltpu.VMEM((MLA_HEADS, TK_MLA, TQ_MLA), F32), pltpu.VMEM((MLA_HEADS, TK_MLA, TQ_MLA), F32)],
        compiler_params=pltpu.CompilerParams(dimension_semantics=("arbitrary", "arbitrary"),
                                             vmem_limit_bytes=VMEM_LIMIT),
        name="mla",
    )(q, k, vt, kc, vct)


def _outproj_kernel(oa_ref, ob_ref, x_ref, woa_ref, wob_ref, gpost_ref, gate_ref, gffn_ref, shift_ref, scale_ref,
                    wrh_ref, wrl_ref, br_ref, tri_ref, upper_ref,
                    x1_ref, h2_ref, ri_ref, rw_ref, cnt_ref):
    y = _dot(oa_ref[0], woa_ref[...]) + _dot(ob_ref[0], wob_ref[...])
    x1 = x_ref[0] + gate_ref[0] * _rms(y, gpost_ref[...])
    x1_ref[0] = x1
    h2 = _rms(x1, gffn_ref[...]) * (1.0 + scale_ref[0]) + shift_ref[0]
    tm = h2.shape[0]
    h2_ref[0] = h2.astype(BF16)

    h_hi, h_lo = _split_bf16(h2)
    logits = _dot(h_hi, wrh_ref[...]) + _dot(h_hi, wrl_ref[...]) + _dot(h_lo, wrh_ref[...]) + br_ref[...]
    lane = lax.broadcasted_iota(jnp.int32, (tm, LANES), 1)
    lane_f = lane.astype(F32)
    tops, idxs = [], []
    cur = logits
    for _ in range(TOP_K):
        mk = jnp.max(cur, axis=-1, keepdims=True)
        ik = jnp.min(jnp.where(cur == mk, lane_f, float(LANES)), axis=-1, keepdims=True)
        tops.append(mk)
        idxs.append(ik)
        cur = jnp.where(lane_f == ik, -jnp.inf, cur)
    es = [jnp.exp(t - tops[0]) for t in tops]
    inv = 1.0 / functools.reduce(jnp.add, es)

    onehots = [(lane_f == ik) for ik in idxs]
    onehot = functools.reduce(jnp.add, [o.astype(F32) for o in onehots])
    cnt = jnp.sum(onehot, axis=0, keepdims=True)
    prefix = _dot(tri_ref[...], onehot.astype(BF16))
    chunks = jnp.floor((cnt + (RUN_ALIGN - 1.0)) * (1.0 / RUN_ALIGN))
    run_start = _dot(jnp.broadcast_to(chunks, (8, LANES)).astype(BF16), upper_ref[...])[0:1, :] * RUN_ALIGN
    base = prefix + run_start
    ri = jnp.zeros((tm, LANES), jnp.int32)
    rw = jnp.zeros((tm, LANES), F32)
    for k in range(TOP_K):
        pos = jnp.sum(jnp.where(onehots[k], base, 0.0), axis=-1, keepdims=True).astype(jnp.int32)
        ri = jnp.where(lane == k, idxs[k].astype(jnp.int32), ri)
        ri = jnp.where(lane == TOP_K + k, pos, ri)
        rw = jnp.where(lane == k, es[k] * inv, rw)
    ri_ref[0] = ri
    rw_ref[0] = rw
    cnt_ref[0] = cnt


def _outproj(oa, ob, x, woa, wob, gpost, gate, gffn, shift, scale, wrh, wrl, br, tri, upper):
    b, s, d = x.shape
    tm = TM_OUT
    nt = s // tm
    tok = lambda bb, i: (bb, i, 0)
    vec = lambda bb, i: (bb, 0, 0)
    cst = lambda bb, i: (0, 0)
    return pl.pallas_call(
        _outproj_kernel,
        grid=(b, nt),
        in_specs=[pl.BlockSpec((1, tm, 512), tok), pl.BlockSpec((1, tm, 512), tok), pl.BlockSpec((1, tm, d), tok),
                  pl.BlockSpec((512, d), cst), pl.BlockSpec((512, d), cst), pl.BlockSpec((1, d), cst),
                  pl.BlockSpec((1, 1, d), vec), pl.BlockSpec((1, d), cst),
                  pl.BlockSpec((1, 1, d), vec), pl.BlockSpec((1, 1, d), vec),
                  pl.BlockSpec((d, LANES), cst), pl.BlockSpec((d, LANES), cst), pl.BlockSpec((1, LANES), cst),
                  pl.BlockSpec((tm, tm), cst), pl.BlockSpec((LANES, LANES), cst)],
        out_specs=[pl.BlockSpec((1, tm, d), tok), pl.BlockSpec((1, tm, d), tok),
                   pl.BlockSpec((1, tm, LANES), tok), pl.BlockSpec((1, tm, LANES), tok),
                   pl.BlockSpec((1, 1, LANES), lambda bb, i: (bb * nt + i, 0, 0))],
        out_shape=[jax.ShapeDtypeStruct((b, s, d), F32), jax.ShapeDtypeStruct((b, s, d), BF16),
                   jax.ShapeDtypeStruct((b, s, LANES), jnp.int32), jax.ShapeDtypeStruct((b, s, LANES), F32),
                   jax.ShapeDtypeStruct((b * nt, 1, LANES), F32)],
        compiler_params=pltpu.CompilerParams(dimension_semantics=("arbitrary", "arbitrary"),
                                             vmem_limit_bytes=VMEM_LIMIT),
        name="outproj",
    )(oa, ob, x, woa, wob, gpost, gate, gffn, shift, scale, wrh, wrl, br, tri, upper)


def _run_copies(gs_ref, to_ref, nr_ref, tile, make_copy):
    def per_expert(e, totals):
        j = tile * N_EXPERTS + e
        g, o, rows = gs_ref[j], to_ref[j], nr_ref[j]
        n_big = rows // BIG_COPY
        n_small = (rows - n_big * BIG_COPY) // RUN_ALIGN

        def big(c, _):
            off = c * BIG_COPY
            make_copy(pl.multiple_of(g + off, RUN_ALIGN), pl.multiple_of(o + off, RUN_ALIGN), BIG_COPY).start()
            return 0

        def small(c, _):
            off = n_big * BIG_COPY + c * RUN_ALIGN
            make_copy(pl.multiple_of(g + off, RUN_ALIGN), pl.multiple_of(o + off, RUN_ALIGN), RUN_ALIGN).start()
            return 0

        lax.fori_loop(0, n_big, big, 0)
        lax.fori_loop(0, n_small, small, 0)
        return totals[0] + n_big, totals[1] + n_small

    n_big, n_small = lax.fori_loop(0, N_EXPERTS, per_expert, (0, 0))

    def wait_big(i, _):
        make_copy(0, 0, BIG_COPY).wait()
        return 0

    def wait_small(i, _):
        make_copy(0, 0, RUN_ALIGN).wait()
        return 0

    lax.fori_loop(0, n_big, wait_big, 0)
    lax.fori_loop(0, n_small, wait_small, 0)


def _dispatch_kernel(gs_ref, to_ref, nc_ref, ri_ref, h_ref, xs_in_ref, xs_ref, sorted_ref, sem):
    del xs_in_ref
    tile = pl.program_id(0)
    tm = h_ref.shape[0]
    post = ri_ref[...].astype(F32).T
    h = h_ref[...]
    rb_rows = 256

    def fill(rb, _):
        r0 = pl.multiple_of(rb * rb_rows, rb_rows)
        row = (lax.broadcasted_iota(jnp.int32, (rb_rows, tm), 0) + r0).astype(F32)
        perm = functools.reduce(jnp.add, [(row == post[TOP_K + k:TOP_K + k + 1, :]).astype(F32)
                                          for k in range(TOP_K)]).astype(BF16)
        xr = _dot(perm, h)
        lo = lax.bitcast_convert_type(xr[:, :512], jnp.uint32)
        hi = lax.bitcast_convert_type(xr[:, 512:], jnp.uint32)
        sorted_ref[pl.ds(r0, rb_rows), :] = (lo >> 16) | (hi & jnp.uint32(0xFFFF0000))
        return 0

    lax.fori_loop(0, SORTED_ROWS // rb_rows, fill, 0)

    def make_copy(g, o, rows):
        return pltpu.make_async_copy(sorted_ref.at[pl.ds(o, rows)], xs_ref.at[pl.ds(g, rows)], sem)

    _run_copies(gs_ref, to_ref, nc_ref, tile, make_copy)


def _dispatch(gstart, toff, nch, ri, h2, xs0):
    t, d = h2.shape
    tm = TM_OUT
    return pl.pallas_call(
        _dispatch_kernel,
        grid_spec=pltpu.PrefetchScalarGridSpec(
            num_scalar_prefetch=3,
            grid=(t // tm,),
            in_specs=[pl.BlockSpec((tm, LANES), lambda i, *_: (i, 0)),
                      pl.BlockSpec((tm, d), lambda i, *_: (i, 0)),
                      pl.BlockSpec(memory_space=pl.ANY)],
            out_specs=pl.BlockSpec(memory_space=pl.ANY),
            scratch_shapes=[pltpu.VMEM((SORTED_ROWS, d // 2), jnp.uint32), pltpu.SemaphoreType.DMA(())]),
        out_shape=jax.ShapeDtypeStruct(xs0.shape, xs0.dtype),
        input_output_aliases={5: 0},
        compiler_params=pltpu.CompilerParams(dimension_semantics=("arbitrary",),
                                             vmem_limit_bytes=VMEM_LIMIT),
        name="dispatch",
    )(gstart, toff, nch, ri, h2, xs0)


def _experts_kernel(be_ref, nu_ref, xs_ref, wgu_ref, bgu_ref, wd_ref, bd_ref, ys_ref):
    i = pl.program_id(0)

    @pl.when(i < nu_ref[0])
    def _():
        w = xs_ref[...]
        x_lo = lax.bitcast_convert_type(w << 16, F32).astype(BF16)
        x_hi = lax.bitcast_convert_type(w & jnp.uint32(0xFFFF0000), F32).astype(BF16)
        gu = (_dot(x_lo, wgu_ref[0, :512, :].astype(BF16)) + _dot(x_hi, wgu_ref[0, 512:, :].astype(BF16))
              + bgu_ref[0])
        gate = jnp.minimum(gu[:, :D_FF], SWIGLU_LIMIT)
        lin = jnp.clip(gu[:, D_FF:], -SWIGLU_LIMIT, SWIGLU_LIMIT)
        act = (lin + 1.0) * (gate * jax.nn.sigmoid(SWIGLU_ALPHA * gate))
        ys_ref[...] = _dot(act.astype(BF16), wd_ref[0].astype(BF16)) + bd_ref[0]

    @pl.when(i >= nu_ref[0])
    def _():
        ys_ref[...] = jnp.zeros_like(ys_ref)


def _experts(block_e, n_used, xs, wgu, bgu, wd, bd):
    cap = xs.shape[0]
    nblk = cap // MOE_BLOCK
    return pl.pallas_call(
        _experts_kernel,
        grid_spec=pltpu.PrefetchScalarGridSpec(
            num_scalar_prefetch=2,
            grid=(nblk,),
            in_specs=[pl.BlockSpec((MOE_BLOCK, 512), lambda i, be, nu: (i, 0)),
                      pl.BlockSpec((1, D_MODEL, 2 * D_FF), lambda i, be, nu: (be[i], 0, 0)),
                      pl.BlockSpec((1, 1, 2 * D_FF), lambda i, be, nu: (be[i], 0, 0)),
                      pl.BlockSpec((1, D_FF, D_MODEL), lambda i, be, nu: (be[i], 0, 0)),
                      pl.BlockSpec((1, 1, D_MODEL), lambda i, be, nu: (be[i], 0, 0))],
            out_specs=pl.BlockSpec((MOE_BLOCK, D_MODEL), lambda i, be, nu: (i, 0))),
        out_shape=jax.ShapeDtypeStruct((cap, D_MODEL), F32),
        compiler_params=pltpu.CompilerParams(dimension_semantics=("arbitrary",),
                                             vmem_limit_bytes=VMEM_LIMIT),
        name="experts",
    )(block_e, n_used, xs, wgu, bgu, wd, bd)


def _combine_kernel(gs_ref, to_ref, nc_ref, ys_ref, ri_ref, rw_ref, x1_ref, gate_ref, g_ref, o_ref, ybuf, sem):
    nt = pl.num_programs(1)
    tile = pl.program_id(0) * nt + pl.program_id(1)
    tm = x1_ref.shape[1]

    @pl.when(tile == 0)
    def _():
        ybuf[...] = jnp.zeros_like(ybuf)

    def make_copy(g, o, rows):
        return pltpu.make_async_copy(ys_ref.at[pl.ds(g, rows)], ybuf.at[pl.ds(o, rows)], sem)

    _run_copies(gs_ref, to_ref, nc_ref, tile, make_copy)

    posf = ri_ref[0].astype(F32)
    rw = rw_ref[0]
    cb_cols = 256
    f = jnp.zeros((tm, D_MODEL), F32)
    for cb in range(SORTED_ROWS // cb_cols):
        col = (lax.broadcasted_iota(jnp.int32, (tm, cb_cols), 1) + cb * cb_cols).astype(F32)
        wp = functools.reduce(jnp.add, [jnp.where(col == posf[:, TOP_K + k:TOP_K + k + 1], rw[:, k:k + 1], 0.0)
                                        for k in range(TOP_K)])
        f = f + _dot(wp.astype(BF16), ybuf[cb * cb_cols:(cb + 1) * cb_cols, :].astype(BF16))
    o_ref[0] = x1_ref[0] + gate_ref[0] * _rms(f, g_ref[...])


def _combine(gstart, toff, nch, ys, ri, rw, x1, gate, g):
    b, s, d = x1.shape
    tm = TM_OUT
    tok = lambda bb, i, *_: (bb, i, 0)
    return pl.pallas_call(
        _combine_kernel,
        grid_spec=pltpu.PrefetchScalarGridSpec(
            num_scalar_prefetch=3,
            grid=(b, s // tm),
            in_specs=[pl.BlockSpec(memory_space=pl.ANY),
                      pl.BlockSpec((1, tm, LANES), tok), pl.BlockSpec((1, tm, LANES), tok),
                      pl.BlockSpec((1, tm, d), tok),
                      pl.BlockSpec((1, 1, d), lambda bb, i, *_: (bb, 0, 0)),
                      pl.BlockSpec((1, d), lambda bb, i, *_: (0, 0))],
            out_specs=pl.BlockSpec((1, tm, d), tok),
            scratch_shapes=[pltpu.VMEM((SORTED_ROWS, d), F32), pltpu.SemaphoreType.DMA(())]),
        out_shape=jax.ShapeDtypeStruct((b, s, d), F32),
        compiler_params=pltpu.CompilerParams(dimension_semantics=("arbitrary", "arbitrary"),
                                             vmem_limit_bytes=VMEM_LIMIT),
        name="combine",
    )(gstart, toff, nch, ys, ri, rw, x1, gate, g)


def _rope_angles(rows, rot_dim):
    row = jnp.broadcast_to(jnp.arange(rows, dtype=F32)[:, None], (rows, GRID_W)).reshape(-1)
    col = jnp.broadcast_to(jnp.arange(GRID_W, dtype=F32)[None, :], (rows, GRID_W)).reshape(-1)
    quarter = rot_dim // 4
    inv_freq = ROPE_BASE ** (-jnp.arange(quarter, dtype=F32) / quarter)
    ang = jnp.concatenate([row[:, None] * inv_freq, col[:, None] * inv_freq], axis=-1)
    return jnp.cos(ang), jnp.sin(ang)


def _rope_tables(s):
    ca, sa = _rope_angles(s // GRID_W, HEAD_DIM)
    cb, sb = _rope_angles(s // GRID_W, MLA_ROPE)
    one = lambda w: jnp.ones((s, w), F32)
    zero = lambda w: jnp.zeros((s, w), F32)
    cos_a = jnp.tile(jnp.concatenate([ca, ca], 1), (1, 2))
    sin_a = jnp.tile(jnp.concatenate([-sa, sa], 1), (1, 2))
    cos_q = jnp.concatenate([one(64), cb, cb, one(32)], 1)
    sin_q = jnp.concatenate([zero(64), -sb, sb, zero(32)], 1)
    cos_k = jnp.concatenate([cb, cb, one(96)], 1)
    sin_k = jnp.concatenate([-sb, sb, zero(96)], 1)
    return cos_a, sin_a, cos_q, sin_q, cos_k, sin_k


def _identity_tables(n):
    one, zero = jnp.ones((n, LANES), F32), jnp.zeros((n, LANES), F32)
    return one, zero, one, zero, one, zero


def _inproj_weights(w_in, g_q_a, w_uq, g_kv_a, w_ukv):
    d = w_in.shape[0]
    wq, wk, wv = w_in[:, :512], w_in[:, 512:640], w_in[:, 640:768]
    wcq, wckv, wkr = w_in[:, 768:1152], w_in[:, 1152:1408], w_in[:, 1408:1440]
    z64 = jnp.zeros((d, 64), F32)
    group = A_HEADS // A_KV_HEADS
    wq_ext = []
    for hh in range(A_HEADS):
        wq_h = wq[:, hh * HEAD_DIM:(hh + 1) * HEAD_DIM]
        wq_ext += [wq_h, z64] if hh // group == 0 else [z64, wq_h]
    wkr_p = jnp.concatenate([wkr, jnp.zeros((d, 96), F32)], 1)
    win = jnp.concatenate(wq_ext + [wk, wv, wcq, wckv, wkr_p], 1).astype(BF16)
    wuq = jnp.pad(w_uq.reshape(MLA_Q_LORA, MLA_HEADS, MLA_NOPE + MLA_ROPE), ((0, 0), (0, 0), (0, 32)))
    wuq = wuq.reshape(MLA_Q_LORA, MLA_HEADS * LANES).astype(BF16)
    wukv = w_ukv.reshape(MLA_KV_LORA, MLA_HEADS, MLA_NOPE + MLA_V)
    wkk = jnp.pad(wukv[:, :, :MLA_NOPE], ((0, 0), (0, 0), (0, 64))).reshape(MLA_KV_LORA, MLA_HEADS * LANES)
    wvv = wukv[:, :, MLA_NOPE:].reshape(MLA_KV_LORA, MLA_HEADS * MLA_V)
    j = jnp.arange(LANES)[:, None]
    cidx = jnp.arange(MLA_HEADS * LANES)[None, :]
    e = ((j < MLA_ROPE) & ((cidx % LANES) == MLA_NOPE + j)).astype(BF16)
    return (win, g_q_a.reshape(1, -1), wuq, g_kv_a.reshape(1, -1), wkk.astype(BF16), wvv.T.astype(BF16), e)


def kernel(x, c, ctx, c_ctx, w_ada, b_ada, g_mix_pre, g_mix_post, w_in, sink, g_q_a, w_uq, g_kv_a, w_ukv, w_o,
           g_ffn_pre, g_ffn_post, w_router, b_router, w_gate_up, b_gate_up, w_down, b_down):
    b, s, d = x.shape
    cl = ctx.shape[1]
    t = b * s

    cc = jnp.zeros((8, d), F32).at[:b].set(c).at[b].set(c_ctx)
    mod_all = _adaln(cc, w_ada[0], b_ada[0].reshape(1, -1))
    mod = [mod_all[:b, i * d:(i + 1) * d].reshape(b, 1, d) for i in range(6)]
    mod_c = [jnp.broadcast_to(mod_all[b, i * d:(i + 1) * d].reshape(1, 1, d), (b, 1, d)) for i in range(2)]

    wts = _inproj_weights(w_in[0], g_q_a[0], w_uq[0], g_kv_a[0], w_ukv[0])
    gpre = g_mix_pre[0].reshape(1, d)
    qa, ka, vat, q, k, vt = _inproj(x, mod[0], mod[1], gpre, _rope_tables(s), wts, TM_IN)
    _, kac, vatc, _, kc, vct = _inproj(ctx, mod_c[0], mod_c[1], gpre, _identity_tables(cl), wts, cl)

    sinkv = jnp.repeat(sink[0] * LOG2_E, BLOCK).reshape(A_KV_HEADS, 1, -1)
    out_a = _gqa(sinkv, qa, ka, vat, kac, vatc)
    out_b = _mla(q, k, vt, kc, vct)

    wo = w_o[0].astype(BF16)
    wr = jnp.pad(w_router[0], ((0, 0), (0, LANES - N_EXPERTS)))
    wr_hi, wr_lo = _split_bf16(wr)
    br = jnp.concatenate([b_router[0], jnp.full((LANES - N_EXPERTS,), NEG_INF, F32)]).reshape(1, LANES)
    ii = jnp.arange(TM_OUT)
    tri = (ii[:, None] > ii[None, :]).astype(BF16)
    jj = jnp.arange(LANES)
    upper = (jj[:, None] < jj[None, :]).astype(BF16)
    x1, h2, ri, rw, cnt = _outproj(out_a, out_b, x, wo[:512], wo[512:], g_mix_post[0].reshape(1, d), mod[2],
                                   g_ffn_pre[0].reshape(1, d), mod[3], mod[4], wr_hi, wr_lo, br, tri, upper)

    cnt = cnt[:, 0, :N_EXPERTS].astype(jnp.int32)
    nt = cnt.shape[0]
    rows = (cnt + RUN_ALIGN - 1) // RUN_ALIGN * RUN_ALIGN
    tot = jnp.sum(rows, axis=0)
    carry = jnp.cumsum(rows, axis=0) - rows
    padded = (tot + MOE_BLOCK - 1) // MOE_BLOCK * MOE_BLOCK
    pad_end = jnp.cumsum(padded)
    pad_start = pad_end - padded
    nblk = -(-(t * TOP_K + nt * N_EXPERTS * (RUN_ALIGN - 1)) // MOE_BLOCK) + N_EXPERTS
    cap = nblk * MOE_BLOCK
    blk_row = jnp.arange(nblk, dtype=jnp.int32) * MOE_BLOCK
    block_e = jnp.minimum(jnp.sum((pad_end[None, :] <= blk_row[:, None]).astype(jnp.int32), axis=1), N_EXPERTS - 1)
    n_used = (pad_end[-1] // MOE_BLOCK).astype(jnp.int32).reshape(1)
    toff = (jnp.cumsum(rows, axis=1) - rows).reshape(-1).astype(jnp.int32)
    gstart = (pad_start[None, :] + carry).reshape(-1).astype(jnp.int32)
    nrows = rows.reshape(-1).astype(jnp.int32)

    xs = _dispatch(gstart, toff, nrows, ri.reshape(t, LANES), h2.reshape(t, d),
                   jnp.zeros((cap, d // 2), jnp.uint32))
    ys = _experts(block_e.astype(jnp.int32), n_used, xs, w_gate_up[0], b_gate_up[0].reshape(N_EXPERTS, 1, -1),
                  w_down[0], b_down[0].reshape(N_EXPERTS, 1, -1))
    return _combine(gstart, toff, nrows, ys, ri, rw, x1, mod[5], g_ffn_post[0].reshape(1, d))
```

```python
import functools

import jax
import jax.numpy as jnp
from jax import lax
from jax.experimental import pallas as pl
from jax.experimental.pallas import tpu as pltpu

F32 = jnp.float32
BF16 = jnp.bfloat16

D_MODEL = 1024
GRID_W = 64
HEAD_DIM = 64
A_HEADS = 8
A_KV_HEADS = 2
BLOCK = 128
A_SCALE = HEAD_DIM ** -0.5
MLA_HEADS = 8
MLA_NOPE = 64
MLA_ROPE = 32
MLA_V = 64
MLA_Q_LORA = 384
MLA_KV_LORA = 256
MLA_SCALE = (MLA_NOPE + MLA_ROPE) ** -0.5
LOG2_E = 1.4426950408889634
N_EXPERTS = 32
TOP_K = 4
D_FF = 1024
SWIGLU_LIMIT = 7.0
SWIGLU_ALPHA = 1.702
ROPE_BASE = 10000.0
EPS = 1e-6
NEG_INF = -1e30

LANES = 128
VMEM_LIMIT = 56 * 1024 * 1024

TM_IN = 512
GQA_QB = 4
TQ_MLA = 256
TK_MLA = 512
VT_ROWS = 80
TM_OUT = 512
MOE_BLOCK = 256
RUN_ALIGN = 8
BIG_COPY = 32
SORTED_ROWS = TM_OUT * TOP_K + N_EXPERTS * RUN_ALIGN


def _dot(a, b):
    return jnp.dot(a, b, preferred_element_type=F32)


def _dot_nt(a, b):
    return lax.dot_general(a, b, (((1,), (1,)), ((), ())), preferred_element_type=F32)


def _split_bf16(x):
    hi = x.astype(BF16)
    lo = (x - hi.astype(F32)).astype(BF16)
    return hi, lo


def _rms(x, g):
    return x * lax.rsqrt(jnp.mean(x * x, axis=-1, keepdims=True) + EPS) * g


def _rope(x, cos, sin_signed, first_half, half):
    n = x.shape[-1]
    partner = jnp.where(first_half, pltpu.roll(x, n - half, 1), pltpu.roll(x, half, 1))
    return x * cos + partner * sin_signed


def _adaln_kernel(c_ref, w_ref, b_ref, o_ref):
    c = c_ref[...]
    s = c * jax.nn.sigmoid(c)
    s_hi, s_lo = _split_bf16(s)
    w_hi, w_lo = _split_bf16(w_ref[...])
    o_ref[...] = _dot(s_hi, w_hi) + _dot(s_hi, w_lo) + _dot(s_lo, w_hi) + b_ref[...]


def _adaln(cc, w, b):
    n = w.shape[1]
    tn = 1024
    return pl.pallas_call(
        _adaln_kernel,
        grid=(n // tn,),
        in_specs=[pl.BlockSpec((8, D_MODEL), lambda j: (0, 0)),
                  pl.BlockSpec((D_MODEL, tn), lambda j: (0, j)),
                  pl.BlockSpec((1, tn), lambda j: (0, j))],
        out_specs=pl.BlockSpec((8, tn), lambda j: (0, j)),
        out_shape=jax.ShapeDtypeStruct((8, n), F32),
        compiler_params=pltpu.CompilerParams(dimension_semantics=("arbitrary",),
                                             vmem_limit_bytes=VMEM_LIMIT),
        name="adaln",
    )(cc, w, b)


def _inproj_kernel(x_ref, shift_ref, scale_ref, gpre_ref, ca_ref, sa_ref, cq_ref, sq_ref, ck_ref, sk_ref,
                   win_ref, gq_ref, wuq_ref, gkv_ref, wk_ref, wv_ref, e_ref,
                   qa_ref, ka_ref, vat_ref, q_ref, k_ref, vt_ref):
    x = x_ref[0]
    tm = x.shape[0]
    h = _rms(x, gpre_ref[...]) * (1.0 + scale_ref[0]) + shift_ref[0]
    p = _dot(h.astype(BF16), win_ref[...])
    lane = lax.broadcasted_iota(jnp.int32, (tm, LANES), 1)

    first_a = (lane & 32) == 0
    ca, sa = ca_ref[...], sa_ref[...]
    for hh in range(A_HEADS):
        sl = slice(hh * LANES, (hh + 1) * LANES)
        qa_ref[0, :, sl] = (_rope(p[:, sl], ca, sa, first_a, 32) * (A_SCALE * LOG2_E)).astype(BF16)
    ka_ref[0] = _rope(p[:, 1024:1152], ca, sa, first_a, 32).astype(BF16)
    vat = p[:, 1152:1280].T
    ones_a = (lax.broadcasted_iota(jnp.int32, (VT_ROWS - HEAD_DIM, BLOCK), 0) == 0).astype(BF16)
    for j in range(tm // BLOCK):
        for hk in range(A_KV_HEADS):
            vat_ref[0, j, hk, 0:HEAD_DIM, :] = vat[hk * HEAD_DIM:(hk + 1) * HEAD_DIM,
                                                   j * BLOCK:(j + 1) * BLOCK].astype(BF16)
            vat_ref[0, j, hk, HEAD_DIM:VT_ROWS, :] = ones_a

    cqn = _rms(p[:, 1280:1664], gq_ref[...]).astype(BF16)
    qf = _dot(cqn, wuq_ref[...])
    first_q = (lane >= 64) & (lane < 80)
    cq, sq = cq_ref[...], sq_ref[...]
    for hh in range(MLA_HEADS):
        sl = slice(hh * LANES, (hh + 1) * LANES)
        q_ref[0, :, sl] = (_rope(qf[:, sl], cq, sq, first_q, 16) * (MLA_SCALE * LOG2_E)).astype(BF16)

    ckvn = _rms(p[:, 1664:1920], gkv_ref[...]).astype(BF16)
    kr = _rope(p[:, 1920:2048], ck_ref[...], sk_ref[...], lane < 16, 16)
    k_ref[0] = (_dot(ckvn, wk_ref[...]) + _dot(kr.astype(BF16), e_ref[...])).astype(BF16)
    vt = _dot_nt(wv_ref[...], ckvn).astype(BF16)
    ones_row = (lax.broadcasted_iota(jnp.int32, (VT_ROWS - MLA_V, tm), 0) == 0).astype(BF16)
    for hh in range(MLA_HEADS):
        vt_ref[0, 0, hh, 0:MLA_V, :] = vt[hh * MLA_V:(hh + 1) * MLA_V, :]
        vt_ref[0, 0, hh, MLA_V:VT_ROWS, :] = ones_row


def _inproj(x, shift, scale, gpre, tabs, wts, tm):
    bx, n, d = x.shape
    ca, sa, cq, sq, ck, sk = tabs
    win, gq, wuq, gkv, wk, wv, e = wts
    tok = lambda i, b: (b, i, 0)
    vec = lambda i, b: (b, 0, 0)
    tab = lambda i, b: (i, 0)
    cst = lambda i, b: (0, 0)
    in_specs = [pl.BlockSpec((1, tm, d), tok), pl.BlockSpec((1, 1, d), vec), pl.BlockSpec((1, 1, d), vec),
                pl.BlockSpec((1, d), cst)]
    in_specs += [pl.BlockSpec((tm, LANES), tab)] * 6
    in_specs += [pl.BlockSpec(w.shape, cst) for w in wts]
    row_out = lambda w: (pl.BlockSpec((1, tm, w), tok), jax.ShapeDtypeStruct((bx, n, w), BF16))
    outs = [row_out(A_HEADS * LANES), row_out(LANES),
            (pl.BlockSpec((1, tm // BLOCK, A_KV_HEADS, VT_ROWS, BLOCK), lambda i, b: (b, i, 0, 0, 0)),
             jax.ShapeDtypeStruct((bx, n // BLOCK, A_KV_HEADS, VT_ROWS, BLOCK), BF16)),
            row_out(MLA_HEADS * LANES), row_out(MLA_HEADS * LANES),
            (pl.BlockSpec((1, 1, MLA_HEADS, VT_ROWS, tm), lambda i, b: (b, i, 0, 0, 0)),
             jax.ShapeDtypeStruct((bx, n // tm, MLA_HEADS, VT_ROWS, tm), BF16))]
    out_specs = [o[0] for o in outs]
    out_shape = [o[1] for o in outs]
    return pl.pallas_call(
        _inproj_kernel,
        grid=(n // tm, bx),
        in_specs=in_specs,
        out_specs=out_specs,
        out_shape=out_shape,
        compiler_params=pltpu.CompilerParams(dimension_semantics=("arbitrary", "arbitrary"),
                                             vmem_limit_bytes=VMEM_LIMIT),
        name="inproj",
    )(x, shift, scale, gpre, ca, sa, cq, sq, ck, sk, win, gq, wuq, gkv, wk, wv, e)


def _gqa_kernel(sink_ref, q_ref, *refs):
    nwin = GQA_QB + 2
    k_refs, kx_ref = refs[:nwin], refs[nwin]
    v_refs, vx_ref = refs[nwin + 1:2 * nwin + 1], refs[2 * nwin + 1]
    o_ref = refs[2 * nwin + 2]
    n = pl.program_id(1)
    nsteps = pl.num_programs(1)
    nctx = kx_ref.shape[1] // BLOCK
    group = A_HEADS // A_KV_HEADS
    key = lax.broadcasted_iota(jnp.int32, (BLOCK, group * BLOCK), 0)
    qry = lax.broadcasted_iota(jnp.int32, (BLOCK, group * BLOCK), 1) & (BLOCK - 1)
    sts = {}
    for qb in range(GQA_QB):
        kcat = jnp.concatenate([k_refs[qb + j][0] for j in range(3)] + [kx_ref[0]], axis=0)
        for hk in range(A_KV_HEADS):
            qg = jnp.concatenate([q_ref[0, qb * BLOCK:(qb + 1) * BLOCK, hh * LANES:(hh + 1) * LANES]
                                  for hh in range(hk * group, (hk + 1) * group)], axis=0)
            sts[qb, hk] = _dot_nt(kcat, qg)
    for qb in range(GQA_QB):
        vis_prev = key >= qry
        vis_next = key <= qry
        if qb == 0:
            vis_prev = key >= qry + jnp.where(n > 0, 0, 2 * BLOCK)
        if qb == GQA_QB - 1:
            vis_next = key <= qry - jnp.where(n < nsteps - 1, 0, 2 * BLOCK)
        outs = []
        for hk in range(A_KV_HEADS):
            st = sts[qb, hk]
            pieces = [jnp.where(vis_prev, st[0:BLOCK], NEG_INF), st[BLOCK:2 * BLOCK],
                      jnp.where(vis_next, st[2 * BLOCK:3 * BLOCK], NEG_INF)]
            pieces += [st[(3 + j) * BLOCK:(4 + j) * BLOCK] for j in range(nctx)]
            sink = sink_ref[hk]
            m = jnp.maximum(functools.reduce(jnp.maximum, [jnp.max(p, axis=0, keepdims=True) for p in pieces]),
                            sink)
            ps = [jnp.exp2(p - m).astype(BF16) for p in pieces]
            vts = [v_refs[qb + j][0, 0, hk] for j in range(3)] + [vx_ref[0, j, hk] for j in range(nctx)]
            pv = functools.reduce(jnp.add, [_dot(vt, p) for vt, p in zip(vts, ps)])
            l = pv[HEAD_DIM:HEAD_DIM + 1, :] + jnp.exp2(sink - m)
            o = pv[0:HEAD_DIM, :] * (1.0 / l)
            outs += [o[:, g * BLOCK:(g + 1) * BLOCK] for g in range(group)]
        for pair in range(A_HEADS // 2):
            both = jnp.concatenate([outs[2 * pair], outs[2 * pair + 1]], axis=0)
            o_ref[0, qb * BLOCK:(qb + 1) * BLOCK, pair * LANES:(pair + 1) * LANES] = both.T.astype(BF16)


def _gqa(sinkv, qa, ka, vat, kac, vatc):
    b, s, _ = qa.shape
    nb = s // BLOCK
    c = kac.shape[1]
    tq = GQA_QB * BLOCK
    cur = lambda bb, n: (bb, n, 0)
    win = lambda j: (lambda bb, n: (bb, jnp.clip(n * GQA_QB + j - 1, 0, nb - 1), 0))
    vwin = lambda j: (lambda bb, n: (bb, jnp.clip(n * GQA_QB + j - 1, 0, nb - 1), 0, 0, 0))
    vblk = (1, 1, A_KV_HEADS, VT_ROWS, BLOCK)
    in_specs = [pl.BlockSpec(sinkv.shape, lambda bb, n: (0, 0, 0)), pl.BlockSpec((1, tq, A_HEADS * LANES), cur)]
    in_specs += [pl.BlockSpec((1, BLOCK, LANES), win(j)) for j in range(GQA_QB + 2)]
    in_specs += [pl.BlockSpec((1, c, LANES), lambda bb, n: (bb, 0, 0))]
    in_specs += [pl.BlockSpec(vblk, vwin(j)) for j in range(GQA_QB + 2)]
    in_specs += [pl.BlockSpec((1, c // BLOCK, A_KV_HEADS, VT_ROWS, BLOCK), lambda bb, n: (bb, 0, 0, 0, 0))]
    return pl.pallas_call(
        _gqa_kernel,
        grid=(b, nb // GQA_QB),
        in_specs=in_specs,
        out_specs=pl.BlockSpec((1, tq, 512), cur),
        out_shape=jax.ShapeDtypeStruct((b, s, 512), BF16),
        compiler_params=pltpu.CompilerParams(dimension_semantics=("arbitrary", "arbitrary"),
                                             vmem_limit_bytes=VMEM_LIMIT),
        name="gqa",
    )(sinkv, qa, *([ka] * (GQA_QB + 2)), kac, *([vat] * (GQA_QB + 2)), vatc)


def _mla_kernel(q_ref, k_ref, vt_ref, kc_ref, vct_ref, o_ref, m_ref, acc_ref, sa_ref, sb_ref):
    nk = k_ref.shape[1] // TK_MLA

    def scores_into(buf, k_blk):
        n = k_blk.shape[0]
        for hh in range(MLA_HEADS):
            hsl = slice(hh * LANES, (hh + 1) * LANES)
            buf[hh, 0:n, :] = _dot_nt(k_blk[:, hsl], q_ref[0, :, hsl])

    def consume(buf, vt_blk, first):
        n = vt_blk.shape[2]
        for hh in range(MLA_HEADS):
            st = buf[hh, 0:n, :]
            cmax = jnp.max(st, axis=0, keepdims=True)
            if first:
                m_new = cmax
            else:
                m_old = m_ref[hh, 0:1, :]
                m_new = jnp.maximum(m_old, cmax)
                alpha = jnp.exp2(m_old - m_new)
            p = jnp.exp2(st - m_new).astype(BF16)
            pv = _dot(vt_blk[hh], p)
            acc_ref[hh] = pv if first else alpha * acc_ref[hh] + pv
            m_ref[hh, 0:1, :] = m_new

    def k_chunk(c):
        return k_ref[0, pl.ds(pl.multiple_of(c * TK_MLA, TK_MLA), TK_MLA), :]

    scores_into(sa_ref, k_chunk(0))
    scores_into(sb_ref, k_chunk(1))
    consume(sa_ref, vt_ref[0, 0], True)

    def body(i, _):
        c = 2 * i + 1
        scores_into(sa_ref, k_chunk(c + 1))
        consume(sb_ref, vt_ref[0, c], False)
        scores_into(sb_ref, k_chunk(c + 2))
        consume(sa_ref, vt_ref[0, c + 1], False)
        return 0

    lax.fori_loop(0, (nk - 2) // 2, body, 0)
    scores_into(sa_ref, kc_ref[0])
    consume(sb_ref, vt_ref[0, nk - 1], False)
    consume(sa_ref, vct_ref[0, 0], False)

    outs = [acc_ref[hh, 0:MLA_V, :] * (1.0 / acc_ref[hh, MLA_V:MLA_V + 1, :]) for hh in range(MLA_HEADS)]
    o_ref[0] = jnp.concatenate(outs, axis=0).T.astype(BF16)


def _mla(q, k, vt, kc, vct):
    b, s, _ = q.shape
    c = kc.shape[1]
    nk = s // TK_MLA
    assert vt.shape == (b, nk, MLA_HEADS, VT_ROWS, TK_MLA) and nk % 2 == 0
    return pl.pallas_call(
        _mla_kernel,
        grid=(b, s // TQ_MLA),
        in_specs=[pl.BlockSpec((1, TQ_MLA, 1024), lambda bb, i: (bb, i, 0)),
                  pl.BlockSpec((1, s, 1024), lambda bb, i: (bb, 0, 0)),
                  pl.BlockSpec((1, nk, MLA_HEADS, VT_ROWS, TK_MLA), lambda bb, i: (bb, 0, 0, 0, 0)),
                  pl.BlockSpec((1, c, 1024), lambda bb, i: (bb, 0, 0)),
                  pl.BlockSpec((1, 1, MLA_HEADS, VT_ROWS, c), lambda bb, i: (bb, 0, 0, 0, 0))],
        out_specs=pl.BlockSpec((1, TQ_MLA, 512), lambda bb, i: (bb, i, 0)),
        out_shape=jax.ShapeDtypeStruct((b, s, 512), BF16),
        scratch_shapes=[pltpu.VMEM((MLA_HEADS, 8, TQ_MLA), F32),
                        pltpu.VMEM((MLA_HEADS, VT_ROWS, TQ_MLA), F32),
                        pltpu.VMEM((MLA_HEADS, TK_MLA, TQ_MLA), F32), pltpu.VMEM((MLA_HEADS, TK_MLA, TQ_MLA), F32)],
        compiler_params=pltpu.CompilerParams(dimension_semantics=("arbitrary", "arbitrary"),
                                             vmem_limit_bytes=VMEM_LIMIT),
        name="mla",
    )(q, k, vt, kc, vct)


def _outproj_kernel(oa_ref, ob_ref, x_ref, woa_ref, wob_ref, gpost_ref, gate_ref, gffn_ref, shift_ref, scale_ref,
                    wrh_ref, wrl_ref, br_ref, tri_ref, upper_ref,
                    x1_ref, h2_ref, ri_ref, rw_ref, cnt_ref):
    y = _dot(oa_ref[0], woa_ref[...]) + _dot(ob_ref[0], wob_ref[...])
    x1 = x_ref[0] + gate_ref[0] * _rms(y, gpost_ref[...])
    x1_ref[0] = x1
    h2 = _rms(x1, gffn_ref[...]) * (1.0 + scale_ref[0]) + shift_ref[0]
    tm = h2.shape[0]
    h2_ref[0] = h2.astype(BF16)

    h_hi, h_lo = _split_bf16(h2)
    logits = _dot(h_hi, wrh_ref[...]) + _dot(h_hi, wrl_ref[...]) + _dot(h_lo, wrh_ref[...]) + br_ref[...]
    lane = lax.broadcasted_iota(jnp.int32, (tm, LANES), 1)
    lane_f = lane.astype(F32)
    tops, idxs = [], []
    cur = logits
    for _ in range(TOP_K):
        mk = jnp.max(cur, axis=-1, keepdims=True)
        ik = jnp.min(jnp.where(cur == mk, lane_f, float(LANES)), axis=-1, keepdims=True)
        tops.append(mk)
        idxs.append(ik)
        cur = jnp.where(lane_f == ik, -jnp.inf, cur)
    es = [jnp.exp(t - tops[0]) for t in tops]
    inv = 1.0 / functools.reduce(jnp.add, es)

    onehots = [(lane_f == ik) for ik in idxs]
    onehot = functools.reduce(jnp.add, [o.astype(F32) for o in onehots])
    cnt = jnp.sum(onehot, axis=0, keepdims=True)
    prefix = _dot(tri_ref[...], onehot.astype(BF16))
    chunks = jnp.floor((cnt + (RUN_ALIGN - 1.0)) * (1.0 / RUN_ALIGN))
    run_start = _dot(jnp.broadcast_to(chunks, (8, LANES)).astype(BF16), upper_ref[...])[0:1, :] * RUN_ALIGN
    base = prefix + run_start
    ri = jnp.zeros((tm, LANES), jnp.int32)
    rw = jnp.zeros((tm, LANES), F32)
    for k in range(TOP_K):
        pos = jnp.sum(jnp.where(onehots[k], base, 0.0), axis=-1, keepdims=True).astype(jnp.int32)
        ri = jnp.where(lane == k, idxs[k].astype(jnp.int32), ri)
        ri = jnp.where(lane == TOP_K + k, pos, ri)
        rw = jnp.where(lane == k, es[k] * inv, rw)
    ri_ref[0] = ri
    rw_ref[0] = rw
    cnt_ref[0] = cnt


def _outproj(oa, ob, x, woa, wob, gpost, gate, gffn, shift, scale, wrh, wrl, br, tri, upper):
    b, s, d = x.shape
    tm = TM_OUT
    nt = s // tm
    tok = lambda bb, i: (bb, i, 0)
    vec = lambda bb, i: (bb, 0, 0)
    cst = lambda bb, i: (0, 0)
    return pl.pallas_call(
        _outproj_kernel,
        grid=(b, nt),
        in_specs=[pl.BlockSpec((1, tm, 512), tok), pl.BlockSpec((1, tm, 512), tok), pl.BlockSpec((1, tm, d), tok),
                  pl.BlockSpec((512, d), cst), pl.BlockSpec((512, d), cst), pl.BlockSpec((1, d), cst),
                  pl.BlockSpec((1, 1, d), vec), pl.BlockSpec((1, d), cst),
                  pl.BlockSpec((1, 1, d), vec), pl.BlockSpec((1, 1, d), vec),
                  pl.BlockSpec((d, LANES), cst), pl.BlockSpec((d, LANES), cst), pl.BlockSpec((1, LANES), cst),
                  pl.BlockSpec((tm, tm), cst), pl.BlockSpec((LANES, LANES), cst)],
        out_specs=[pl.BlockSpec((1, tm, d), tok), pl.BlockSpec((1, tm, d), tok),
                   pl.BlockSpec((1, tm, LANES), tok), pl.BlockSpec((1, tm, LANES), tok),
                   pl.BlockSpec((1, 1, LANES), lambda bb, i: (bb * nt + i, 0, 0))],
        out_shape=[jax.ShapeDtypeStruct((b, s, d), F32), jax.ShapeDtypeStruct((b, s, d), BF16),
                   jax.ShapeDtypeStruct((b, s, LANES), jnp.int32), jax.ShapeDtypeStruct((b, s, LANES), F32),
                   jax.ShapeDtypeStruct((b * nt, 1, LANES), F32)],
        compiler_params=pltpu.CompilerParams(dimension_semantics=("arbitrary", "arbitrary"),
                                             vmem_limit_bytes=VMEM_LIMIT),
        name="outproj",
    )(oa, ob, x, woa, wob, gpost, gate, gffn, shift, scale, wrh, wrl, br, tri, upper)


def _start_pieces(g, o, rows, make_copy):
    n_big = rows // BIG_COPY
    n_small = (rows - n_big * BIG_COPY) // RUN_ALIGN

    def big(c, _):
        off = c * BIG_COPY
        make_copy(pl.multiple_of(g + off, RUN_ALIGN), pl.multiple_of(o + off, RUN_ALIGN), BIG_COPY).start()
        return 0

    def small(c, _):
        off = n_big * BIG_COPY + c * RUN_ALIGN
        make_copy(pl.multiple_of(g + off, RUN_ALIGN), pl.multiple_of(o + off, RUN_ALIGN), RUN_ALIGN).start()
        return 0

    lax.fori_loop(0, n_big, big, 0)
    lax.fori_loop(0, n_small, small, 0)
    return n_big, n_small


def _wait_pieces(n_big, n_small, make_copy):
    def wait_big(i, _):
        make_copy(0, 0, BIG_COPY).wait()
        return 0

    def wait_small(i, _):
        make_copy(0, 0, RUN_ALIGN).wait()
        return 0

    lax.fori_loop(0, n_big, wait_big, 0)
    lax.fori_loop(0, n_small, wait_small, 0)


def _start_run_copies(gs_ref, to_ref, nr_ref, tile, make_copy):
    def per_expert(e, totals):
        j = tile * N_EXPERTS + e
        n_big, n_small = _start_pieces(gs_ref[j], to_ref[j], nr_ref[j], make_copy)
        return totals[0] + n_big, totals[1] + n_small

    return lax.fori_loop(0, N_EXPERTS, per_expert, (0, 0))


def _dispatch_kernel(gs_ref, to_ref, nr_ref, zs_ref, zr_ref, ri_ref, h_ref, xs_ref, sorted_ref, zero_ref, cnt_ref,
                     sem):
    tile = pl.program_id(0)
    last = pl.num_programs(0) - 1
    slot = tile % 2
    tm = h_ref.shape[0]
    post = ri_ref[...].astype(F32).T
    h = h_ref[...]
    rb_rows = 256

    def fill(rb, _):
        r0 = pl.multiple_of(rb * rb_rows, rb_rows)
        row = (lax.broadcasted_iota(jnp.int32, (rb_rows, tm), 0) + r0).astype(F32)
        perm = functools.reduce(jnp.add, [(row == post[TOP_K + k:TOP_K + k + 1, :]).astype(F32)
                                          for k in range(TOP_K)]).astype(BF16)
        xr = _dot(perm, h)
        lo = lax.bitcast_convert_type(xr[:, :512], jnp.uint32)
        hi = lax.bitcast_convert_type(xr[:, 512:], jnp.uint32)
        sorted_ref[slot, pl.ds(r0, rb_rows), :] = (lo >> 16) | (hi & jnp.uint32(0xFFFF0000))
        return 0

    lax.fori_loop(0, SORTED_ROWS // rb_rows, fill, 0)

    def run_copy(sl):
        def make_copy(g, o, rows):
            return pltpu.make_async_copy(sorted_ref.at[sl, pl.ds(o, rows)], xs_ref.at[pl.ds(g, rows)], sem.at[sl])
        return make_copy

    n_big, n_small = _start_run_copies(gs_ref, to_ref, nr_ref, tile, run_copy(slot))

    @pl.when(tile > 0)
    def _():
        _wait_pieces(cnt_ref[0], cnt_ref[1], run_copy(1 - slot))

    cnt_ref[0] = n_big
    cnt_ref[1] = n_small

    @pl.when(tile == last)
    def _():
        _wait_pieces(n_big, n_small, run_copy(slot))
        zero_ref[...] = jnp.zeros_like(zero_ref)

        def zero_copy(g, o, rows):
            return pltpu.make_async_copy(zero_ref.at[pl.ds(0, rows)], xs_ref.at[pl.ds(g, rows)], sem.at[2])

        def per_expert(e, totals):
            nb, ns = _start_pieces(zs_ref[e], 0, zr_ref[e], zero_copy)
            return totals[0] + nb, totals[1] + ns

        zb, zs = lax.fori_loop(0, zs_ref.shape[0], per_expert, (0, 0))
        _wait_pieces(zb, zs, zero_copy)


def _dispatch(gstart, toff, nrows, zstart, zrows, ri, h2, cap):
    t, d = h2.shape
    tm = TM_OUT
    return pl.pallas_call(
        _dispatch_kernel,
        grid_spec=pltpu.PrefetchScalarGridSpec(
            num_scalar_prefetch=5,
            grid=(t // tm,),
            in_specs=[pl.BlockSpec((tm, LANES), lambda i, *_: (i, 0)),
                      pl.BlockSpec((tm, d), lambda i, *_: (i, 0))],
            out_specs=pl.BlockSpec(memory_space=pl.ANY),
            scratch_shapes=[pltpu.VMEM((2, SORTED_ROWS, d // 2), jnp.uint32),
                            pltpu.VMEM((BIG_COPY, d // 2), jnp.uint32),
                            pltpu.SMEM((2,), jnp.int32),
                            pltpu.SemaphoreType.DMA((3,))]),
        out_shape=jax.ShapeDtypeStruct((cap, d // 2), jnp.uint32),
        compiler_params=pltpu.CompilerParams(dimension_semantics=("arbitrary",),
                                             vmem_limit_bytes=VMEM_LIMIT),
        name="dispatch",
    )(gstart, toff, nrows, zstart, zrows, ri, h2)


def _pack_bf16_pairs(x):
    n = x.shape[1] // 2
    bits = lax.bitcast_convert_type(x.astype(BF16).astype(F32), jnp.uint32)
    return (bits[:, :n] >> 16) | (bits[:, n:] & jnp.uint32(0xFFFF0000))


def _unpack_bf16_pairs(w):
    lo = lax.bitcast_convert_type(w << 16, F32).astype(BF16)
    hi = lax.bitcast_convert_type(w & jnp.uint32(0xFFFF0000), F32).astype(BF16)
    return lo, hi


def _experts_kernel(be_ref, nu_ref, slot_ref, nxt_ref, xs_ref, wgu_hbm, bgu_ref, wd_hbm, bd_ref, ys_ref,
                    wgu_f, wd_f, wgu_b, wd_b, sem):
    i = pl.program_id(0)
    e = be_ref[i]
    slot = slot_ref[i]
    used = i < nu_ref[0]
    run_start = used & ((i == 0) | (be_ref[jnp.maximum(i - 1, 0)] != e))

    def weight_copies(ex, sl):
        return (pltpu.make_async_copy(wgu_hbm.at[ex], wgu_f.at[sl], sem.at[0, sl]),
                pltpu.make_async_copy(wd_hbm.at[ex], wd_f.at[sl], sem.at[1, sl]))

    @pl.when(i == 0)
    def _():
        for cp in weight_copies(e, slot):
            cp.start()

    @pl.when(run_start)
    def _():
        for cp in weight_copies(e, slot):
            cp.wait()
        nxt = nxt_ref[i]

        @pl.when(nxt >= 0)
        def _():
            for cp in weight_copies(nxt, 1 - slot):
                cp.start()

        wgu_b[...] = wgu_f[slot].astype(BF16)
        wd_b[...] = wd_f[slot].astype(BF16)

    @pl.when(used)
    def _():
        x_lo, x_hi = _unpack_bf16_pairs(xs_ref[...])
        gu = _dot(x_lo, wgu_b[0:512, :]) + _dot(x_hi, wgu_b[512:, :]) + bgu_ref[0]
        gate = jnp.minimum(gu[:, :D_FF], SWIGLU_LIMIT)
        lin = jnp.clip(gu[:, D_FF:], -SWIGLU_LIMIT, SWIGLU_LIMIT)
        act = (lin + 1.0) * (gate * jax.nn.sigmoid(SWIGLU_ALPHA * gate))
        ys_ref[...] = _pack_bf16_pairs(_dot(act.astype(BF16), wd_b[...]) + bd_ref[0])

    @pl.when(jnp.logical_not(used))
    def _():
        ys_ref[...] = jnp.zeros_like(ys_ref)


def _experts(block_e, n_used, slot, nxt, xs, wgu, bgu, wd, bd):
    cap = xs.shape[0]
    nblk = cap // MOE_BLOCK
    return pl.pallas_call(
        _experts_kernel,
        grid_spec=pltpu.PrefetchScalarGridSpec(
            num_scalar_prefetch=4,
            grid=(nblk,),
            in_specs=[pl.BlockSpec((MOE_BLOCK, 512), lambda i, be, nu, *_: (jnp.minimum(i, nu[0] - 1), 0)),
                      pl.BlockSpec(memory_space=pl.ANY),
                      pl.BlockSpec((1, 1, 2 * D_FF), lambda i, be, *_: (be[i], 0, 0)),
                      pl.BlockSpec(memory_space=pl.ANY),
                      pl.BlockSpec((1, 1, D_MODEL), lambda i, be, *_: (be[i], 0, 0))],
            out_specs=pl.BlockSpec((MOE_BLOCK, D_MODEL // 2), lambda i, *_: (i, 0)),
            scratch_shapes=[pltpu.VMEM((2, D_MODEL, 2 * D_FF), F32), pltpu.VMEM((2, D_FF, D_MODEL), F32),
                            pltpu.VMEM((D_MODEL, 2 * D_FF), BF16), pltpu.VMEM((D_FF, D_MODEL), BF16),
                            pltpu.SemaphoreType.DMA((2, 2))]),
        out_shape=jax.ShapeDtypeStruct((cap, D_MODEL // 2), jnp.uint32),
        compiler_params=pltpu.CompilerParams(dimension_semantics=("arbitrary",),
                                             vmem_limit_bytes=VMEM_LIMIT),
        name="experts",
    )(block_e, n_used, slot, nxt, xs, wgu, bgu, wd, bd)


def _combine_kernel(gs_ref, to_ref, nr_ref, ys_ref, ri_ref, rw_ref, x1_ref, gate_ref, g_ref, o_ref, ybuf, cnt_ref,
                    sem):
    nt = pl.num_programs(1)
    tile = pl.program_id(0) * nt + pl.program_id(1)
    ntiles = pl.num_programs(0) * nt
    slot = tile % 2
    tm = x1_ref.shape[1]

    def run_copy(sl):
        def make_copy(g, o, rows):
            return pltpu.make_async_copy(ys_ref.at[pl.ds(g, rows)], ybuf.at[sl, pl.ds(o, rows)], sem.at[sl])
        return make_copy

    @pl.when(tile == 0)
    def _():
        ybuf[...] = jnp.zeros_like(ybuf)
        n_big, n_small = _start_run_copies(gs_ref, to_ref, nr_ref, tile, run_copy(slot))
        cnt_ref[0] = n_big
        cnt_ref[1] = n_small

    cur_big, cur_small = cnt_ref[0], cnt_ref[1]

    @pl.when(tile + 1 < ntiles)
    def _():
        n_big, n_small = _start_run_copies(gs_ref, to_ref, nr_ref, tile + 1, run_copy(1 - slot))
        cnt_ref[0] = n_big
        cnt_ref[1] = n_small

    _wait_pieces(cur_big, cur_small, run_copy(slot))

    posf = ri_ref[0].astype(F32)
    rw = rw_ref[0]
    cb_cols = 256
    f_lo = jnp.zeros((tm, D_MODEL // 2), F32)
    f_hi = jnp.zeros((tm, D_MODEL // 2), F32)
    for cb in range(SORTED_ROWS // cb_cols):
        col = (lax.broadcasted_iota(jnp.int32, (tm, cb_cols), 1) + cb * cb_cols).astype(F32)
        wp = functools.reduce(jnp.add, [jnp.where(col == posf[:, TOP_K + k:TOP_K + k + 1], rw[:, k:k + 1], 0.0)
                                        for k in range(TOP_K)]).astype(BF16)
        y_lo, y_hi = _unpack_bf16_pairs(ybuf[slot, cb * cb_cols:(cb + 1) * cb_cols, :])
        f_lo = f_lo + _dot(wp, y_lo)
        f_hi = f_hi + _dot(wp, y_hi)
    f = jnp.concatenate([f_lo, f_hi], axis=1)
    o_ref[0] = x1_ref[0] + gate_ref[0] * _rms(f, g_ref[...])


def _combine(gstart, toff, nch, ys, ri, rw, x1, gate, g):
    b, s, d = x1.shape
    tm = TM_OUT
    tok = lambda bb, i, *_: (bb, i, 0)
    return pl.pallas_call(
        _combine_kernel,
        grid_spec=pltpu.PrefetchScalarGridSpec(
            num_scalar_prefetch=3,
            grid=(b, s // tm),
            in_specs=[pl.BlockSpec(memory_space=pl.ANY),
                      pl.BlockSpec((1, tm, LANES), tok), pl.BlockSpec((1, tm, LANES), tok),
                      pl.BlockSpec((1, tm, d), tok),
                      pl.BlockSpec((1, 1, d), lambda bb, i, *_: (bb, 0, 0)),
                      pl.BlockSpec((1, d), lambda bb, i, *_: (0, 0))],
            out_specs=pl.BlockSpec((1, tm, d), tok),
            scratch_shapes=[pltpu.VMEM((2, SORTED_ROWS, d // 2), jnp.uint32), pltpu.SMEM((2,), jnp.int32),
                            pltpu.SemaphoreType.DMA((2,))]),
        out_shape=jax.ShapeDtypeStruct((b, s, d), F32),
        compiler_params=pltpu.CompilerParams(dimension_semantics=("arbitrary", "arbitrary"),
                                             vmem_limit_bytes=VMEM_LIMIT),
        name="combine",
    )(gstart, toff, nch, ys, ri, rw, x1, gate, g)


def _rope_angles(rows, rot_dim):
    row = jnp.broadcast_to(jnp.arange(rows, dtype=F32)[:, None], (rows, GRID_W)).reshape(-1)
    col = jnp.broadcast_to(jnp.arange(GRID_W, dtype=F32)[None, :], (rows, GRID_W)).reshape(-1)
    quarter = rot_dim // 4
    inv_freq = ROPE_BASE ** (-jnp.arange(quarter, dtype=F32) / quarter)
    ang = jnp.concatenate([row[:, None] * inv_freq, col[:, None] * inv_freq], axis=-1)
    return jnp.cos(ang), jnp.sin(ang)


def _rope_tables(s):
    ca, sa = _rope_angles(s // GRID_W, HEAD_DIM)
    cb, sb = _rope_angles(s // GRID_W, MLA_ROPE)
    one = lambda w: jnp.ones((s, w), F32)
    zero = lambda w: jnp.zeros((s, w), F32)
    cos_a = jnp.tile(jnp.concatenate([ca, ca], 1), (1, 2))
    sin_a = jnp.tile(jnp.concatenate([-sa, sa], 1), (1, 2))
    cos_q = jnp.concatenate([one(64), cb, cb, one(32)], 1)
    sin_q = jnp.concatenate([zero(64), -sb, sb, zero(32)], 1)
    cos_k = jnp.concatenate([cb, cb, one(96)], 1)
    sin_k = jnp.concatenate([-sb, sb, zero(96)], 1)
    return cos_a, sin_a, cos_q, sin_q, cos_k, sin_k


def _identity_tables(n):
    one, zero = jnp.ones((n, LANES), F32), jnp.zeros((n, LANES), F32)
    return one, zero, one, zero, one, zero


def _inproj_weights(w_in, g_q_a, w_uq, g_kv_a, w_ukv):
    d = w_in.shape[0]
    wq, wk, wv = w_in[:, :512], w_in[:, 512:640], w_in[:, 640:768]
    wcq, wckv, wkr = w_in[:, 768:1152], w_in[:, 1152:1408], w_in[:, 1408:1440]
    z64 = jnp.zeros((d, 64), F32)
    group = A_HEADS // A_KV_HEADS
    wq_ext = []
    for hh in range(A_HEADS):
        wq_h = wq[:, hh * HEAD_DIM:(hh + 1) * HEAD_DIM]
        wq_ext += [wq_h, z64] if hh // group == 0 else [z64, wq_h]
    wkr_p = jnp.concatenate([wkr, jnp.zeros((d, 96), F32)], 1)
    win = jnp.concatenate(wq_ext + [wk, wv, wcq, wckv, wkr_p], 1).astype(BF16)
    wuq = jnp.pad(w_uq.reshape(MLA_Q_LORA, MLA_HEADS, MLA_NOPE + MLA_ROPE), ((0, 0), (0, 0), (0, 32)))
    wuq = wuq.reshape(MLA_Q_LORA, MLA_HEADS * LANES).astype(BF16)
    wukv = w_ukv.reshape(MLA_KV_LORA, MLA_HEADS, MLA_NOPE + MLA_V)
    wkk = jnp.pad(wukv[:, :, :MLA_NOPE], ((0, 0), (0, 0), (0, 64))).reshape(MLA_KV_LORA, MLA_HEADS * LANES)
    wvv = wukv[:, :, MLA_NOPE:].reshape(MLA_KV_LORA, MLA_HEADS * MLA_V)
    j = jnp.arange(LANES)[:, None]
    cidx = jnp.arange(MLA_HEADS * LANES)[None, :]
    e = ((j < MLA_ROPE) & ((cidx % LANES) == MLA_NOPE + j)).astype(BF16)
    return (win, g_q_a.reshape(1, -1), wuq, g_kv_a.reshape(1, -1), wkk.astype(BF16), wvv.T.astype(BF16), e)


def kernel(x, c, ctx, c_ctx, w_ada, b_ada, g_mix_pre, g_mix_post, w_in, sink, g_q_a, w_uq, g_kv_a, w_ukv, w_o,
           g_ffn_pre, g_ffn_post, w_router, b_router, w_gate_up, b_gate_up, w_down, b_down):
    b, s, d = x.shape
    cl = ctx.shape[1]
    t = b * s

    cc = jnp.zeros((8, d), F32).at[:b].set(c).at[b].set(c_ctx)
    mod_all = _adaln(cc, w_ada[0], b_ada[0].reshape(1, -1))
    mod = [mod_all[:b, i * d:(i + 1) * d].reshape(b, 1, d) for i in range(6)]
    mod_c = [jnp.broadcast_to(mod_all[b, i * d:(i + 1) * d].reshape(1, 1, d), (b, 1, d)) for i in range(2)]

    wts = _inproj_weights(w_in[0], g_q_a[0], w_uq[0], g_kv_a[0], w_ukv[0])
    gpre = g_mix_pre[0].reshape(1, d)
    qa, ka, vat, q, k, vt = _inproj(x, mod[0], mod[1], gpre, _rope_tables(s), wts, TM_IN)
    _, kac, vatc, _, kc, vct = _inproj(ctx, mod_c[0], mod_c[1], gpre, _identity_tables(cl), wts, cl)

    sinkv = jnp.repeat(sink[0] * LOG2_E, BLOCK).reshape(A_KV_HEADS, 1, -1)
    out_a = _gqa(sinkv, qa, ka, vat, kac, vatc)
    out_b = _mla(q, k, vt, kc, vct)

    wo = w_o[0].astype(BF16)
    wr = jnp.pad(w_router[0], ((0, 0), (0, LANES - N_EXPERTS)))
    wr_hi, wr_lo = _split_bf16(wr)
    br = jnp.concatenate([b_router[0], jnp.full((LANES - N_EXPERTS,), NEG_INF, F32)]).reshape(1, LANES)
    ii = jnp.arange(TM_OUT)
    tri = (ii[:, None] > ii[None, :]).astype(BF16)
    jj = jnp.arange(LANES)
    upper = (jj[:, None] < jj[None, :]).astype(BF16)
    x1, h2, ri, rw, cnt = _outproj(out_a, out_b, x, wo[:512], wo[512:], g_mix_post[0].reshape(1, d), mod[2],
                                   g_ffn_pre[0].reshape(1, d), mod[3], mod[4], wr_hi, wr_lo, br, tri, upper)

    cnt = cnt[:, 0, :N_EXPERTS].astype(jnp.int32)
    nt = cnt.shape[0]
    rows = (cnt + RUN_ALIGN - 1) // RUN_ALIGN * RUN_ALIGN
    tot = jnp.sum(rows, axis=0)
    carry = jnp.cumsum(rows, axis=0) - rows
    padded = (tot + MOE_BLOCK - 1) // MOE_BLOCK * MOE_BLOCK
    pad_end = jnp.cumsum(padded)
    pad_start = pad_end - padded
    nblk = -(-(t * TOP_K + nt * N_EXPERTS * (RUN_ALIGN - 1)) // MOE_BLOCK) + N_EXPERTS
    cap = nblk * MOE_BLOCK
    blk_row = jnp.arange(nblk, dtype=jnp.int32) * MOE_BLOCK
    block_e = jnp.minimum(jnp.sum((pad_end[None, :] <= blk_row[:, None]).astype(jnp.int32), axis=1), N_EXPERTS - 1)
    n_used = (pad_end[-1] // MOE_BLOCK).astype(jnp.int32).reshape(1)
    toff = (jnp.cumsum(rows, axis=1) - rows).reshape(-1).astype(jnp.int32)
    gstart = (pad_start[None, :] + carry).reshape(-1).astype(jnp.int32)
    nrows = rows.reshape(-1).astype(jnp.int32)
    zstart = jnp.concatenate([pad_start + tot, pad_end[-1:]]).astype(jnp.int32)
    zrows = jnp.concatenate([padded - tot, cap - pad_end[-1:]]).astype(jnp.int32)
    block_e = block_e.astype(jnp.int32)
    run_idx = jnp.cumsum(jnp.concatenate([jnp.zeros((1,), jnp.int32),
                                          (block_e[1:] != block_e[:-1]).astype(jnp.int32)]))
    eid = jnp.arange(N_EXPERTS, dtype=jnp.int32)
    later_used = (tot[None, :] > 0) & (eid[None, :] > eid[:, None])
    next_e = jnp.min(jnp.where(later_used, eid[None, :], N_EXPERTS), axis=1)
    next_e = jnp.where(next_e < N_EXPERTS, next_e, -1).astype(jnp.int32)
    nxt = jnp.sum(jnp.where(block_e[:, None] == eid[None, :], next_e[None, :], 0), axis=1).astype(jnp.int32)

    xs = _dispatch(gstart, toff, nrows, zstart, zrows, ri.reshape(t, LANES), h2.reshape(t, d), cap)
    ys = _experts(block_e, n_used, (run_idx % 2).astype(jnp.int32), nxt, xs, w_gate_up[0],
                  b_gate_up[0].reshape(N_EXPERTS, 1, -1), w_down[0], b_down[0].reshape(N_EXPERTS, 1, -1))
    return _combine(gstart, toff, nrows, ys, ri, rw, x1, mod[5], g_ffn_post[0].reshape(1, d))
```

```python
import functools

import jax
import jax.numpy as jnp
from jax import lax
from jax.experimental import pallas as pl
from jax.experimental.pallas import tpu as pltpu

F32 = jnp.float32
BF16 = jnp.bfloat16

D_MODEL = 1024
GRID_W = 64
HEAD_DIM = 64
A_HEADS = 8
A_KV_HEADS = 2
BLOCK = 128
A_SCALE = HEAD_DIM ** -0.5
MLA_HEADS = 8
MLA_NOPE = 64
MLA_ROPE = 32
MLA_V = 64
MLA_Q_LORA = 384
MLA_KV_LORA = 256
MLA_SCALE = (MLA_NOPE + MLA_ROPE) ** -0.5
LOG2_E = 1.4426950408889634
N_EXPERTS = 32
TOP_K = 4
D_FF = 1024
SWIGLU_LIMIT = 7.0
SWIGLU_ALPHA = 1.702
ROPE_BASE = 10000.0
EPS = 1e-6
NEG_INF = -1e30

LANES = 128
VMEM_LIMIT = 56 * 1024 * 1024

TM_IN = 512
GQA_QB = 4
TQ_MLA = 256
TK_MLA = 512
VT_ROWS = 80
TM_OUT = 512
MOE_BLOCK = 512
RUN_ALIGN = 8
BIG_COPY = 32
SORTED_ROWS = TM_OUT * TOP_K + N_EXPERTS * RUN_ALIGN
MAX_BIG = SORTED_ROWS // BIG_COPY
MAX_SMALL = N_EXPERTS * (BIG_COPY // RUN_ALIGN - 1)


def _dot(a, b):
    return jnp.dot(a, b, preferred_element_type=F32)


def _dot_nt(a, b):
    return lax.dot_general(a, b, (((1,), (1,)), ((), ())), preferred_element_type=F32)


def _split_bf16(x):
    hi = x.astype(BF16)
    lo = (x - hi.astype(F32)).astype(BF16)
    return hi, lo


def _rms(x, g):
    return x * lax.rsqrt(jnp.mean(x * x, axis=-1, keepdims=True) + EPS) * g


def _rope(x, cos, sin_signed, first_half, half):
    n = x.shape[-1]
    partner = jnp.where(first_half, pltpu.roll(x, n - half, 1), pltpu.roll(x, half, 1))
    return x * cos + partner * sin_signed


def _adaln_kernel(c_ref, w_ref, b_ref, o_ref):
    c = c_ref[...]
    s = c * jax.nn.sigmoid(c)
    s_hi, s_lo = _split_bf16(s)
    w_hi, w_lo = _split_bf16(w_ref[...])
    o_ref[...] = _dot(s_hi, w_hi) + _dot(s_hi, w_lo) + _dot(s_lo, w_hi) + b_ref[...]


def _adaln(cc, w, b):
    n = w.shape[1]
    tn = 1024
    return pl.pallas_call(
        _adaln_kernel,
        grid=(n // tn,),
        in_specs=[pl.BlockSpec((8, D_MODEL), lambda j: (0, 0)),
                  pl.BlockSpec((D_MODEL, tn), lambda j: (0, j)),
                  pl.BlockSpec((1, tn), lambda j: (0, j))],
        out_specs=pl.BlockSpec((8, tn), lambda j: (0, j)),
        out_shape=jax.ShapeDtypeStruct((8, n), F32),
        compiler_params=pltpu.CompilerParams(dimension_semantics=("arbitrary",),
                                             vmem_limit_bytes=VMEM_LIMIT),
        name="adaln",
    )(cc, w, b)


def _inproj_kernel(x_ref, shift_ref, scale_ref, gpre_ref, ca_ref, sa_ref, cq_ref, sq_ref, ck_ref, sk_ref,
                   win_ref, gq_ref, wuq_ref, gkv_ref, wk_ref, wv_ref, e_ref,
                   qa_ref, ka_ref, vat_ref, q_ref, k_ref, vt_ref):
    x = x_ref[0]
    tm = x.shape[0]
    h = _rms(x, gpre_ref[...]) * (1.0 + scale_ref[0]) + shift_ref[0]
    p = _dot(h.astype(BF16), win_ref[...])
    lane = lax.broadcasted_iota(jnp.int32, (tm, LANES), 1)

    first_a = (lane & 32) == 0
    ca, sa = ca_ref[...], sa_ref[...]
    for hh in range(A_HEADS):
        sl = slice(hh * LANES, (hh + 1) * LANES)
        qa_ref[0, :, sl] = (_rope(p[:, sl], ca, sa, first_a, 32) * (A_SCALE * LOG2_E)).astype(BF16)
    ka_ref[0] = _rope(p[:, 1024:1152], ca, sa, first_a, 32).astype(BF16)
    vat = p[:, 1152:1280].T
    ones_a = (lax.broadcasted_iota(jnp.int32, (VT_ROWS - HEAD_DIM, BLOCK), 0) == 0).astype(BF16)
    for j in range(tm // BLOCK):
        for hk in range(A_KV_HEADS):
            vat_ref[0, j, hk, 0:HEAD_DIM, :] = vat[hk * HEAD_DIM:(hk + 1) * HEAD_DIM,
                                                   j * BLOCK:(j + 1) * BLOCK].astype(BF16)
            vat_ref[0, j, hk, HEAD_DIM:VT_ROWS, :] = ones_a

    cqn = _rms(p[:, 1280:1664], gq_ref[...]).astype(BF16)
    qf = _dot(cqn, wuq_ref[...])
    first_q = (lane >= 64) & (lane < 80)
    cq, sq = cq_ref[...], sq_ref[...]
    for hh in range(MLA_HEADS):
        sl = slice(hh * LANES, (hh + 1) * LANES)
        q_ref[0, :, sl] = (_rope(qf[:, sl], cq, sq, first_q, 16) * (MLA_SCALE * LOG2_E)).astype(BF16)

    ckvn = _rms(p[:, 1664:1920], gkv_ref[...]).astype(BF16)
    kr = _rope(p[:, 1920:2048], ck_ref[...], sk_ref[...], lane < 16, 16)
    k_ref[0] = (_dot(ckvn, wk_ref[...]) + _dot(kr.astype(BF16), e_ref[...])).astype(BF16)
    vt = _dot_nt(wv_ref[...], ckvn).astype(BF16)
    ones_row = (lax.broadcasted_iota(jnp.int32, (VT_ROWS - MLA_V, tm), 0) == 0).astype(BF16)
    for hh in range(MLA_HEADS):
        vt_ref[0, 0, hh, 0:MLA_V, :] = vt[hh * MLA_V:(hh + 1) * MLA_V, :]
        vt_ref[0, 0, hh, MLA_V:VT_ROWS, :] = ones_row


def _inproj(x, shift, scale, gpre, tabs, wts, tm):
    bx, n, d = x.shape
    ca, sa, cq, sq, ck, sk = tabs
    win, gq, wuq, gkv, wk, wv, e = wts
    tok = lambda i, b: (b, i, 0)
    vec = lambda i, b: (b, 0, 0)
    tab = lambda i, b: (i, 0)
    cst = lambda i, b: (0, 0)
    in_specs = [pl.BlockSpec((1, tm, d), tok), pl.BlockSpec((1, 1, d), vec), pl.BlockSpec((1, 1, d), vec),
                pl.BlockSpec((1, d), cst)]
    in_specs += [pl.BlockSpec((tm, LANES), tab)] * 6
    in_specs += [pl.BlockSpec(w.shape, cst) for w in wts]
    row_out = lambda w: (pl.BlockSpec((1, tm, w), tok), jax.ShapeDtypeStruct((bx, n, w), BF16))
    outs = [row_out(A_HEADS * LANES), row_out(LANES),
            (pl.BlockSpec((1, tm // BLOCK, A_KV_HEADS, VT_ROWS, BLOCK), lambda i, b: (b, i, 0, 0, 0)),
             jax.ShapeDtypeStruct((bx, n // BLOCK, A_KV_HEADS, VT_ROWS, BLOCK), BF16)),
            row_out(MLA_HEADS * LANES), row_out(MLA_HEADS * LANES),
            (pl.BlockSpec((1, 1, MLA_HEADS, VT_ROWS, tm), lambda i, b: (b, i, 0, 0, 0)),
             jax.ShapeDtypeStruct((bx, n // tm, MLA_HEADS, VT_ROWS, tm), BF16))]
    out_specs = [o[0] for o in outs]
    out_shape = [o[1] for o in outs]
    return pl.pallas_call(
        _inproj_kernel,
        grid=(n // tm, bx),
        in_specs=in_specs,
        out_specs=out_specs,
        out_shape=out_shape,
        compiler_params=pltpu.CompilerParams(dimension_semantics=("arbitrary", "arbitrary"),
                                             vmem_limit_bytes=VMEM_LIMIT),
        name="inproj",
    )(x, shift, scale, gpre, ca, sa, cq, sq, ck, sk, win, gq, wuq, gkv, wk, wv, e)


def _gqa_kernel(sink_ref, q_ref, *refs):
    nwin = GQA_QB + 2
    k_refs, kx_ref = refs[:nwin], refs[nwin]
    v_refs, vx_ref = refs[nwin + 1:2 * nwin + 1], refs[2 * nwin + 1]
    o_ref = refs[2 * nwin + 2]
    n = pl.program_id(1)
    nsteps = pl.num_programs(1)
    nctx = kx_ref.shape[1] // BLOCK
    group = A_HEADS // A_KV_HEADS
    key = lax.broadcasted_iota(jnp.int32, (BLOCK, group * BLOCK), 0)
    qry = lax.broadcasted_iota(jnp.int32, (BLOCK, group * BLOCK), 1) & (BLOCK - 1)
    sts = {}
    for qb in range(GQA_QB):
        kcat = jnp.concatenate([k_refs[qb + j][0] for j in range(3)] + [kx_ref[0]], axis=0)
        for hk in range(A_KV_HEADS):
            qg = jnp.concatenate([q_ref[0, qb * BLOCK:(qb + 1) * BLOCK, hh * LANES:(hh + 1) * LANES]
                                  for hh in range(hk * group, (hk + 1) * group)], axis=0)
            sts[qb, hk] = _dot_nt(kcat, qg)
    for qb in range(GQA_QB):
        vis_prev = key >= qry
        vis_next = key <= qry
        if qb == 0:
            vis_prev = key >= qry + jnp.where(n > 0, 0, 2 * BLOCK)
        if qb == GQA_QB - 1:
            vis_next = key <= qry - jnp.where(n < nsteps - 1, 0, 2 * BLOCK)
        outs = []
        for hk in range(A_KV_HEADS):
            st = sts[qb, hk]
            pieces = [jnp.where(vis_prev, st[0:BLOCK], NEG_INF), st[BLOCK:2 * BLOCK],
                      jnp.where(vis_next, st[2 * BLOCK:3 * BLOCK], NEG_INF)]
            pieces += [st[(3 + j) * BLOCK:(4 + j) * BLOCK] for j in range(nctx)]
            sink = sink_ref[hk]
            m = jnp.maximum(functools.reduce(jnp.maximum, [jnp.max(p, axis=0, keepdims=True) for p in pieces]),
                            sink)
            ps = [jnp.exp2(p - m).astype(BF16) for p in pieces]
            vts = [v_refs[qb + j][0, 0, hk] for j in range(3)] + [vx_ref[0, j, hk] for j in range(nctx)]
            pv = functools.reduce(jnp.add, [_dot(vt, p) for vt, p in zip(vts, ps)])
            l = pv[HEAD_DIM:HEAD_DIM + 1, :] + jnp.exp2(sink - m)
            o = pv[0:HEAD_DIM, :] * (1.0 / l)
            outs += [o[:, g * BLOCK:(g + 1) * BLOCK] for g in range(group)]
        for pair in range(A_HEADS // 2):
            both = jnp.concatenate([outs[2 * pair], outs[2 * pair + 1]], axis=0)
            o_ref[0, qb * BLOCK:(qb + 1) * BLOCK, pair * LANES:(pair + 1) * LANES] = both.T.astype(BF16)


def _gqa(sinkv, qa, ka, vat, kac, vatc):
    b, s, _ = qa.shape
    nb = s // BLOCK
    c = kac.shape[1]
    tq = GQA_QB * BLOCK
    cur = lambda bb, n: (bb, n, 0)
    win = lambda j: (lambda bb, n: (bb, jnp.clip(n * GQA_QB + j - 1, 0, nb - 1), 0))
    vwin = lambda j: (lambda bb, n: (bb, jnp.clip(n * GQA_QB + j - 1, 0, nb - 1), 0, 0, 0))
    vblk = (1, 1, A_KV_HEADS, VT_ROWS, BLOCK)
    in_specs = [pl.BlockSpec(sinkv.shape, lambda bb, n: (0, 0, 0)), pl.BlockSpec((1, tq, A_HEADS * LANES), cur)]
    in_specs += [pl.BlockSpec((1, BLOCK, LANES), win(j)) for j in range(GQA_QB + 2)]
    in_specs += [pl.BlockSpec((1, c, LANES), lambda bb, n: (bb, 0, 0))]
    in_specs += [pl.BlockSpec(vblk, vwin(j)) for j in range(GQA_QB + 2)]
    in_specs += [pl.BlockSpec((1, c // BLOCK, A_KV_HEADS, VT_ROWS, BLOCK), lambda bb, n: (bb, 0, 0, 0, 0))]
    return pl.pallas_call(
        _gqa_kernel,
        grid=(b, nb // GQA_QB),
        in_specs=in_specs,
        out_specs=pl.BlockSpec((1, tq, 512), cur),
        out_shape=jax.ShapeDtypeStruct((b, s, 512), BF16),
        compiler_params=pltpu.CompilerParams(dimension_semantics=("arbitrary", "arbitrary"),
                                             vmem_limit_bytes=VMEM_LIMIT),
        name="gqa",
    )(sinkv, qa, *([ka] * (GQA_QB + 2)), kac, *([vat] * (GQA_QB + 2)), vatc)


def _mla_kernel(q_ref, k_ref, vt_ref, kc_ref, vct_ref, o_ref, m_ref, acc_ref, sa_ref, sb_ref):
    nk = k_ref.shape[1] // TK_MLA

    def scores_into(buf, k_blk):
        n = k_blk.shape[0]
        for hh in range(MLA_HEADS):
            hsl = slice(hh * LANES, (hh + 1) * LANES)
            buf[hh, 0:n, :] = _dot_nt(k_blk[:, hsl], q_ref[0, :, hsl])

    def consume(buf, vt_blk, first):
        n = vt_blk.shape[2]
        for hh in range(MLA_HEADS):
            st = buf[hh, 0:n, :]
            cmax = jnp.max(st, axis=0, keepdims=True)
            if first:
                m_new = cmax
            else:
                m_old = m_ref[hh, 0:1, :]
                m_new = jnp.maximum(m_old, cmax)
                alpha = jnp.exp2(m_old - m_new)
            p = jnp.exp2(st - m_new).astype(BF16)
            pv = _dot(vt_blk[hh], p)
            acc_ref[hh] = pv if first else alpha * acc_ref[hh] + pv
            m_ref[hh, 0:1, :] = m_new

    def k_chunk(c):
        return k_ref[0, pl.ds(pl.multiple_of(c * TK_MLA, TK_MLA), TK_MLA), :]

    scores_into(sa_ref, k_chunk(0))
    scores_into(sb_ref, k_chunk(1))
    consume(sa_ref, vt_ref[0, 0], True)

    def body(i, _):
        c = 2 * i + 1
        scores_into(sa_ref, k_chunk(c + 1))
        consume(sb_ref, vt_ref[0, c], False)
        scores_into(sb_ref, k_chunk(c + 2))
        consume(sa_ref, vt_ref[0, c + 1], False)
        return 0

    lax.fori_loop(0, (nk - 2) // 2, body, 0)
    scores_into(sa_ref, kc_ref[0])
    consume(sb_ref, vt_ref[0, nk - 1], False)
    consume(sa_ref, vct_ref[0, 0], False)

    outs = [acc_ref[hh, 0:MLA_V, :] * (1.0 / acc_ref[hh, MLA_V:MLA_V + 1, :]) for hh in range(MLA_HEADS)]
    o_ref[0] = jnp.concatenate(outs, axis=0).T.astype(BF16)


def _mla(q, k, vt, kc, vct):
    b, s, _ = q.shape
    c = kc.shape[1]
    nk = s // TK_MLA
    assert vt.shape == (b, nk, MLA_HEADS, VT_ROWS, TK_MLA) and nk % 2 == 0
    return pl.pallas_call(
        _mla_kernel,
        grid=(b, s // TQ_MLA),
        in_specs=[pl.BlockSpec((1, TQ_MLA, 1024), lambda bb, i: (bb, i, 0)),
                  pl.BlockSpec((1, s, 1024), lambda bb, i: (bb, 0, 0)),
                  pl.BlockSpec((1, nk, MLA_HEADS, VT_ROWS, TK_MLA), lambda bb, i: (bb, 0, 0, 0, 0)),
                  pl.BlockSpec((1, c, 1024), lambda bb, i: (bb, 0, 0)),
                  pl.BlockSpec((1, 1, MLA_HEADS, VT_ROWS, c), lambda bb, i: (bb, 0, 0, 0, 0))],
        out_specs=pl.BlockSpec((1, TQ_MLA, 512), lambda bb, i: (bb, i, 0)),
        out_shape=jax.ShapeDtypeStruct((b, s, 512), BF16),
        scratch_shapes=[pltpu.VMEM((MLA_HEADS, 8, TQ_MLA), F32),
                        pltpu.VMEM((MLA_HEADS, VT_ROWS, TQ_MLA), F32),
                        pltpu.VMEM((MLA_HEADS, TK_MLA, TQ_MLA), F32), pltpu.VMEM((MLA_HEADS, TK_MLA, TQ_MLA), F32)],
        compiler_params=pltpu.CompilerParams(dimension_semantics=("arbitrary", "arbitrary"),
                                             vmem_limit_bytes=VMEM_LIMIT),
        name="mla",
    )(q, k, vt, kc, vct)


def _outproj_kernel(oa_ref, ob_ref, x_ref, woa_ref, wob_ref, gpost_ref, gate_ref, gffn_ref, shift_ref, scale_ref,
                    wrh_ref, wrl_ref, br_ref, tri_ref, upper_ref,
                    x1_ref, h2_ref, ri_ref, rw_ref, cnt_ref):
    y = _dot(oa_ref[0], woa_ref[...]) + _dot(ob_ref[0], wob_ref[...])
    x1 = x_ref[0] + gate_ref[0] * _rms(y, gpost_ref[...])
    x1_ref[0] = x1
    h2 = _rms(x1, gffn_ref[...]) * (1.0 + scale_ref[0]) + shift_ref[0]
    tm = h2.shape[0]
    h2_ref[0] = h2.astype(BF16)

    h_hi, h_lo = _split_bf16(h2)
    logits = _dot(h_hi, wrh_ref[...]) + _dot(h_hi, wrl_ref[...]) + _dot(h_lo, wrh_ref[...]) + br_ref[...]
    lane = lax.broadcasted_iota(jnp.int32, (tm, LANES), 1)
    lane_f = lane.astype(F32)
    tops, idxs = [], []
    cur = logits
    for _ in range(TOP_K):
        mk = jnp.max(cur, axis=-1, keepdims=True)
        ik = jnp.min(jnp.where(cur == mk, lane_f, float(LANES)), axis=-1, keepdims=True)
        tops.append(mk)
        idxs.append(ik)
        cur = jnp.where(lane_f == ik, -jnp.inf, cur)
    es = [jnp.exp(t - tops[0]) for t in tops]
    inv = 1.0 / functools.reduce(jnp.add, es)

    onehots = [(lane_f == ik) for ik in idxs]
    onehot = functools.reduce(jnp.add, [o.astype(F32) for o in onehots])
    cnt = jnp.sum(onehot, axis=0, keepdims=True)
    prefix = _dot(tri_ref[...], onehot.astype(BF16))
    chunks = jnp.floor((cnt + (RUN_ALIGN - 1.0)) * (1.0 / RUN_ALIGN))
    run_start = _dot(jnp.broadcast_to(chunks, (8, LANES)).astype(BF16), upper_ref[...])[0:1, :] * RUN_ALIGN
    base = prefix + run_start
    ri = jnp.zeros((tm, LANES), jnp.int32)
    rw = jnp.zeros((tm, LANES), F32)
    for k in range(TOP_K):
        pos = jnp.sum(jnp.where(onehots[k], base, 0.0), axis=-1, keepdims=True).astype(jnp.int32)
        ri = jnp.where(lane == k, idxs[k].astype(jnp.int32), ri)
        ri = jnp.where(lane == TOP_K + k, pos, ri)
        rw = jnp.where(lane == k, es[k] * inv, rw)
    ri_ref[0] = ri
    rw_ref[0] = rw
    cnt_ref[0] = cnt


def _outproj(oa, ob, x, woa, wob, gpost, gate, gffn, shift, scale, wrh, wrl, br, tri, upper):
    b, s, d = x.shape
    tm = TM_OUT
    nt = s // tm
    tok = lambda bb, i: (bb, i, 0)
    vec = lambda bb, i: (bb, 0, 0)
    cst = lambda bb, i: (0, 0)
    return pl.pallas_call(
        _outproj_kernel,
        grid=(b, nt),
        in_specs=[pl.BlockSpec((1, tm, 512), tok), pl.BlockSpec((1, tm, 512), tok), pl.BlockSpec((1, tm, d), tok),
                  pl.BlockSpec((512, d), cst), pl.BlockSpec((512, d), cst), pl.BlockSpec((1, d), cst),
                  pl.BlockSpec((1, 1, d), vec), pl.BlockSpec((1, d), cst),
                  pl.BlockSpec((1, 1, d), vec), pl.BlockSpec((1, 1, d), vec),
                  pl.BlockSpec((d, LANES), cst), pl.BlockSpec((d, LANES), cst), pl.BlockSpec((1, LANES), cst),
                  pl.BlockSpec((tm, tm), cst), pl.BlockSpec((LANES, LANES), cst)],
        out_specs=[pl.BlockSpec((1, tm, d), tok), pl.BlockSpec((1, tm, d), tok),
                   pl.BlockSpec((1, tm, LANES), tok), pl.BlockSpec((1, tm, LANES), tok),
                   pl.BlockSpec((1, 1, LANES), lambda bb, i: (bb * nt + i, 0, 0))],
        out_shape=[jax.ShapeDtypeStruct((b, s, d), F32), jax.ShapeDtypeStruct((b, s, d), BF16),
                   jax.ShapeDtypeStruct((b, s, LANES), jnp.int32), jax.ShapeDtypeStruct((b, s, LANES), F32),
                   jax.ShapeDtypeStruct((b * nt, 1, LANES), F32)],
        compiler_params=pltpu.CompilerParams(dimension_semantics=("arbitrary", "arbitrary"),
                                             vmem_limit_bytes=VMEM_LIMIT),
        name="outproj",
    )(oa, ob, x, woa, wob, gpost, gate, gffn, shift, scale, wrh, wrl, br, tri, upper)


def _start_pieces(g, o, rows, make_copy):
    n_big = lax.shift_right_logical(rows, BIG_COPY.bit_length() - 1)
    n_small = lax.shift_right_logical(rows, RUN_ALIGN.bit_length() - 1) & (BIG_COPY // RUN_ALIGN - 1)

    def big(c, _):
        off = c * BIG_COPY
        make_copy(pl.multiple_of(g + off, RUN_ALIGN), pl.multiple_of(o + off, RUN_ALIGN), BIG_COPY).start()
        return 0

    def small(c, _):
        off = n_big * BIG_COPY + c * RUN_ALIGN
        make_copy(pl.multiple_of(g + off, RUN_ALIGN), pl.multiple_of(o + off, RUN_ALIGN), RUN_ALIGN).start()
        return 0

    lax.fori_loop(0, n_big, big, 0)
    lax.fori_loop(0, n_small, small, 0)
    return n_big, n_small


def _wait_pieces(n_big, n_small, make_copy):
    def wait_big(i, _):
        make_copy(0, 0, BIG_COPY).wait()
        return 0

    def wait_small(i, _):
        make_copy(0, 0, RUN_ALIGN).wait()
        return 0

    lax.fori_loop(0, n_big, wait_big, 0)
    lax.fori_loop(0, n_small, wait_small, 0)


def _start_listed_copies(plan, tile, make_copy):
    big_g, big_o, n_big, small_g, small_o, n_small = plan

    def start_list(g_ref, o_ref, n, width, rows):
        def body(p, _):
            j = tile * width + p
            make_copy(pl.multiple_of(g_ref[j], RUN_ALIGN), pl.multiple_of(o_ref[j], RUN_ALIGN), rows).start()
            return 0

        lax.fori_loop(0, n, body, 0)

    start_list(big_g, big_o, n_big[tile], MAX_BIG, BIG_COPY)
    start_list(small_g, small_o, n_small[tile], MAX_SMALL, RUN_ALIGN)


def _dispatch_kernel(bg_ref, bo_ref, nb_ref, sg_ref, so_ref, ns_ref, zs_ref, zr_ref, ri_ref, h_ref, xs_ref,
                     sorted_ref, zero_ref, sem):
    plan = (bg_ref, bo_ref, nb_ref, sg_ref, so_ref, ns_ref)
    tile = pl.program_id(0)
    last = pl.num_programs(0) - 1
    slot = tile % 2
    tm = h_ref.shape[0]
    post = ri_ref[...].astype(F32).T
    h = h_ref[...]
    rb_rows = 256
    pos_blk = [jnp.floor(post[TOP_K + k:TOP_K + k + 1, :] * (1.0 / rb_rows)) for k in range(TOP_K)]
    pos_off = [post[TOP_K + k:TOP_K + k + 1, :] - rb_rows * pos_blk[k] for k in range(TOP_K)]
    row = lax.broadcasted_iota(jnp.int32, (rb_rows, tm), 0).astype(F32).astype(BF16)
    one, zero = jnp.ones((), BF16), jnp.zeros((), BF16)

    def fill(rb, _):
        r0 = pl.multiple_of(rb * rb_rows, rb_rows)
        rbf = lax.convert_element_type(rb, F32)
        perm = functools.reduce(jnp.add, [
            jnp.where(row == jnp.where(pos_blk[k] == rbf, pos_off[k], -1.0).astype(BF16), one, zero)
            for k in range(TOP_K)])
        xr = _dot(perm, h)
        lo = lax.bitcast_convert_type(xr[:, :512], jnp.uint32)
        hi = lax.bitcast_convert_type(xr[:, 512:], jnp.uint32)
        sorted_ref[slot, pl.ds(r0, rb_rows), :] = (lo >> 16) | (hi & jnp.uint32(0xFFFF0000))
        return 0

    lax.fori_loop(0, SORTED_ROWS // rb_rows, fill, 0)

    def run_copy(sl):
        def make_copy(g, o, rows):
            return pltpu.make_async_copy(sorted_ref.at[sl, pl.ds(o, rows)], xs_ref.at[pl.ds(g, rows)], sem.at[sl])
        return make_copy

    _start_listed_copies(plan, tile, run_copy(slot))

    @pl.when(tile > 0)
    def _():
        _wait_pieces(nb_ref[tile - 1], ns_ref[tile - 1], run_copy(1 - slot))

    @pl.when(tile == last)
    def _():
        _wait_pieces(nb_ref[tile], ns_ref[tile], run_copy(slot))
        zero_ref[...] = jnp.zeros_like(zero_ref)

        def zero_copy(g, o, rows):
            return pltpu.make_async_copy(zero_ref.at[pl.ds(0, rows)], xs_ref.at[pl.ds(g, rows)], sem.at[2])

        def per_expert(e, totals):
            nb, ns = _start_pieces(zs_ref[e], 0, zr_ref[e], zero_copy)
            return totals[0] + nb, totals[1] + ns

        zb, zs = lax.fori_loop(0, zs_ref.shape[0], per_expert, (0, 0))
        _wait_pieces(zb, zs, zero_copy)


def _dispatch(plan, zstart, zrows, ri, h2, cap):
    t, d = h2.shape
    tm = TM_OUT
    return pl.pallas_call(
        _dispatch_kernel,
        grid_spec=pltpu.PrefetchScalarGridSpec(
            num_scalar_prefetch=8,
            grid=(t // tm,),
            in_specs=[pl.BlockSpec((tm, LANES), lambda i, *_: (i, 0)),
                      pl.BlockSpec((tm, d), lambda i, *_: (i, 0))],
            out_specs=pl.BlockSpec(memory_space=pl.ANY),
            scratch_shapes=[pltpu.VMEM((2, SORTED_ROWS, d // 2), jnp.uint32),
                            pltpu.VMEM((BIG_COPY, d // 2), jnp.uint32),
                            pltpu.SemaphoreType.DMA((3,))]),
        out_shape=jax.ShapeDtypeStruct((cap, d // 2), jnp.uint32),
        compiler_params=pltpu.CompilerParams(dimension_semantics=("arbitrary",),
                                             vmem_limit_bytes=VMEM_LIMIT),
        name="dispatch",
    )(*plan, zstart, zrows, ri, h2)


def _pack_bf16_pairs(x):
    n = x.shape[1] // 2
    bits = lax.bitcast_convert_type(x.astype(BF16).astype(F32), jnp.uint32)
    return (bits[:, :n] >> 16) | (bits[:, n:] & jnp.uint32(0xFFFF0000))


def _unpack_bf16_pairs(w):
    lo = lax.bitcast_convert_type(w << 16, F32).astype(BF16)
    hi = lax.bitcast_convert_type(w & jnp.uint32(0xFFFF0000), F32).astype(BF16)
    return lo, hi


def _experts_kernel(be_ref, nu_ref, slot_ref, nxt_ref, xs_ref, wgu_hbm, bgu_ref, wd_hbm, bd_ref, ys_ref,
                    wgu_f, wd_f, wgu_b, wd_b, sem):
    i = pl.program_id(0)
    e = be_ref[i]
    slot = slot_ref[i]
    used = i < nu_ref[0]
    run_start = used & ((i == 0) | (be_ref[jnp.maximum(i - 1, 0)] != e))

    def weight_copies(ex, sl):
        return (pltpu.make_async_copy(wgu_hbm.at[ex], wgu_f.at[sl], sem.at[0, sl]),
                pltpu.make_async_copy(wd_hbm.at[ex], wd_f.at[sl], sem.at[1, sl]))

    @pl.when(i == 0)
    def _():
        for cp in weight_copies(e, slot):
            cp.start()

    @pl.when(run_start)
    def _():
        for cp in weight_copies(e, slot):
            cp.wait()
        nxt = nxt_ref[i]

        @pl.when(nxt >= 0)
        def _():
            for cp in weight_copies(nxt, 1 - slot):
                cp.start()

        wgu_b[...] = wgu_f[slot].astype(BF16)
        wd_b[...] = wd_f[slot].astype(BF16)

    @pl.when(used)
    def _():
        x_lo, x_hi = _unpack_bf16_pairs(xs_ref[...])
        gu = _dot(x_lo, wgu_b[0:512, :]) + _dot(x_hi, wgu_b[512:, :]) + bgu_ref[0]
        gate = jnp.minimum(gu[:, :D_FF], SWIGLU_LIMIT)
        lin = jnp.clip(gu[:, D_FF:], -SWIGLU_LIMIT, SWIGLU_LIMIT)
        act = (lin + 1.0) * (gate * jax.nn.sigmoid(SWIGLU_ALPHA * gate))
        ys_ref[...] = _pack_bf16_pairs(_dot(act.astype(BF16), wd_b[...]) + bd_ref[0])

    @pl.when(jnp.logical_not(used))
    def _():
        ys_ref[...] = jnp.zeros_like(ys_ref)


def _experts(block_e, n_used, slot, nxt, xs, wgu, bgu, wd, bd):
    cap = xs.shape[0]
    nblk = cap // MOE_BLOCK
    return pl.pallas_call(
        _experts_kernel,
        grid_spec=pltpu.PrefetchScalarGridSpec(
            num_scalar_prefetch=4,
            grid=(nblk,),
            in_specs=[pl.BlockSpec((MOE_BLOCK, 512), lambda i, be, nu, *_: (jnp.minimum(i, nu[0] - 1), 0)),
                      pl.BlockSpec(memory_space=pl.ANY),
                      pl.BlockSpec((1, 1, 2 * D_FF), lambda i, be, *_: (be[i], 0, 0)),
                      pl.BlockSpec(memory_space=pl.ANY),
                      pl.BlockSpec((1, 1, D_MODEL), lambda i, be, *_: (be[i], 0, 0))],
            out_specs=pl.BlockSpec((MOE_BLOCK, D_MODEL // 2), lambda i, *_: (i, 0)),
            scratch_shapes=[pltpu.VMEM((2, D_MODEL, 2 * D_FF), F32), pltpu.VMEM((2, D_FF, D_MODEL), F32),
                            pltpu.VMEM((D_MODEL, 2 * D_FF), BF16), pltpu.VMEM((D_FF, D_MODEL), BF16),
                            pltpu.SemaphoreType.DMA((2, 2))]),
        out_shape=jax.ShapeDtypeStruct((cap, D_MODEL // 2), jnp.uint32),
        compiler_params=pltpu.CompilerParams(dimension_semantics=("arbitrary",),
                                             vmem_limit_bytes=VMEM_LIMIT),
        name="experts",
    )(block_e, n_used, slot, nxt, xs, wgu, bgu, wd, bd)


def _combine_kernel(bg_ref, bo_ref, nb_ref, sg_ref, so_ref, ns_ref, ys_ref, ri_ref, rw_ref, x1_ref, gate_ref, g_ref,
                    o_ref, ybuf, sem):
    plan = (bg_ref, bo_ref, nb_ref, sg_ref, so_ref, ns_ref)
    nt = pl.num_programs(1)
    tile = pl.program_id(0) * nt + pl.program_id(1)
    ntiles = pl.num_programs(0) * nt
    slot = tile % 2
    tm = x1_ref.shape[1]

    def run_copy(sl):
        def make_copy(g, o, rows):
            return pltpu.make_async_copy(ys_ref.at[pl.ds(g, rows)], ybuf.at[sl, pl.ds(o, rows)], sem.at[sl])
        return make_copy

    @pl.when(tile == 0)
    def _():
        ybuf[...] = jnp.zeros_like(ybuf)
        _start_listed_copies(plan, tile, run_copy(slot))

    @pl.when(tile + 1 < ntiles)
    def _():
        _start_listed_copies(plan, tile + 1, run_copy(1 - slot))

    _wait_pieces(nb_ref[tile], ns_ref[tile], run_copy(slot))

    posf = ri_ref[0].astype(F32)
    rw = rw_ref[0]
    cb_cols = 256
    pos = [posf[:, TOP_K + k:TOP_K + k + 1] for k in range(TOP_K)]
    pos_blk = [jnp.floor(p * (1.0 / cb_cols)) for p in pos]
    pos_off = [p - cb_cols * b for p, b in zip(pos, pos_blk)]
    wk = [rw[:, k:k + 1].astype(BF16) for k in range(TOP_K)]
    col = lax.broadcasted_iota(jnp.int32, (tm, cb_cols), 1).astype(F32).astype(BF16)
    zero = jnp.zeros((), BF16)
    f_lo = jnp.zeros((tm, D_MODEL // 2), F32)
    f_hi = jnp.zeros((tm, D_MODEL // 2), F32)
    for cb in range(SORTED_ROWS // cb_cols):
        wp = functools.reduce(jnp.add, [
            jnp.where(col == jnp.where(pos_blk[k] == float(cb), pos_off[k], -1.0).astype(BF16), wk[k], zero)
            for k in range(TOP_K)])
        y_lo, y_hi = _unpack_bf16_pairs(ybuf[slot, cb * cb_cols:(cb + 1) * cb_cols, :])
        f_lo = f_lo + _dot(wp, y_lo)
        f_hi = f_hi + _dot(wp, y_hi)
    f = jnp.concatenate([f_lo, f_hi], axis=1)
    o_ref[0] = x1_ref[0] + gate_ref[0] * _rms(f, g_ref[...])


def _combine(plan, ys, ri, rw, x1, gate, g):
    b, s, d = x1.shape
    tm = TM_OUT
    tok = lambda bb, i, *_: (bb, i, 0)
    return pl.pallas_call(
        _combine_kernel,
        grid_spec=pltpu.PrefetchScalarGridSpec(
            num_scalar_prefetch=6,
            grid=(b, s // tm),
            in_specs=[pl.BlockSpec(memory_space=pl.ANY),
                      pl.BlockSpec((1, tm, LANES), tok), pl.BlockSpec((1, tm, LANES), tok),
                      pl.BlockSpec((1, tm, d), tok),
                      pl.BlockSpec((1, 1, d), lambda bb, i, *_: (bb, 0, 0)),
                      pl.BlockSpec((1, d), lambda bb, i, *_: (0, 0))],
            out_specs=pl.BlockSpec((1, tm, d), tok),
            scratch_shapes=[pltpu.VMEM((2, SORTED_ROWS, d // 2), jnp.uint32), pltpu.SemaphoreType.DMA((2,))]),
        out_shape=jax.ShapeDtypeStruct((b, s, d), F32),
        compiler_params=pltpu.CompilerParams(dimension_semantics=("arbitrary", "arbitrary"),
                                             vmem_limit_bytes=VMEM_LIMIT),
        name="combine",
    )(*plan, ys, ri, rw, x1, gate, g)


def _rope_angles(rows, rot_dim):
    row = jnp.broadcast_to(jnp.arange(rows, dtype=F32)[:, None], (rows, GRID_W)).reshape(-1)
    col = jnp.broadcast_to(jnp.arange(GRID_W, dtype=F32)[None, :], (rows, GRID_W)).reshape(-1)
    quarter = rot_dim // 4
    inv_freq = ROPE_BASE ** (-jnp.arange(quarter, dtype=F32) / quarter)
    ang = jnp.concatenate([row[:, None] * inv_freq, col[:, None] * inv_freq], axis=-1)
    return jnp.cos(ang), jnp.sin(ang)


def _rope_tables(s):
    ca, sa = _rope_angles(s // GRID_W, HEAD_DIM)
    cb, sb = _rope_angles(s // GRID_W, MLA_ROPE)
    one = lambda w: jnp.ones((s, w), F32)
    zero = lambda w: jnp.zeros((s, w), F32)
    cos_a = jnp.tile(jnp.concatenate([ca, ca], 1), (1, 2))
    sin_a = jnp.tile(jnp.concatenate([-sa, sa], 1), (1, 2))
    cos_q = jnp.concatenate([one(64), cb, cb, one(32)], 1)
    sin_q = jnp.concatenate([zero(64), -sb, sb, zero(32)], 1)
    cos_k = jnp.concatenate([cb, cb, one(96)], 1)
    sin_k = jnp.concatenate([-sb, sb, zero(96)], 1)
    return cos_a, sin_a, cos_q, sin_q, cos_k, sin_k


def _piece_lists(slot_start, tile_start, counts, first_off, piece, width):
    ne = counts.shape[1]
    cum = jnp.cumsum(counts, axis=1)
    p = jnp.arange(width, dtype=jnp.int32)
    owner = jnp.minimum(jnp.sum((cum[:, None, :] <= p[None, :, None]).astype(jnp.int32), axis=2), ne - 1)
    sel = owner[:, :, None] == jnp.arange(ne, dtype=jnp.int32)[None, None, :]
    pick = lambda a: jnp.sum(jnp.where(sel, a[:, None, :], 0), axis=2)
    off = pick(first_off) + (p[None, :] - pick(cum - counts)) * piece
    as_list = lambda a: a.reshape(-1).astype(jnp.int32)
    return as_list(pick(slot_start) + off), as_list(pick(tile_start) + off), cum[:, -1].astype(jnp.int32)


def _identity_tables(n):
    one, zero = jnp.ones((n, LANES), F32), jnp.zeros((n, LANES), F32)
    return one, zero, one, zero, one, zero


def _inproj_weights(w_in, g_q_a, w_uq, g_kv_a, w_ukv):
    d = w_in.shape[0]
    wq, wk, wv = w_in[:, :512], w_in[:, 512:640], w_in[:, 640:768]
    wcq, wckv, wkr = w_in[:, 768:1152], w_in[:, 1152:1408], w_in[:, 1408:1440]
    z64 = jnp.zeros((d, 64), F32)
    group = A_HEADS // A_KV_HEADS
    wq_ext = []
    for hh in range(A_HEADS):
        wq_h = wq[:, hh * HEAD_DIM:(hh + 1) * HEAD_DIM]
        wq_ext += [wq_h, z64] if hh // group == 0 else [z64, wq_h]
    wkr_p = jnp.concatenate([wkr, jnp.zeros((d, 96), F32)], 1)
    win = jnp.concatenate(wq_ext + [wk, wv, wcq, wckv, wkr_p], 1).astype(BF16)
    wuq = jnp.pad(w_uq.reshape(MLA_Q_LORA, MLA_HEADS, MLA_NOPE + MLA_ROPE), ((0, 0), (0, 0), (0, 32)))
    wuq = wuq.reshape(MLA_Q_LORA, MLA_HEADS * LANES).astype(BF16)
    wukv = w_ukv.reshape(MLA_KV_LORA, MLA_HEADS, MLA_NOPE + MLA_V)
    wkk = jnp.pad(wukv[:, :, :MLA_NOPE], ((0, 0), (0, 0), (0, 64))).reshape(MLA_KV_LORA, MLA_HEADS * LANES)
    wvv = wukv[:, :, MLA_NOPE:].reshape(MLA_KV_LORA, MLA_HEADS * MLA_V)
    j = jnp.arange(LANES)[:, None]
    cidx = jnp.arange(MLA_HEADS * LANES)[None, :]
    e = ((j < MLA_ROPE) & ((cidx % LANES) == MLA_NOPE + j)).astype(BF16)
    return (win, g_q_a.reshape(1, -1), wuq, g_kv_a.reshape(1, -1), wkk.astype(BF16), wvv.T.astype(BF16), e)


def kernel(x, c, ctx, c_ctx, w_ada, b_ada, g_mix_pre, g_mix_post, w_in, sink, g_q_a, w_uq, g_kv_a, w_ukv, w_o,
           g_ffn_pre, g_ffn_post, w_router, b_router, w_gate_up, b_gate_up, w_down, b_down):
    b, s, d = x.shape
    cl = ctx.shape[1]
    t = b * s

    cc = jnp.zeros((8, d), F32).at[:b].set(c).at[b].set(c_ctx)
    mod_all = _adaln(cc, w_ada[0], b_ada[0].reshape(1, -1))
    mod = [mod_all[:b, i * d:(i + 1) * d].reshape(b, 1, d) for i in range(6)]
    mod_c = [jnp.broadcast_to(mod_all[b, i * d:(i + 1) * d].reshape(1, 1, d), (b, 1, d)) for i in range(2)]

    wts = _inproj_weights(w_in[0], g_q_a[0], w_uq[0], g_kv_a[0], w_ukv[0])
    gpre = g_mix_pre[0].reshape(1, d)
    qa, ka, vat, q, k, vt = _inproj(x, mod[0], mod[1], gpre, _rope_tables(s), wts, TM_IN)
    _, kac, vatc, _, kc, vct = _inproj(ctx, mod_c[0], mod_c[1], gpre, _identity_tables(cl), wts, cl)

    sinkv = jnp.repeat(sink[0] * LOG2_E, BLOCK).reshape(A_KV_HEADS, 1, -1)
    out_a = _gqa(sinkv, qa, ka, vat, kac, vatc)
    out_b = _mla(q, k, vt, kc, vct)

    wo = w_o[0].astype(BF16)
    wr = jnp.pad(w_router[0], ((0, 0), (0, LANES - N_EXPERTS)))
    wr_hi, wr_lo = _split_bf16(wr)
    br = jnp.concatenate([b_router[0], jnp.full((LANES - N_EXPERTS,), NEG_INF, F32)]).reshape(1, LANES)
    ii = jnp.arange(TM_OUT)
    tri = (ii[:, None] > ii[None, :]).astype(BF16)
    jj = jnp.arange(LANES)
    upper = (jj[:, None] < jj[None, :]).astype(BF16)
    x1, h2, ri, rw, cnt = _outproj(out_a, out_b, x, wo[:512], wo[512:], g_mix_post[0].reshape(1, d), mod[2],
                                   g_ffn_pre[0].reshape(1, d), mod[3], mod[4], wr_hi, wr_lo, br, tri, upper)

    cnt = cnt[:, 0, :N_EXPERTS].astype(jnp.int32)
    nt = cnt.shape[0]
    rows = (cnt + RUN_ALIGN - 1) // RUN_ALIGN * RUN_ALIGN
    tot = jnp.sum(rows, axis=0)
    carry = jnp.cumsum(rows, axis=0) - rows
    padded = (tot + MOE_BLOCK - 1) // MOE_BLOCK * MOE_BLOCK
    pad_end = jnp.cumsum(padded)
    pad_start = pad_end - padded
    nblk = -(-(t * TOP_K + nt * N_EXPERTS * (RUN_ALIGN - 1)) // MOE_BLOCK) + N_EXPERTS
    cap = nblk * MOE_BLOCK
    blk_row = jnp.arange(nblk, dtype=jnp.int32) * MOE_BLOCK
    block_e = jnp.minimum(jnp.sum((pad_end[None, :] <= blk_row[:, None]).astype(jnp.int32), axis=1), N_EXPERTS - 1)
    n_used = (pad_end[-1] // MOE_BLOCK).astype(jnp.int32).reshape(1)
    toff = jnp.cumsum(rows, axis=1) - rows
    gstart = pad_start[None, :] + carry
    n_big = rows // BIG_COPY
    n_small = (rows - n_big * BIG_COPY) // RUN_ALIGN
    big_g, big_o, tot_big = _piece_lists(gstart, toff, n_big, jnp.zeros_like(rows), BIG_COPY, MAX_BIG)
    small_g, small_o, tot_small = _piece_lists(gstart, toff, n_small, n_big * BIG_COPY, RUN_ALIGN, MAX_SMALL)
    plan = (big_g, big_o, tot_big, small_g, small_o, tot_small)
    zstart = jnp.concatenate([pad_start + tot, pad_end[-1:]]).astype(jnp.int32)
    zrows = jnp.concatenate([padded - tot, cap - pad_end[-1:]]).astype(jnp.int32)
    block_e = block_e.astype(jnp.int32)
    run_idx = jnp.cumsum(jnp.concatenate([jnp.zeros((1,), jnp.int32),
                                          (block_e[1:] != block_e[:-1]).astype(jnp.int32)]))
    eid = jnp.arange(N_EXPERTS, dtype=jnp.int32)
    later_used = (tot[None, :] > 0) & (eid[None, :] > eid[:, None])
    next_e = jnp.min(jnp.where(later_used, eid[None, :], N_EXPERTS), axis=1)
    next_e = jnp.where(next_e < N_EXPERTS, next_e, -1).astype(jnp.int32)
    nxt = jnp.sum(jnp.where(block_e[:, None] == eid[None, :], next_e[None, :], 0), axis=1).astype(jnp.int32)

    xs = _dispatch(plan, zstart, zrows, ri.reshape(t, LANES), h2.reshape(t, d), cap)
    ys = _experts(block_e, n_used, (run_idx % 2).astype(jnp.int32), nxt, xs, w_gate_up[0],
                  b_gate_up[0].reshape(N_EXPERTS, 1, -1), w_down[0], b_down[0].reshape(N_EXPERTS, 1, -1))
    return _combine(plan, ys, ri, rw, x1, mod[5], g_ffn_post[0].reshape(1, d))
```

```python
import functools

import jax
import jax.numpy as jnp
from jax import lax
from jax.experimental import pallas as pl
from jax.experimental.pallas import tpu as pltpu

F32 = jnp.float32
BF16 = jnp.bfloat16

D_MODEL = 1024
GRID_W = 64
HEAD_DIM = 64
A_HEADS = 8
A_KV_HEADS = 2
BLOCK = 128
A_SCALE = HEAD_DIM ** -0.5
MLA_HEADS = 8
MLA_NOPE = 64
MLA_ROPE = 32
MLA_V = 64
MLA_Q_LORA = 384
MLA_KV_LORA = 256
MLA_SCALE = (MLA_NOPE + MLA_ROPE) ** -0.5
LOG2_E = 1.4426950408889634
N_EXPERTS = 32
TOP_K = 4
D_FF = 1024
SWIGLU_LIMIT = 7.0
SWIGLU_ALPHA = 1.702
ROPE_BASE = 10000.0
EPS = 1e-6
NEG_INF = -1e30

LANES = 128
VMEM_LIMIT = 56 * 1024 * 1024

TM_IN = 512
GQA_QB = 4
TQ_MLA = 256
TK_MLA = 512
VT_ROWS = 80
TM_OUT = 512
MOE_BLOCK = 512
EXPERT_SUB = 256
RUN_ALIGN = 8
BIG_COPY = 32
SORTED_ROWS = TM_OUT * TOP_K + N_EXPERTS * RUN_ALIGN
MAX_BIG = SORTED_ROWS // BIG_COPY
MAX_SMALL = N_EXPERTS * (BIG_COPY // RUN_ALIGN - 1)


def _dot(a, b):
    return jnp.dot(a, b, preferred_element_type=F32)


def _dot_nt(a, b):
    return lax.dot_general(a, b, (((1,), (1,)), ((), ())), preferred_element_type=F32)


def _split_bf16(x):
    hi = x.astype(BF16)
    lo = (x - hi.astype(F32)).astype(BF16)
    return hi, lo


def _rms(x, g):
    return x * lax.rsqrt(jnp.mean(x * x, axis=-1, keepdims=True) + EPS) * g


def _rope(x, cos, sin_signed, first_half, half):
    n = x.shape[-1]
    partner = jnp.where(first_half, pltpu.roll(x, n - half, 1), pltpu.roll(x, half, 1))
    return x * cos + partner * sin_signed


def _adaln_kernel(c_ref, w_ref, b_ref, o_ref):
    c = c_ref[...]
    s = c * jax.nn.sigmoid(c)
    s_hi, s_lo = _split_bf16(s)
    w_hi, w_lo = _split_bf16(w_ref[...])
    o_ref[...] = _dot(s_hi, w_hi) + _dot(s_hi, w_lo) + _dot(s_lo, w_hi) + b_ref[...]


def _adaln(cc, w, b):
    n = w.shape[1]
    tn = 1024
    return pl.pallas_call(
        _adaln_kernel,
        grid=(n // tn,),
        in_specs=[pl.BlockSpec((8, D_MODEL), lambda j: (0, 0)),
                  pl.BlockSpec((D_MODEL, tn), lambda j: (0, j)),
                  pl.BlockSpec((1, tn), lambda j: (0, j))],
        out_specs=pl.BlockSpec((8, tn), lambda j: (0, j)),
        out_shape=jax.ShapeDtypeStruct((8, n), F32),
        compiler_params=pltpu.CompilerParams(dimension_semantics=("arbitrary",),
                                             vmem_limit_bytes=VMEM_LIMIT),
        name="adaln",
    )(cc, w, b)


def _inproj_kernel(x_ref, shift_ref, scale_ref, gpre_ref, ca_ref, sa_ref, cq_ref, sq_ref, ck_ref, sk_ref,
                   wlat_ref, wwin_ref, gq_ref, wuq_ref, gkv_ref, wk_ref, wv_ref,
                   qa_ref, ka_ref, vat_ref, q_ref, k_ref, vt_ref):
    x = x_ref[0]
    tm = x.shape[0]
    h = _rms(x, gpre_ref[...]) * (1.0 + scale_ref[0]) + shift_ref[0]
    hb = h.astype(BF16)
    lat = _dot(hb, wlat_ref[...])
    p = _dot(hb, wwin_ref[...])
    c_ka, c_va = A_HEADS * HEAD_DIM, A_HEADS * HEAD_DIM + LANES
    c_ckv, c_kr = MLA_Q_LORA, MLA_Q_LORA + MLA_KV_LORA
    lane = lax.broadcasted_iota(jnp.int32, (tm, LANES), 1)

    first_a = (lane & 32) == 0
    ca, sa = ca_ref[...], sa_ref[...]
    for j in range(A_HEADS * HEAD_DIM // LANES):
        sl = slice(j * LANES, (j + 1) * LANES)
        qa_ref[0, :, sl] = (_rope(p[:, sl], ca, sa, first_a, 32) * (A_SCALE * LOG2_E)).astype(BF16)
    ka_ref[0] = _rope(p[:, c_ka:c_va], ca, sa, first_a, 32).astype(BF16)
    vat = p[:, c_va:c_va + LANES].T
    ones_a = (lax.broadcasted_iota(jnp.int32, (VT_ROWS - HEAD_DIM, BLOCK), 0) == 0).astype(BF16)
    for j in range(tm // BLOCK):
        for hk in range(A_KV_HEADS):
            vat_ref[0, j, hk, 0:HEAD_DIM, :] = vat[hk * HEAD_DIM:(hk + 1) * HEAD_DIM,
                                                   j * BLOCK:(j + 1) * BLOCK].astype(BF16)
            vat_ref[0, j, hk, HEAD_DIM:VT_ROWS, :] = ones_a

    cqn = _rms(lat[:, 0:c_ckv], gq_ref[...]).astype(BF16)
    qf = _dot(cqn, wuq_ref[...])
    first_q = (lane >= 64) & (lane < 80)
    cq, sq = cq_ref[...], sq_ref[...]
    for hh in range(MLA_HEADS):
        sl = slice(hh * LANES, (hh + 1) * LANES)
        q_ref[0, :, sl] = (_rope(qf[:, sl], cq, sq, first_q, 16) * (MLA_SCALE * LOG2_E)).astype(BF16)

    ckvn = _rms(lat[:, c_ckv:c_kr], gkv_ref[...]).astype(BF16)
    kr = _rope(lat[:, c_kr:c_kr + LANES], ck_ref[...], sk_ref[...], lane < 16, 16)
    kr = pltpu.roll(kr, MLA_NOPE, 1)
    kn = _dot(ckvn, wk_ref[...])
    for hh in range(MLA_HEADS):
        sl = slice(hh * LANES, (hh + 1) * LANES)
        k_ref[0, :, sl] = (kn[:, sl] + kr).astype(BF16)
    vt = _dot_nt(wv_ref[...], ckvn).astype(BF16)
    ones_row = (lax.broadcasted_iota(jnp.int32, (VT_ROWS - MLA_V, tm), 0) == 0).astype(BF16)
    for hh in range(MLA_HEADS):
        vt_ref[0, 0, hh, 0:MLA_V, :] = vt[hh * MLA_V:(hh + 1) * MLA_V, :]
        vt_ref[0, 0, hh, MLA_V:VT_ROWS, :] = ones_row


def _inproj(x, shift, scale, gpre, tabs, wts, tm):
    bx, n, d = x.shape
    ca, sa, cq, sq, ck, sk = tabs
    wlat, wwin, gq, wuq, gkv, wk, wv = wts
    tok = lambda i, b: (b, i, 0)
    vec = lambda i, b: (b, 0, 0)
    tab = lambda i, b: (i, 0)
    cst = lambda i, b: (0, 0)
    in_specs = [pl.BlockSpec((1, tm, d), tok), pl.BlockSpec((1, 1, d), vec), pl.BlockSpec((1, 1, d), vec),
                pl.BlockSpec((1, d), cst)]
    in_specs += [pl.BlockSpec((tm, LANES), tab)] * 6
    in_specs += [pl.BlockSpec(w.shape, cst) for w in wts]
    row_out = lambda w: (pl.BlockSpec((1, tm, w), tok), jax.ShapeDtypeStruct((bx, n, w), BF16))
    outs = [row_out(A_HEADS * HEAD_DIM), row_out(LANES),
            (pl.BlockSpec((1, tm // BLOCK, A_KV_HEADS, VT_ROWS, BLOCK), lambda i, b: (b, i, 0, 0, 0)),
             jax.ShapeDtypeStruct((bx, n // BLOCK, A_KV_HEADS, VT_ROWS, BLOCK), BF16)),
            row_out(MLA_HEADS * LANES), row_out(MLA_HEADS * LANES),
            (pl.BlockSpec((1, 1, MLA_HEADS, VT_ROWS, tm), lambda i, b: (b, i, 0, 0, 0)),
             jax.ShapeDtypeStruct((bx, n // tm, MLA_HEADS, VT_ROWS, tm), BF16))]
    out_specs = [o[0] for o in outs]
    out_shape = [o[1] for o in outs]
    return pl.pallas_call(
        _inproj_kernel,
        grid=(n // tm, bx),
        in_specs=in_specs,
        out_specs=out_specs,
        out_shape=out_shape,
        compiler_params=pltpu.CompilerParams(dimension_semantics=("arbitrary", "arbitrary"),
                                             vmem_limit_bytes=VMEM_LIMIT),
        name="inproj",
    )(x, shift, scale, gpre, ca, sa, cq, sq, ck, sk, wlat, wwin, gq, wuq, gkv, wk, wv)


def _gqa_kernel(sink_ref, q_ref, *refs):
    nwin = GQA_QB + 2
    k_refs, kx_ref = refs[:nwin], refs[nwin]
    v_refs, vx_ref = refs[nwin + 1:2 * nwin + 1], refs[2 * nwin + 1]
    o_ref = refs[2 * nwin + 2]
    n = pl.program_id(1)
    nsteps = pl.num_programs(1)
    nctx = kx_ref.shape[1] // BLOCK
    group = A_HEADS // A_KV_HEADS
    key = lax.broadcasted_iota(jnp.int32, (BLOCK, group * BLOCK), 0)
    qry = lax.broadcasted_iota(jnp.int32, (BLOCK, group * BLOCK), 1) & (BLOCK - 1)
    sts = {}
    for qb in range(GQA_QB):
        kcat = jnp.concatenate([k_refs[qb + j][0] for j in range(3)] + [kx_ref[0]], axis=0)
        for hk in range(A_KV_HEADS):
            qg = jnp.concatenate([q_ref[0, qb * BLOCK:(qb + 1) * BLOCK, hh * HEAD_DIM:(hh + 1) * HEAD_DIM]
                                  for hh in range(hk * group, (hk + 1) * group)], axis=0)
            k_hk = kcat[:, hk * HEAD_DIM:(hk + 1) * HEAD_DIM]
            sts[qb, hk] = _dot_nt(k_hk, qg)
    for qb in range(GQA_QB):
        vis_prev = key >= qry
        vis_next = key <= qry
        if qb == 0:
            vis_prev = key >= qry + jnp.where(n > 0, 0, 2 * BLOCK)
        if qb == GQA_QB - 1:
            vis_next = key <= qry - jnp.where(n < nsteps - 1, 0, 2 * BLOCK)
        outs = []
        for hk in range(A_KV_HEADS):
            st = sts[qb, hk]
            pieces = [jnp.where(vis_prev, st[0:BLOCK], NEG_INF), st[BLOCK:2 * BLOCK],
                      jnp.where(vis_next, st[2 * BLOCK:3 * BLOCK], NEG_INF)]
            pieces += [st[(3 + j) * BLOCK:(4 + j) * BLOCK] for j in range(nctx)]
            sink = sink_ref[hk]
            m = jnp.maximum(functools.reduce(jnp.maximum, [jnp.max(p, axis=0, keepdims=True) for p in pieces]),
                            sink)
            ps = [jnp.exp2(p - m).astype(BF16) for p in pieces]
            vts = [v_refs[qb + j][0, 0, hk] for j in range(3)] + [vx_ref[0, j, hk] for j in range(nctx)]
            pv = functools.reduce(jnp.add, [_dot(vt, p) for vt, p in zip(vts, ps)])
            l = pv[HEAD_DIM:HEAD_DIM + 1, :] + jnp.exp2(sink - m)
            o = pv[0:HEAD_DIM, :] * (1.0 / l)
            outs += [o[:, g * BLOCK:(g + 1) * BLOCK] for g in range(group)]
        for pair in range(A_HEADS // 2):
            both = jnp.concatenate([outs[2 * pair], outs[2 * pair + 1]], axis=0)
            o_ref[0, qb * BLOCK:(qb + 1) * BLOCK, pair * LANES:(pair + 1) * LANES] = both.T.astype(BF16)


def _gqa(sinkv, qa, ka, vat, kac, vatc):
    b, s, _ = qa.shape
    nb = s // BLOCK
    c = kac.shape[1]
    tq = GQA_QB * BLOCK
    cur = lambda bb, n: (bb, n, 0)
    win = lambda j: (lambda bb, n: (bb, jnp.clip(n * GQA_QB + j - 1, 0, nb - 1), 0))
    vwin = lambda j: (lambda bb, n: (bb, jnp.clip(n * GQA_QB + j - 1, 0, nb - 1), 0, 0, 0))
    vblk = (1, 1, A_KV_HEADS, VT_ROWS, BLOCK)
    in_specs = [pl.BlockSpec(sinkv.shape, lambda bb, n: (0, 0, 0)), pl.BlockSpec((1, tq, A_HEADS * HEAD_DIM), cur)]
    in_specs += [pl.BlockSpec((1, BLOCK, LANES), win(j)) for j in range(GQA_QB + 2)]
    in_specs += [pl.BlockSpec((1, c, LANES), lambda bb, n: (bb, 0, 0))]
    in_specs += [pl.BlockSpec(vblk, vwin(j)) for j in range(GQA_QB + 2)]
    in_specs += [pl.BlockSpec((1, c // BLOCK, A_KV_HEADS, VT_ROWS, BLOCK), lambda bb, n: (bb, 0, 0, 0, 0))]
    return pl.pallas_call(
        _gqa_kernel,
        grid=(b, nb // GQA_QB),
        in_specs=in_specs,
        out_specs=pl.BlockSpec((1, tq, 512), cur),
        out_shape=jax.ShapeDtypeStruct((b, s, 512), BF16),
        compiler_params=pltpu.CompilerParams(dimension_semantics=("arbitrary", "arbitrary"),
                                             vmem_limit_bytes=VMEM_LIMIT),
        name="gqa",
    )(sinkv, qa, *([ka] * (GQA_QB + 2)), kac, *([vat] * (GQA_QB + 2)), vatc)


def _mla_kernel(q_ref, k_ref, vt_ref, kc_ref, vct_ref, o_ref, m_ref, acc_ref, sa_ref, sb_ref):
    nk = k_ref.shape[1] // TK_MLA

    def scores_into(buf, k_blk):
        n = k_blk.shape[0]
        for hh in range(MLA_HEADS):
            hsl = slice(hh * LANES, (hh + 1) * LANES)
            buf[hh, 0:n, :] = _dot_nt(k_blk[:, hsl], q_ref[0, :, hsl])

    def consume(buf, vt_blk, first):
        n = vt_blk.shape[2]
        for hh in range(MLA_HEADS):
            st = buf[hh, 0:n, :]
            cmax = jnp.max(st, axis=0, keepdims=True)
            if first:
                m_new = cmax
            else:
                m_old = m_ref[hh, 0:1, :]
                m_new = jnp.maximum(m_old, cmax)
                alpha = jnp.exp2(m_old - m_new)
            p = jnp.exp2(st - m_new).astype(BF16)
            pv = _dot(vt_blk[hh], p)
            acc_ref[hh] = pv if first else alpha * acc_ref[hh] + pv
            m_ref[hh, 0:1, :] = m_new

    def k_chunk(c):
        return k_ref[0, pl.ds(pl.multiple_of(c * TK_MLA, TK_MLA), TK_MLA), :]

    scores_into(sa_ref, k_chunk(0))
    scores_into(sb_ref, k_chunk(1))
    consume(sa_ref, vt_ref[0, 0], True)

    def body(i, _):
        c = 2 * i + 1
        scores_into(sa_ref, k_chunk(c + 1))
        consume(sb_ref, vt_ref[0, c], False)
        scores_into(sb_ref, k_chunk(c + 2))
        consume(sa_ref, vt_ref[0, c + 1], False)
        return 0

    lax.fori_loop(0, (nk - 2) // 2, body, 0)
    scores_into(sa_ref, kc_ref[0])
    consume(sb_ref, vt_ref[0, nk - 1], False)
    consume(sa_ref, vct_ref[0, 0], False)

    outs = [acc_ref[hh, 0:MLA_V, :] * (1.0 / acc_ref[hh, MLA_V:MLA_V + 1, :]) for hh in range(MLA_HEADS)]
    o_ref[0] = jnp.concatenate(outs, axis=0).T.astype(BF16)


def _mla(q, k, vt, kc, vct):
    b, s, _ = q.shape
    c = kc.shape[1]
    nk = s // TK_MLA
    assert vt.shape == (b, nk, MLA_HEADS, VT_ROWS, TK_MLA) and nk % 2 == 0
    return pl.pallas_call(
        _mla_kernel,
        grid=(b, s // TQ_MLA),
        in_specs=[pl.BlockSpec((1, TQ_MLA, 1024), lambda bb, i: (bb, i, 0)),
                  pl.BlockSpec((1, s, 1024), lambda bb, i: (bb, 0, 0)),
                  pl.BlockSpec((1, nk, MLA_HEADS, VT_ROWS, TK_MLA), lambda bb, i: (bb, 0, 0, 0, 0)),
                  pl.BlockSpec((1, c, 1024), lambda bb, i: (bb, 0, 0)),
                  pl.BlockSpec((1, 1, MLA_HEADS, VT_ROWS, c), lambda bb, i: (bb, 0, 0, 0, 0))],
        out_specs=pl.BlockSpec((1, TQ_MLA, 512), lambda bb, i: (bb, i, 0)),
        out_shape=jax.ShapeDtypeStruct((b, s, 512), BF16),
        scratch_shapes=[pltpu.VMEM((MLA_HEADS, 8, TQ_MLA), F32),
                        pltpu.VMEM((MLA_HEADS, VT_ROWS, TQ_MLA), F32),
                        pltpu.VMEM((MLA_HEADS, TK_MLA, TQ_MLA), F32), pltpu.VMEM((MLA_HEADS, TK_MLA, TQ_MLA), F32)],
        compiler_params=pltpu.CompilerParams(dimension_semantics=("arbitrary", "arbitrary"),
                                             vmem_limit_bytes=VMEM_LIMIT),
        name="mla",
    )(q, k, vt, kc, vct)


def _outproj_kernel(oa_ref, ob_ref, x_ref, woa_ref, wob_ref, gpost_ref, gate_ref, gffn_ref, shift_ref, scale_ref,
                    wrh_ref, wrl_ref, br_ref, tri_ref, upper_ref,
                    x1_ref, h2_ref, ri_ref, rw_ref, cnt_ref):
    tm = x_ref.shape[1]
    nsub = 2
    sub = tm // nsub
    rows = [slice(j * sub, (j + 1) * sub) for j in range(nsub)]
    ys = [_dot(oa_ref[0, r, :], woa_ref[...]) + _dot(ob_ref[0, r, :], wob_ref[...]) for r in rows]
    sub_lane_f = lax.broadcasted_iota(jnp.int32, (sub, LANES), 1).astype(F32)
    idx_parts, w_parts = [], []
    for r, y in zip(rows, ys):
        x1 = x_ref[0, r, :] + gate_ref[0] * _rms(y, gpost_ref[...])
        x1_ref[0, r, :] = x1
        h2 = _rms(x1, gffn_ref[...]) * (1.0 + scale_ref[0]) + shift_ref[0]
        h2_ref[0, r, :] = h2.astype(BF16)
        h_hi, h_lo = _split_bf16(h2)
        cur = _dot(h_hi, wrh_ref[...]) + _dot(h_hi, wrl_ref[...]) + _dot(h_lo, wrh_ref[...]) + br_ref[...]
        tops, idxs = [], []
        for _ in range(TOP_K):
            mk = jnp.max(cur, axis=-1, keepdims=True)
            ik = jnp.min(jnp.where(cur == mk, sub_lane_f, float(LANES)), axis=-1, keepdims=True)
            tops.append(mk)
            idxs.append(ik)
            cur = jnp.where(sub_lane_f == ik, -jnp.inf, cur)
        es = [jnp.exp(t - tops[0]) for t in tops]
        inv = 1.0 / functools.reduce(jnp.add, es)
        idx_parts.append(idxs)
        w_parts.append([e * inv for e in es])

    hot_parts = [[(sub_lane_f == ik) for ik in idxs] for idxs in idx_parts]
    onehot = jnp.concatenate([functools.reduce(jnp.add, [o.astype(F32) for o in hots]) for hots in hot_parts],
                             axis=0)
    cnt = jnp.sum(onehot, axis=0, keepdims=True)
    prefix = _dot(tri_ref[...], onehot.astype(BF16))
    chunks = jnp.floor((cnt + (RUN_ALIGN - 1.0)) * (1.0 / RUN_ALIGN))
    run_start = _dot(jnp.broadcast_to(chunks, (8, LANES)).astype(BF16), upper_ref[...])[0:1, :] * RUN_ALIGN
    base = prefix + run_start
    sub_lane = lax.broadcasted_iota(jnp.int32, (sub, LANES), 1)
    for r, hots, idxs, wts in zip(rows, hot_parts, idx_parts, w_parts):
        ri = jnp.zeros((sub, LANES), jnp.int32)
        rw = jnp.zeros((sub, LANES), F32)
        for k in range(TOP_K):
            pos = jnp.sum(jnp.where(hots[k], base[r, :], 0.0), axis=-1, keepdims=True).astype(jnp.int32)
            ri = jnp.where(sub_lane == k, idxs[k].astype(jnp.int32), ri)
            ri = jnp.where(sub_lane == TOP_K + k, pos, ri)
            rw = jnp.where(sub_lane == k, wts[k], rw)
        ri_ref[0, r, :] = ri
        rw_ref[0, r, :] = rw
    cnt_ref[0] = cnt


def _outproj(oa, ob, x, woa, wob, gpost, gate, gffn, shift, scale, wrh, wrl, br, tri, upper):
    b, s, d = x.shape
    tm = TM_OUT
    nt = s // tm
    tok = lambda bb, i: (bb, i, 0)
    vec = lambda bb, i: (bb, 0, 0)
    cst = lambda bb, i: (0, 0)
    return pl.pallas_call(
        _outproj_kernel,
        grid=(b, nt),
        in_specs=[pl.BlockSpec((1, tm, 512), tok), pl.BlockSpec((1, tm, 512), tok), pl.BlockSpec((1, tm, d), tok),
                  pl.BlockSpec((512, d), cst), pl.BlockSpec((512, d), cst), pl.BlockSpec((1, d), cst),
                  pl.BlockSpec((1, 1, d), vec), pl.BlockSpec((1, d), cst),
                  pl.BlockSpec((1, 1, d), vec), pl.BlockSpec((1, 1, d), vec),
                  pl.BlockSpec((d, LANES), cst), pl.BlockSpec((d, LANES), cst), pl.BlockSpec((1, LANES), cst),
                  pl.BlockSpec((tm, tm), cst), pl.BlockSpec((LANES, LANES), cst)],
        out_specs=[pl.BlockSpec((1, tm, d), tok), pl.BlockSpec((1, tm, d), tok),
                   pl.BlockSpec((1, tm, LANES), tok), pl.BlockSpec((1, tm, LANES), tok),
                   pl.BlockSpec((1, 1, LANES), lambda bb, i: (bb * nt + i, 0, 0))],
        out_shape=[jax.ShapeDtypeStruct((b, s, d), F32), jax.ShapeDtypeStruct((b, s, d), BF16),
                   jax.ShapeDtypeStruct((b, s, LANES), jnp.int32), jax.ShapeDtypeStruct((b, s, LANES), F32),
                   jax.ShapeDtypeStruct((b * nt, 1, LANES), F32)],
        compiler_params=pltpu.CompilerParams(dimension_semantics=("arbitrary", "arbitrary"),
                                             vmem_limit_bytes=VMEM_LIMIT),
        name="outproj",
    )(oa, ob, x, woa, wob, gpost, gate, gffn, shift, scale, wrh, wrl, br, tri, upper)


def _start_pieces(g, o, rows, make_copy):
    n_big = lax.shift_right_logical(rows, BIG_COPY.bit_length() - 1)
    n_small = lax.shift_right_logical(rows, RUN_ALIGN.bit_length() - 1) & (BIG_COPY // RUN_ALIGN - 1)

    def big(c, _):
        off = c * BIG_COPY
        make_copy(pl.multiple_of(g + off, RUN_ALIGN), pl.multiple_of(o + off, RUN_ALIGN), BIG_COPY).start()
        return 0

    def small(c, _):
        off = n_big * BIG_COPY + c * RUN_ALIGN
        make_copy(pl.multiple_of(g + off, RUN_ALIGN), pl.multiple_of(o + off, RUN_ALIGN), RUN_ALIGN).start()
        return 0

    lax.fori_loop(0, n_big, big, 0)
    lax.fori_loop(0, n_small, small, 0)
    return n_big, n_small


def _wait_pieces(n_big, n_small, make_copy):
    def wait_big(i, _):
        make_copy(0, 0, BIG_COPY).wait()
        return 0

    def wait_small(i, _):
        make_copy(0, 0, RUN_ALIGN).wait()
        return 0

    lax.fori_loop(0, n_big, wait_big, 0)
    lax.fori_loop(0, n_small, wait_small, 0)


def _start_listed_copies(plan, tile, make_copy):
    big_g, big_o, n_big, small_g, small_o, n_small = plan

    def start_list(g_ref, o_ref, n, width, rows):
        def body(p, _):
            j = tile * width + p
            make_copy(pl.multiple_of(g_ref[j], RUN_ALIGN), pl.multiple_of(o_ref[j], RUN_ALIGN), rows).start()
            return 0

        lax.fori_loop(0, n, body, 0)

    start_list(big_g, big_o, n_big[tile], MAX_BIG, BIG_COPY)
    start_list(small_g, small_o, n_small[tile], MAX_SMALL, RUN_ALIGN)


def _dispatch_kernel(bg_ref, bo_ref, nb_ref, sg_ref, so_ref, ns_ref, zs_ref, zr_ref, ri_ref, h_ref, xs_ref,
                     sorted_ref, zero_ref, sem):
    plan = (bg_ref, bo_ref, nb_ref, sg_ref, so_ref, ns_ref)
    tile = pl.program_id(0)
    last = pl.num_programs(0) - 1
    slot = tile % 2
    tm = h_ref.shape[0]
    post = ri_ref[...].astype(F32).T
    h = h_ref[...]
    rb_rows = 256
    pos_blk = [jnp.floor(post[TOP_K + k:TOP_K + k + 1, :] * (1.0 / rb_rows)) for k in range(TOP_K)]
    pos_off = [post[TOP_K + k:TOP_K + k + 1, :] - rb_rows * pos_blk[k] for k in range(TOP_K)]
    row = lax.broadcasted_iota(jnp.int32, (rb_rows, tm), 0).astype(F32).astype(BF16)
    one, zero = jnp.ones((), BF16), jnp.zeros((), BF16)

    def fill(rb, _):
        r0 = pl.multiple_of(rb * rb_rows, rb_rows)
        rbf = lax.convert_element_type(rb, F32)
        perm = functools.reduce(jnp.add, [
            jnp.where(row == jnp.where(pos_blk[k] == rbf, pos_off[k], -1.0).astype(BF16), one, zero)
            for k in range(TOP_K)])
        xr = _dot(perm, h)
        lo = lax.bitcast_convert_type(xr[:, :512], jnp.uint32)
        hi = lax.bitcast_convert_type(xr[:, 512:], jnp.uint32)
        sorted_ref[slot, pl.ds(r0, rb_rows), :] = (lo >> 16) | (hi & jnp.uint32(0xFFFF0000))
        return 0

    lax.fori_loop(0, SORTED_ROWS // rb_rows, fill, 0)

    def run_copy(sl):
        def make_copy(g, o, rows):
            return pltpu.make_async_copy(sorted_ref.at[sl, pl.ds(o, rows)], xs_ref.at[pl.ds(g, rows)], sem.at[sl])
        return make_copy

    _start_listed_copies(plan, tile, run_copy(slot))

    @pl.when(tile > 0)
    def _():
        _wait_pieces(nb_ref[tile - 1], ns_ref[tile - 1], run_copy(1 - slot))

    @pl.when(tile == last)
    def _():
        _wait_pieces(nb_ref[tile], ns_ref[tile], run_copy(slot))
        zero_ref[...] = jnp.zeros_like(zero_ref)

        def zero_copy(g, o, rows):
            return pltpu.make_async_copy(zero_ref.at[pl.ds(0, rows)], xs_ref.at[pl.ds(g, rows)], sem.at[2])

        def per_expert(e, totals):
            nb, ns = _start_pieces(zs_ref[e], 0, zr_ref[e], zero_copy)
            return totals[0] + nb, totals[1] + ns

        zb, zs = lax.fori_loop(0, zs_ref.shape[0], per_expert, (0, 0))
        _wait_pieces(zb, zs, zero_copy)


def _dispatch(plan, zstart, zrows, ri, h2, cap):
    t, d = h2.shape
    tm = TM_OUT
    return pl.pallas_call(
        _dispatch_kernel,
        grid_spec=pltpu.PrefetchScalarGridSpec(
            num_scalar_prefetch=8,
            grid=(t // tm,),
            in_specs=[pl.BlockSpec((tm, LANES), lambda i, *_: (i, 0)),
                      pl.BlockSpec((tm, d), lambda i, *_: (i, 0))],
            out_specs=pl.BlockSpec(memory_space=pl.ANY),
            scratch_shapes=[pltpu.VMEM((2, SORTED_ROWS, d // 2), jnp.uint32),
                            pltpu.VMEM((BIG_COPY, d // 2), jnp.uint32),
                            pltpu.SemaphoreType.DMA((3,))]),
        out_shape=jax.ShapeDtypeStruct((cap, d // 2), jnp.uint32),
        compiler_params=pltpu.CompilerParams(dimension_semantics=("arbitrary",),
                                             vmem_limit_bytes=VMEM_LIMIT),
        name="dispatch",
    )(*plan, zstart, zrows, ri, h2)


def _pack_bf16_pairs(x):
    n = x.shape[1] // 2
    bits = lax.bitcast_convert_type(x.astype(BF16).astype(F32), jnp.uint32)
    return (bits[:, :n] >> 16) | (bits[:, n:] & jnp.uint32(0xFFFF0000))


def _unpack_bf16_pairs(w):
    lo = lax.bitcast_convert_type(w << 16, F32).astype(BF16)
    hi = lax.bitcast_convert_type(w & jnp.uint32(0xFFFF0000), F32).astype(BF16)
    return lo, hi


def _experts_kernel(be_ref, nu_ref, slot_ref, nxt_ref, nv_ref, xs_ref, wgu_hbm, bgu_ref, wd_hbm, bd_ref, ys_ref,
                    wgu_f, wd_f, wgu_b, wd_b, sem):
    del nu_ref
    i = pl.program_id(0)
    e = be_ref[i]
    slot = slot_ref[i]
    valid = nv_ref[i]
    used = valid > 0
    run_start = used & ((i == 0) | (be_ref[jnp.maximum(i - 1, 0)] != e))

    def weight_copies(ex, sl):
        return (pltpu.make_async_copy(wgu_hbm.at[ex], wgu_f.at[sl], sem.at[0, sl]),
                pltpu.make_async_copy(wd_hbm.at[ex], wd_f.at[sl], sem.at[1, sl]))

    @pl.when(i == 0)
    def _():
        for cp in weight_copies(e, slot):
            cp.start()

    @pl.when(run_start)
    def _():
        for cp in weight_copies(e, slot):
            cp.wait()
        nxt = nxt_ref[i]

        @pl.when(nxt >= 0)
        def _():
            for cp in weight_copies(nxt, 1 - slot):
                cp.start()

        wgu_b[...] = wgu_f[slot].astype(BF16)
        wd_b[...] = wd_f[slot].astype(BF16)

    for j in range(MOE_BLOCK // EXPERT_SUB):
        r = slice(j * EXPERT_SUB, (j + 1) * EXPERT_SUB)
        live = valid > j * EXPERT_SUB

        @pl.when(live)
        def _(r=r):
            x_lo, x_hi = _unpack_bf16_pairs(xs_ref[r, :])
            gu = _dot(x_lo, wgu_b[0:512, :]) + _dot(x_hi, wgu_b[512:, :]) + bgu_ref[0]
            gate = jnp.minimum(gu[:, :D_FF], SWIGLU_LIMIT)
            lin = jnp.clip(gu[:, D_FF:], -SWIGLU_LIMIT, SWIGLU_LIMIT)
            act = (lin + 1.0) * (gate * jax.nn.sigmoid(SWIGLU_ALPHA * gate))
            ys_ref[r, :] = _pack_bf16_pairs(_dot(act.astype(BF16), wd_b[...]) + bd_ref[0])

        @pl.when(jnp.logical_not(live))
        def _(r=r):
            ys_ref[r, :] = jnp.zeros((EXPERT_SUB, D_MODEL // 2), jnp.uint32)


def _experts(block_e, n_used, slot, nxt, nvalid, xs, wgu, bgu, wd, bd):
    cap = xs.shape[0]
    nblk = cap // MOE_BLOCK
    return pl.pallas_call(
        _experts_kernel,
        grid_spec=pltpu.PrefetchScalarGridSpec(
            num_scalar_prefetch=5,
            grid=(nblk,),
            in_specs=[pl.BlockSpec((MOE_BLOCK, 512), lambda i, be, nu, *_: (jnp.minimum(i, nu[0] - 1), 0)),
                      pl.BlockSpec(memory_space=pl.ANY),
                      pl.BlockSpec((1, 1, 2 * D_FF), lambda i, be, *_: (be[i], 0, 0)),
                      pl.BlockSpec(memory_space=pl.ANY),
                      pl.BlockSpec((1, 1, D_MODEL), lambda i, be, *_: (be[i], 0, 0))],
            out_specs=pl.BlockSpec((MOE_BLOCK, D_MODEL // 2), lambda i, *_: (i, 0)),
            scratch_shapes=[pltpu.VMEM((2, D_MODEL, 2 * D_FF), F32), pltpu.VMEM((2, D_FF, D_MODEL), F32),
                            pltpu.VMEM((D_MODEL, 2 * D_FF), BF16), pltpu.VMEM((D_FF, D_MODEL), BF16),
                            pltpu.SemaphoreType.DMA((2, 2))]),
        out_shape=jax.ShapeDtypeStruct((cap, D_MODEL // 2), jnp.uint32),
        compiler_params=pltpu.CompilerParams(dimension_semantics=("arbitrary",),
                                             vmem_limit_bytes=VMEM_LIMIT),
        name="experts",
    )(block_e, n_used, slot, nxt, nvalid, xs, wgu, bgu, wd, bd)


def _combine_kernel(bg_ref, bo_ref, nb_ref, sg_ref, so_ref, ns_ref, ys_ref, ri_ref, rw_ref, x1_ref, gate_ref, g_ref,
                    o_ref, ybuf, sem):
    plan = (bg_ref, bo_ref, nb_ref, sg_ref, so_ref, ns_ref)
    nt = pl.num_programs(1)
    tile = pl.program_id(0) * nt + pl.program_id(1)
    ntiles = pl.num_programs(0) * nt
    slot = tile % 2
    tm = x1_ref.shape[1]

    def run_copy(sl):
        def make_copy(g, o, rows):
            return pltpu.make_async_copy(ys_ref.at[pl.ds(g, rows)], ybuf.at[sl, pl.ds(o, rows)], sem.at[sl])
        return make_copy

    @pl.when(tile == 0)
    def _():
        ybuf[...] = jnp.zeros_like(ybuf)
        _start_listed_copies(plan, tile, run_copy(slot))

    @pl.when(tile + 1 < ntiles)
    def _():
        _start_listed_copies(plan, tile + 1, run_copy(1 - slot))

    _wait_pieces(nb_ref[tile], ns_ref[tile], run_copy(slot))

    posf = ri_ref[0].astype(F32)
    rw = rw_ref[0]
    cb_cols = 256
    pos = [posf[:, TOP_K + k:TOP_K + k + 1] for k in range(TOP_K)]
    pos_blk = [jnp.floor(p * (1.0 / cb_cols)) for p in pos]
    pos_off = [p - cb_cols * b for p, b in zip(pos, pos_blk)]
    wk = [rw[:, k:k + 1].astype(BF16) for k in range(TOP_K)]
    col = lax.broadcasted_iota(jnp.int32, (tm, cb_cols), 1).astype(F32).astype(BF16)
    zero = jnp.zeros((), BF16)
    f_lo = jnp.zeros((tm, D_MODEL // 2), F32)
    f_hi = jnp.zeros((tm, D_MODEL // 2), F32)
    for cb in range(SORTED_ROWS // cb_cols):
        wp = functools.reduce(jnp.add, [
            jnp.where(col == jnp.where(pos_blk[k] == float(cb), pos_off[k], -1.0).astype(BF16), wk[k], zero)
            for k in range(TOP_K)])
        y_lo, y_hi = _unpack_bf16_pairs(ybuf[slot, cb * cb_cols:(cb + 1) * cb_cols, :])
        f_lo = f_lo + _dot(wp, y_lo)
        f_hi = f_hi + _dot(wp, y_hi)
    f = jnp.concatenate([f_lo, f_hi], axis=1)
    o_ref[0] = x1_ref[0] + gate_ref[0] * _rms(f, g_ref[...])


def _combine(plan, ys, ri, rw, x1, gate, g):
    b, s, d = x1.shape
    tm = TM_OUT
    tok = lambda bb, i, *_: (bb, i, 0)
    return pl.pallas_call(
        _combine_kernel,
        grid_spec=pltpu.PrefetchScalarGridSpec(
            num_scalar_prefetch=6,
            grid=(b, s // tm),
            in_specs=[pl.BlockSpec(memory_space=pl.ANY),
                      pl.BlockSpec((1, tm, LANES), tok), pl.BlockSpec((1, tm, LANES), tok),
                      pl.BlockSpec((1, tm, d), tok),
                      pl.BlockSpec((1, 1, d), lambda bb, i, *_: (bb, 0, 0)),
                      pl.BlockSpec((1, d), lambda bb, i, *_: (0, 0))],
            out_specs=pl.BlockSpec((1, tm, d), tok),
            scratch_shapes=[pltpu.VMEM((2, SORTED_ROWS, d // 2), jnp.uint32), pltpu.SemaphoreType.DMA((2,))]),
        out_shape=jax.ShapeDtypeStruct((b, s, d), F32),
        compiler_params=pltpu.CompilerParams(dimension_semantics=("arbitrary", "arbitrary"),
                                             vmem_limit_bytes=VMEM_LIMIT),
        name="combine",
    )(*plan, ys, ri, rw, x1, gate, g)


def _rope_angles(rows, rot_dim):
    row = jnp.broadcast_to(jnp.arange(rows, dtype=F32)[:, None], (rows, GRID_W)).reshape(-1)
    col = jnp.broadcast_to(jnp.arange(GRID_W, dtype=F32)[None, :], (rows, GRID_W)).reshape(-1)
    quarter = rot_dim // 4
    inv_freq = ROPE_BASE ** (-jnp.arange(quarter, dtype=F32) / quarter)
    ang = jnp.concatenate([row[:, None] * inv_freq, col[:, None] * inv_freq], axis=-1)
    return jnp.cos(ang), jnp.sin(ang)


def _rope_tables(s):
    ca, sa = _rope_angles(s // GRID_W, HEAD_DIM)
    cb, sb = _rope_angles(s // GRID_W, MLA_ROPE)
    one = lambda w: jnp.ones((s, w), F32)
    zero = lambda w: jnp.zeros((s, w), F32)
    cos_a = jnp.tile(jnp.concatenate([ca, ca], 1), (1, 2))
    sin_a = jnp.tile(jnp.concatenate([-sa, sa], 1), (1, 2))
    cos_q = jnp.concatenate([one(64), cb, cb, one(32)], 1)
    sin_q = jnp.concatenate([zero(64), -sb, sb, zero(32)], 1)
    cos_k = jnp.concatenate([cb, cb, one(96)], 1)
    sin_k = jnp.concatenate([-sb, sb, zero(96)], 1)
    return cos_a, sin_a, cos_q, sin_q, cos_k, sin_k


def _piece_lists(slot_start, tile_start, counts, first_off, piece, width):
    ne = counts.shape[1]
    cum = jnp.cumsum(counts, axis=1)
    p = jnp.arange(width, dtype=jnp.int32)
    owner = jnp.minimum(jnp.sum((cum[:, None, :] <= p[None, :, None]).astype(jnp.int32), axis=2), ne - 1)
    sel = owner[:, :, None] == jnp.arange(ne, dtype=jnp.int32)[None, None, :]
    pick = lambda a: jnp.sum(jnp.where(sel, a[:, None, :], 0), axis=2)
    off = pick(first_off) + (p[None, :] - pick(cum - counts)) * piece
    as_list = lambda a: a.reshape(-1).astype(jnp.int32)
    return as_list(pick(slot_start) + off), as_list(pick(tile_start) + off), cum[:, -1].astype(jnp.int32)


def _identity_tables(n):
    one, zero = jnp.ones((n, LANES), F32), jnp.zeros((n, LANES), F32)
    return one, zero, one, zero, one, zero


def _inproj_weights(w_in, g_q_a, w_uq, g_kv_a, w_ukv):
    d = w_in.shape[0]
    wq, wk, wv = w_in[:, :512], w_in[:, 512:640], w_in[:, 640:768]
    wcq, wckv, wkr = w_in[:, 768:1152], w_in[:, 1152:1408], w_in[:, 1408:1440]
    wkr_p = jnp.concatenate([wkr, jnp.zeros((d, LANES - MLA_ROPE), F32)], 1)
    wlat = jnp.concatenate([wcq, wckv, wkr_p], 1).astype(BF16)
    wwin = jnp.concatenate([wq, wk, wv], 1).astype(BF16)
    wuq = jnp.pad(w_uq.reshape(MLA_Q_LORA, MLA_HEADS, MLA_NOPE + MLA_ROPE), ((0, 0), (0, 0), (0, 32)))
    wuq = wuq.reshape(MLA_Q_LORA, MLA_HEADS * LANES).astype(BF16)
    wukv = w_ukv.reshape(MLA_KV_LORA, MLA_HEADS, MLA_NOPE + MLA_V)
    wkk = jnp.pad(wukv[:, :, :MLA_NOPE], ((0, 0), (0, 0), (0, 64))).reshape(MLA_KV_LORA, MLA_HEADS * LANES)
    wvv = wukv[:, :, MLA_NOPE:].reshape(MLA_KV_LORA, MLA_HEADS * MLA_V)
    return (wlat, wwin, g_q_a.reshape(1, -1), wuq, g_kv_a.reshape(1, -1), wkk.astype(BF16), wvv.T.astype(BF16))


def kernel(x, c, ctx, c_ctx, w_ada, b_ada, g_mix_pre, g_mix_post, w_in, sink, g_q_a, w_uq, g_kv_a, w_ukv, w_o,
           g_ffn_pre, g_ffn_post, w_router, b_router, w_gate_up, b_gate_up, w_down, b_down):
    b, s, d = x.shape
    cl = ctx.shape[1]
    t = b * s

    cc = jnp.zeros((8, d), F32).at[:b].set(c).at[b].set(c_ctx)
    mod_all = _adaln(cc, w_ada[0], b_ada[0].reshape(1, -1))
    mod = [mod_all[:b, i * d:(i + 1) * d].reshape(b, 1, d) for i in range(6)]
    mod_c = [jnp.broadcast_to(mod_all[b, i * d:(i + 1) * d].reshape(1, 1, d), (b, 1, d)) for i in range(2)]

    wts = _inproj_weights(w_in[0], g_q_a[0], w_uq[0], g_kv_a[0], w_ukv[0])
    gpre = g_mix_pre[0].reshape(1, d)
    qa, ka, vat, q, k, vt = _inproj(x, mod[0], mod[1], gpre, _rope_tables(s), wts, TM_IN)
    _, kac, vatc, _, kc, vct = _inproj(ctx, mod_c[0], mod_c[1], gpre, _identity_tables(cl), wts, cl)

    sinkv = jnp.repeat(sink[0] * LOG2_E, BLOCK).reshape(A_KV_HEADS, 1, -1)
    out_a = _gqa(sinkv, qa, ka, vat, kac, vatc)
    out_b = _mla(q, k, vt, kc, vct)

    wo = w_o[0].astype(BF16)
    wr = jnp.pad(w_router[0], ((0, 0), (0, LANES - N_EXPERTS)))
    wr_hi, wr_lo = _split_bf16(wr)
    br = jnp.concatenate([b_router[0], jnp.full((LANES - N_EXPERTS,), NEG_INF, F32)]).reshape(1, LANES)
    ii = jnp.arange(TM_OUT)
    tri = (ii[:, None] > ii[None, :]).astype(BF16)
    jj = jnp.arange(LANES)
    upper = (jj[:, None] < jj[None, :]).astype(BF16)
    x1, h2, ri, rw, cnt = _outproj(out_a, out_b, x, wo[:512], wo[512:], g_mix_post[0].reshape(1, d), mod[2],
                                   g_ffn_pre[0].reshape(1, d), mod[3], mod[4], wr_hi, wr_lo, br, tri, upper)

    cnt = cnt[:, 0, :N_EXPERTS].astype(jnp.int32)
    nt = cnt.shape[0]
    rows = (cnt + RUN_ALIGN - 1) // RUN_ALIGN * RUN_ALIGN
    tot = jnp.sum(rows, axis=0)
    carry = jnp.cumsum(rows, axis=0) - rows
    padded = (tot + MOE_BLOCK - 1) // MOE_BLOCK * MOE_BLOCK
    pad_end = jnp.cumsum(padded)
    pad_start = pad_end - padded
    nblk = -(-(t * TOP_K + nt * N_EXPERTS * (RUN_ALIGN - 1)) // MOE_BLOCK) + N_EXPERTS
    cap = nblk * MOE_BLOCK
    blk_row = jnp.arange(nblk, dtype=jnp.int32) * MOE_BLOCK
    block_e = jnp.minimum(jnp.sum((pad_end[None, :] <= blk_row[:, None]).astype(jnp.int32), axis=1), N_EXPERTS - 1)
    n_used = (pad_end[-1] // MOE_BLOCK).astype(jnp.int32).reshape(1)
    toff = jnp.cumsum(rows, axis=1) - rows
    gstart = pad_start[None, :] + carry
    n_big = rows // BIG_COPY
    n_small = (rows - n_big * BIG_COPY) // RUN_ALIGN
    big_g, big_o, tot_big = _piece_lists(gstart, toff, n_big, jnp.zeros_like(rows), BIG_COPY, MAX_BIG)
    small_g, small_o, tot_small = _piece_lists(gstart, toff, n_small, n_big * BIG_COPY, RUN_ALIGN, MAX_SMALL)
    plan = (big_g, big_o, tot_big, small_g, small_o, tot_small)
    zstart = jnp.concatenate([pad_start + tot, pad_end[-1:]]).astype(jnp.int32)
    zrows = jnp.concatenate([padded - tot, cap - pad_end[-1:]]).astype(jnp.int32)
    block_e = block_e.astype(jnp.int32)
    run_idx = jnp.cumsum(jnp.concatenate([jnp.zeros((1,), jnp.int32),
                                          (block_e[1:] != block_e[:-1]).astype(jnp.int32)]))
    eid = jnp.arange(N_EXPERTS, dtype=jnp.int32)
    later_used = (tot[None, :] > 0) & (eid[None, :] > eid[:, None])
    next_e = jnp.min(jnp.where(later_used, eid[None, :], N_EXPERTS), axis=1)
    next_e = jnp.where(next_e < N_EXPERTS, next_e, -1).astype(jnp.int32)
    nxt = jnp.sum(jnp.where(block_e[:, None] == eid[None, :], next_e[None, :], 0), axis=1).astype(jnp.int32)

    xs = _dispatch(plan, zstart, zrows, ri.reshape(t, LANES), h2.reshape(t, d), cap)
    run_end = jnp.sum(jnp.where(block_e[:, None] == eid[None, :], (pad_start + tot)[None, :], 0), axis=1)
    nvalid = jnp.where(blk_row < pad_end[-1], jnp.clip(run_end - blk_row, 0, MOE_BLOCK), 0).astype(jnp.int32)
    ys = _experts(block_e, n_used, (run_idx % 2).astype(jnp.int32), nxt, nvalid, xs, w_gate_up[0],
                  b_gate_up[0].reshape(N_EXPERTS, 1, -1), w_down[0], b_down[0].reshape(N_EXPERTS, 1, -1))
    return _combine(plan, ys, ri, rw, x1, mod[5], g_ffn_post[0].reshape(1, d))
```

```python
import functools

import jax
import jax.numpy as jnp
from jax import lax
from jax.experimental import pallas as pl
from jax.experimental.pallas import tpu as pltpu

F32 = jnp.float32
BF16 = jnp.bfloat16

D_MODEL = 1024
GRID_W = 64
HEAD_DIM = 64
A_HEADS = 8
A_KV_HEADS = 2
BLOCK = 128
A_SCALE = HEAD_DIM ** -0.5
MLA_HEADS = 8
MLA_NOPE = 64
MLA_ROPE = 32
MLA_V = 64
MLA_Q_LORA = 384
MLA_KV_LORA = 256
MLA_SCALE = (MLA_NOPE + MLA_ROPE) ** -0.5
LOG2_E = 1.4426950408889634
N_EXPERTS = 32
TOP_K = 4
D_FF = 1024
SWIGLU_LIMIT = 7.0
SWIGLU_ALPHA = 1.702
ROPE_BASE = 10000.0
EPS = 1e-6
NEG_INF = -1e30

LANES = 128
VMEM_LIMIT = 56 * 1024 * 1024

TM_IN = 512
GQA_QB = 4
TQ_MLA = 256
TK_MLA = 512
VT_ROWS = 80
TM_OUT = 512
MOE_BLOCK = 512
EXPERT_SUB = 256
RUN_ALIGN = 8
BIG_COPY = 32
SORTED_ROWS = TM_OUT * TOP_K + N_EXPERTS * RUN_ALIGN
MAX_BIG = SORTED_ROWS // BIG_COPY
MAX_SMALL = N_EXPERTS * (BIG_COPY // RUN_ALIGN - 1)


def _dot(a, b):
    return jnp.dot(a, b, preferred_element_type=F32)


def _dot_nt(a, b):
    return lax.dot_general(a, b, (((1,), (1,)), ((), ())), preferred_element_type=F32)


def _split_bf16(x):
    hi = x.astype(BF16)
    lo = (x - hi.astype(F32)).astype(BF16)
    return hi, lo


def _rms(x, g):
    return x * lax.rsqrt(jnp.mean(x * x, axis=-1, keepdims=True) + EPS) * g


def _rope(x, cos, sin_signed, first_half, half):
    n = x.shape[-1]
    partner = jnp.where(first_half, pltpu.roll(x, n - half, 1), pltpu.roll(x, half, 1))
    return x * cos + partner * sin_signed


def _adaln_kernel(c_ref, w_ref, b_ref, o_ref):
    c = c_ref[...]
    s = c * jax.nn.sigmoid(c)
    s_hi, s_lo = _split_bf16(s)
    w_hi, w_lo = _split_bf16(w_ref[...])
    o_ref[...] = _dot(s_hi, w_hi) + _dot(s_hi, w_lo) + _dot(s_lo, w_hi) + b_ref[...]


def _adaln(cc, w, b):
    n = w.shape[1]
    tn = 1024
    return pl.pallas_call(
        _adaln_kernel,
        grid=(n // tn,),
        in_specs=[pl.BlockSpec((8, D_MODEL), lambda j: (0, 0)),
                  pl.BlockSpec((D_MODEL, tn), lambda j: (0, j)),
                  pl.BlockSpec((1, tn), lambda j: (0, j))],
        out_specs=pl.BlockSpec((8, tn), lambda j: (0, j)),
        out_shape=jax.ShapeDtypeStruct((8, n), F32),
        compiler_params=pltpu.CompilerParams(dimension_semantics=("arbitrary",),
                                             vmem_limit_bytes=VMEM_LIMIT),
        name="adaln",
    )(cc, w, b)


def _inproj_kernel(x_ref, shift_ref, scale_ref, gpre_ref, ca_ref, sa_ref, cq_ref, sq_ref, ck_ref, sk_ref,
                   wlat_ref, wwin_ref, gq_ref, wuq_ref, gkv_ref, wk_ref, wv_ref,
                   qa_ref, ka_ref, vat_ref, q_ref, k_ref, vt_ref):
    x = x_ref[0]
    tm = x.shape[0]
    h = _rms(x, gpre_ref[...]) * (1.0 + scale_ref[0]) + shift_ref[0]
    hb = h.astype(BF16)
    lat = _dot(hb, wlat_ref[...])
    p = _dot(hb, wwin_ref[...])
    c_ka, c_va = A_HEADS * HEAD_DIM, A_HEADS * HEAD_DIM + LANES
    c_ckv, c_kr = MLA_Q_LORA, MLA_Q_LORA + MLA_KV_LORA
    lane = lax.broadcasted_iota(jnp.int32, (tm, LANES), 1)

    first_a = (lane & 32) == 0
    ca, sa = ca_ref[...], sa_ref[...]
    for j in range(A_HEADS * HEAD_DIM // LANES):
        sl = slice(j * LANES, (j + 1) * LANES)
        qa_ref[0, :, sl] = (_rope(p[:, sl], ca, sa, first_a, 32) * (A_SCALE * LOG2_E)).astype(BF16)
    ka_ref[0] = _rope(p[:, c_ka:c_va], ca, sa, first_a, 32).astype(BF16)
    vat = p[:, c_va:c_va + LANES].T
    ones_a = (lax.broadcasted_iota(jnp.int32, (VT_ROWS - HEAD_DIM, BLOCK), 0) == 0).astype(BF16)
    for j in range(tm // BLOCK):
        for hk in range(A_KV_HEADS):
            vat_ref[0, j, hk, 0:HEAD_DIM, :] = vat[hk * HEAD_DIM:(hk + 1) * HEAD_DIM,
                                                   j * BLOCK:(j + 1) * BLOCK].astype(BF16)
            vat_ref[0, j, hk, HEAD_DIM:VT_ROWS, :] = ones_a

    cqn = _rms(lat[:, 0:c_ckv], gq_ref[...]).astype(BF16)
    qf = _dot(cqn, wuq_ref[...])
    first_q = (lane >= 64) & (lane < 80)
    cq, sq = cq_ref[...], sq_ref[...]
    for hh in range(MLA_HEADS):
        sl = slice(hh * LANES, (hh + 1) * LANES)
        q_ref[0, :, sl] = (_rope(qf[:, sl], cq, sq, first_q, 16) * (MLA_SCALE * LOG2_E)).astype(BF16)

    ckvn = _rms(lat[:, c_ckv:c_kr], gkv_ref[...]).astype(BF16)
    kr = _rope(lat[:, c_kr:c_kr + LANES], ck_ref[...], sk_ref[...], lane < 16, 16)
    kr = pltpu.roll(kr, MLA_NOPE, 1)
    kn = _dot(ckvn, wk_ref[...])
    for hh in range(MLA_HEADS):
        sl = slice(hh * LANES, (hh + 1) * LANES)
        k_ref[0, :, sl] = (kn[:, sl] + kr).astype(BF16)
    vt = _dot_nt(wv_ref[...], ckvn).astype(BF16)
    ones_row = (lax.broadcasted_iota(jnp.int32, (VT_ROWS - MLA_V, tm), 0) == 0).astype(BF16)
    for hh in range(MLA_HEADS):
        vt_ref[0, 0, hh, 0:MLA_V, :] = vt[hh * MLA_V:(hh + 1) * MLA_V, :]
        vt_ref[0, 0, hh, MLA_V:VT_ROWS, :] = ones_row


def _inproj(x, shift, scale, gpre, tabs, wts, tm):
    bx, n, d = x.shape
    ca, sa, cq, sq, ck, sk = tabs
    wlat, wwin, gq, wuq, gkv, wk, wv = wts
    tok = lambda i, b: (b, i, 0)
    vec = lambda i, b: (b, 0, 0)
    tab = lambda i, b: (i, 0)
    cst = lambda i, b: (0, 0)
    in_specs = [pl.BlockSpec((1, tm, d), tok), pl.BlockSpec((1, 1, d), vec), pl.BlockSpec((1, 1, d), vec),
                pl.BlockSpec((1, d), cst)]
    in_specs += [pl.BlockSpec((tm, LANES), tab)] * 6
    in_specs += [pl.BlockSpec(w.shape, cst) for w in wts]
    row_out = lambda w: (pl.BlockSpec((1, tm, w), tok), jax.ShapeDtypeStruct((bx, n, w), BF16))
    outs = [row_out(A_HEADS * HEAD_DIM), row_out(LANES),
            (pl.BlockSpec((1, tm // BLOCK, A_KV_HEADS, VT_ROWS, BLOCK), lambda i, b: (b, i, 0, 0, 0)),
             jax.ShapeDtypeStruct((bx, n // BLOCK, A_KV_HEADS, VT_ROWS, BLOCK), BF16)),
            row_out(MLA_HEADS * LANES), row_out(MLA_HEADS * LANES),
            (pl.BlockSpec((1, 1, MLA_HEADS, VT_ROWS, tm), lambda i, b: (b, i, 0, 0, 0)),
             jax.ShapeDtypeStruct((bx, n // tm, MLA_HEADS, VT_ROWS, tm), BF16))]
    out_specs = [o[0] for o in outs]
    out_shape = [o[1] for o in outs]
    return pl.pallas_call(
        _inproj_kernel,
        grid=(n // tm, bx),
        in_specs=in_specs,
        out_specs=out_specs,
        out_shape=out_shape,
        compiler_params=pltpu.CompilerParams(dimension_semantics=("arbitrary", "arbitrary"),
                                             vmem_limit_bytes=VMEM_LIMIT),
        name="inproj",
    )(x, shift, scale, gpre, ca, sa, cq, sq, ck, sk, wlat, wwin, gq, wuq, gkv, wk, wv)


def _gqa_kernel(sink_ref, q_ref, *refs):
    nwin = GQA_QB + 2
    k_refs, kx_ref = refs[:nwin], refs[nwin]
    v_refs, vx_ref = refs[nwin + 1:2 * nwin + 1], refs[2 * nwin + 1]
    o_ref = refs[2 * nwin + 2]
    n = pl.program_id(1)
    nsteps = pl.num_programs(1)
    nctx = kx_ref.shape[1] // BLOCK
    group = A_HEADS // A_KV_HEADS
    key = lax.broadcasted_iota(jnp.int32, (BLOCK, group * BLOCK), 0)
    qry = lax.broadcasted_iota(jnp.int32, (BLOCK, group * BLOCK), 1) & (BLOCK - 1)
    sts = {}
    for qb in range(GQA_QB):
        kcat = jnp.concatenate([k_refs[qb + j][0] for j in range(3)] + [kx_ref[0]], axis=0)
        for hk in range(A_KV_HEADS):
            qg = jnp.concatenate([q_ref[0, qb * BLOCK:(qb + 1) * BLOCK, hh * HEAD_DIM:(hh + 1) * HEAD_DIM]
                                  for hh in range(hk * group, (hk + 1) * group)], axis=0)
            k_hk = kcat[:, hk * HEAD_DIM:(hk + 1) * HEAD_DIM]
            sts[qb, hk] = _dot_nt(k_hk, qg)
    for qb in range(GQA_QB):
        vis_prev = key >= qry
        vis_next = key <= qry
        if qb == 0:
            vis_prev = key >= qry + jnp.where(n > 0, 0, 2 * BLOCK)
        if qb == GQA_QB - 1:
            vis_next = key <= qry - jnp.where(n < nsteps - 1, 0, 2 * BLOCK)
        outs = []
        for hk in range(A_KV_HEADS):
            st = sts[qb, hk]
            pieces = [jnp.where(vis_prev, st[0:BLOCK], NEG_INF), st[BLOCK:2 * BLOCK],
                      jnp.where(vis_next, st[2 * BLOCK:3 * BLOCK], NEG_INF)]
            pieces += [st[(3 + j) * BLOCK:(4 + j) * BLOCK] for j in range(nctx)]
            sink = sink_ref[hk]
            m = jnp.maximum(functools.reduce(jnp.maximum, [jnp.max(p, axis=0, keepdims=True) for p in pieces]),
                            sink)
            ps = [jnp.exp2(p - m).astype(BF16) for p in pieces]
            vts = [v_refs[qb + j][0, 0, hk] for j in range(3)] + [vx_ref[0, j, hk] for j in range(nctx)]
            pv = functools.reduce(jnp.add, [_dot(vt, p) for vt, p in zip(vts, ps)])
            l = pv[HEAD_DIM:HEAD_DIM + 1, :] + jnp.exp2(sink - m)
            o = pv[0:HEAD_DIM, :] * (1.0 / l)
            outs += [o[:, g * BLOCK:(g + 1) * BLOCK] for g in range(group)]
        for pair in range(A_HEADS // 2):
            both = jnp.concatenate([outs[2 * pair], outs[2 * pair + 1]], axis=0)
            o_ref[0, qb * BLOCK:(qb + 1) * BLOCK, pair * LANES:(pair + 1) * LANES] = both.T.astype(BF16)


def _gqa(sinkv, qa, ka, vat, kac, vatc):
    b, s, _ = qa.shape
    nb = s // BLOCK
    c = kac.shape[1]
    tq = GQA_QB * BLOCK
    cur = lambda bb, n: (bb, n, 0)
    win = lambda j: (lambda bb, n: (bb, jnp.clip(n * GQA_QB + j - 1, 0, nb - 1), 0))
    vwin = lambda j: (lambda bb, n: (bb, jnp.clip(n * GQA_QB + j - 1, 0, nb - 1), 0, 0, 0))
    vblk = (1, 1, A_KV_HEADS, VT_ROWS, BLOCK)
    in_specs = [pl.BlockSpec(sinkv.shape, lambda bb, n: (0, 0, 0)), pl.BlockSpec((1, tq, A_HEADS * HEAD_DIM), cur)]
    in_specs += [pl.BlockSpec((1, BLOCK, LANES), win(j)) for j in range(GQA_QB + 2)]
    in_specs += [pl.BlockSpec((1, c, LANES), lambda bb, n: (bb, 0, 0))]
    in_specs += [pl.BlockSpec(vblk, vwin(j)) for j in range(GQA_QB + 2)]
    in_specs += [pl.BlockSpec((1, c // BLOCK, A_KV_HEADS, VT_ROWS, BLOCK), lambda bb, n: (bb, 0, 0, 0, 0))]
    return pl.pallas_call(
        _gqa_kernel,
        grid=(b, nb // GQA_QB),
        in_specs=in_specs,
        out_specs=pl.BlockSpec((1, tq, 512), cur),
        out_shape=jax.ShapeDtypeStruct((b, s, 512), BF16),
        compiler_params=pltpu.CompilerParams(dimension_semantics=("arbitrary", "arbitrary"),
                                             vmem_limit_bytes=VMEM_LIMIT),
        name="gqa",
    )(sinkv, qa, *([ka] * (GQA_QB + 2)), kac, *([vat] * (GQA_QB + 2)), vatc)


def _mla_kernel(q_ref, k_ref, vt_ref, kc_ref, vct_ref, o_ref, m_ref, acc_ref, sa_ref, sb_ref):
    nk = k_ref.shape[1] // TK_MLA

    def scores_into(buf, k_blk):
        n = k_blk.shape[0]
        for hh in range(MLA_HEADS):
            hsl = slice(hh * LANES, (hh + 1) * LANES)
            buf[hh, 0:n, :] = _dot_nt(k_blk[:, hsl], q_ref[0, :, hsl])

    def consume(buf, vt_blk, first):
        n = vt_blk.shape[2]
        for hh in range(MLA_HEADS):
            st = buf[hh, 0:n, :]
            cmax = jnp.max(st, axis=0, keepdims=True)
            if first:
                m_new = cmax
            else:
                m_old = m_ref[hh, 0:1, :]
                m_new = jnp.maximum(m_old, cmax)
                alpha = jnp.exp2(m_old - m_new)
            p = jnp.exp2(st - m_new).astype(BF16)
            pv = _dot(vt_blk[hh], p)
            acc_ref[hh] = pv if first else alpha * acc_ref[hh] + pv
            m_ref[hh, 0:1, :] = m_new

    def k_chunk(c):
        return k_ref[0, pl.ds(pl.multiple_of(c * TK_MLA, TK_MLA), TK_MLA), :]

    scores_into(sa_ref, k_chunk(0))
    scores_into(sb_ref, k_chunk(1))
    consume(sa_ref, vt_ref[0, 0], True)

    def body(i, _):
        c = 2 * i + 1
        scores_into(sa_ref, k_chunk(c + 1))
        consume(sb_ref, vt_ref[0, c], False)
        scores_into(sb_ref, k_chunk(c + 2))
        consume(sa_ref, vt_ref[0, c + 1], False)
        return 0

    lax.fori_loop(0, (nk - 2) // 2, body, 0)
    scores_into(sa_ref, kc_ref[0])
    consume(sb_ref, vt_ref[0, nk - 1], False)
    consume(sa_ref, vct_ref[0, 0], False)

    outs = [acc_ref[hh, 0:MLA_V, :] * (1.0 / acc_ref[hh, MLA_V:MLA_V + 1, :]) for hh in range(MLA_HEADS)]
    o_ref[0] = jnp.concatenate(outs, axis=0).T.astype(BF16)


def _mla(q, k, vt, kc, vct):
    b, s, _ = q.shape
    c = kc.shape[1]
    nk = s // TK_MLA
    assert vt.shape == (b, nk, MLA_HEADS, VT_ROWS, TK_MLA) and nk % 2 == 0
    return pl.pallas_call(
        _mla_kernel,
        grid=(b, s // TQ_MLA),
        in_specs=[pl.BlockSpec((1, TQ_MLA, 1024), lambda bb, i: (bb, i, 0)),
                  pl.BlockSpec((1, s, 1024), lambda bb, i: (bb, 0, 0)),
                  pl.BlockSpec((1, nk, MLA_HEADS, VT_ROWS, TK_MLA), lambda bb, i: (bb, 0, 0, 0, 0)),
                  pl.BlockSpec((1, c, 1024), lambda bb, i: (bb, 0, 0)),
                  pl.BlockSpec((1, 1, MLA_HEADS, VT_ROWS, c), lambda bb, i: (bb, 0, 0, 0, 0))],
        out_specs=pl.BlockSpec((1, TQ_MLA, 512), lambda bb, i: (bb, i, 0)),
        out_shape=jax.ShapeDtypeStruct((b, s, 512), BF16),
        scratch_shapes=[pltpu.VMEM((MLA_HEADS, 8, TQ_MLA), F32),
                        pltpu.VMEM((MLA_HEADS, VT_ROWS, TQ_MLA), F32),
                        pltpu.VMEM((MLA_HEADS, TK_MLA, TQ_MLA), F32), pltpu.VMEM((MLA_HEADS, TK_MLA, TQ_MLA), F32)],
        compiler_params=pltpu.CompilerParams(dimension_semantics=("arbitrary", "arbitrary"),
                                             vmem_limit_bytes=VMEM_LIMIT),
        name="mla",
    )(q, k, vt, kc, vct)


def _outproj_kernel(oa_ref, ob_ref, x_ref, woa_ref, wob_ref, gpost_ref, gate_ref, gffn_ref, shift_ref, scale_ref,
                    wrh_ref, wrl_ref, br_ref, tri_ref, upper_ref,
                    x1_ref, h2_ref, ri_ref, rw_ref, cnt_ref):
    tm = x_ref.shape[1]
    nsub = 2
    sub = tm // nsub
    rows = [slice(j * sub, (j + 1) * sub) for j in range(nsub)]
    ys = [_dot(oa_ref[0, r, :], woa_ref[...]) + _dot(ob_ref[0, r, :], wob_ref[...]) for r in rows]
    sub_lane_f = lax.broadcasted_iota(jnp.int32, (sub, LANES), 1).astype(F32)
    idx_parts, w_parts = [], []
    for r, y in zip(rows, ys):
        x1 = x_ref[0, r, :] + gate_ref[0] * _rms(y, gpost_ref[...])
        x1_ref[0, r, :] = x1
        h2 = _rms(x1, gffn_ref[...]) * (1.0 + scale_ref[0]) + shift_ref[0]
        h2_ref[0, r, :] = h2.astype(BF16)
        h_hi, h_lo = _split_bf16(h2)
        cur = _dot(h_hi, wrh_ref[...]) + _dot(h_hi, wrl_ref[...]) + _dot(h_lo, wrh_ref[...]) + br_ref[...]
        tops, idxs = [], []
        for _ in range(TOP_K):
            mk = jnp.max(cur, axis=-1, keepdims=True)
            ik = jnp.min(jnp.where(cur == mk, sub_lane_f, float(LANES)), axis=-1, keepdims=True)
            tops.append(mk)
            idxs.append(ik)
            cur = jnp.where(sub_lane_f == ik, -jnp.inf, cur)
        es = [jnp.exp(t - tops[0]) for t in tops]
        inv = 1.0 / functools.reduce(jnp.add, es)
        idx_parts.append(idxs)
        w_parts.append([e * inv for e in es])

    hot_parts = [[(sub_lane_f == ik) for ik in idxs] for idxs in idx_parts]
    onehot = jnp.concatenate([functools.reduce(jnp.add, [o.astype(F32) for o in hots]) for hots in hot_parts],
                             axis=0)
    cnt = jnp.sum(onehot, axis=0, keepdims=True)
    prefix = _dot(tri_ref[...], onehot.astype(BF16))
    chunks = jnp.floor((cnt + (RUN_ALIGN - 1.0)) * (1.0 / RUN_ALIGN))
    run_start = _dot(jnp.broadcast_to(chunks, (8, LANES)).astype(BF16), upper_ref[...])[0:1, :] * RUN_ALIGN
    base = prefix + run_start
    sub_lane = lax.broadcasted_iota(jnp.int32, (sub, LANES), 1)
    for r, hots, idxs, wts in zip(rows, hot_parts, idx_parts, w_parts):
        ri = jnp.zeros((sub, LANES), jnp.int32)
        rw = jnp.zeros((sub, LANES), F32)
        for k in range(TOP_K):
            pos = jnp.sum(jnp.where(hots[k], base[r, :], 0.0), axis=-1, keepdims=True).astype(jnp.int32)
            ri = jnp.where(sub_lane == k, idxs[k].astype(jnp.int32), ri)
            ri = jnp.where(sub_lane == TOP_K + k, pos, ri)
            rw = jnp.where(sub_lane == k, wts[k], rw)
        ri_ref[0, r, :] = ri
        rw_ref[0, r, :] = rw
    cnt_ref[0] = cnt


def _outproj(oa, ob, x, woa, wob, gpost, gate, gffn, shift, scale, wrh, wrl, br, tri, upper):
    b, s, d = x.shape
    tm = TM_OUT
    nt = s // tm
    tok = lambda bb, i: (bb, i, 0)
    vec = lambda bb, i: (bb, 0, 0)
    cst = lambda bb, i: (0, 0)
    return pl.pallas_call(
        _outproj_kernel,
        grid=(b, nt),
        in_specs=[pl.BlockSpec((1, tm, 512), tok), pl.BlockSpec((1, tm, 512), tok), pl.BlockSpec((1, tm, d), tok),
                  pl.BlockSpec((512, d), cst), pl.BlockSpec((512, d), cst), pl.BlockSpec((1, d), cst),
                  pl.BlockSpec((1, 1, d), vec), pl.BlockSpec((1, d), cst),
                  pl.BlockSpec((1, 1, d), vec), pl.BlockSpec((1, 1, d), vec),
                  pl.BlockSpec((d, LANES), cst), pl.BlockSpec((d, LANES), cst), pl.BlockSpec((1, LANES), cst),
                  pl.BlockSpec((tm, tm), cst), pl.BlockSpec((LANES, LANES), cst)],
        out_specs=[pl.BlockSpec((1, tm, d), tok), pl.BlockSpec((1, tm, d), tok),
                   pl.BlockSpec((1, tm, LANES), tok), pl.BlockSpec((1, tm, LANES), tok),
                   pl.BlockSpec((1, 1, LANES), lambda bb, i: (bb * nt + i, 0, 0))],
        out_shape=[jax.ShapeDtypeStruct((b, s, d), F32), jax.ShapeDtypeStruct((b, s, d), BF16),
                   jax.ShapeDtypeStruct((b, s, LANES), jnp.int32), jax.ShapeDtypeStruct((b, s, LANES), F32),
                   jax.ShapeDtypeStruct((b * nt, 1, LANES), F32)],
        compiler_params=pltpu.CompilerParams(dimension_semantics=("arbitrary", "arbitrary"),
                                             vmem_limit_bytes=VMEM_LIMIT),
        name="outproj",
    )(oa, ob, x, woa, wob, gpost, gate, gffn, shift, scale, wrh, wrl, br, tri, upper)


def _start_pieces(g, o, rows, make_copy):
    n_big = lax.shift_right_logical(rows, BIG_COPY.bit_length() - 1)
    n_small = lax.shift_right_logical(rows, RUN_ALIGN.bit_length() - 1) & (BIG_COPY // RUN_ALIGN - 1)

    def big(c, _):
        off = c * BIG_COPY
        make_copy(pl.multiple_of(g + off, RUN_ALIGN), pl.multiple_of(o + off, RUN_ALIGN), BIG_COPY).start()
        return 0

    def small(c, _):
        off = n_big * BIG_COPY + c * RUN_ALIGN
        make_copy(pl.multiple_of(g + off, RUN_ALIGN), pl.multiple_of(o + off, RUN_ALIGN), RUN_ALIGN).start()
        return 0

    lax.fori_loop(0, n_big, big, 0)
    lax.fori_loop(0, n_small, small, 0)
    return n_big, n_small


def _wait_pieces(n_big, n_small, make_copy):
    def wait_big(i, _):
        make_copy(0, 0, BIG_COPY).wait()
        return 0

    def wait_small(i, _):
        make_copy(0, 0, RUN_ALIGN).wait()
        return 0

    lax.fori_loop(0, n_big, wait_big, 0)
    lax.fori_loop(0, n_small, wait_small, 0)


def _start_listed_copies(plan, tile, make_copy):
    big_g, big_o, n_big, small_g, small_o, n_small = plan

    def start_list(g_ref, o_ref, n, width, rows):
        def body(p, _):
            j = tile * width + p
            make_copy(pl.multiple_of(g_ref[j], RUN_ALIGN), pl.multiple_of(o_ref[j], RUN_ALIGN), rows).start()
            return 0

        lax.fori_loop(0, n, body, 0)

    start_list(big_g, big_o, n_big[tile], MAX_BIG, BIG_COPY)
    start_list(small_g, small_o, n_small[tile], MAX_SMALL, RUN_ALIGN)


def _dispatch_kernel(bg_ref, bo_ref, nb_ref, sg_ref, so_ref, ns_ref, zs_ref, zr_ref, ri_ref, h_ref, xs_ref,
                     sorted_ref, zero_ref, sem):
    plan = (bg_ref, bo_ref, nb_ref, sg_ref, so_ref, ns_ref)
    tile = pl.program_id(0)
    last = pl.num_programs(0) - 1
    slot = tile % 2
    tm = h_ref.shape[0]
    post = ri_ref[...].astype(F32).T
    h = h_ref[...]
    rb_rows = 256
    pos_blk = [jnp.floor(post[TOP_K + k:TOP_K + k + 1, :] * (1.0 / rb_rows)) for k in range(TOP_K)]
    pos_off = [post[TOP_K + k:TOP_K + k + 1, :] - rb_rows * pos_blk[k] for k in range(TOP_K)]
    row = lax.broadcasted_iota(jnp.int32, (rb_rows, tm), 0).astype(F32).astype(BF16)
    one, zero = jnp.ones((), BF16), jnp.zeros((), BF16)

    for rb in range(SORTED_ROWS // rb_rows):
        perm = functools.reduce(jnp.add, [
            jnp.where(row == jnp.where(pos_blk[k] == float(rb), pos_off[k], -1.0).astype(BF16), one, zero)
            for k in range(TOP_K)])
        xr = _dot(perm, h)
        lo = lax.bitcast_convert_type(xr[:, :512], jnp.uint32)
        hi = lax.bitcast_convert_type(xr[:, 512:], jnp.uint32)
        sorted_ref[slot, rb * rb_rows:(rb + 1) * rb_rows, :] = (lo >> 16) | (hi & jnp.uint32(0xFFFF0000))

    def run_copy(sl):
        def make_copy(g, o, rows):
            return pltpu.make_async_copy(sorted_ref.at[sl, pl.ds(o, rows)], xs_ref.at[pl.ds(g, rows)], sem.at[sl])
        return make_copy

    _start_listed_copies(plan, tile, run_copy(slot))

    @pl.when(tile > 0)
    def _():
        _wait_pieces(nb_ref[tile - 1], ns_ref[tile - 1], run_copy(1 - slot))

    @pl.when(tile == last)
    def _():
        _wait_pieces(nb_ref[tile], ns_ref[tile], run_copy(slot))
        zero_ref[...] = jnp.zeros_like(zero_ref)

        def zero_copy(g, o, rows):
            return pltpu.make_async_copy(zero_ref.at[pl.ds(0, rows)], xs_ref.at[pl.ds(g, rows)], sem.at[2])

        def per_expert(e, totals):
            nb, ns = _start_pieces(zs_ref[e], 0, zr_ref[e], zero_copy)
            return totals[0] + nb, totals[1] + ns

        zb, zs = lax.fori_loop(0, zs_ref.shape[0], per_expert, (0, 0))
        _wait_pieces(zb, zs, zero_copy)


def _dispatch(plan, zstart, zrows, ri, h2, cap):
    t, d = h2.shape
    tm = TM_OUT
    return pl.pallas_call(
        _dispatch_kernel,
        grid_spec=pltpu.PrefetchScalarGridSpec(
            num_scalar_prefetch=8,
            grid=(t // tm,),
            in_specs=[pl.BlockSpec((tm, LANES), lambda i, *_: (i, 0)),
                      pl.BlockSpec((tm, d), lambda i, *_: (i, 0))],
            out_specs=pl.BlockSpec(memory_space=pl.ANY),
            scratch_shapes=[pltpu.VMEM((2, SORTED_ROWS, d // 2), jnp.uint32),
                            pltpu.VMEM((BIG_COPY, d // 2), jnp.uint32),
                            pltpu.SemaphoreType.DMA((3,))]),
        out_shape=jax.ShapeDtypeStruct((cap, d // 2), jnp.uint32),
        compiler_params=pltpu.CompilerParams(dimension_semantics=("arbitrary",),
                                             vmem_limit_bytes=VMEM_LIMIT),
        name="dispatch",
    )(*plan, zstart, zrows, ri, h2)


def _pack_bf16_pairs(x):
    n = x.shape[1] // 2
    bits = lax.bitcast_convert_type(x.astype(BF16).astype(F32), jnp.uint32)
    return (bits[:, :n] >> 16) | (bits[:, n:] & jnp.uint32(0xFFFF0000))


def _unpack_bf16_pairs(w):
    lo = lax.bitcast_convert_type(w << 16, F32).astype(BF16)
    hi = lax.bitcast_convert_type(w & jnp.uint32(0xFFFF0000), F32).astype(BF16)
    return lo, hi


def _experts_kernel(be_ref, nu_ref, slot_ref, nxt_ref, nv_ref, xs_ref, wgu_hbm, bgu_ref, wd_hbm, bd_ref, ys_ref,
                    wgu_f, wd_f, wgu_b, wd_b, sem):
    del nu_ref
    i = pl.program_id(0)
    e = be_ref[i]
    slot = slot_ref[i]
    valid = nv_ref[i]
    used = valid > 0
    run_start = used & ((i == 0) | (be_ref[jnp.maximum(i - 1, 0)] != e))

    def weight_copies(ex, sl):
        return (pltpu.make_async_copy(wgu_hbm.at[ex], wgu_f.at[sl], sem.at[0, sl]),
                pltpu.make_async_copy(wd_hbm.at[ex], wd_f.at[sl], sem.at[1, sl]))

    @pl.when(i == 0)
    def _():
        for cp in weight_copies(e, slot):
            cp.start()

    @pl.when(run_start)
    def _():
        for cp in weight_copies(e, slot):
            cp.wait()
        nxt = nxt_ref[i]

        @pl.when(nxt >= 0)
        def _():
            for cp in weight_copies(nxt, 1 - slot):
                cp.start()

        wgu_b[...] = wgu_f[slot].astype(BF16)
        wd_b[...] = wd_f[slot].astype(BF16)

    for j in range(MOE_BLOCK // EXPERT_SUB):
        r = slice(j * EXPERT_SUB, (j + 1) * EXPERT_SUB)
        live = valid > j * EXPERT_SUB

        @pl.when(live)
        def _(r=r):
            x_lo, x_hi = _unpack_bf16_pairs(xs_ref[r, :])
            gu = _dot(x_lo, wgu_b[0:512, :]) + _dot(x_hi, wgu_b[512:, :]) + bgu_ref[0]
            gate = jnp.minimum(gu[:, :D_FF], SWIGLU_LIMIT)
            lin = jnp.clip(gu[:, D_FF:], -SWIGLU_LIMIT, SWIGLU_LIMIT)
            act = (lin + 1.0) * (gate * jax.nn.sigmoid(SWIGLU_ALPHA * gate))
            ys_ref[r, :] = _pack_bf16_pairs(_dot(act.astype(BF16), wd_b[...]) + bd_ref[0])

        @pl.when(jnp.logical_not(live))
        def _(r=r):
            ys_ref[r, :] = jnp.zeros((EXPERT_SUB, D_MODEL // 2), jnp.uint32)


def _experts(block_e, n_used, slot, nxt, nvalid, xs, wgu, bgu, wd, bd):
    cap = xs.shape[0]
    nblk = cap // MOE_BLOCK
    return pl.pallas_call(
        _experts_kernel,
        grid_spec=pltpu.PrefetchScalarGridSpec(
            num_scalar_prefetch=5,
            grid=(nblk,),
            in_specs=[pl.BlockSpec((MOE_BLOCK, 512), lambda i, be, nu, *_: (jnp.minimum(i, nu[0] - 1), 0)),
                      pl.BlockSpec(memory_space=pl.ANY),
                      pl.BlockSpec((1, 1, 2 * D_FF), lambda i, be, *_: (be[i], 0, 0)),
                      pl.BlockSpec(memory_space=pl.ANY),
                      pl.BlockSpec((1, 1, D_MODEL), lambda i, be, *_: (be[i], 0, 0))],
            out_specs=pl.BlockSpec((MOE_BLOCK, D_MODEL // 2), lambda i, *_: (i, 0)),
            scratch_shapes=[pltpu.VMEM((2, D_MODEL, 2 * D_FF), F32), pltpu.VMEM((2, D_FF, D_MODEL), F32),
                            pltpu.VMEM((D_MODEL, 2 * D_FF), BF16), pltpu.VMEM((D_FF, D_MODEL), BF16),
                            pltpu.SemaphoreType.DMA((2, 2))]),
        out_shape=jax.ShapeDtypeStruct((cap, D_MODEL // 2), jnp.uint32),
        compiler_params=pltpu.CompilerParams(dimension_semantics=("arbitrary",),
                                             vmem_limit_bytes=VMEM_LIMIT),
        name="experts",
    )(block_e, n_used, slot, nxt, nvalid, xs, wgu, bgu, wd, bd)


def _combine_kernel(bg_ref, bo_ref, nb_ref, sg_ref, so_ref, ns_ref, ys_ref, ri_ref, rw_ref, x1_ref, gate_ref, g_ref,
                    o_ref, ybuf, sem):
    plan = (bg_ref, bo_ref, nb_ref, sg_ref, so_ref, ns_ref)
    nt = pl.num_programs(1)
    tile = pl.program_id(0) * nt + pl.program_id(1)
    ntiles = pl.num_programs(0) * nt
    slot = tile % 2
    tm = x1_ref.shape[1]

    def run_copy(sl):
        def make_copy(g, o, rows):
            return pltpu.make_async_copy(ys_ref.at[pl.ds(g, rows)], ybuf.at[sl, pl.ds(o, rows)], sem.at[sl])
        return make_copy

    @pl.when(tile == 0)
    def _():
        ybuf[...] = jnp.zeros_like(ybuf)
        _start_listed_copies(plan, tile, run_copy(slot))

    @pl.when(tile + 1 < ntiles)
    def _():
        _start_listed_copies(plan, tile + 1, run_copy(1 - slot))

    _wait_pieces(nb_ref[tile], ns_ref[tile], run_copy(slot))

    posf = ri_ref[0].astype(F32)
    rw = rw_ref[0]
    cb_cols = 256
    pos = [posf[:, TOP_K + k:TOP_K + k + 1] for k in range(TOP_K)]
    pos_blk = [jnp.floor(p * (1.0 / cb_cols)) for p in pos]
    pos_off = [p - cb_cols * b for p, b in zip(pos, pos_blk)]
    wk = [rw[:, k:k + 1].astype(BF16) for k in range(TOP_K)]
    col = lax.broadcasted_iota(jnp.int32, (tm, cb_cols), 1).astype(F32).astype(BF16)
    zero = jnp.zeros((), BF16)
    f_lo = jnp.zeros((tm, D_MODEL // 2), F32)
    f_hi = jnp.zeros((tm, D_MODEL // 2), F32)
    for cb in range(SORTED_ROWS // cb_cols):
        wp = functools.reduce(jnp.add, [
            jnp.where(col == jnp.where(pos_blk[k] == float(cb), pos_off[k], -1.0).astype(BF16), wk[k], zero)
            for k in range(TOP_K)])
        y_lo, y_hi = _unpack_bf16_pairs(ybuf[slot, cb * cb_cols:(cb + 1) * cb_cols, :])
        f_lo = f_lo + _dot(wp, y_lo)
        f_hi = f_hi + _dot(wp, y_hi)
    f = jnp.concatenate([f_lo, f_hi], axis=1)
    o_ref[0] = x1_ref[0] + gate_ref[0] * _rms(f, g_ref[...])


def _combine(plan, ys, ri, rw, x1, gate, g):
    b, s, d = x1.shape
    tm = TM_OUT
    tok = lambda bb, i, *_: (bb, i, 0)
    return pl.pallas_call(
        _combine_kernel,
        grid_spec=pltpu.PrefetchScalarGridSpec(
            num_scalar_prefetch=6,
            grid=(b, s // tm),
            in_specs=[pl.BlockSpec(memory_space=pl.ANY),
                      pl.BlockSpec((1, tm, LANES), tok), pl.BlockSpec((1, tm, LANES), tok),
                      pl.BlockSpec((1, tm, d), tok),
                      pl.BlockSpec((1, 1, d), lambda bb, i, *_: (bb, 0, 0)),
                      pl.BlockSpec((1, d), lambda bb, i, *_: (0, 0))],
            out_specs=pl.BlockSpec((1, tm, d), tok),
            scratch_shapes=[pltpu.VMEM((2, SORTED_ROWS, d // 2), jnp.uint32), pltpu.SemaphoreType.DMA((2,))]),
        out_shape=jax.ShapeDtypeStruct((b, s, d), F32),
        compiler_params=pltpu.CompilerParams(dimension_semantics=("arbitrary", "arbitrary"),
                                             vmem_limit_bytes=VMEM_LIMIT),
        name="combine",
    )(*plan, ys, ri, rw, x1, gate, g)


def _rope_angles(rows, rot_dim):
    row = jnp.broadcast_to(jnp.arange(rows, dtype=F32)[:, None], (rows, GRID_W)).reshape(-1)
    col = jnp.broadcast_to(jnp.arange(GRID_W, dtype=F32)[None, :], (rows, GRID_W)).reshape(-1)
    quarter = rot_dim // 4
    inv_freq = ROPE_BASE ** (-jnp.arange(quarter, dtype=F32) / quarter)
    ang = jnp.concatenate([row[:, None] * inv_freq, col[:, None] * inv_freq], axis=-1)
    return jnp.cos(ang), jnp.sin(ang)


def _rope_tables(s):
    ca, sa = _rope_angles(s // GRID_W, HEAD_DIM)
    cb, sb = _rope_angles(s // GRID_W, MLA_ROPE)
    one = lambda w: jnp.ones((s, w), F32)
    zero = lambda w: jnp.zeros((s, w), F32)
    cos_a = jnp.tile(jnp.concatenate([ca, ca], 1), (1, 2))
    sin_a = jnp.tile(jnp.concatenate([-sa, sa], 1), (1, 2))
    cos_q = jnp.concatenate([one(64), cb, cb, one(32)], 1)
    sin_q = jnp.concatenate([zero(64), -sb, sb, zero(32)], 1)
    cos_k = jnp.concatenate([cb, cb, one(96)], 1)
    sin_k = jnp.concatenate([-sb, sb, zero(96)], 1)
    return cos_a, sin_a, cos_q, sin_q, cos_k, sin_k


def _piece_lists(slot_start, tile_start, counts, first_off, piece, width):
    ne = counts.shape[1]
    cum = jnp.cumsum(counts, axis=1)
    p = jnp.arange(width, dtype=jnp.int32)
    owner = jnp.minimum(jnp.sum((cum[:, None, :] <= p[None, :, None]).astype(jnp.int32), axis=2), ne - 1)
    sel = owner[:, :, None] == jnp.arange(ne, dtype=jnp.int32)[None, None, :]
    pick = lambda a: jnp.sum(jnp.where(sel, a[:, None, :], 0), axis=2)
    off = pick(first_off) + (p[None, :] - pick(cum - counts)) * piece
    as_list = lambda a: a.reshape(-1).astype(jnp.int32)
    return as_list(pick(slot_start) + off), as_list(pick(tile_start) + off), cum[:, -1].astype(jnp.int32)


def _identity_tables(n):
    one, zero = jnp.ones((n, LANES), F32), jnp.zeros((n, LANES), F32)
    return one, zero, one, zero, one, zero


def _inproj_weights(w_in, g_q_a, w_uq, g_kv_a, w_ukv):
    d = w_in.shape[0]
    wq, wk, wv = w_in[:, :512], w_in[:, 512:640], w_in[:, 640:768]
    wcq, wckv, wkr = w_in[:, 768:1152], w_in[:, 1152:1408], w_in[:, 1408:1440]
    wkr_p = jnp.concatenate([wkr, jnp.zeros((d, LANES - MLA_ROPE), F32)], 1)
    wlat = jnp.concatenate([wcq, wckv, wkr_p], 1).astype(BF16)
    wwin = jnp.concatenate([wq, wk, wv], 1).astype(BF16)
    wuq = jnp.pad(w_uq.reshape(MLA_Q_LORA, MLA_HEADS, MLA_NOPE + MLA_ROPE), ((0, 0), (0, 0), (0, 32)))
    wuq = wuq.reshape(MLA_Q_LORA, MLA_HEADS * LANES).astype(BF16)
    wukv = w_ukv.reshape(MLA_KV_LORA, MLA_HEADS, MLA_NOPE + MLA_V)
    wkk = jnp.pad(wukv[:, :, :MLA_NOPE], ((0, 0), (0, 0), (0, 64))).reshape(MLA_KV_LORA, MLA_HEADS * LANES)
    wvv = wukv[:, :, MLA_NOPE:].reshape(MLA_KV_LORA, MLA_HEADS * MLA_V)
    return (wlat, wwin, g_q_a.reshape(1, -1), wuq, g_kv_a.reshape(1, -1), wkk.astype(BF16), wvv.T.astype(BF16))


def kernel(x, c, ctx, c_ctx, w_ada, b_ada, g_mix_pre, g_mix_post, w_in, sink, g_q_a, w_uq, g_kv_a, w_ukv, w_o,
           g_ffn_pre, g_ffn_post, w_router, b_router, w_gate_up, b_gate_up, w_down, b_down):
    b, s, d = x.shape
    cl = ctx.shape[1]
    t = b * s

    cc = jnp.zeros((8, d), F32).at[:b].set(c).at[b].set(c_ctx)
    mod_all = _adaln(cc, w_ada[0], b_ada[0].reshape(1, -1))
    mod = [mod_all[:b, i * d:(i + 1) * d].reshape(b, 1, d) for i in range(6)]
    mod_c = [jnp.broadcast_to(mod_all[b, i * d:(i + 1) * d].reshape(1, 1, d), (b, 1, d)) for i in range(2)]

    wts = _inproj_weights(w_in[0], g_q_a[0], w_uq[0], g_kv_a[0], w_ukv[0])
    gpre = g_mix_pre[0].reshape(1, d)
    qa, ka, vat, q, k, vt = _inproj(x, mod[0], mod[1], gpre, _rope_tables(s), wts, TM_IN)
    _, kac, vatc, _, kc, vct = _inproj(ctx, mod_c[0], mod_c[1], gpre, _identity_tables(cl), wts, cl)

    sinkv = jnp.repeat(sink[0] * LOG2_E, BLOCK).reshape(A_KV_HEADS, 1, -1)
    out_a = _gqa(sinkv, qa, ka, vat, kac, vatc)
    out_b = _mla(q, k, vt, kc, vct)

    wo = w_o[0].astype(BF16)
    wr = jnp.pad(w_router[0], ((0, 0), (0, LANES - N_EXPERTS)))
    wr_hi, wr_lo = _split_bf16(wr)
    br = jnp.concatenate([b_router[0], jnp.full((LANES - N_EXPERTS,), NEG_INF, F32)]).reshape(1, LANES)
    ii = jnp.arange(TM_OUT)
    tri = (ii[:, None] > ii[None, :]).astype(BF16)
    jj = jnp.arange(LANES)
    upper = (jj[:, None] < jj[None, :]).astype(BF16)
    x1, h2, ri, rw, cnt = _outproj(out_a, out_b, x, wo[:512], wo[512:], g_mix_post[0].reshape(1, d), mod[2],
                                   g_ffn_pre[0].reshape(1, d), mod[3], mod[4], wr_hi, wr_lo, br, tri, upper)

    cnt = cnt[:, 0, :N_EXPERTS].astype(jnp.int32)
    nt = cnt.shape[0]
    rows = (cnt + RUN_ALIGN - 1) // RUN_ALIGN * RUN_ALIGN
    tot = jnp.sum(rows, axis=0)
    carry = jnp.cumsum(rows, axis=0) - rows
    padded = (tot + MOE_BLOCK - 1) // MOE_BLOCK * MOE_BLOCK
    pad_end = jnp.cumsum(padded)
    pad_start = pad_end - padded
    nblk = -(-(t * TOP_K + nt * N_EXPERTS * (RUN_ALIGN - 1)) // MOE_BLOCK) + N_EXPERTS
    cap = nblk * MOE_BLOCK
    blk_row = jnp.arange(nblk, dtype=jnp.int32) * MOE_BLOCK
    block_e = jnp.minimum(jnp.sum((pad_end[None, :] <= blk_row[:, None]).astype(jnp.int32), axis=1), N_EXPERTS - 1)
    n_used = (pad_end[-1] // MOE_BLOCK).astype(jnp.int32).reshape(1)
    toff = jnp.cumsum(rows, axis=1) - rows
    gstart = pad_start[None, :] + carry
    n_big = rows // BIG_COPY
    n_small = (rows - n_big * BIG_COPY) // RUN_ALIGN
    big_g, big_o, tot_big = _piece_lists(gstart, toff, n_big, jnp.zeros_like(rows), BIG_COPY, MAX_BIG)
    small_g, small_o, tot_small = _piece_lists(gstart, toff, n_small, n_big * BIG_COPY, RUN_ALIGN, MAX_SMALL)
    plan = (big_g, big_o, tot_big, small_g, small_o, tot_small)
    zstart = jnp.concatenate([pad_start + tot, pad_end[-1:]]).astype(jnp.int32)
    zrows = jnp.concatenate([padded - tot, cap - pad_end[-1:]]).astype(jnp.int32)
    block_e = block_e.astype(jnp.int32)
    run_idx = jnp.cumsum(jnp.concatenate([jnp.zeros((1,), jnp.int32),
                                          (block_e[1:] != block_e[:-1]).astype(jnp.int32)]))
    eid = jnp.arange(N_EXPERTS, dtype=jnp.int32)
    later_used = (tot[None, :] > 0) & (eid[None, :] > eid[:, None])
    next_e = jnp.min(jnp.where(later_used, eid[None, :], N_EXPERTS), axis=1)
    next_e = jnp.where(next_e < N_EXPERTS, next_e, -1).astype(jnp.int32)
    nxt = jnp.sum(jnp.where(block_e[:, None] == eid[None, :], next_e[None, :], 0), axis=1).astype(jnp.int32)

    xs = _dispatch(plan, zstart, zrows, ri.reshape(t, LANES), h2.reshape(t, d), cap)
    run_end = jnp.sum(jnp.where(block_e[:, None] == eid[None, :], (pad_start + tot)[None, :], 0), axis=1)
    nvalid = jnp.where(blk_row < pad_end[-1], jnp.clip(run_end - blk_row, 0, MOE_BLOCK), 0).astype(jnp.int32)
    ys = _experts(block_e, n_used, (run_idx % 2).astype(jnp.int32), nxt, nvalid, xs, w_gate_up[0],
                  b_gate_up[0].reshape(N_EXPERTS, 1, -1), w_down[0], b_down[0].reshape(N_EXPERTS, 1, -1))
    return _combine(plan, ys, ri, rw, x1, mod[5], g_ffn_post[0].reshape(1, d))
```

```python
import functools

import jax
import jax.numpy as jnp
from jax import lax
from jax.experimental import pallas as pl
from jax.experimental.pallas import tpu as pltpu

F32 = jnp.float32
BF16 = jnp.bfloat16

D_MODEL = 1024
GRID_W = 64
HEAD_DIM = 64
A_HEADS = 8
A_KV_HEADS = 2
BLOCK = 128
A_SCALE = HEAD_DIM ** -0.5
MLA_HEADS = 8
MLA_NOPE = 64
MLA_ROPE = 32
MLA_V = 64
MLA_Q_LORA = 384
MLA_KV_LORA = 256
MLA_SCALE = (MLA_NOPE + MLA_ROPE) ** -0.5
LOG2_E = 1.4426950408889634
N_EXPERTS = 32
TOP_K = 4
D_FF = 1024
SWIGLU_LIMIT = 7.0
SWIGLU_ALPHA = 1.702
ROPE_BASE = 10000.0
EPS = 1e-6
NEG_INF = -1e30

LANES = 128
VMEM_LIMIT = 56 * 1024 * 1024

TM_IN = 512
GQA_QB = 4
TQ_MLA = 256
TK_MLA = 512
MLA_GROUP = 2
VT_ROWS = 80
TM_OUT = 512
MOE_BLOCK = 512
EXPERT_SUB = 256
RUN_ALIGN = 8
BIG_COPY = 32
SORTED_ROWS = TM_OUT * TOP_K + N_EXPERTS * RUN_ALIGN
MAX_BIG = SORTED_ROWS // BIG_COPY
MAX_SMALL = N_EXPERTS * (BIG_COPY // RUN_ALIGN - 1)


def _dot(a, b):
    return jnp.dot(a, b, preferred_element_type=F32)


def _dot_nt(a, b):
    return lax.dot_general(a, b, (((1,), (1,)), ((), ())), preferred_element_type=F32)


def _split_bf16(x):
    hi = x.astype(BF16)
    lo = (x - hi.astype(F32)).astype(BF16)
    return hi, lo


def _rms(x, g):
    return x * lax.rsqrt(jnp.mean(x * x, axis=-1, keepdims=True) + EPS) * g


def _rope(x, cos, sin_signed, first_half, half):
    n = x.shape[-1]
    partner = jnp.where(first_half, pltpu.roll(x, n - half, 1), pltpu.roll(x, half, 1))
    return x * cos + partner * sin_signed


def _adaln_kernel(c_ref, w_ref, b_ref, o_ref):
    c = c_ref[...]
    s = c * jax.nn.sigmoid(c)
    s_hi, s_lo = _split_bf16(s)
    w_hi, w_lo = _split_bf16(w_ref[...])
    o_ref[...] = _dot(s_hi, w_hi) + _dot(s_hi, w_lo) + _dot(s_lo, w_hi) + b_ref[...]


def _adaln(cc, w, b):
    n = w.shape[1]
    tn = 1024
    return pl.pallas_call(
        _adaln_kernel,
        grid=(n // tn,),
        in_specs=[pl.BlockSpec((8, D_MODEL), lambda j: (0, 0)),
                  pl.BlockSpec((D_MODEL, tn), lambda j: (0, j)),
                  pl.BlockSpec((1, tn), lambda j: (0, j))],
        out_specs=pl.BlockSpec((8, tn), lambda j: (0, j)),
        out_shape=jax.ShapeDtypeStruct((8, n), F32),
        compiler_params=pltpu.CompilerParams(dimension_semantics=("arbitrary",),
                                             vmem_limit_bytes=VMEM_LIMIT),
        name="adaln",
    )(cc, w, b)


def _inproj_kernel(x_ref, shift_ref, scale_ref, gpre_ref, ca_ref, sa_ref, cq_ref, sq_ref, ck_ref, sk_ref,
                   wlat_ref, wwin_ref, gq_ref, wuq_ref, gkv_ref, wk_ref, wv_ref,
                   qa_ref, ka_ref, vat_ref, q_ref, k_ref, vt_ref):
    x = x_ref[0]
    tm = x.shape[0]
    h = _rms(x, gpre_ref[...]) * (1.0 + scale_ref[0]) + shift_ref[0]
    hb = h.astype(BF16)
    lat = _dot(hb, wlat_ref[...])
    p = _dot(hb, wwin_ref[...])
    c_ka, c_va = A_HEADS * HEAD_DIM, A_HEADS * HEAD_DIM + LANES
    c_ckv, c_kr = MLA_Q_LORA, MLA_Q_LORA + MLA_KV_LORA
    lane = lax.broadcasted_iota(jnp.int32, (tm, LANES), 1)

    first_a = (lane & 32) == 0
    ca, sa = ca_ref[...], sa_ref[...]
    for j in range(A_HEADS * HEAD_DIM // LANES):
        sl = slice(j * LANES, (j + 1) * LANES)
        qa_ref[0, :, sl] = (_rope(p[:, sl], ca, sa, first_a, 32) * (A_SCALE * LOG2_E)).astype(BF16)
    ka_ref[0] = _rope(p[:, c_ka:c_va], ca, sa, first_a, 32).astype(BF16)
    vat = p[:, c_va:c_va + LANES].T
    ones_a = (lax.broadcasted_iota(jnp.int32, (VT_ROWS - HEAD_DIM, BLOCK), 0) == 0).astype(BF16)
    for j in range(tm // BLOCK):
        for hk in range(A_KV_HEADS):
            vat_ref[0, j, hk, 0:HEAD_DIM, :] = vat[hk * HEAD_DIM:(hk + 1) * HEAD_DIM,
                                                   j * BLOCK:(j + 1) * BLOCK].astype(BF16)
            vat_ref[0, j, hk, HEAD_DIM:VT_ROWS, :] = ones_a

    cqn = _rms(lat[:, 0:c_ckv], gq_ref[...]).astype(BF16)
    qf = _dot(cqn, wuq_ref[...])
    first_q = (lane >= 64) & (lane < 80)
    cq, sq = cq_ref[...], sq_ref[...]
    for hh in range(MLA_HEADS):
        sl = slice(hh * LANES, (hh + 1) * LANES)
        q_ref[0, :, sl] = (_rope(qf[:, sl], cq, sq, first_q, 16) * (MLA_SCALE * LOG2_E)).astype(BF16)

    ckvn = _rms(lat[:, c_ckv:c_kr], gkv_ref[...]).astype(BF16)
    kr = _rope(lat[:, c_kr:c_kr + LANES], ck_ref[...], sk_ref[...], lane < 16, 16)
    kr = pltpu.roll(kr, MLA_NOPE, 1)
    kn = _dot(ckvn, wk_ref[...])
    for hh in range(MLA_HEADS):
        sl = slice(hh * LANES, (hh + 1) * LANES)
        k_ref[0, :, sl] = (kn[:, sl] + kr).astype(BF16)
    vt = _dot_nt(wv_ref[...], ckvn).astype(BF16)
    ones_row = (lax.broadcasted_iota(jnp.int32, (VT_ROWS - MLA_V, tm), 0) == 0).astype(BF16)
    for hh in range(MLA_HEADS):
        vt_ref[0, 0, hh, 0:MLA_V, :] = vt[hh * MLA_V:(hh + 1) * MLA_V, :]
        vt_ref[0, 0, hh, MLA_V:VT_ROWS, :] = ones_row


def _inproj(x, shift, scale, gpre, tabs, wts, tm):
    bx, n, d = x.shape
    ca, sa, cq, sq, ck, sk = tabs
    wlat, wwin, gq, wuq, gkv, wk, wv = wts
    tok = lambda i, b: (b, i, 0)
    vec = lambda i, b: (b, 0, 0)
    tab = lambda i, b: (i, 0)
    cst = lambda i, b: (0, 0)
    in_specs = [pl.BlockSpec((1, tm, d), tok), pl.BlockSpec((1, 1, d), vec), pl.BlockSpec((1, 1, d), vec),
                pl.BlockSpec((1, d), cst)]
    in_specs += [pl.BlockSpec((tm, LANES), tab)] * 6
    in_specs += [pl.BlockSpec(w.shape, cst) for w in wts]
    row_out = lambda w: (pl.BlockSpec((1, tm, w), tok), jax.ShapeDtypeStruct((bx, n, w), BF16))
    outs = [row_out(A_HEADS * HEAD_DIM), row_out(LANES),
            (pl.BlockSpec((1, tm // BLOCK, A_KV_HEADS, VT_ROWS, BLOCK), lambda i, b: (b, i, 0, 0, 0)),
             jax.ShapeDtypeStruct((bx, n // BLOCK, A_KV_HEADS, VT_ROWS, BLOCK), BF16)),
            row_out(MLA_HEADS * LANES), row_out(MLA_HEADS * LANES),
            (pl.BlockSpec((1, 1, MLA_HEADS, VT_ROWS, tm), lambda i, b: (b, i, 0, 0, 0)),
             jax.ShapeDtypeStruct((bx, n // tm, MLA_HEADS, VT_ROWS, tm), BF16))]
    out_specs = [o[0] for o in outs]
    out_shape = [o[1] for o in outs]
    return pl.pallas_call(
        _inproj_kernel,
        grid=(n // tm, bx),
        in_specs=in_specs,
        out_specs=out_specs,
        out_shape=out_shape,
        compiler_params=pltpu.CompilerParams(dimension_semantics=("arbitrary", "arbitrary"),
                                             vmem_limit_bytes=VMEM_LIMIT),
        name="inproj",
    )(x, shift, scale, gpre, ca, sa, cq, sq, ck, sk, wlat, wwin, gq, wuq, gkv, wk, wv)


def _gqa_kernel(sink_ref, q_ref, *refs):
    nwin = GQA_QB + 2
    k_refs, kx_ref = refs[:nwin], refs[nwin]
    v_refs, vx_ref = refs[nwin + 1:2 * nwin + 1], refs[2 * nwin + 1]
    o_ref = refs[2 * nwin + 2]
    n = pl.program_id(1)
    nsteps = pl.num_programs(1)
    nctx = kx_ref.shape[1] // BLOCK
    group = A_HEADS // A_KV_HEADS
    key = lax.broadcasted_iota(jnp.int32, (BLOCK, group * BLOCK), 0)
    qry = lax.broadcasted_iota(jnp.int32, (BLOCK, group * BLOCK), 1) & (BLOCK - 1)
    sts = {}

    def scores(qb):
        kcat = jnp.concatenate([k_refs[qb + j][0] for j in range(3)] + [kx_ref[0]], axis=0)
        for hk in range(A_KV_HEADS):
            qg = jnp.concatenate([q_ref[0, qb * BLOCK:(qb + 1) * BLOCK, hh * HEAD_DIM:(hh + 1) * HEAD_DIM]
                                  for hh in range(hk * group, (hk + 1) * group)], axis=0)
            k_hk = kcat[:, hk * HEAD_DIM:(hk + 1) * HEAD_DIM]
            sts[qb, hk] = _dot_nt(k_hk, qg)

    def finish(qb):
        vis_prev = key >= qry
        vis_next = key <= qry
        if qb == 0:
            vis_prev = key >= qry + jnp.where(n > 0, 0, 2 * BLOCK)
        if qb == GQA_QB - 1:
            vis_next = key <= qry - jnp.where(n < nsteps - 1, 0, 2 * BLOCK)
        outs = []
        for hk in range(A_KV_HEADS):
            st = sts[qb, hk]
            pieces = [jnp.where(vis_prev, st[0:BLOCK], NEG_INF), st[BLOCK:2 * BLOCK],
                      jnp.where(vis_next, st[2 * BLOCK:3 * BLOCK], NEG_INF)]
            pieces += [st[(3 + j) * BLOCK:(4 + j) * BLOCK] for j in range(nctx)]
            sink = sink_ref[hk]
            m = jnp.maximum(functools.reduce(jnp.maximum, [jnp.max(p, axis=0, keepdims=True) for p in pieces]),
                            sink)
            ps = [jnp.exp2(p - m).astype(BF16) for p in pieces]
            vts = [v_refs[qb + j][0, 0, hk] for j in range(3)] + [vx_ref[0, j, hk] for j in range(nctx)]
            pv = functools.reduce(jnp.add, [_dot(vt, p) for vt, p in zip(vts, ps)])
            l = pv[HEAD_DIM:HEAD_DIM + 1, :] + jnp.exp2(sink - m)
            o = pv[0:HEAD_DIM, :] * (1.0 / l)
            outs += [o[:, g * BLOCK:(g + 1) * BLOCK] for g in range(group)]
        for pair in range(A_HEADS // 2):
            both = jnp.concatenate([outs[2 * pair], outs[2 * pair + 1]], axis=0)
            o_ref[0, qb * BLOCK:(qb + 1) * BLOCK, pair * LANES:(pair + 1) * LANES] = both.T.astype(BF16)

    for qb in range(GQA_QB):
        scores(qb)
    for qb in range(GQA_QB):
        finish(qb)


def _gqa(sinkv, qa, ka, vat, kac, vatc):
    b, s, _ = qa.shape
    nb = s // BLOCK
    c = kac.shape[1]
    tq = GQA_QB * BLOCK
    cur = lambda bb, n: (bb, n, 0)
    win = lambda j: (lambda bb, n: (bb, jnp.clip(n * GQA_QB + j - 1, 0, nb - 1), 0))
    vwin = lambda j: (lambda bb, n: (bb, jnp.clip(n * GQA_QB + j - 1, 0, nb - 1), 0, 0, 0))
    vblk = (1, 1, A_KV_HEADS, VT_ROWS, BLOCK)
    in_specs = [pl.BlockSpec(sinkv.shape, lambda bb, n: (0, 0, 0)), pl.BlockSpec((1, tq, A_HEADS * HEAD_DIM), cur)]
    in_specs += [pl.BlockSpec((1, BLOCK, LANES), win(j)) for j in range(GQA_QB + 2)]
    in_specs += [pl.BlockSpec((1, c, LANES), lambda bb, n: (bb, 0, 0))]
    in_specs += [pl.BlockSpec(vblk, vwin(j)) for j in range(GQA_QB + 2)]
    in_specs += [pl.BlockSpec((1, c // BLOCK, A_KV_HEADS, VT_ROWS, BLOCK), lambda bb, n: (bb, 0, 0, 0, 0))]
    return pl.pallas_call(
        _gqa_kernel,
        grid=(b, nb // GQA_QB),
        in_specs=in_specs,
        out_specs=pl.BlockSpec((1, tq, 512), cur),
        out_shape=jax.ShapeDtypeStruct((b, s, 512), BF16),
        compiler_params=pltpu.CompilerParams(dimension_semantics=("arbitrary", "arbitrary"),
                                             vmem_limit_bytes=VMEM_LIMIT),
        name="gqa",
    )(sinkv, qa, *([ka] * (GQA_QB + 2)), kac, *([vat] * (GQA_QB + 2)), vatc)


def _mla_kernel(q_ref, k_ref, vt_ref, kc_ref, vct_ref, o_ref, m_ref, acc_ref, sa_ref, sb_ref):
    nk = k_ref.shape[1] // TK_MLA

    def scores_into(buf, k_blk, heads=range(MLA_HEADS)):
        n = k_blk.shape[0]
        for hh in heads:
            hsl = slice(hh * LANES, (hh + 1) * LANES)
            buf[hh, 0:n, :] = _dot_nt(k_blk[:, hsl], q_ref[0, :, hsl])

    def consume(buf, vt_blk, first, heads=range(MLA_HEADS)):
        n = vt_blk.shape[2]
        for hh in heads:
            st = buf[hh, 0:n, :]
            cmax = jnp.max(st, axis=0, keepdims=True)
            if first:
                m_new = cmax
            else:
                m_old = m_ref[hh, 0:1, :]
                m_new = jnp.maximum(m_old, cmax)
                alpha = jnp.exp2(m_old - m_new)
            p = jnp.exp2(st - m_new).astype(BF16)
            pv = _dot(vt_blk[hh], p)
            acc_ref[hh] = pv if first else alpha * acc_ref[hh] + pv
            m_ref[hh, 0:1, :] = m_new

    def k_chunk(c):
        return k_ref[0, pl.ds(pl.multiple_of(c * TK_MLA, TK_MLA), TK_MLA), :]

    def overlapped(buf_w, k_blk, buf_r, vt_blk, first):
        for g in range(0, MLA_HEADS, MLA_GROUP):
            hs = range(g, g + MLA_GROUP)
            scores_into(buf_w, k_blk, hs)
            consume(buf_r, vt_blk, first, hs)

    scores_into(sa_ref, k_chunk(0))
    overlapped(sb_ref, k_chunk(1), sa_ref, vt_ref[0, 0], True)

    def body(i, _):
        c = 2 * i + 1
        overlapped(sa_ref, k_chunk(c + 1), sb_ref, vt_ref[0, c], False)
        overlapped(sb_ref, k_chunk(c + 2), sa_ref, vt_ref[0, c + 1], False)
        return 0

    lax.fori_loop(0, (nk - 2) // 2, body, 0)
    overlapped(sa_ref, kc_ref[0], sb_ref, vt_ref[0, nk - 1], False)
    consume(sa_ref, vct_ref[0, 0], False)

    outs = [acc_ref[hh, 0:MLA_V, :] * (1.0 / acc_ref[hh, MLA_V:MLA_V + 1, :]) for hh in range(MLA_HEADS)]
    o_ref[0] = jnp.concatenate(outs, axis=0).T.astype(BF16)


def _mla(q, k, vt, kc, vct):
    b, s, _ = q.shape
    c = kc.shape[1]
    nk = s // TK_MLA
    assert vt.shape == (b, nk, MLA_HEADS, VT_ROWS, TK_MLA) and nk % 2 == 0
    return pl.pallas_call(
        _mla_kernel,
        grid=(b, s // TQ_MLA),
        in_specs=[pl.BlockSpec((1, TQ_MLA, 1024), lambda bb, i: (bb, i, 0)),
                  pl.BlockSpec((1, s, 1024), lambda bb, i: (bb, 0, 0)),
                  pl.BlockSpec((1, nk, MLA_HEADS, VT_ROWS, TK_MLA), lambda bb, i: (bb, 0, 0, 0, 0)),
                  pl.BlockSpec((1, c, 1024), lambda bb, i: (bb, 0, 0)),
                  pl.BlockSpec((1, 1, MLA_HEADS, VT_ROWS, c), lambda bb, i: (bb, 0, 0, 0, 0))],
        out_specs=pl.BlockSpec((1, TQ_MLA, 512), lambda bb, i: (bb, i, 0)),
        out_shape=jax.ShapeDtypeStruct((b, s, 512), BF16),
        scratch_shapes=[pltpu.VMEM((MLA_HEADS, 8, TQ_MLA), F32),
                        pltpu.VMEM((MLA_HEADS, VT_ROWS, TQ_MLA), F32),
                        pltpu.VMEM((MLA_HEADS, TK_MLA, TQ_MLA), F32), pltpu.VMEM((MLA_HEADS, TK_MLA, TQ_MLA), F32)],
        compiler_params=pltpu.CompilerParams(dimension_semantics=("arbitrary", "arbitrary"),
                                             vmem_limit_bytes=VMEM_LIMIT),
        name="mla",
    )(q, k, vt, kc, vct)


def _outproj_kernel(oa_ref, ob_ref, x_ref, woa_ref, wob_ref, gpost_ref, gate_ref, gffn_ref, shift_ref, scale_ref,
                    wrh_ref, wrl_ref, br_ref, tri_ref, upper_ref,
                    x1_ref, h2_ref, ri_ref, rw_ref, cnt_ref):
    tm = x_ref.shape[1]
    nsub = 2
    sub = tm // nsub
    rows = [slice(j * sub, (j + 1) * sub) for j in range(nsub)]
    ys = [_dot(oa_ref[0, r, :], woa_ref[...]) + _dot(ob_ref[0, r, :], wob_ref[...]) for r in rows]
    sub_lane_f = lax.broadcasted_iota(jnp.int32, (sub, LANES), 1).astype(F32)
    idx_parts, w_parts = [], []
    for r, y in zip(rows, ys):
        x1 = x_ref[0, r, :] + gate_ref[0] * _rms(y, gpost_ref[...])
        x1_ref[0, r, :] = x1
        h2 = _rms(x1, gffn_ref[...]) * (1.0 + scale_ref[0]) + shift_ref[0]
        h2_ref[0, r, :] = h2.astype(BF16)
        h_hi, h_lo = _split_bf16(h2)
        cur = _dot(h_hi, wrh_ref[...]) + _dot(h_hi, wrl_ref[...]) + _dot(h_lo, wrh_ref[...]) + br_ref[...]
        tops, idxs = [], []
        for _ in range(TOP_K):
            mk = jnp.max(cur, axis=-1, keepdims=True)
            ik = jnp.min(jnp.where(cur == mk, sub_lane_f, float(LANES)), axis=-1, keepdims=True)
            tops.append(mk)
            idxs.append(ik)
            cur = jnp.where(sub_lane_f == ik, -jnp.inf, cur)
        es = [jnp.exp(t - tops[0]) for t in tops]
        inv = 1.0 / functools.reduce(jnp.add, es)
        idx_parts.append(idxs)
        w_parts.append([e * inv for e in es])

    hot_parts = [[(sub_lane_f == ik) for ik in idxs] for idxs in idx_parts]
    onehot = jnp.concatenate([functools.reduce(jnp.add, [o.astype(F32) for o in hots]) for hots in hot_parts],
                             axis=0)
    cnt = jnp.sum(onehot, axis=0, keepdims=True)
    prefix = _dot(tri_ref[...], onehot.astype(BF16))
    chunks = jnp.floor((cnt + (RUN_ALIGN - 1.0)) * (1.0 / RUN_ALIGN))
    run_start = _dot(jnp.broadcast_to(chunks, (8, LANES)).astype(BF16), upper_ref[...])[0:1, :] * RUN_ALIGN
    base = prefix + run_start
    sub_lane = lax.broadcasted_iota(jnp.int32, (sub, LANES), 1)
    for r, hots, idxs, wts in zip(rows, hot_parts, idx_parts, w_parts):
        ri = jnp.zeros((sub, LANES), jnp.int32)
        rw = jnp.zeros((sub, LANES), F32)
        for k in range(TOP_K):
            pos = jnp.sum(jnp.where(hots[k], base[r, :], 0.0), axis=-1, keepdims=True).astype(jnp.int32)
            ri = jnp.where(sub_lane == k, idxs[k].astype(jnp.int32), ri)
            ri = jnp.where(sub_lane == TOP_K + k, pos, ri)
            rw = jnp.where(sub_lane == k, wts[k], rw)
        ri_ref[0, r, :] = ri
        rw_ref[0, r, :] = rw
    cnt_ref[0] = cnt


def _outproj(oa, ob, x, woa, wob, gpost, gate, gffn, shift, scale, wrh, wrl, br, tri, upper):
    b, s, d = x.shape
    tm = TM_OUT
    nt = s // tm
    tok = lambda bb, i: (bb, i, 0)
    vec = lambda bb, i: (bb, 0, 0)
    cst = lambda bb, i: (0, 0)
    return pl.pallas_call(
        _outproj_kernel,
        grid=(b, nt),
        in_specs=[pl.BlockSpec((1, tm, 512), tok), pl.BlockSpec((1, tm, 512), tok), pl.BlockSpec((1, tm, d), tok),
                  pl.BlockSpec((512, d), cst), pl.BlockSpec((512, d), cst), pl.BlockSpec((1, d), cst),
                  pl.BlockSpec((1, 1, d), vec), pl.BlockSpec((1, d), cst),
                  pl.BlockSpec((1, 1, d), vec), pl.BlockSpec((1, 1, d), vec),
                  pl.BlockSpec((d, LANES), cst), pl.BlockSpec((d, LANES), cst), pl.BlockSpec((1, LANES), cst),
                  pl.BlockSpec((tm, tm), cst), pl.BlockSpec((LANES, LANES), cst)],
        out_specs=[pl.BlockSpec((1, tm, d), tok), pl.BlockSpec((1, tm, d), tok),
                   pl.BlockSpec((1, tm, LANES), tok), pl.BlockSpec((1, tm, LANES), tok),
                   pl.BlockSpec((1, 1, LANES), lambda bb, i: (bb * nt + i, 0, 0))],
        out_shape=[jax.ShapeDtypeStruct((b, s, d), F32), jax.ShapeDtypeStruct((b, s, d), BF16),
                   jax.ShapeDtypeStruct((b, s, LANES), jnp.int32), jax.ShapeDtypeStruct((b, s, LANES), F32),
                   jax.ShapeDtypeStruct((b * nt, 1, LANES), F32)],
        compiler_params=pltpu.CompilerParams(dimension_semantics=("arbitrary", "arbitrary"),
                                             vmem_limit_bytes=VMEM_LIMIT),
        name="outproj",
    )(oa, ob, x, woa, wob, gpost, gate, gffn, shift, scale, wrh, wrl, br, tri, upper)


def _start_pieces(g, o, rows, make_copy):
    n_big = lax.shift_right_logical(rows, BIG_COPY.bit_length() - 1)
    n_small = lax.shift_right_logical(rows, RUN_ALIGN.bit_length() - 1) & (BIG_COPY // RUN_ALIGN - 1)

    def big(c, _):
        off = c * BIG_COPY
        make_copy(pl.multiple_of(g + off, RUN_ALIGN), pl.multiple_of(o + off, RUN_ALIGN), BIG_COPY).start()
        return 0

    def small(c, _):
        off = n_big * BIG_COPY + c * RUN_ALIGN
        make_copy(pl.multiple_of(g + off, RUN_ALIGN), pl.multiple_of(o + off, RUN_ALIGN), RUN_ALIGN).start()
        return 0

    lax.fori_loop(0, n_big, big, 0)
    lax.fori_loop(0, n_small, small, 0)
    return n_big, n_small


def _wait_pieces(n_big, n_small, make_copy):
    def wait_big(i, _):
        make_copy(0, 0, BIG_COPY).wait()
        return 0

    def wait_small(i, _):
        make_copy(0, 0, RUN_ALIGN).wait()
        return 0

    lax.fori_loop(0, n_big, wait_big, 0)
    lax.fori_loop(0, n_small, wait_small, 0)


def _start_listed_copies(plan, tile, make_copy):
    big_g, big_o, n_big, small_g, small_o, n_small = plan

    def start_list(g_ref, o_ref, n, width, rows):
        def body(p, _):
            j = tile * width + p
            make_copy(pl.multiple_of(g_ref[j], RUN_ALIGN), pl.multiple_of(o_ref[j], RUN_ALIGN), rows).start()
            return 0

        lax.fori_loop(0, n, body, 0)

    start_list(big_g, big_o, n_big[tile], MAX_BIG, BIG_COPY)
    start_list(small_g, small_o, n_small[tile], MAX_SMALL, RUN_ALIGN)


def _dispatch_kernel(bg_ref, bo_ref, nb_ref, sg_ref, so_ref, ns_ref, zs_ref, zr_ref, ri_ref, h_ref, xs_ref,
                     sorted_ref, zero_ref, sem):
    plan = (bg_ref, bo_ref, nb_ref, sg_ref, so_ref, ns_ref)
    tile = pl.program_id(0)
    last = pl.num_programs(0) - 1
    slot = tile % 2
    tm = h_ref.shape[0]
    post = ri_ref[...].astype(F32).T
    h = h_ref[...]
    rb_rows = 256
    pos_blk = [jnp.floor(post[TOP_K + k:TOP_K + k + 1, :] * (1.0 / rb_rows)) for k in range(TOP_K)]
    pos_off = [post[TOP_K + k:TOP_K + k + 1, :] - rb_rows * pos_blk[k] for k in range(TOP_K)]
    row = lax.broadcasted_iota(jnp.int32, (rb_rows, tm), 0).astype(F32).astype(BF16)
    one, zero = jnp.ones((), BF16), jnp.zeros((), BF16)

    for rb in range(SORTED_ROWS // rb_rows):
        perm = functools.reduce(jnp.add, [
            jnp.where(row == jnp.where(pos_blk[k] == float(rb), pos_off[k], -1.0).astype(BF16), one, zero)
            for k in range(TOP_K)])
        xr = _dot(perm, h)
        lo = lax.bitcast_convert_type(xr[:, :512], jnp.uint32)
        hi = lax.bitcast_convert_type(xr[:, 512:], jnp.uint32)
        sorted_ref[slot, rb * rb_rows:(rb + 1) * rb_rows, :] = (lo >> 16) | (hi & jnp.uint32(0xFFFF0000))

    def run_copy(sl):
        def make_copy(g, o, rows):
            return pltpu.make_async_copy(sorted_ref.at[sl, pl.ds(o, rows)], xs_ref.at[pl.ds(g, rows)], sem.at[sl])
        return make_copy

    _start_listed_copies(plan, tile, run_copy(slot))

    @pl.when(tile > 0)
    def _():
        _wait_pieces(nb_ref[tile - 1], ns_ref[tile - 1], run_copy(1 - slot))

    @pl.when(tile == last)
    def _():
        _wait_pieces(nb_ref[tile], ns_ref[tile], run_copy(slot))
        zero_ref[...] = jnp.zeros_like(zero_ref)

        def zero_copy(g, o, rows):
            return pltpu.make_async_copy(zero_ref.at[pl.ds(0, rows)], xs_ref.at[pl.ds(g, rows)], sem.at[2])

        def per_expert(e, totals):
            nb, ns = _start_pieces(zs_ref[e], 0, zr_ref[e], zero_copy)
            return totals[0] + nb, totals[1] + ns

        zb, zs = lax.fori_loop(0, zs_ref.shape[0], per_expert, (0, 0))
        _wait_pieces(zb, zs, zero_copy)


def _dispatch(plan, zstart, zrows, ri, h2, cap):
    t, d = h2.shape
    tm = TM_OUT
    return pl.pallas_call(
        _dispatch_kernel,
        grid_spec=pltpu.PrefetchScalarGridSpec(
            num_scalar_prefetch=8,
            grid=(t // tm,),
            in_specs=[pl.BlockSpec((tm, LANES), lambda i, *_: (i, 0)),
                      pl.BlockSpec((tm, d), lambda i, *_: (i, 0))],
            out_specs=pl.BlockSpec(memory_space=pl.ANY),
            scratch_shapes=[pltpu.VMEM((2, SORTED_ROWS, d // 2), jnp.uint32),
                            pltpu.VMEM((BIG_COPY, d // 2), jnp.uint32),
                            pltpu.SemaphoreType.DMA((3,))]),
        out_shape=jax.ShapeDtypeStruct((cap, d // 2), jnp.uint32),
        compiler_params=pltpu.CompilerParams(dimension_semantics=("arbitrary",),
                                             vmem_limit_bytes=VMEM_LIMIT),
        name="dispatch",
    )(*plan, zstart, zrows, ri, h2)


def _pack_bf16_pairs(x):
    n = x.shape[1] // 2
    bits = lax.bitcast_convert_type(x.astype(BF16).astype(F32), jnp.uint32)
    return (bits[:, :n] >> 16) | (bits[:, n:] & jnp.uint32(0xFFFF0000))


def _unpack_bf16_pairs(w):
    lo = lax.bitcast_convert_type(w << 16, F32).astype(BF16)
    hi = lax.bitcast_convert_type(w & jnp.uint32(0xFFFF0000), F32).astype(BF16)
    return lo, hi


def _experts_kernel(be_ref, nu_ref, slot_ref, nxt_ref, nv_ref, xs_ref, wgu_hbm, bgu_ref, wd_hbm, bd_ref, ys_ref,
                    wgu_f, wd_f, wgu_b, wd_b, sem):
    del nu_ref
    i = pl.program_id(0)
    e = be_ref[i]
    slot = slot_ref[i]
    valid = nv_ref[i]
    used = valid > 0
    run_start = used & ((i == 0) | (be_ref[jnp.maximum(i - 1, 0)] != e))

    def weight_copies(ex, sl):
        return (pltpu.make_async_copy(wgu_hbm.at[ex], wgu_f.at[sl], sem.at[0, sl]),
                pltpu.make_async_copy(wd_hbm.at[ex], wd_f.at[sl], sem.at[1, sl]))

    @pl.when(i == 0)
    def _():
        for cp in weight_copies(e, slot):
            cp.start()

    @pl.when(run_start)
    def _():
        for cp in weight_copies(e, slot):
            cp.wait()
        nxt = nxt_ref[i]

        @pl.when(nxt >= 0)
        def _():
            for cp in weight_copies(nxt, 1 - slot):
                cp.start()

        wgu_b[...] = wgu_f[slot].astype(BF16)
        wd_b[...] = wd_f[slot].astype(BF16)

    def compute(pieces):
        rows = [slice(j * EXPERT_SUB, (j + 1) * EXPERT_SUB) for j in pieces]
        gus = []
        for r in rows:
            x_lo, x_hi = _unpack_bf16_pairs(xs_ref[r, :])
            gus.append(_dot(x_lo, wgu_b[0:512, :]) + _dot(x_hi, wgu_b[512:, :]) + bgu_ref[0])
        for r, gu in zip(rows, gus):
            gate = jnp.minimum(gu[:, :D_FF], SWIGLU_LIMIT)
            lin = jnp.clip(gu[:, D_FF:], -SWIGLU_LIMIT, SWIGLU_LIMIT)
            act = (lin + 1.0) * (gate * jax.nn.sigmoid(SWIGLU_ALPHA * gate))
            ys_ref[r, :] = _pack_bf16_pairs(_dot(act.astype(BF16), wd_b[...]) + bd_ref[0])

    npieces = MOE_BLOCK // EXPERT_SUB
    for n in range(npieces + 1):
        lo, hi = (n - 1) * EXPERT_SUB, n * EXPERT_SUB
        cond = (valid == 0) if n == 0 else ((valid > lo) & (valid <= hi))

        @pl.when(cond)
        def _(n=n):
            if n:
                compute(range(n))
            if n < npieces:
                ys_ref[n * EXPERT_SUB:, :] = jnp.zeros(((npieces - n) * EXPERT_SUB, D_MODEL // 2), jnp.uint32)


def _experts(block_e, n_used, slot, nxt, nvalid, xs, wgu, bgu, wd, bd):
    cap = xs.shape[0]
    nblk = cap // MOE_BLOCK
    return pl.pallas_call(
        _experts_kernel,
        grid_spec=pltpu.PrefetchScalarGridSpec(
            num_scalar_prefetch=5,
            grid=(nblk,),
            in_specs=[pl.BlockSpec((MOE_BLOCK, 512), lambda i, be, nu, *_: (jnp.minimum(i, nu[0] - 1), 0)),
                      pl.BlockSpec(memory_space=pl.ANY),
                      pl.BlockSpec((1, 1, 2 * D_FF), lambda i, be, *_: (be[i], 0, 0)),
                      pl.BlockSpec(memory_space=pl.ANY),
                      pl.BlockSpec((1, 1, D_MODEL), lambda i, be, *_: (be[i], 0, 0))],
            out_specs=pl.BlockSpec((MOE_BLOCK, D_MODEL // 2), lambda i, *_: (i, 0)),
            scratch_shapes=[pltpu.VMEM((2, D_MODEL, 2 * D_FF), F32), pltpu.VMEM((2, D_FF, D_MODEL), F32),
                            pltpu.VMEM((D_MODEL, 2 * D_FF), BF16), pltpu.VMEM((D_FF, D_MODEL), BF16),
                            pltpu.SemaphoreType.DMA((2, 2))]),
        out_shape=jax.ShapeDtypeStruct((cap, D_MODEL // 2), jnp.uint32),
        compiler_params=pltpu.CompilerParams(dimension_semantics=("arbitrary",),
                                             vmem_limit_bytes=VMEM_LIMIT),
        name="experts",
    )(block_e, n_used, slot, nxt, nvalid, xs, wgu, bgu, wd, bd)


def _combine_kernel(bg_ref, bo_ref, nb_ref, sg_ref, so_ref, ns_ref, ys_ref, ri_ref, rw_ref, x1_ref, gate_ref, g_ref,
                    o_ref, ybuf, sem):
    plan = (bg_ref, bo_ref, nb_ref, sg_ref, so_ref, ns_ref)
    nt = pl.num_programs(1)
    tile = pl.program_id(0) * nt + pl.program_id(1)
    ntiles = pl.num_programs(0) * nt
    slot = tile % 2
    tm = x1_ref.shape[1]

    def run_copy(sl):
        def make_copy(g, o, rows):
            return pltpu.make_async_copy(ys_ref.at[pl.ds(g, rows)], ybuf.at[sl, pl.ds(o, rows)], sem.at[sl])
        return make_copy

    @pl.when(tile == 0)
    def _():
        ybuf[...] = jnp.zeros_like(ybuf)
        _start_listed_copies(plan, tile, run_copy(slot))

    @pl.when(tile + 1 < ntiles)
    def _():
        _start_listed_copies(plan, tile + 1, run_copy(1 - slot))

    _wait_pieces(nb_ref[tile], ns_ref[tile], run_copy(slot))

    posf = ri_ref[0].astype(F32)
    rw = rw_ref[0]
    cb_cols = 256
    pos = [posf[:, TOP_K + k:TOP_K + k + 1] for k in range(TOP_K)]
    pos_blk = [jnp.floor(p * (1.0 / cb_cols)) for p in pos]
    pos_off = [p - cb_cols * b for p, b in zip(pos, pos_blk)]
    wk = [rw[:, k:k + 1].astype(BF16) for k in range(TOP_K)]
    col = lax.broadcasted_iota(jnp.int32, (tm, cb_cols), 1).astype(F32).astype(BF16)
    zero = jnp.zeros((), BF16)
    f_lo = jnp.zeros((tm, D_MODEL // 2), F32)
    f_hi = jnp.zeros((tm, D_MODEL // 2), F32)
    for cb in range(SORTED_ROWS // cb_cols):
        wp = functools.reduce(jnp.add, [
            jnp.where(col == jnp.where(pos_blk[k] == float(cb), pos_off[k], -1.0).astype(BF16), wk[k], zero)
            for k in range(TOP_K)])
        y_lo, y_hi = _unpack_bf16_pairs(ybuf[slot, cb * cb_cols:(cb + 1) * cb_cols, :])
        f_lo = f_lo + _dot(wp, y_lo)
        f_hi = f_hi + _dot(wp, y_hi)
    f = jnp.concatenate([f_lo, f_hi], axis=1)
    o_ref[0] = x1_ref[0] + gate_ref[0] * _rms(f, g_ref[...])


def _combine(plan, ys, ri, rw, x1, gate, g):
    b, s, d = x1.shape
    tm = TM_OUT
    tok = lambda bb, i, *_: (bb, i, 0)
    return pl.pallas_call(
        _combine_kernel,
        grid_spec=pltpu.PrefetchScalarGridSpec(
            num_scalar_prefetch=6,
            grid=(b, s // tm),
            in_specs=[pl.BlockSpec(memory_space=pl.ANY),
                      pl.BlockSpec((1, tm, LANES), tok), pl.BlockSpec((1, tm, LANES), tok),
                      pl.BlockSpec((1, tm, d), tok),
                      pl.BlockSpec((1, 1, d), lambda bb, i, *_: (bb, 0, 0)),
                      pl.BlockSpec((1, d), lambda bb, i, *_: (0, 0))],
            out_specs=pl.BlockSpec((1, tm, d), tok),
            scratch_shapes=[pltpu.VMEM((2, SORTED_ROWS, d // 2), jnp.uint32), pltpu.SemaphoreType.DMA((2,))]),
        out_shape=jax.ShapeDtypeStruct((b, s, d), F32),
        compiler_params=pltpu.CompilerParams(dimension_semantics=("arbitrary", "arbitrary"),
                                             vmem_limit_bytes=VMEM_LIMIT),
        name="combine",
    )(*plan, ys, ri, rw, x1, gate, g)


def _rope_angles(rows, rot_dim):
    row = jnp.broadcast_to(jnp.arange(rows, dtype=F32)[:, None], (rows, GRID_W)).reshape(-1)
    col = jnp.broadcast_to(jnp.arange(GRID_W, dtype=F32)[None, :], (rows, GRID_W)).reshape(-1)
    quarter = rot_dim // 4
    inv_freq = ROPE_BASE ** (-jnp.arange(quarter, dtype=F32) / quarter)
    ang = jnp.concatenate([row[:, None] * inv_freq, col[:, None] * inv_freq], axis=-1)
    return jnp.cos(ang), jnp.sin(ang)


def _rope_tables(s):
    ca, sa = _rope_angles(s // GRID_W, HEAD_DIM)
    cb, sb = _rope_angles(s // GRID_W, MLA_ROPE)
    one = lambda w: jnp.ones((s, w), F32)
    zero = lambda w: jnp.zeros((s, w), F32)
    cos_a = jnp.tile(jnp.concatenate([ca, ca], 1), (1, 2))
    sin_a = jnp.tile(jnp.concatenate([-sa, sa], 1), (1, 2))
    cos_q = jnp.concatenate([one(64), cb, cb, one(32)], 1)
    sin_q = jnp.concatenate([zero(64), -sb, sb, zero(32)], 1)
    cos_k = jnp.concatenate([cb, cb, one(96)], 1)
    sin_k = jnp.concatenate([-sb, sb, zero(96)], 1)
    return cos_a, sin_a, cos_q, sin_q, cos_k, sin_k


def _piece_lists(slot_start, tile_start, counts, first_off, piece, width):
    ne = counts.shape[1]
    cum = jnp.cumsum(counts, axis=1)
    p = jnp.arange(width, dtype=jnp.int32)
    owner = jnp.minimum(jnp.sum((cum[:, None, :] <= p[None, :, None]).astype(jnp.int32), axis=2), ne - 1)
    sel = owner[:, :, None] == jnp.arange(ne, dtype=jnp.int32)[None, None, :]
    pick = lambda a: jnp.sum(jnp.where(sel, a[:, None, :], 0), axis=2)
    off = pick(first_off) + (p[None, :] - pick(cum - counts)) * piece
    as_list = lambda a: a.reshape(-1).astype(jnp.int32)
    return as_list(pick(slot_start) + off), as_list(pick(tile_start) + off), cum[:, -1].astype(jnp.int32)


def _identity_tables(n):
    one, zero = jnp.ones((n, LANES), F32), jnp.zeros((n, LANES), F32)
    return one, zero, one, zero, one, zero


def _inproj_weights(w_in, g_q_a, w_uq, g_kv_a, w_ukv):
    d = w_in.shape[0]
    wq, wk, wv = w_in[:, :512], w_in[:, 512:640], w_in[:, 640:768]
    wcq, wckv, wkr = w_in[:, 768:1152], w_in[:, 1152:1408], w_in[:, 1408:1440]
    wkr_p = jnp.concatenate([wkr, jnp.zeros((d, LANES - MLA_ROPE), F32)], 1)
    wlat = jnp.concatenate([wcq, wckv, wkr_p], 1).astype(BF16)
    wwin = jnp.concatenate([wq, wk, wv], 1).astype(BF16)
    wuq = jnp.pad(w_uq.reshape(MLA_Q_LORA, MLA_HEADS, MLA_NOPE + MLA_ROPE), ((0, 0), (0, 0), (0, 32)))
    wuq = wuq.reshape(MLA_Q_LORA, MLA_HEADS * LANES).astype(BF16)
    wukv = w_ukv.reshape(MLA_KV_LORA, MLA_HEADS, MLA_NOPE + MLA_V)
    wkk = jnp.pad(wukv[:, :, :MLA_NOPE], ((0, 0), (0, 0), (0, 64))).reshape(MLA_KV_LORA, MLA_HEADS * LANES)
    wvv = wukv[:, :, MLA_NOPE:].reshape(MLA_KV_LORA, MLA_HEADS * MLA_V)
    return (wlat, wwin, g_q_a.reshape(1, -1), wuq, g_kv_a.reshape(1, -1), wkk.astype(BF16), wvv.T.astype(BF16))


def kernel(x, c, ctx, c_ctx, w_ada, b_ada, g_mix_pre, g_mix_post, w_in, sink, g_q_a, w_uq, g_kv_a, w_ukv, w_o,
           g_ffn_pre, g_ffn_post, w_router, b_router, w_gate_up, b_gate_up, w_down, b_down):
    b, s, d = x.shape
    cl = ctx.shape[1]
    t = b * s

    cc = jnp.zeros((8, d), F32).at[:b].set(c).at[b].set(c_ctx)
    mod_all = _adaln(cc, w_ada[0], b_ada[0].reshape(1, -1))
    mod = [mod_all[:b, i * d:(i + 1) * d].reshape(b, 1, d) for i in range(6)]
    mod_c = [jnp.broadcast_to(mod_all[b, i * d:(i + 1) * d].reshape(1, 1, d), (b, 1, d)) for i in range(2)]

    wts = _inproj_weights(w_in[0], g_q_a[0], w_uq[0], g_kv_a[0], w_ukv[0])
    gpre = g_mix_pre[0].reshape(1, d)
    qa, ka, vat, q, k, vt = _inproj(x, mod[0], mod[1], gpre, _rope_tables(s), wts, TM_IN)
    _, kac, vatc, _, kc, vct = _inproj(ctx, mod_c[0], mod_c[1], gpre, _identity_tables(cl), wts, cl)

    sinkv = jnp.repeat(sink[0] * LOG2_E, BLOCK).reshape(A_KV_HEADS, 1, -1)
    out_a = _gqa(sinkv, qa, ka, vat, kac, vatc)
    out_b = _mla(q, k, vt, kc, vct)

    wo = w_o[0].astype(BF16)
    wr = jnp.pad(w_router[0], ((0, 0), (0, LANES - N_EXPERTS)))
    wr_hi, wr_lo = _split_bf16(wr)
    br = jnp.concatenate([b_router[0], jnp.full((LANES - N_EXPERTS,), NEG_INF, F32)]).reshape(1, LANES)
    ii = jnp.arange(TM_OUT)
    tri = (ii[:, None] > ii[None, :]).astype(BF16)
    jj = jnp.arange(LANES)
    upper = (jj[:, None] < jj[None, :]).astype(BF16)
    x1, h2, ri, rw, cnt = _outproj(out_a, out_b, x, wo[:512], wo[512:], g_mix_post[0].reshape(1, d), mod[2],
                                   g_ffn_pre[0].reshape(1, d), mod[3], mod[4], wr_hi, wr_lo, br, tri, upper)

    cnt = cnt[:, 0, :N_EXPERTS].astype(jnp.int32)
    nt = cnt.shape[0]
    rows = (cnt + RUN_ALIGN - 1) // RUN_ALIGN * RUN_ALIGN
    tot = jnp.sum(rows, axis=0)
    carry = jnp.cumsum(rows, axis=0) - rows
    padded = (tot + MOE_BLOCK - 1) // MOE_BLOCK * MOE_BLOCK
    pad_end = jnp.cumsum(padded)
    pad_start = pad_end - padded
    nblk = -(-(t * TOP_K + nt * N_EXPERTS * (RUN_ALIGN - 1)) // MOE_BLOCK) + N_EXPERTS
    cap = nblk * MOE_BLOCK
    blk_row = jnp.arange(nblk, dtype=jnp.int32) * MOE_BLOCK
    block_e = jnp.minimum(jnp.sum((pad_end[None, :] <= blk_row[:, None]).astype(jnp.int32), axis=1), N_EXPERTS - 1)
    n_used = (pad_end[-1] // MOE_BLOCK).astype(jnp.int32).reshape(1)
    toff = jnp.cumsum(rows, axis=1) - rows
    gstart = pad_start[None, :] + carry
    n_big = rows // BIG_COPY
    n_small = (rows - n_big * BIG_COPY) // RUN_ALIGN
    big_g, big_o, tot_big = _piece_lists(gstart, toff, n_big, jnp.zeros_like(rows), BIG_COPY, MAX_BIG)
    small_g, small_o, tot_small = _piece_lists(gstart, toff, n_small, n_big * BIG_COPY, RUN_ALIGN, MAX_SMALL)
    plan = (big_g, big_o, tot_big, small_g, small_o, tot_small)
    zstart = jnp.concatenate([pad_start + tot, pad_end[-1:]]).astype(jnp.int32)
    zrows = jnp.concatenate([padded - tot, cap - pad_end[-1:]]).astype(jnp.int32)
    block_e = block_e.astype(jnp.int32)
    run_idx = jnp.cumsum(jnp.concatenate([jnp.zeros((1,), jnp.int32),
                                          (block_e[1:] != block_e[:-1]).astype(jnp.int32)]))
    eid = jnp.arange(N_EXPERTS, dtype=jnp.int32)
    later_used = (tot[None, :] > 0) & (eid[None, :] > eid[:, None])
    next_e = jnp.min(jnp.where(later_used, eid[None, :], N_EXPERTS), axis=1)
    next_e = jnp.where(next_e < N_EXPERTS, next_e, -1).astype(jnp.int32)
    nxt = jnp.sum(jnp.where(block_e[:, None] == eid[None, :], next_e[None, :], 0), axis=1).astype(jnp.int32)

    xs = _dispatch(plan, zstart, zrows, ri.reshape(t, LANES), h2.reshape(t, d), cap)
    run_end = jnp.sum(jnp.where(block_e[:, None] == eid[None, :], (pad_start + tot)[None, :], 0), axis=1)
    nvalid = jnp.where(blk_row < pad_end[-1], jnp.clip(run_end - blk_row, 0, MOE_BLOCK), 0).astype(jnp.int32)
    ys = _experts(block_e, n_used, (run_idx % 2).astype(jnp.int32), nxt, nvalid, xs, w_gate_up[0],
                  b_gate_up[0].reshape(N_EXPERTS, 1, -1), w_down[0], b_down[0].reshape(N_EXPERTS, 1, -1))
    return _combine(plan, ys, ri, rw, x1, mod[5], g_ffn_post[0].reshape(1, d))
```

```python
import functools

import jax
import jax.numpy as jnp
from jax import lax
from jax.experimental import pallas as pl
from jax.experimental.pallas import tpu as pltpu

F32 = jnp.float32
BF16 = jnp.bfloat16

D_MODEL = 1024
GRID_W = 64
HEAD_DIM = 64
A_HEADS = 8
A_KV_HEADS = 2
BLOCK = 128
A_SCALE = HEAD_DIM ** -0.5
MLA_HEADS = 8
MLA_NOPE = 64
MLA_ROPE = 32
MLA_V = 64
MLA_Q_LORA = 384
MLA_KV_LORA = 256
MLA_SCALE = (MLA_NOPE + MLA_ROPE) ** -0.5
LOG2_E = 1.4426950408889634
N_EXPERTS = 32
TOP_K = 4
D_FF = 1024
SWIGLU_LIMIT = 7.0
SWIGLU_ALPHA = 1.702
ROPE_BASE = 10000.0
EPS = 1e-6
NEG_INF = -1e30

LANES = 128
VMEM_LIMIT = 56 * 1024 * 1024

TM_IN = 512
GQA_QB = 4
TQ_MLA = 256
TK_MLA = 512
MLA_GROUP = 2
VT_ROWS = 80
TM_OUT = 512
MOE_BLOCK = 512
EXPERT_SUB = 256
RUN_ALIGN = 8
BIG_COPY = 32
SORTED_ROWS = TM_OUT * TOP_K + N_EXPERTS * RUN_ALIGN
LOOP_UNROLL = 4
MAX_BIG = SORTED_ROWS // BIG_COPY
MAX_SMALL = N_EXPERTS * (BIG_COPY // RUN_ALIGN - 1)


def _dot(a, b):
    return jnp.dot(a, b, preferred_element_type=F32)


def _dot_nt(a, b):
    return lax.dot_general(a, b, (((1,), (1,)), ((), ())), preferred_element_type=F32)


def _split_bf16(x):
    hi = x.astype(BF16)
    lo = (x - hi.astype(F32)).astype(BF16)
    return hi, lo


def _rms(x, g):
    return x * lax.rsqrt(jnp.mean(x * x, axis=-1, keepdims=True) + EPS) * g


def _rope(x, cos, sin_signed, first_half, half):
    n = x.shape[-1]
    partner = jnp.where(first_half, pltpu.roll(x, n - half, 1), pltpu.roll(x, half, 1))
    return x * cos + partner * sin_signed


def _adaln_kernel(c_ref, w_ref, b_ref, o_ref):
    c = c_ref[...]
    s = c * jax.nn.sigmoid(c)
    s_hi, s_lo = _split_bf16(s)
    w_hi, w_lo = _split_bf16(w_ref[...])
    o_ref[...] = _dot(s_hi, w_hi) + _dot(s_hi, w_lo) + _dot(s_lo, w_hi) + b_ref[...]


def _adaln(cc, w, b):
    n = w.shape[1]
    tn = 1024
    return pl.pallas_call(
        _adaln_kernel,
        grid=(n // tn,),
        in_specs=[pl.BlockSpec((8, D_MODEL), lambda j: (0, 0)),
                  pl.BlockSpec((D_MODEL, tn), lambda j: (0, j)),
                  pl.BlockSpec((1, tn), lambda j: (0, j))],
        out_specs=pl.BlockSpec((8, tn), lambda j: (0, j)),
        out_shape=jax.ShapeDtypeStruct((8, n), F32),
        compiler_params=pltpu.CompilerParams(dimension_semantics=("arbitrary",),
                                             vmem_limit_bytes=VMEM_LIMIT),
        name="adaln",
    )(cc, w, b)


def _inproj_kernel(x_ref, shift_ref, scale_ref, gpre_ref, ca_ref, sa_ref, cq_ref, sq_ref, ck_ref, sk_ref,
                   wlat_ref, wwin_ref, gq_ref, wuq_ref, gkv_ref, wk_ref, wv_ref,
                   qa_ref, ka_ref, vat_ref, q_ref, k_ref, vt_ref):
    x = x_ref[0]
    tm = x.shape[0]
    h = _rms(x, gpre_ref[...]) * (1.0 + scale_ref[0]) + shift_ref[0]
    hb = h.astype(BF16)
    lat = _dot(hb, wlat_ref[...])
    p = _dot(hb, wwin_ref[...])
    c_ka, c_va = A_HEADS * HEAD_DIM, A_HEADS * HEAD_DIM + LANES
    c_ckv, c_kr = MLA_Q_LORA, MLA_Q_LORA + MLA_KV_LORA
    lane = lax.broadcasted_iota(jnp.int32, (tm, LANES), 1)

    first_a = (lane & 32) == 0
    ca, sa = ca_ref[...], sa_ref[...]
    for j in range(A_HEADS * HEAD_DIM // LANES):
        sl = slice(j * LANES, (j + 1) * LANES)
        qa_ref[0, :, sl] = (_rope(p[:, sl], ca, sa, first_a, 32) * (A_SCALE * LOG2_E)).astype(BF16)
    ka_ref[0] = _rope(p[:, c_ka:c_va], ca, sa, first_a, 32).astype(BF16)
    vat = p[:, c_va:c_va + LANES].T
    ones_a = (lax.broadcasted_iota(jnp.int32, (VT_ROWS - HEAD_DIM, BLOCK), 0) == 0).astype(BF16)
    for j in range(tm // BLOCK):
        for hk in range(A_KV_HEADS):
            vat_ref[0, j, hk, 0:HEAD_DIM, :] = vat[hk * HEAD_DIM:(hk + 1) * HEAD_DIM,
                                                   j * BLOCK:(j + 1) * BLOCK].astype(BF16)
            vat_ref[0, j, hk, HEAD_DIM:VT_ROWS, :] = ones_a

    cqn = _rms(lat[:, 0:c_ckv], gq_ref[...]).astype(BF16)
    qf = _dot(cqn, wuq_ref[...])
    first_q = (lane >= 64) & (lane < 80)
    cq, sq = cq_ref[...], sq_ref[...]
    for hh in range(MLA_HEADS):
        sl = slice(hh * LANES, (hh + 1) * LANES)
        q_ref[0, :, sl] = (_rope(qf[:, sl], cq, sq, first_q, 16) * (MLA_SCALE * LOG2_E)).astype(BF16)

    ckvn = _rms(lat[:, c_ckv:c_kr], gkv_ref[...]).astype(BF16)
    kr = _rope(lat[:, c_kr:c_kr + LANES], ck_ref[...], sk_ref[...], lane < 16, 16)
    kr = pltpu.roll(kr, MLA_NOPE, 1)
    kn = _dot(ckvn, wk_ref[...])
    for hh in range(MLA_HEADS):
        sl = slice(hh * LANES, (hh + 1) * LANES)
        k_ref[0, :, sl] = (kn[:, sl] + kr).astype(BF16)
    vt = _dot_nt(wv_ref[...], ckvn).astype(BF16)
    ones_row = (lax.broadcasted_iota(jnp.int32, (VT_ROWS - MLA_V, tm), 0) == 0).astype(BF16)
    for hh in range(MLA_HEADS):
        vt_ref[0, 0, hh, 0:MLA_V, :] = vt[hh * MLA_V:(hh + 1) * MLA_V, :]
        vt_ref[0, 0, hh, MLA_V:VT_ROWS, :] = ones_row


def _inproj(x, shift, scale, gpre, tabs, wts, tm):
    bx, n, d = x.shape
    ca, sa, cq, sq, ck, sk = tabs
    wlat, wwin, gq, wuq, gkv, wk, wv = wts
    tok = lambda i, b: (b, i, 0)
    vec = lambda i, b: (b, 0, 0)
    tab = lambda i, b: (i, 0)
    cst = lambda i, b: (0, 0)
    in_specs = [pl.BlockSpec((1, tm, d), tok), pl.BlockSpec((1, 1, d), vec), pl.BlockSpec((1, 1, d), vec),
                pl.BlockSpec((1, d), cst)]
    in_specs += [pl.BlockSpec((tm, LANES), tab)] * 6
    in_specs += [pl.BlockSpec(w.shape, cst) for w in wts]
    row_out = lambda w: (pl.BlockSpec((1, tm, w), tok), jax.ShapeDtypeStruct((bx, n, w), BF16))
    outs = [row_out(A_HEADS * HEAD_DIM), row_out(LANES),
            (pl.BlockSpec((1, tm // BLOCK, A_KV_HEADS, VT_ROWS, BLOCK), lambda i, b: (b, i, 0, 0, 0)),
             jax.ShapeDtypeStruct((bx, n // BLOCK, A_KV_HEADS, VT_ROWS, BLOCK), BF16)),
            row_out(MLA_HEADS * LANES), row_out(MLA_HEADS * LANES),
            (pl.BlockSpec((1, 1, MLA_HEADS, VT_ROWS, tm), lambda i, b: (b, i, 0, 0, 0)),
             jax.ShapeDtypeStruct((bx, n // tm, MLA_HEADS, VT_ROWS, tm), BF16))]
    out_specs = [o[0] for o in outs]
    out_shape = [o[1] for o in outs]
    return pl.pallas_call(
        _inproj_kernel,
        grid=(n // tm, bx),
        in_specs=in_specs,
        out_specs=out_specs,
        out_shape=out_shape,
        compiler_params=pltpu.CompilerParams(dimension_semantics=("arbitrary", "arbitrary"),
                                             vmem_limit_bytes=VMEM_LIMIT),
        name="inproj",
    )(x, shift, scale, gpre, ca, sa, cq, sq, ck, sk, wlat, wwin, gq, wuq, gkv, wk, wv)


def _gqa_kernel(sink_ref, q_ref, *refs):
    nwin = GQA_QB + 2
    k_refs, kx_ref = refs[:nwin], refs[nwin]
    v_refs, vx_ref = refs[nwin + 1:2 * nwin + 1], refs[2 * nwin + 1]
    o_ref = refs[2 * nwin + 2]
    n = pl.program_id(1)
    nsteps = pl.num_programs(1)
    nctx = kx_ref.shape[1] // BLOCK
    group = A_HEADS // A_KV_HEADS
    key = lax.broadcasted_iota(jnp.int32, (BLOCK, group * BLOCK), 0)
    qry = lax.broadcasted_iota(jnp.int32, (BLOCK, group * BLOCK), 1) & (BLOCK - 1)
    sts = {}

    def scores(qb):
        kcat = jnp.concatenate([k_refs[qb + j][0] for j in range(3)] + [kx_ref[0]], axis=0)
        for hk in range(A_KV_HEADS):
            qg = jnp.concatenate([q_ref[0, qb * BLOCK:(qb + 1) * BLOCK, hh * HEAD_DIM:(hh + 1) * HEAD_DIM]
                                  for hh in range(hk * group, (hk + 1) * group)], axis=0)
            k_hk = kcat[:, hk * HEAD_DIM:(hk + 1) * HEAD_DIM]
            sts[qb, hk] = _dot_nt(k_hk, qg)

    def finish(qb):
        vis_prev = key >= qry
        vis_next = key <= qry
        if qb == 0:
            vis_prev = key >= qry + jnp.where(n > 0, 0, 2 * BLOCK)
        if qb == GQA_QB - 1:
            vis_next = key <= qry - jnp.where(n < nsteps - 1, 0, 2 * BLOCK)
        outs = []
        for hk in range(A_KV_HEADS):
            st = sts[qb, hk]
            pieces = [jnp.where(vis_prev, st[0:BLOCK], NEG_INF), st[BLOCK:2 * BLOCK],
                      jnp.where(vis_next, st[2 * BLOCK:3 * BLOCK], NEG_INF)]
            pieces += [st[(3 + j) * BLOCK:(4 + j) * BLOCK] for j in range(nctx)]
            sink = sink_ref[hk]
            m = jnp.maximum(functools.reduce(jnp.maximum, [jnp.max(p, axis=0, keepdims=True) for p in pieces]),
                            sink)
            ps = [jnp.exp2(p - m).astype(BF16) for p in pieces]
            vts = [v_refs[qb + j][0, 0, hk] for j in range(3)] + [vx_ref[0, j, hk] for j in range(nctx)]
            pv = functools.reduce(jnp.add, [_dot(vt, p) for vt, p in zip(vts, ps)])
            l = pv[HEAD_DIM:HEAD_DIM + 1, :] + jnp.exp2(sink - m)
            o = pv[0:HEAD_DIM, :] * (1.0 / l)
            outs += [o[:, g * BLOCK:(g + 1) * BLOCK] for g in range(group)]
        for pair in range(A_HEADS // 2):
            both = jnp.concatenate([outs[2 * pair], outs[2 * pair + 1]], axis=0)
            o_ref[0, qb * BLOCK:(qb + 1) * BLOCK, pair * LANES:(pair + 1) * LANES] = both.T.astype(BF16)

    for qb in range(GQA_QB):
        scores(qb)
    for qb in range(GQA_QB):
        finish(qb)


def _gqa(sinkv, qa, ka, vat, kac, vatc):
    b, s, _ = qa.shape
    nb = s // BLOCK
    c = kac.shape[1]
    tq = GQA_QB * BLOCK
    cur = lambda bb, n: (bb, n, 0)
    win = lambda j: (lambda bb, n: (bb, jnp.clip(n * GQA_QB + j - 1, 0, nb - 1), 0))
    vwin = lambda j: (lambda bb, n: (bb, jnp.clip(n * GQA_QB + j - 1, 0, nb - 1), 0, 0, 0))
    vblk = (1, 1, A_KV_HEADS, VT_ROWS, BLOCK)
    in_specs = [pl.BlockSpec(sinkv.shape, lambda bb, n: (0, 0, 0)), pl.BlockSpec((1, tq, A_HEADS * HEAD_DIM), cur)]
    in_specs += [pl.BlockSpec((1, BLOCK, LANES), win(j)) for j in range(GQA_QB + 2)]
    in_specs += [pl.BlockSpec((1, c, LANES), lambda bb, n: (bb, 0, 0))]
    in_specs += [pl.BlockSpec(vblk, vwin(j)) for j in range(GQA_QB + 2)]
    in_specs += [pl.BlockSpec((1, c // BLOCK, A_KV_HEADS, VT_ROWS, BLOCK), lambda bb, n: (bb, 0, 0, 0, 0))]
    return pl.pallas_call(
        _gqa_kernel,
        grid=(b, nb // GQA_QB),
        in_specs=in_specs,
        out_specs=pl.BlockSpec((1, tq, 512), cur),
        out_shape=jax.ShapeDtypeStruct((b, s, 512), BF16),
        compiler_params=pltpu.CompilerParams(dimension_semantics=("arbitrary", "arbitrary"),
                                             vmem_limit_bytes=VMEM_LIMIT),
        name="gqa",
    )(sinkv, qa, *([ka] * (GQA_QB + 2)), kac, *([vat] * (GQA_QB + 2)), vatc)


def _mla_kernel(q_ref, k_ref, vt_ref, kc_ref, vct_ref, o_ref, m_ref, acc_ref, sa_ref, sb_ref):
    nk = k_ref.shape[1] // TK_MLA

    def scores_into(buf, k_blk, heads=range(MLA_HEADS)):
        n = k_blk.shape[0]
        for hh in heads:
            hsl = slice(hh * LANES, (hh + 1) * LANES)
            buf[hh, 0:n, :] = _dot_nt(k_blk[:, hsl], q_ref[0, :, hsl])

    def consume(buf, vt_blk, first, heads=range(MLA_HEADS)):
        n = vt_blk.shape[2]
        for hh in heads:
            st = buf[hh, 0:n, :]
            cmax = jnp.max(st, axis=0, keepdims=True)
            if first:
                m_new = cmax
            else:
                m_old = m_ref[hh, 0:1, :]
                m_new = jnp.maximum(m_old, cmax)
                alpha = jnp.exp2(m_old - m_new)
            p = jnp.exp2(st - m_new).astype(BF16)
            pv = _dot(vt_blk[hh], p)
            acc_ref[hh] = pv if first else alpha * acc_ref[hh] + pv
            m_ref[hh, 0:1, :] = m_new

    def k_chunk(c):
        return k_ref[0, pl.ds(pl.multiple_of(c * TK_MLA, TK_MLA), TK_MLA), :]

    def overlapped(buf_w, k_blk, buf_r, vt_blk, first):
        for g in range(0, MLA_HEADS, MLA_GROUP):
            hs = range(g, g + MLA_GROUP)
            scores_into(buf_w, k_blk, hs)
            consume(buf_r, vt_blk, first, hs)

    scores_into(sa_ref, k_chunk(0))
    overlapped(sb_ref, k_chunk(1), sa_ref, vt_ref[0, 0], True)

    def body(i, _):
        c = 2 * i + 1
        overlapped(sa_ref, k_chunk(c + 1), sb_ref, vt_ref[0, c], False)
        overlapped(sb_ref, k_chunk(c + 2), sa_ref, vt_ref[0, c + 1], False)
        return 0

    lax.fori_loop(0, (nk - 2) // 2, body, 0)
    overlapped(sa_ref, kc_ref[0], sb_ref, vt_ref[0, nk - 1], False)
    consume(sa_ref, vct_ref[0, 0], False)

    outs = [acc_ref[hh, 0:MLA_V, :] * (1.0 / acc_ref[hh, MLA_V:MLA_V + 1, :]) for hh in range(MLA_HEADS)]
    o_ref[0] = jnp.concatenate(outs, axis=0).T.astype(BF16)


def _mla(q, k, vt, kc, vct):
    b, s, _ = q.shape
    c = kc.shape[1]
    nk = s // TK_MLA
    assert vt.shape == (b, nk, MLA_HEADS, VT_ROWS, TK_MLA) and nk % 2 == 0
    return pl.pallas_call(
        _mla_kernel,
        grid=(b, s // TQ_MLA),
        in_specs=[pl.BlockSpec((1, TQ_MLA, 1024), lambda bb, i: (bb, i, 0)),
                  pl.BlockSpec((1, s, 1024), lambda bb, i: (bb, 0, 0)),
                  pl.BlockSpec((1, nk, MLA_HEADS, VT_ROWS, TK_MLA), lambda bb, i: (bb, 0, 0, 0, 0)),
                  pl.BlockSpec((1, c, 1024), lambda bb, i: (bb, 0, 0)),
                  pl.BlockSpec((1, 1, MLA_HEADS, VT_ROWS, c), lambda bb, i: (bb, 0, 0, 0, 0))],
        out_specs=pl.BlockSpec((1, TQ_MLA, 512), lambda bb, i: (bb, i, 0)),
        out_shape=jax.ShapeDtypeStruct((b, s, 512), BF16),
        scratch_shapes=[pltpu.VMEM((MLA_HEADS, 8, TQ_MLA), F32),
                        pltpu.VMEM((MLA_HEADS, VT_ROWS, TQ_MLA), F32),
                        pltpu.VMEM((MLA_HEADS, TK_MLA, TQ_MLA), F32), pltpu.VMEM((MLA_HEADS, TK_MLA, TQ_MLA), F32)],
        compiler_params=pltpu.CompilerParams(dimension_semantics=("arbitrary", "arbitrary"),
                                             vmem_limit_bytes=VMEM_LIMIT),
        name="mla",
    )(q, k, vt, kc, vct)


def _outproj_kernel(oa_ref, ob_ref, x_ref, woa_ref, wob_ref, gpost_ref, gate_ref, gffn_ref, shift_ref, scale_ref,
                    wrh_ref, wrl_ref, br_ref, tri_ref, upper_ref,
                    x1_ref, h2_ref, ri_ref, rw_ref, cnt_ref):
    tm = x_ref.shape[1]
    nsub = 2
    sub = tm // nsub
    rows = [slice(j * sub, (j + 1) * sub) for j in range(nsub)]
    ys = [_dot(oa_ref[0, r, :], woa_ref[...]) + _dot(ob_ref[0, r, :], wob_ref[...]) for r in rows]
    sub_lane_f = lax.broadcasted_iota(jnp.int32, (sub, LANES), 1).astype(F32)
    idx_parts, w_parts = [], []
    for r, y in zip(rows, ys):
        x1 = x_ref[0, r, :] + gate_ref[0] * _rms(y, gpost_ref[...])
        x1_ref[0, r, :] = x1
        h2 = _rms(x1, gffn_ref[...]) * (1.0 + scale_ref[0]) + shift_ref[0]
        h2_ref[0, r, :] = h2.astype(BF16)
        h_hi, h_lo = _split_bf16(h2)
        cur = _dot(h_hi, wrh_ref[...]) + _dot(h_hi, wrl_ref[...]) + _dot(h_lo, wrh_ref[...]) + br_ref[...]
        tops, idxs = [], []
        for _ in range(TOP_K):
            mk = jnp.max(cur, axis=-1, keepdims=True)
            ik = jnp.min(jnp.where(cur == mk, sub_lane_f, float(LANES)), axis=-1, keepdims=True)
            tops.append(mk)
            idxs.append(ik)
            cur = jnp.where(sub_lane_f == ik, -jnp.inf, cur)
        es = [jnp.exp(t - tops[0]) for t in tops]
        inv = 1.0 / functools.reduce(jnp.add, es)
        idx_parts.append(idxs)
        w_parts.append([e * inv for e in es])

    hot_parts = [[(sub_lane_f == ik) for ik in idxs] for idxs in idx_parts]
    onehot = jnp.concatenate([functools.reduce(jnp.add, [o.astype(F32) for o in hots]) for hots in hot_parts],
                             axis=0)
    cnt = jnp.sum(onehot, axis=0, keepdims=True)
    prefix = _dot(tri_ref[...], onehot.astype(BF16))
    chunks = jnp.floor((cnt + (RUN_ALIGN - 1.0)) * (1.0 / RUN_ALIGN))
    run_start = _dot(jnp.broadcast_to(chunks, (8, LANES)).astype(BF16), upper_ref[...])[0:1, :] * RUN_ALIGN
    base = prefix + run_start
    sub_lane = lax.broadcasted_iota(jnp.int32, (sub, LANES), 1)
    for r, hots, idxs, wts in zip(rows, hot_parts, idx_parts, w_parts):
        ri = jnp.zeros((sub, LANES), jnp.int32)
        rw = jnp.zeros((sub, LANES), F32)
        for k in range(TOP_K):
            pos = jnp.sum(jnp.where(hots[k], base[r, :], 0.0), axis=-1, keepdims=True).astype(jnp.int32)
            ri = jnp.where(sub_lane == k, idxs[k].astype(jnp.int32), ri)
            ri = jnp.where(sub_lane == TOP_K + k, pos, ri)
            rw = jnp.where(sub_lane == k, wts[k], rw)
        ri_ref[0, r, :] = ri
        rw_ref[0, r, :] = rw
    cnt_ref[0] = cnt


def _outproj(oa, ob, x, woa, wob, gpost, gate, gffn, shift, scale, wrh, wrl, br, tri, upper):
    b, s, d = x.shape
    tm = TM_OUT
    nt = s // tm
    tok = lambda bb, i: (bb, i, 0)
    vec = lambda bb, i: (bb, 0, 0)
    cst = lambda bb, i: (0, 0)
    return pl.pallas_call(
        _outproj_kernel,
        grid=(b, nt),
        in_specs=[pl.BlockSpec((1, tm, 512), tok), pl.BlockSpec((1, tm, 512), tok), pl.BlockSpec((1, tm, d), tok),
                  pl.BlockSpec((512, d), cst), pl.BlockSpec((512, d), cst), pl.BlockSpec((1, d), cst),
                  pl.BlockSpec((1, 1, d), vec), pl.BlockSpec((1, d), cst),
                  pl.BlockSpec((1, 1, d), vec), pl.BlockSpec((1, 1, d), vec),
                  pl.BlockSpec((d, LANES), cst), pl.BlockSpec((d, LANES), cst), pl.BlockSpec((1, LANES), cst),
                  pl.BlockSpec((tm, tm), cst), pl.BlockSpec((LANES, LANES), cst)],
        out_specs=[pl.BlockSpec((1, tm, d), tok), pl.BlockSpec((1, tm, d), tok),
                   pl.BlockSpec((1, tm, LANES), tok), pl.BlockSpec((1, tm, LANES), tok),
                   pl.BlockSpec((1, 1, LANES), lambda bb, i: (bb * nt + i, 0, 0))],
        out_shape=[jax.ShapeDtypeStruct((b, s, d), F32), jax.ShapeDtypeStruct((b, s, d), BF16),
                   jax.ShapeDtypeStruct((b, s, LANES), jnp.int32), jax.ShapeDtypeStruct((b, s, LANES), F32),
                   jax.ShapeDtypeStruct((b * nt, 1, LANES), F32)],
        compiler_params=pltpu.CompilerParams(dimension_semantics=("arbitrary", "arbitrary"),
                                             vmem_limit_bytes=VMEM_LIMIT),
        name="outproj",
    )(oa, ob, x, woa, wob, gpost, gate, gffn, shift, scale, wrh, wrl, br, tri, upper)


def _start_pieces(g, o, rows, make_copy):
    n_big = lax.shift_right_logical(rows, BIG_COPY.bit_length() - 1)
    n_small = lax.shift_right_logical(rows, RUN_ALIGN.bit_length() - 1) & (BIG_COPY // RUN_ALIGN - 1)

    def big(c, _):
        off = c * BIG_COPY
        make_copy(pl.multiple_of(g + off, RUN_ALIGN), pl.multiple_of(o + off, RUN_ALIGN), BIG_COPY).start()
        return 0

    def small(c, _):
        off = n_big * BIG_COPY + c * RUN_ALIGN
        make_copy(pl.multiple_of(g + off, RUN_ALIGN), pl.multiple_of(o + off, RUN_ALIGN), RUN_ALIGN).start()
        return 0

    lax.fori_loop(0, n_big, big, 0)
    lax.fori_loop(0, n_small, small, 0)
    return n_big, n_small


def _unrolled_loop(n, step):
    shift = LOOP_UNROLL.bit_length() - 1
    trips = lax.shift_right_logical(n, shift)

    def many(t, _):
        for u in range(LOOP_UNROLL):
            step(t * LOOP_UNROLL + u)
        return 0

    def one(i, _):
        step(i)
        return 0

    lax.fori_loop(0, trips, many, 0)
    lax.fori_loop(trips * LOOP_UNROLL, n, one, 0)


def _wait_pieces(n_big, n_small, make_copy):
    _unrolled_loop(n_big, lambda i: make_copy(0, 0, BIG_COPY).wait())
    _unrolled_loop(n_small, lambda i: make_copy(0, 0, RUN_ALIGN).wait())


def _start_listed_copies(plan, tile, make_copy):
    big_g, big_o, n_big, small_g, small_o, n_small = plan

    def start_list(g_ref, o_ref, n, width, rows):
        def start(p):
            j = tile * width + p
            make_copy(pl.multiple_of(g_ref[j], RUN_ALIGN), pl.multiple_of(o_ref[j], RUN_ALIGN), rows).start()

        _unrolled_loop(n, start)

    start_list(big_g, big_o, n_big[tile], MAX_BIG, BIG_COPY)
    start_list(small_g, small_o, n_small[tile], MAX_SMALL, RUN_ALIGN)


def _dispatch_kernel(bg_ref, bo_ref, nb_ref, sg_ref, so_ref, ns_ref, zs_ref, zr_ref, ri_ref, h_ref, xs_ref,
                     sorted_ref, zero_ref, sem):
    plan = (bg_ref, bo_ref, nb_ref, sg_ref, so_ref, ns_ref)
    tile = pl.program_id(0)
    last = pl.num_programs(0) - 1
    slot = tile % 2
    tm = h_ref.shape[0]
    post = ri_ref[...].astype(F32).T
    h = h_ref[...]
    rb_rows = 256
    pos_blk = [jnp.floor(post[TOP_K + k:TOP_K + k + 1, :] * (1.0 / rb_rows)) for k in range(TOP_K)]
    pos_off = [post[TOP_K + k:TOP_K + k + 1, :] - rb_rows * pos_blk[k] for k in range(TOP_K)]
    row = lax.broadcasted_iota(jnp.int32, (rb_rows, tm), 0).astype(F32).astype(BF16)
    one, zero = jnp.ones((), BF16), jnp.zeros((), BF16)

    for rb in range(SORTED_ROWS // rb_rows):
        perm = functools.reduce(jnp.add, [
            jnp.where(row == jnp.where(pos_blk[k] == float(rb), pos_off[k], -1.0).astype(BF16), one, zero)
            for k in range(TOP_K)])
        xr = _dot(perm, h)
        lo = lax.bitcast_convert_type(xr[:, :512], jnp.uint32)
        hi = lax.bitcast_convert_type(xr[:, 512:], jnp.uint32)
        sorted_ref[slot, rb * rb_rows:(rb + 1) * rb_rows, :] = (lo >> 16) | (hi & jnp.uint32(0xFFFF0000))

    def run_copy(sl):
        def make_copy(g, o, rows):
            return pltpu.make_async_copy(sorted_ref.at[sl, pl.ds(o, rows)], xs_ref.at[pl.ds(g, rows)], sem.at[sl])
        return make_copy

    _start_listed_copies(plan, tile, run_copy(slot))

    @pl.when(tile > 0)
    def _():
        _wait_pieces(nb_ref[tile - 1], ns_ref[tile - 1], run_copy(1 - slot))

    @pl.when(tile == last)
    def _():
        _wait_pieces(nb_ref[tile], ns_ref[tile], run_copy(slot))
        zero_ref[...] = jnp.zeros_like(zero_ref)

        def zero_copy(g, o, rows):
            return pltpu.make_async_copy(zero_ref.at[pl.ds(0, rows)], xs_ref.at[pl.ds(g, rows)], sem.at[2])

        def per_expert(e, totals):
            nb, ns = _start_pieces(zs_ref[e], 0, zr_ref[e], zero_copy)
            return totals[0] + nb, totals[1] + ns

        zb, zs = lax.fori_loop(0, zs_ref.shape[0], per_expert, (0, 0))
        _wait_pieces(zb, zs, zero_copy)


def _dispatch(plan, zstart, zrows, ri, h2, cap):
    t, d = h2.shape
    tm = TM_OUT
    return pl.pallas_call(
        _dispatch_kernel,
        grid_spec=pltpu.PrefetchScalarGridSpec(
            num_scalar_prefetch=8,
            grid=(t // tm,),
            in_specs=[pl.BlockSpec((tm, LANES), lambda i, *_: (i, 0)),
                      pl.BlockSpec((tm, d), lambda i, *_: (i, 0))],
            out_specs=pl.BlockSpec(memory_space=pl.ANY),
            scratch_shapes=[pltpu.VMEM((2, SORTED_ROWS, d // 2), jnp.uint32),
                            pltpu.VMEM((BIG_COPY, d // 2), jnp.uint32),
                            pltpu.SemaphoreType.DMA((3,))]),
        out_shape=jax.ShapeDtypeStruct((cap, d // 2), jnp.uint32),
        compiler_params=pltpu.CompilerParams(dimension_semantics=("arbitrary",),
                                             vmem_limit_bytes=VMEM_LIMIT),
        name="dispatch",
    )(*plan, zstart, zrows, ri, h2)


def _pack_bf16_pairs(x):
    n = x.shape[1] // 2
    bits = lax.bitcast_convert_type(x.astype(BF16).astype(F32), jnp.uint32)
    return (bits[:, :n] >> 16) | (bits[:, n:] & jnp.uint32(0xFFFF0000))


def _unpack_bf16_pairs(w):
    lo = lax.bitcast_convert_type(w << 16, F32).astype(BF16)
    hi = lax.bitcast_convert_type(w & jnp.uint32(0xFFFF0000), F32).astype(BF16)
    return lo, hi


def _experts_kernel(be_ref, nu_ref, slot_ref, nxt_ref, nv_ref, xs_ref, wgu_hbm, bgu_ref, wd_hbm, bd_ref, ys_ref,
                    wgu_f, wd_f, wgu_b, wd_b, sem):
    del nu_ref
    i = pl.program_id(0)
    e = be_ref[i]
    slot = slot_ref[i]
    valid = nv_ref[i]
    used = valid > 0
    run_start = used & ((i == 0) | (be_ref[jnp.maximum(i - 1, 0)] != e))

    def weight_copies(ex, sl):
        return (pltpu.make_async_copy(wgu_hbm.at[ex], wgu_f.at[sl], sem.at[0, sl]),
                pltpu.make_async_copy(wd_hbm.at[ex], wd_f.at[sl], sem.at[1, sl]))

    @pl.when(i == 0)
    def _():
        for cp in weight_copies(e, slot):
            cp.start()

    @pl.when(run_start)
    def _():
        for cp in weight_copies(e, slot):
            cp.wait()
        nxt = nxt_ref[i]

        @pl.when(nxt >= 0)
        def _():
            for cp in weight_copies(nxt, 1 - slot):
                cp.start()

        wgu_b[...] = wgu_f[slot].astype(BF16)
        wd_b[...] = wd_f[slot].astype(BF16)

    def compute(pieces):
        rows = [slice(j * EXPERT_SUB, (j + 1) * EXPERT_SUB) for j in pieces]
        gus = []
        for r in rows:
            x_lo, x_hi = _unpack_bf16_pairs(xs_ref[r, :])
            gus.append(_dot(x_lo, wgu_b[0:512, :]) + _dot(x_hi, wgu_b[512:, :]) + bgu_ref[0])
        for r, gu in zip(rows, gus):
            gate = jnp.minimum(gu[:, :D_FF], SWIGLU_LIMIT)
            lin = jnp.clip(gu[:, D_FF:], -SWIGLU_LIMIT, SWIGLU_LIMIT)
            act = (lin + 1.0) * (gate * jax.nn.sigmoid(SWIGLU_ALPHA * gate))
            ys_ref[r, :] = _pack_bf16_pairs(_dot(act.astype(BF16), wd_b[...]) + bd_ref[0])

    npieces = MOE_BLOCK // EXPERT_SUB
    for n in range(npieces + 1):
        lo, hi = (n - 1) * EXPERT_SUB, n * EXPERT_SUB
        cond = (valid == 0) if n == 0 else ((valid > lo) & (valid <= hi))

        @pl.when(cond)
        def _(n=n):
            if n:
                compute(range(n))
            if n < npieces:
                ys_ref[n * EXPERT_SUB:, :] = jnp.zeros(((npieces - n) * EXPERT_SUB, D_MODEL // 2), jnp.uint32)


def _experts(block_e, n_used, slot, nxt, nvalid, xs, wgu, bgu, wd, bd):
    cap = xs.shape[0]
    nblk = cap // MOE_BLOCK
    return pl.pallas_call(
        _experts_kernel,
        grid_spec=pltpu.PrefetchScalarGridSpec(
            num_scalar_prefetch=5,
            grid=(nblk,),
            in_specs=[pl.BlockSpec((MOE_BLOCK, 512), lambda i, be, nu, *_: (jnp.minimum(i, nu[0] - 1), 0)),
                      pl.BlockSpec(memory_space=pl.ANY),
                      pl.BlockSpec((1, 1, 2 * D_FF), lambda i, be, *_: (be[i], 0, 0)),
                      pl.BlockSpec(memory_space=pl.ANY),
                      pl.BlockSpec((1, 1, D_MODEL), lambda i, be, *_: (be[i], 0, 0))],
            out_specs=pl.BlockSpec((MOE_BLOCK, D_MODEL // 2), lambda i, *_: (i, 0)),
            scratch_shapes=[pltpu.VMEM((2, D_MODEL, 2 * D_FF), F32), pltpu.VMEM((2, D_FF, D_MODEL), F32),
                            pltpu.VMEM((D_MODEL, 2 * D_FF), BF16), pltpu.VMEM((D_FF, D_MODEL), BF16),
                            pltpu.SemaphoreType.DMA((2, 2))]),
        out_shape=jax.ShapeDtypeStruct((cap, D_MODEL // 2), jnp.uint32),
        compiler_params=pltpu.CompilerParams(dimension_semantics=("arbitrary",),
                                             vmem_limit_bytes=VMEM_LIMIT),
        name="experts",
    )(block_e, n_used, slot, nxt, nvalid, xs, wgu, bgu, wd, bd)


def _combine_kernel(bg_ref, bo_ref, nb_ref, sg_ref, so_ref, ns_ref, ys_ref, ri_ref, rw_ref, x1_ref, gate_ref, g_ref,
                    o_ref, ybuf, sem):
    plan = (bg_ref, bo_ref, nb_ref, sg_ref, so_ref, ns_ref)
    nt = pl.num_programs(1)
    tile = pl.program_id(0) * nt + pl.program_id(1)
    ntiles = pl.num_programs(0) * nt
    slot = tile % 2
    tm = x1_ref.shape[1]

    def run_copy(sl):
        def make_copy(g, o, rows):
            return pltpu.make_async_copy(ys_ref.at[pl.ds(g, rows)], ybuf.at[sl, pl.ds(o, rows)], sem.at[sl])
        return make_copy

    @pl.when(tile == 0)
    def _():
        ybuf[...] = jnp.zeros_like(ybuf)
        _start_listed_copies(plan, tile, run_copy(slot))

    @pl.when(tile + 1 < ntiles)
    def _():
        _start_listed_copies(plan, tile + 1, run_copy(1 - slot))

    _wait_pieces(nb_ref[tile], ns_ref[tile], run_copy(slot))

    posf = ri_ref[0].astype(F32)
    rw = rw_ref[0]
    cb_cols = 256
    pos = [posf[:, TOP_K + k:TOP_K + k + 1] for k in range(TOP_K)]
    pos_blk = [jnp.floor(p * (1.0 / cb_cols)) for p in pos]
    pos_off = [p - cb_cols * b for p, b in zip(pos, pos_blk)]
    wk = [rw[:, k:k + 1].astype(BF16) for k in range(TOP_K)]
    col = lax.broadcasted_iota(jnp.int32, (tm, cb_cols), 1).astype(F32).astype(BF16)
    zero = jnp.zeros((), BF16)
    f_lo = jnp.zeros((tm, D_MODEL // 2), F32)
    f_hi = jnp.zeros((tm, D_MODEL // 2), F32)
    for cb in range(SORTED_ROWS // cb_cols):
        wp = functools.reduce(jnp.add, [
            jnp.where(col == jnp.where(pos_blk[k] == float(cb), pos_off[k], -1.0).astype(BF16), wk[k], zero)
            for k in range(TOP_K)])
        y_lo, y_hi = _unpack_bf16_pairs(ybuf[slot, cb * cb_cols:(cb + 1) * cb_cols, :])
        f_lo = f_lo + _dot(wp, y_lo)
        f_hi = f_hi + _dot(wp, y_hi)
    f = jnp.concatenate([f_lo, f_hi], axis=1)
    o_ref[0] = x1_ref[0] + gate_ref[0] * _rms(f, g_ref[...])


def _combine(plan, ys, ri, rw, x1, gate, g):
    b, s, d = x1.shape
    tm = TM_OUT
    tok = lambda bb, i, *_: (bb, i, 0)
    return pl.pallas_call(
        _combine_kernel,
        grid_spec=pltpu.PrefetchScalarGridSpec(
            num_scalar_prefetch=6,
            grid=(b, s // tm),
            in_specs=[pl.BlockSpec(memory_space=pl.ANY),
                      pl.BlockSpec((1, tm, LANES), tok), pl.BlockSpec((1, tm, LANES), tok),
                      pl.BlockSpec((1, tm, d), tok),
                      pl.BlockSpec((1, 1, d), lambda bb, i, *_: (bb, 0, 0)),
                      pl.BlockSpec((1, d), lambda bb, i, *_: (0, 0))],
            out_specs=pl.BlockSpec((1, tm, d), tok),
            scratch_shapes=[pltpu.VMEM((2, SORTED_ROWS, d // 2), jnp.uint32), pltpu.SemaphoreType.DMA((2,))]),
        out_shape=jax.ShapeDtypeStruct((b, s, d), F32),
        compiler_params=pltpu.CompilerParams(dimension_semantics=("arbitrary", "arbitrary"),
                                             vmem_limit_bytes=VMEM_LIMIT),
        name="combine",
    )(*plan, ys, ri, rw, x1, gate, g)


def _rope_angles(rows, rot_dim):
    row = jnp.broadcast_to(jnp.arange(rows, dtype=F32)[:, None], (rows, GRID_W)).reshape(-1)
    col = jnp.broadcast_to(jnp.arange(GRID_W, dtype=F32)[None, :], (rows, GRID_W)).reshape(-1)
    quarter = rot_dim // 4
    inv_freq = ROPE_BASE ** (-jnp.arange(quarter, dtype=F32) / quarter)
    ang = jnp.concatenate([row[:, None] * inv_freq, col[:, None] * inv_freq], axis=-1)
    return jnp.cos(ang), jnp.sin(ang)


def _rope_tables(s):
    ca, sa = _rope_angles(s // GRID_W, HEAD_DIM)
    cb, sb = _rope_angles(s // GRID_W, MLA_ROPE)
    one = lambda w: jnp.ones((s, w), F32)
    zero = lambda w: jnp.zeros((s, w), F32)
    cos_a = jnp.tile(jnp.concatenate([ca, ca], 1), (1, 2))
    sin_a = jnp.tile(jnp.concatenate([-sa, sa], 1), (1, 2))
    cos_q = jnp.concatenate([one(64), cb, cb, one(32)], 1)
    sin_q = jnp.concatenate([zero(64), -sb, sb, zero(32)], 1)
    cos_k = jnp.concatenate([cb, cb, one(96)], 1)
    sin_k = jnp.concatenate([-sb, sb, zero(96)], 1)
    return cos_a, sin_a, cos_q, sin_q, cos_k, sin_k


def _piece_lists(slot_start, tile_start, counts, first_off, piece, width):
    ne = counts.shape[1]
    cum = jnp.cumsum(counts, axis=1)
    p = jnp.arange(width, dtype=jnp.int32)
    owner = jnp.minimum(jnp.sum((cum[:, None, :] <= p[None, :, None]).astype(jnp.int32), axis=2), ne - 1)
    sel = owner[:, :, None] == jnp.arange(ne, dtype=jnp.int32)[None, None, :]
    pick = lambda a: jnp.sum(jnp.where(sel, a[:, None, :], 0), axis=2)
    off = pick(first_off) + (p[None, :] - pick(cum - counts)) * piece
    as_list = lambda a: a.reshape(-1).astype(jnp.int32)
    return as_list(pick(slot_start) + off), as_list(pick(tile_start) + off), cum[:, -1].astype(jnp.int32)


def _identity_tables(n):
    one, zero = jnp.ones((n, LANES), F32), jnp.zeros((n, LANES), F32)
    return one, zero, one, zero, one, zero


def _inproj_weights(w_in, g_q_a, w_uq, g_kv_a, w_ukv):
    d = w_in.shape[0]
    wq, wk, wv = w_in[:, :512], w_in[:, 512:640], w_in[:, 640:768]
    wcq, wckv, wkr = w_in[:, 768:1152], w_in[:, 1152:1408], w_in[:, 1408:1440]
    wkr_p = jnp.concatenate([wkr, jnp.zeros((d, LANES - MLA_ROPE), F32)], 1)
    wlat = jnp.concatenate([wcq, wckv, wkr_p], 1).astype(BF16)
    wwin = jnp.concatenate([wq, wk, wv], 1).astype(BF16)
    wuq = jnp.pad(w_uq.reshape(MLA_Q_LORA, MLA_HEADS, MLA_NOPE + MLA_ROPE), ((0, 0), (0, 0), (0, 32)))
    wuq = wuq.reshape(MLA_Q_LORA, MLA_HEADS * LANES).astype(BF16)
    wukv = w_ukv.reshape(MLA_KV_LORA, MLA_HEADS, MLA_NOPE + MLA_V)
    wkk = jnp.pad(wukv[:, :, :MLA_NOPE], ((0, 0), (0, 0), (0, 64))).reshape(MLA_KV_LORA, MLA_HEADS * LANES)
    wvv = wukv[:, :, MLA_NOPE:].reshape(MLA_KV_LORA, MLA_HEADS * MLA_V)
    return (wlat, wwin, g_q_a.reshape(1, -1), wuq, g_kv_a.reshape(1, -1), wkk.astype(BF16), wvv.T.astype(BF16))


def kernel(x, c, ctx, c_ctx, w_ada, b_ada, g_mix_pre, g_mix_post, w_in, sink, g_q_a, w_uq, g_kv_a, w_ukv, w_o,
           g_ffn_pre, g_ffn_post, w_router, b_router, w_gate_up, b_gate_up, w_down, b_down):
    b, s, d = x.shape
    cl = ctx.shape[1]
    t = b * s

    cc = jnp.zeros((8, d), F32).at[:b].set(c).at[b].set(c_ctx)
    mod_all = _adaln(cc, w_ada[0], b_ada[0].reshape(1, -1))
    mod = [mod_all[:b, i * d:(i + 1) * d].reshape(b, 1, d) for i in range(6)]
    mod_c = [jnp.broadcast_to(mod_all[b, i * d:(i + 1) * d].reshape(1, 1, d), (b, 1, d)) for i in range(2)]

    wts = _inproj_weights(w_in[0], g_q_a[0], w_uq[0], g_kv_a[0], w_ukv[0])
    gpre = g_mix_pre[0].reshape(1, d)
    qa, ka, vat, q, k, vt = _inproj(x, mod[0], mod[1], gpre, _rope_tables(s), wts, TM_IN)
    _, kac, vatc, _, kc, vct = _inproj(ctx, mod_c[0], mod_c[1], gpre, _identity_tables(cl), wts, cl)

    sinkv = jnp.repeat(sink[0] * LOG2_E, BLOCK).reshape(A_KV_HEADS, 1, -1)
    out_a = _gqa(sinkv, qa, ka, vat, kac, vatc)
    out_b = _mla(q, k, vt, kc, vct)

    wo = w_o[0].astype(BF16)
    wr = jnp.pad(w_router[0], ((0, 0), (0, LANES - N_EXPERTS)))
    wr_hi, wr_lo = _split_bf16(wr)
    br = jnp.concatenate([b_router[0], jnp.full((LANES - N_EXPERTS,), NEG_INF, F32)]).reshape(1, LANES)
    ii = jnp.arange(TM_OUT)
    tri = (ii[:, None] > ii[None, :]).astype(BF16)
    jj = jnp.arange(LANES)
    upper = (jj[:, None] < jj[None, :]).astype(BF16)
    x1, h2, ri, rw, cnt = _outproj(out_a, out_b, x, wo[:512], wo[512:], g_mix_post[0].reshape(1, d), mod[2],
                                   g_ffn_pre[0].reshape(1, d), mod[3], mod[4], wr_hi, wr_lo, br, tri, upper)

    cnt = cnt[:, 0, :N_EXPERTS].astype(jnp.int32)
    nt = cnt.shape[0]
    rows = (cnt + RUN_ALIGN - 1) // RUN_ALIGN * RUN_ALIGN
    tot = jnp.sum(rows, axis=0)
    carry = jnp.cumsum(rows, axis=0) - rows
    padded = (tot + MOE_BLOCK - 1) // MOE_BLOCK * MOE_BLOCK
    pad_end = jnp.cumsum(padded)
    pad_start = pad_end - padded
    nblk = -(-(t * TOP_K + nt * N_EXPERTS * (RUN_ALIGN - 1)) // MOE_BLOCK) + N_EXPERTS
    cap = nblk * MOE_BLOCK
    blk_row = jnp.arange(nblk, dtype=jnp.int32) * MOE_BLOCK
    block_e = jnp.minimum(jnp.sum((pad_end[None, :] <= blk_row[:, None]).astype(jnp.int32), axis=1), N_EXPERTS - 1)
    n_used = (pad_end[-1] // MOE_BLOCK).astype(jnp.int32).reshape(1)
    toff = jnp.cumsum(rows, axis=1) - rows
    gstart = pad_start[None, :] + carry
    n_big = rows // BIG_COPY
    n_small = (rows - n_big * BIG_COPY) // RUN_ALIGN
    big_g, big_o, tot_big = _piece_lists(gstart, toff, n_big, jnp.zeros_like(rows), BIG_COPY, MAX_BIG)
    small_g, small_o, tot_small = _piece_lists(gstart, toff, n_small, n_big * BIG_COPY, RUN_ALIGN, MAX_SMALL)
    plan = (big_g, big_o, tot_big, small_g, small_o, tot_small)
    zstart = jnp.concatenate([pad_start + tot, pad_end[-1:]]).astype(jnp.int32)
    zrows = jnp.concatenate([padded - tot, cap - pad_end[-1:]]).astype(jnp.int32)
    block_e = block_e.astype(jnp.int32)
    run_idx = jnp.cumsum(jnp.concatenate([jnp.zeros((1,), jnp.int32),
                                          (block_e[1:] != block_e[:-1]).astype(jnp.int32)]))
    eid = jnp.arange(N_EXPERTS, dtype=jnp.int32)
    later_used = (tot[None, :] > 0) & (eid[None, :] > eid[:, None])
    next_e = jnp.min(jnp.where(later_used, eid[None, :], N_EXPERTS), axis=1)
    next_e = jnp.where(next_e < N_EXPERTS, next_e, -1).astype(jnp.int32)
    nxt = jnp.sum(jnp.where(block_e[:, None] == eid[None, :], next_e[None, :], 0), axis=1).astype(jnp.int32)

    xs = _dispatch(plan, zstart, zrows, ri.reshape(t, LANES), h2.reshape(t, d), cap)
    run_end = jnp.sum(jnp.where(block_e[:, None] == eid[None, :], (pad_start + tot)[None, :], 0), axis=1)
    nvalid = jnp.where(blk_row < pad_end[-1], jnp.clip(run_end - blk_row, 0, MOE_BLOCK), 0).astype(jnp.int32)
    ys = _experts(block_e, n_used, (run_idx % 2).astype(jnp.int32), nxt, nvalid, xs, w_gate_up[0],
                  b_gate_up[0].reshape(N_EXPERTS, 1, -1), w_down[0], b_down[0].reshape(N_EXPERTS, 1, -1))
    return _combine(plan, ys, ri, rw, x1, mod[5], g_ffn_post[0].reshape(1, d))
```

```python
import functools

import jax
import jax.numpy as jnp
from jax import lax
from jax.experimental import pallas as pl
from jax.experimental.pallas import tpu as pltpu

F32 = jnp.float32
BF16 = jnp.bfloat16

D_MODEL = 1024
GRID_W = 64
HEAD_DIM = 64
A_HEADS = 8
A_KV_HEADS = 2
BLOCK = 128
A_SCALE = HEAD_DIM ** -0.5
MLA_HEADS = 8
MLA_NOPE = 64
MLA_ROPE = 32
MLA_V = 64
MLA_Q_LORA = 384
MLA_KV_LORA = 256
MLA_SCALE = (MLA_NOPE + MLA_ROPE) ** -0.5
LOG2_E = 1.4426950408889634
N_EXPERTS = 32
TOP_K = 4
D_FF = 1024
SWIGLU_LIMIT = 7.0
SWIGLU_ALPHA = 1.702
ROPE_BASE = 10000.0
EPS = 1e-6
NEG_INF = -1e30

LANES = 128
VMEM_LIMIT = 56 * 1024 * 1024

TM_IN = 512
GQA_QB = 8
TQ_MLA = 256
TK_MLA = 512
MLA_GROUP = 2
VT_ROWS = 80
TM_OUT = 512
MOE_BLOCK = 512
EXPERT_SUB = 256
EXPERT_STEP = 128
RUN_ALIGN = 8
BIG_COPY = 32
SORTED_ROWS = TM_OUT * TOP_K + N_EXPERTS * RUN_ALIGN
LOOP_UNROLL = 4
MAX_BIG = SORTED_ROWS // BIG_COPY
MAX_SMALL = N_EXPERTS * (BIG_COPY // RUN_ALIGN - 1)


def _dot(a, b):
    return jnp.dot(a, b, preferred_element_type=F32)


def _dot_nt(a, b):
    return lax.dot_general(a, b, (((1,), (1,)), ((), ())), preferred_element_type=F32)


def _split_bf16(x):
    hi = x.astype(BF16)
    lo = (x - hi.astype(F32)).astype(BF16)
    return hi, lo


def _rms(x, g):
    return x * lax.rsqrt(jnp.mean(x * x, axis=-1, keepdims=True) + EPS) * g


def _rope(x, cos, sin_signed, first_half, half):
    n = x.shape[-1]
    partner = jnp.where(first_half, pltpu.roll(x, n - half, 1), pltpu.roll(x, half, 1))
    return x * cos + partner * sin_signed


def _adaln_kernel(c_ref, w_ref, b_ref, o_ref):
    c = c_ref[...]
    s = c * jax.nn.sigmoid(c)
    s_hi, s_lo = _split_bf16(s)
    w_hi, w_lo = _split_bf16(w_ref[...])
    o_ref[...] = _dot(s_hi, w_hi) + _dot(s_hi, w_lo) + _dot(s_lo, w_hi) + b_ref[...]


def _adaln(cc, w, b):
    n = w.shape[1]
    tn = 1024
    return pl.pallas_call(
        _adaln_kernel,
        grid=(n // tn,),
        in_specs=[pl.BlockSpec((8, D_MODEL), lambda j: (0, 0)),
                  pl.BlockSpec((D_MODEL, tn), lambda j: (0, j)),
                  pl.BlockSpec((1, tn), lambda j: (0, j))],
        out_specs=pl.BlockSpec((8, tn), lambda j: (0, j)),
        out_shape=jax.ShapeDtypeStruct((8, n), F32),
        compiler_params=pltpu.CompilerParams(dimension_semantics=("arbitrary",),
                                             vmem_limit_bytes=VMEM_LIMIT),
        name="adaln",
    )(cc, w, b)


def _inproj_kernel(x_ref, shift_ref, scale_ref, gpre_ref, ca_ref, sa_ref, cq_ref, sq_ref, ck_ref, sk_ref,
                   wlat_ref, wwin_ref, gq_ref, wuq_ref, gkv_ref, wk_ref, wv_ref,
                   qa_ref, ka_ref, vat_ref, q_ref, k_ref, vt_ref):
    x = x_ref[0]
    tm = x.shape[0]
    h = _rms(x, gpre_ref[...]) * (1.0 + scale_ref[0]) + shift_ref[0]
    hb = h.astype(BF16)
    lat = _dot(hb, wlat_ref[...])
    p = _dot(hb, wwin_ref[...])
    c_ka, c_va = A_HEADS * HEAD_DIM, A_HEADS * HEAD_DIM + LANES
    c_ckv, c_kr = MLA_Q_LORA, MLA_Q_LORA + MLA_KV_LORA
    lane = lax.broadcasted_iota(jnp.int32, (tm, LANES), 1)

    first_a = (lane & 32) == 0
    ca, sa = ca_ref[...], sa_ref[...]
    for j in range(A_HEADS * HEAD_DIM // LANES):
        sl = slice(j * LANES, (j + 1) * LANES)
        qa_ref[0, :, sl] = (_rope(p[:, sl], ca, sa, first_a, 32) * (A_SCALE * LOG2_E)).astype(BF16)
    ka_ref[0] = _rope(p[:, c_ka:c_va], ca, sa, first_a, 32).astype(BF16)
    vat = p[:, c_va:c_va + LANES].T
    ones_a = (lax.broadcasted_iota(jnp.int32, (VT_ROWS - HEAD_DIM, BLOCK), 0) == 0).astype(BF16)
    for j in range(tm // BLOCK):
        for hk in range(A_KV_HEADS):
            vat_ref[0, j, hk, 0:HEAD_DIM, :] = vat[hk * HEAD_DIM:(hk + 1) * HEAD_DIM,
                                                   j * BLOCK:(j + 1) * BLOCK].astype(BF16)
            vat_ref[0, j, hk, HEAD_DIM:VT_ROWS, :] = ones_a

    cqn = _rms(lat[:, 0:c_ckv], gq_ref[...]).astype(BF16)
    qf = _dot(cqn, wuq_ref[...])
    first_q = (lane >= 64) & (lane < 80)
    cq, sq = cq_ref[...], sq_ref[...]
    for hh in range(MLA_HEADS):
        sl = slice(hh * LANES, (hh + 1) * LANES)
        q_ref[0, :, sl] = (_rope(qf[:, sl], cq, sq, first_q, 16) * (MLA_SCALE * LOG2_E)).astype(BF16)

    ckvn = _rms(lat[:, c_ckv:c_kr], gkv_ref[...]).astype(BF16)
    kr = _rope(lat[:, c_kr:c_kr + LANES], ck_ref[...], sk_ref[...], lane < 16, 16)
    kr = pltpu.roll(kr, MLA_NOPE, 1)
    kn = _dot(ckvn, wk_ref[...])
    for hh in range(MLA_HEADS):
        sl = slice(hh * LANES, (hh + 1) * LANES)
        k_ref[0, :, sl] = (kn[:, sl] + kr).astype(BF16)
    vt = _dot_nt(wv_ref[...], ckvn).astype(BF16)
    ones_row = (lax.broadcasted_iota(jnp.int32, (VT_ROWS - MLA_V, tm), 0) == 0).astype(BF16)
    for hh in range(MLA_HEADS):
        vt_ref[0, 0, hh, 0:MLA_V, :] = vt[hh * MLA_V:(hh + 1) * MLA_V, :]
        vt_ref[0, 0, hh, MLA_V:VT_ROWS, :] = ones_row


def _inproj(x, shift, scale, gpre, tabs, wts, tm):
    bx, n, d = x.shape
    ca, sa, cq, sq, ck, sk = tabs
    wlat, wwin, gq, wuq, gkv, wk, wv = wts
    tok = lambda i, b: (b, i, 0)
    vec = lambda i, b: (b, 0, 0)
    tab = lambda i, b: (i, 0)
    cst = lambda i, b: (0, 0)
    in_specs = [pl.BlockSpec((1, tm, d), tok), pl.BlockSpec((1, 1, d), vec), pl.BlockSpec((1, 1, d), vec),
                pl.BlockSpec((1, d), cst)]
    in_specs += [pl.BlockSpec((tm, LANES), tab)] * 6
    in_specs += [pl.BlockSpec(w.shape, cst) for w in wts]
    row_out = lambda w: (pl.BlockSpec((1, tm, w), tok), jax.ShapeDtypeStruct((bx, n, w), BF16))
    outs = [row_out(A_HEADS * HEAD_DIM), row_out(LANES),
            (pl.BlockSpec((1, tm // BLOCK, A_KV_HEADS, VT_ROWS, BLOCK), lambda i, b: (b, i, 0, 0, 0)),
             jax.ShapeDtypeStruct((bx, n // BLOCK, A_KV_HEADS, VT_ROWS, BLOCK), BF16)),
            row_out(MLA_HEADS * LANES), row_out(MLA_HEADS * LANES),
            (pl.BlockSpec((1, 1, MLA_HEADS, VT_ROWS, tm), lambda i, b: (b, i, 0, 0, 0)),
             jax.ShapeDtypeStruct((bx, n // tm, MLA_HEADS, VT_ROWS, tm), BF16))]
    out_specs = [o[0] for o in outs]
    out_shape = [o[1] for o in outs]
    return pl.pallas_call(
        _inproj_kernel,
        grid=(n // tm, bx),
        in_specs=in_specs,
        out_specs=out_specs,
        out_shape=out_shape,
        compiler_params=pltpu.CompilerParams(dimension_semantics=("arbitrary", "arbitrary"),
                                             vmem_limit_bytes=VMEM_LIMIT),
        name="inproj",
    )(x, shift, scale, gpre, ca, sa, cq, sq, ck, sk, wlat, wwin, gq, wuq, gkv, wk, wv)


def _gqa_kernel(sink_ref, q_ref, *refs):
    nwin = GQA_QB + 2
    k_refs, kx_ref = refs[:nwin], refs[nwin]
    v_refs, vx_ref = refs[nwin + 1:2 * nwin + 1], refs[2 * nwin + 1]
    o_ref = refs[2 * nwin + 2]
    n = pl.program_id(1)
    nsteps = pl.num_programs(1)
    nctx = kx_ref.shape[1] // BLOCK
    group = A_HEADS // A_KV_HEADS
    key = lax.broadcasted_iota(jnp.int32, (BLOCK, group * BLOCK), 0)
    qry = lax.broadcasted_iota(jnp.int32, (BLOCK, group * BLOCK), 1) & (BLOCK - 1)
    sts = {}

    def scores(qb):
        kcat = jnp.concatenate([k_refs[qb + j][0] for j in range(3)] + [kx_ref[0]], axis=0)
        for hk in range(A_KV_HEADS):
            qg = jnp.concatenate([q_ref[0, qb * BLOCK:(qb + 1) * BLOCK, hh * HEAD_DIM:(hh + 1) * HEAD_DIM]
                                  for hh in range(hk * group, (hk + 1) * group)], axis=0)
            k_hk = kcat[:, hk * HEAD_DIM:(hk + 1) * HEAD_DIM]
            sts[qb, hk] = _dot_nt(k_hk, qg)

    def finish(qb):
        vis_prev = key >= qry
        vis_next = key <= qry
        if qb == 0:
            vis_prev = key >= qry + jnp.where(n > 0, 0, 2 * BLOCK)
        if qb == GQA_QB - 1:
            vis_next = key <= qry - jnp.where(n < nsteps - 1, 0, 2 * BLOCK)
        outs = []
        for hk in range(A_KV_HEADS):
            st = sts[qb, hk]
            pieces = [jnp.where(vis_prev, st[0:BLOCK], NEG_INF), st[BLOCK:2 * BLOCK],
                      jnp.where(vis_next, st[2 * BLOCK:3 * BLOCK], NEG_INF)]
            pieces += [st[(3 + j) * BLOCK:(4 + j) * BLOCK] for j in range(nctx)]
            sink = sink_ref[hk]
            m = jnp.maximum(functools.reduce(jnp.maximum, [jnp.max(p, axis=0, keepdims=True) for p in pieces]),
                            sink)
            ps = [jnp.exp2(p - m).astype(BF16) for p in pieces]
            vts = [v_refs[qb + j][0, 0, hk] for j in range(3)] + [vx_ref[0, j, hk] for j in range(nctx)]
            pv = functools.reduce(jnp.add, [_dot(vt, p) for vt, p in zip(vts, ps)])
            l = pv[HEAD_DIM:HEAD_DIM + 1, :] + jnp.exp2(sink - m)
            o = pv[0:HEAD_DIM, :] * (1.0 / l)
            outs += [o[:, g * BLOCK:(g + 1) * BLOCK] for g in range(group)]
        for pair in range(A_HEADS // 2):
            both = jnp.concatenate([outs[2 * pair], outs[2 * pair + 1]], axis=0)
            o_ref[0, qb * BLOCK:(qb + 1) * BLOCK, pair * LANES:(pair + 1) * LANES] = both.T.astype(BF16)

    for qb in range(GQA_QB):
        scores(qb)
    for qb in range(GQA_QB):
        finish(qb)


def _gqa(sinkv, qa, ka, vat, kac, vatc):
    b, s, _ = qa.shape
    nb = s // BLOCK
    c = kac.shape[1]
    tq = GQA_QB * BLOCK
    cur = lambda bb, n: (bb, n, 0)
    win = lambda j: (lambda bb, n: (bb, jnp.clip(n * GQA_QB + j - 1, 0, nb - 1), 0))
    vwin = lambda j: (lambda bb, n: (bb, jnp.clip(n * GQA_QB + j - 1, 0, nb - 1), 0, 0, 0))
    vblk = (1, 1, A_KV_HEADS, VT_ROWS, BLOCK)
    in_specs = [pl.BlockSpec(sinkv.shape, lambda bb, n: (0, 0, 0)), pl.BlockSpec((1, tq, A_HEADS * HEAD_DIM), cur)]
    in_specs += [pl.BlockSpec((1, BLOCK, LANES), win(j)) for j in range(GQA_QB + 2)]
    in_specs += [pl.BlockSpec((1, c, LANES), lambda bb, n: (bb, 0, 0))]
    in_specs += [pl.BlockSpec(vblk, vwin(j)) for j in range(GQA_QB + 2)]
    in_specs += [pl.BlockSpec((1, c // BLOCK, A_KV_HEADS, VT_ROWS, BLOCK), lambda bb, n: (bb, 0, 0, 0, 0))]
    return pl.pallas_call(
        _gqa_kernel,
        grid=(b, nb // GQA_QB),
        in_specs=in_specs,
        out_specs=pl.BlockSpec((1, tq, 512), cur),
        out_shape=jax.ShapeDtypeStruct((b, s, 512), BF16),
        compiler_params=pltpu.CompilerParams(dimension_semantics=("arbitrary", "arbitrary"),
                                             vmem_limit_bytes=VMEM_LIMIT),
        name="gqa",
    )(sinkv, qa, *([ka] * (GQA_QB + 2)), kac, *([vat] * (GQA_QB + 2)), vatc)


def _mla_kernel(q_ref, k_ref, vt_ref, kc_ref, vct_ref, o_ref, m_ref, acc_ref, sa_ref, sb_ref, cma_ref, cmb_ref):
    nk = k_ref.shape[1] // TK_MLA

    sa_ref, sb_ref = (sa_ref, cma_ref), (sb_ref, cmb_ref)

    def scores_into(bufs, k_blk, heads=range(MLA_HEADS)):
        buf, cm = bufs
        n = k_blk.shape[0]
        for hh in heads:
            hsl = slice(hh * LANES, (hh + 1) * LANES)
            st = _dot_nt(k_blk[:, hsl], q_ref[0, :, hsl])
            buf[hh, 0:n, :] = st
            cm[hh, 0:1, :] = jnp.max(st, axis=0, keepdims=True)

    def consume(bufs, vt_blk, first, heads=range(MLA_HEADS)):
        buf, cm = bufs
        n = vt_blk.shape[2]
        for hh in heads:
            st = buf[hh, 0:n, :]
            cmax = cm[hh, 0:1, :]
            if first:
                m_new = cmax
            else:
                m_old = m_ref[hh, 0:1, :]
                m_new = jnp.maximum(m_old, cmax)
                alpha = jnp.exp2(m_old - m_new)
            p = jnp.exp2(st - m_new).astype(BF16)
            pv = _dot(vt_blk[hh], p)
            acc_ref[hh] = pv if first else alpha * acc_ref[hh] + pv
            m_ref[hh, 0:1, :] = m_new

    def k_chunk(c):
        return k_ref[0, pl.ds(pl.multiple_of(c * TK_MLA, TK_MLA), TK_MLA), :]

    def overlapped(buf_w, k_blk, buf_r, vt_blk, first):
        for g in range(0, MLA_HEADS, MLA_GROUP):
            hs = range(g, g + MLA_GROUP)
            scores_into(buf_w, k_blk, hs)
            consume(buf_r, vt_blk, first, hs)

    scores_into(sa_ref, k_chunk(0))
    overlapped(sb_ref, k_chunk(1), sa_ref, vt_ref[0, 0], True)

    def body(i, _):
        c = 2 * i + 1
        overlapped(sa_ref, k_chunk(c + 1), sb_ref, vt_ref[0, c], False)
        overlapped(sb_ref, k_chunk(c + 2), sa_ref, vt_ref[0, c + 1], False)
        return 0

    lax.fori_loop(0, (nk - 2) // 2, body, 0)
    overlapped(sa_ref, kc_ref[0], sb_ref, vt_ref[0, nk - 1], False)
    consume(sa_ref, vct_ref[0, 0], False)

    outs = [acc_ref[hh, 0:MLA_V, :] * (1.0 / acc_ref[hh, MLA_V:MLA_V + 1, :]) for hh in range(MLA_HEADS)]
    o_ref[0] = jnp.concatenate(outs, axis=0).T.astype(BF16)


def _mla(q, k, vt, kc, vct):
    b, s, _ = q.shape
    c = kc.shape[1]
    nk = s // TK_MLA
    assert vt.shape == (b, nk, MLA_HEADS, VT_ROWS, TK_MLA) and nk % 2 == 0
    return pl.pallas_call(
        _mla_kernel,
        grid=(b, s // TQ_MLA),
        in_specs=[pl.BlockSpec((1, TQ_MLA, 1024), lambda bb, i: (bb, i, 0)),
                  pl.BlockSpec((1, s, 1024), lambda bb, i: (bb, 0, 0)),
                  pl.BlockSpec((1, nk, MLA_HEADS, VT_ROWS, TK_MLA), lambda bb, i: (bb, 0, 0, 0, 0)),
                  pl.BlockSpec((1, c, 1024), lambda bb, i: (bb, 0, 0)),
                  pl.BlockSpec((1, 1, MLA_HEADS, VT_ROWS, c), lambda bb, i: (bb, 0, 0, 0, 0))],
        out_specs=pl.BlockSpec((1, TQ_MLA, 512), lambda bb, i: (bb, i, 0)),
        out_shape=jax.ShapeDtypeStruct((b, s, 512), BF16),
        scratch_shapes=[pltpu.VMEM((MLA_HEADS, 8, TQ_MLA), F32),
                        pltpu.VMEM((MLA_HEADS, VT_ROWS, TQ_MLA), F32),
                        pltpu.VMEM((MLA_HEADS, TK_MLA, TQ_MLA), F32), pltpu.VMEM((MLA_HEADS, TK_MLA, TQ_MLA), F32),
                        pltpu.VMEM((MLA_HEADS, 8, TQ_MLA), F32), pltpu.VMEM((MLA_HEADS, 8, TQ_MLA), F32)],
        compiler_params=pltpu.CompilerParams(dimension_semantics=("arbitrary", "arbitrary"),
                                             vmem_limit_bytes=VMEM_LIMIT),
        name="mla",
    )(q, k, vt, kc, vct)


def _outproj_kernel(oa_ref, ob_ref, x_ref, woa_ref, wob_ref, gpost_ref, gate_ref, gffn_ref, shift_ref, scale_ref,
                    wr_pair_ref, wr_hi_ref, br_ref, tri_ref, upper_ref,
                    x1_ref, h2_ref, ri_ref, rw_ref, cnt_ref):
    tm = x_ref.shape[1]
    nsub = 2
    sub = tm // nsub
    rows = [slice(j * sub, (j + 1) * sub) for j in range(nsub)]
    ys = [_dot(oa_ref[0, r, :], woa_ref[...]) + _dot(ob_ref[0, r, :], wob_ref[...]) for r in rows]
    sub_lane_f = lax.broadcasted_iota(jnp.int32, (sub, LANES), 1).astype(F32)
    idx_parts, w_parts = [], []
    for r, y in zip(rows, ys):
        x1 = x_ref[0, r, :] + gate_ref[0] * _rms(y, gpost_ref[...])
        x1_ref[0, r, :] = x1
        h2 = _rms(x1, gffn_ref[...]) * (1.0 + scale_ref[0]) + shift_ref[0]
        h2_ref[0, r, :] = h2.astype(BF16)
        h_hi, h_lo = _split_bf16(h2)
        both = _dot(h_hi, wr_pair_ref[...])
        cur = both[:, :LANES] + both[:, LANES:] + _dot(h_lo, wr_hi_ref[...]) + br_ref[...]
        tops, idxs = [], []
        for _ in range(TOP_K):
            mk = jnp.max(cur, axis=-1, keepdims=True)
            ik = jnp.min(jnp.where(cur == mk, sub_lane_f, float(LANES)), axis=-1, keepdims=True)
            tops.append(mk)
            idxs.append(ik)
            cur = jnp.where(sub_lane_f == ik, -jnp.inf, cur)
        es = [jnp.exp(t - tops[0]) for t in tops]
        inv = 1.0 / functools.reduce(jnp.add, es)
        idx_parts.append(idxs)
        w_parts.append([e * inv for e in es])

    hot_parts = [[(sub_lane_f == ik) for ik in idxs] for idxs in idx_parts]
    onehot = jnp.concatenate([functools.reduce(jnp.add, [o.astype(F32) for o in hots]) for hots in hot_parts],
                             axis=0)
    cnt = jnp.sum(onehot, axis=0, keepdims=True)
    prefix = _dot(tri_ref[...], onehot.astype(BF16))
    chunks = jnp.floor((cnt + (RUN_ALIGN - 1.0)) * (1.0 / RUN_ALIGN))
    run_start = _dot(jnp.broadcast_to(chunks, (8, LANES)).astype(BF16), upper_ref[...])[0:1, :] * RUN_ALIGN
    base = prefix + run_start
    sub_lane = lax.broadcasted_iota(jnp.int32, (sub, LANES), 1)
    for r, hots, idxs, wts in zip(rows, hot_parts, idx_parts, w_parts):
        ri = jnp.zeros((sub, LANES), jnp.int32)
        rw = jnp.zeros((sub, LANES), F32)
        for k in range(TOP_K):
            pos = jnp.sum(jnp.where(hots[k], base[r, :], 0.0), axis=-1, keepdims=True).astype(jnp.int32)
            ri = jnp.where(sub_lane == k, idxs[k].astype(jnp.int32), ri)
            ri = jnp.where(sub_lane == TOP_K + k, pos, ri)
            rw = jnp.where(sub_lane == k, wts[k], rw)
        ri_ref[0, r, :] = ri
        rw_ref[0, r, :] = rw
    cnt_ref[0] = cnt


def _outproj(oa, ob, x, woa, wob, gpost, gate, gffn, shift, scale, wrh, wrl, br, tri, upper):
    b, s, d = x.shape
    tm = TM_OUT
    nt = s // tm
    tok = lambda bb, i: (bb, i, 0)
    vec = lambda bb, i: (bb, 0, 0)
    cst = lambda bb, i: (0, 0)
    return pl.pallas_call(
        _outproj_kernel,
        grid=(b, nt),
        in_specs=[pl.BlockSpec((1, tm, 512), tok), pl.BlockSpec((1, tm, 512), tok), pl.BlockSpec((1, tm, d), tok),
                  pl.BlockSpec((512, d), cst), pl.BlockSpec((512, d), cst), pl.BlockSpec((1, d), cst),
                  pl.BlockSpec((1, 1, d), vec), pl.BlockSpec((1, d), cst),
                  pl.BlockSpec((1, 1, d), vec), pl.BlockSpec((1, 1, d), vec),
                  pl.BlockSpec((d, 2 * LANES), cst), pl.BlockSpec((d, LANES), cst), pl.BlockSpec((1, LANES), cst),
                  pl.BlockSpec((tm, tm), cst), pl.BlockSpec((LANES, LANES), cst)],
        out_specs=[pl.BlockSpec((1, tm, d), tok), pl.BlockSpec((1, tm, d), tok),
                   pl.BlockSpec((1, tm, LANES), tok), pl.BlockSpec((1, tm, LANES), tok),
                   pl.BlockSpec((1, 1, LANES), lambda bb, i: (bb * nt + i, 0, 0))],
        out_shape=[jax.ShapeDtypeStruct((b, s, d), F32), jax.ShapeDtypeStruct((b, s, d), BF16),
                   jax.ShapeDtypeStruct((b, s, LANES), jnp.int32), jax.ShapeDtypeStruct((b, s, LANES), F32),
                   jax.ShapeDtypeStruct((b * nt, 1, LANES), F32)],
        compiler_params=pltpu.CompilerParams(dimension_semantics=("arbitrary", "arbitrary"),
                                             vmem_limit_bytes=VMEM_LIMIT),
        name="outproj",
    )(oa, ob, x, woa, wob, gpost, gate, gffn, shift, scale, wrh, wrl, br, tri, upper)


def _start_pieces(g, o, rows, make_copy):
    n_big = lax.shift_right_logical(rows, BIG_COPY.bit_length() - 1)
    n_small = lax.shift_right_logical(rows, RUN_ALIGN.bit_length() - 1) & (BIG_COPY // RUN_ALIGN - 1)

    def big(c, _):
        off = c * BIG_COPY
        make_copy(pl.multiple_of(g + off, RUN_ALIGN), pl.multiple_of(o + off, RUN_ALIGN), BIG_COPY).start()
        return 0

    def small(c, _):
        off = n_big * BIG_COPY + c * RUN_ALIGN
        make_copy(pl.multiple_of(g + off, RUN_ALIGN), pl.multiple_of(o + off, RUN_ALIGN), RUN_ALIGN).start()
        return 0

    lax.fori_loop(0, n_big, big, 0)
    lax.fori_loop(0, n_small, small, 0)
    return n_big, n_small


def _unrolled_loop(n, step):
    shift = LOOP_UNROLL.bit_length() - 1
    trips = lax.shift_right_logical(n, shift)

    def many(t, _):
        for u in range(LOOP_UNROLL):
            step(t * LOOP_UNROLL + u)
        return 0

    def one(i, _):
        step(i)
        return 0

    lax.fori_loop(0, trips, many, 0)
    lax.fori_loop(trips * LOOP_UNROLL, n, one, 0)


def _wait_pieces(n_big, n_small, make_copy):
    _unrolled_loop(n_big, lambda i: make_copy(0, 0, BIG_COPY).wait())
    _unrolled_loop(n_small, lambda i: make_copy(0, 0, RUN_ALIGN).wait())


def _start_listed_copies(plan, tile, make_copy):
    big_g, big_o, n_big, small_g, small_o, n_small = plan

    def start_list(g_ref, o_ref, n, width, rows):
        def start(p):
            j = tile * width + p
            make_copy(pl.multiple_of(g_ref[j], RUN_ALIGN), pl.multiple_of(o_ref[j], RUN_ALIGN), rows).start()

        _unrolled_loop(n, start)

    start_list(big_g, big_o, n_big[tile], MAX_BIG, BIG_COPY)
    start_list(small_g, small_o, n_small[tile], MAX_SMALL, RUN_ALIGN)


def _dispatch_kernel(bg_ref, bo_ref, nb_ref, sg_ref, so_ref, ns_ref, zs_ref, zr_ref, ri_ref, h_ref, xs_ref,
                     sorted_ref, zero_ref, sem):
    plan = (bg_ref, bo_ref, nb_ref, sg_ref, so_ref, ns_ref)
    tile = pl.program_id(0)
    last = pl.num_programs(0) - 1
    slot = tile % 2
    tm = h_ref.shape[0]
    post = ri_ref[...].astype(F32).T
    h = h_ref[...]
    rb_rows = 256
    pos_blk = [jnp.floor(post[TOP_K + k:TOP_K + k + 1, :] * (1.0 / rb_rows)) for k in range(TOP_K)]
    pos_off = [post[TOP_K + k:TOP_K + k + 1, :] - rb_rows * pos_blk[k] for k in range(TOP_K)]
    row = lax.broadcasted_iota(jnp.int32, (rb_rows, tm), 0).astype(F32).astype(BF16)
    one, zero = jnp.ones((), BF16), jnp.zeros((), BF16)

    for rb in range(SORTED_ROWS // rb_rows):
        perm = functools.reduce(jnp.add, [
            jnp.where(row == jnp.where(pos_blk[k] == float(rb), pos_off[k], -1.0).astype(BF16), one, zero)
            for k in range(TOP_K)])
        xr = _dot(perm, h)
        lo = lax.bitcast_convert_type(xr[:, :512], jnp.uint32)
        hi = lax.bitcast_convert_type(xr[:, 512:], jnp.uint32)
        sorted_ref[slot, rb * rb_rows:(rb + 1) * rb_rows, :] = (lo >> 16) | (hi & jnp.uint32(0xFFFF0000))

    def run_copy(sl):
        def make_copy(g, o, rows):
            return pltpu.make_async_copy(sorted_ref.at[sl, pl.ds(o, rows)], xs_ref.at[pl.ds(g, rows)], sem.at[sl])
        return make_copy

    _start_listed_copies(plan, tile, run_copy(slot))

    @pl.when(tile > 0)
    def _():
        _wait_pieces(nb_ref[tile - 1], ns_ref[tile - 1], run_copy(1 - slot))

    @pl.when(tile == last)
    def _():
        _wait_pieces(nb_ref[tile], ns_ref[tile], run_copy(slot))
        zero_ref[...] = jnp.zeros_like(zero_ref)

        def zero_copy(g, o, rows):
            return pltpu.make_async_copy(zero_ref.at[pl.ds(0, rows)], xs_ref.at[pl.ds(g, rows)], sem.at[2])

        def per_expert(e, totals):
            nb, ns = _start_pieces(zs_ref[e], 0, zr_ref[e], zero_copy)
            return totals[0] + nb, totals[1] + ns

        zb, zs = lax.fori_loop(0, zs_ref.shape[0], per_expert, (0, 0))
        _wait_pieces(zb, zs, zero_copy)


def _dispatch(plan, zstart, zrows, ri, h2, cap):
    t, d = h2.shape
    tm = TM_OUT
    return pl.pallas_call(
        _dispatch_kernel,
        grid_spec=pltpu.PrefetchScalarGridSpec(
            num_scalar_prefetch=8,
            grid=(t // tm,),
            in_specs=[pl.BlockSpec((tm, LANES), lambda i, *_: (i, 0)),
                      pl.BlockSpec((tm, d), lambda i, *_: (i, 0))],
            out_specs=pl.BlockSpec(memory_space=pl.ANY),
            scratch_shapes=[pltpu.VMEM((2, SORTED_ROWS, d // 2), jnp.uint32),
                            pltpu.VMEM((BIG_COPY, d // 2), jnp.uint32),
                            pltpu.SemaphoreType.DMA((3,))]),
        out_shape=jax.ShapeDtypeStruct((cap, d // 2), jnp.uint32),
        compiler_params=pltpu.CompilerParams(dimension_semantics=("arbitrary",),
                                             vmem_limit_bytes=VMEM_LIMIT),
        name="dispatch",
    )(*plan, zstart, zrows, ri, h2)


def _pack_bf16_pairs(x):
    n = x.shape[1] // 2
    bits = lax.bitcast_convert_type(x.astype(BF16).astype(F32), jnp.uint32)
    return (bits[:, :n] >> 16) | (bits[:, n:] & jnp.uint32(0xFFFF0000))


def _unpack_bf16_pairs(w):
    lo = lax.bitcast_convert_type(w << 16, F32).astype(BF16)
    hi = lax.bitcast_convert_type(w & jnp.uint32(0xFFFF0000), F32).astype(BF16)
    return lo, hi


def _experts_kernel(be_ref, nu_ref, slot_ref, nxt_ref, nv_ref, xs_ref, wgu_hbm, bgu_ref, wd_hbm, bd_ref, ys_ref,
                    wgu_f, wd_f, wgu_b, wd_b, sem):
    del nu_ref
    i = pl.program_id(0)
    e = be_ref[i]
    slot = slot_ref[i]
    valid = nv_ref[i]
    used = valid > 0
    run_start = used & ((i == 0) | (be_ref[jnp.maximum(i - 1, 0)] != e))

    def weight_copies(ex, sl):
        return (pltpu.make_async_copy(wgu_hbm.at[ex], wgu_f.at[sl], sem.at[0, sl]),
                pltpu.make_async_copy(wd_hbm.at[ex], wd_f.at[sl], sem.at[1, sl]))

    @pl.when(i == 0)
    def _():
        for cp in weight_copies(e, slot):
            cp.start()

    @pl.when(run_start)
    def _():
        for cp in weight_copies(e, slot):
            cp.wait()
        nxt = nxt_ref[i]

        @pl.when(nxt >= 0)
        def _():
            for cp in weight_copies(nxt, 1 - slot):
                cp.start()

        wgu_b[...] = wgu_f[slot].astype(BF16)
        wd_b[...] = wd_f[slot].astype(BF16)

    def compute(rows):
        gus = []
        for r in rows:
            x_lo, x_hi = _unpack_bf16_pairs(xs_ref[r, :])
            gus.append(_dot(x_lo, wgu_b[0:512, :]) + _dot(x_hi, wgu_b[512:, :]) + bgu_ref[0])
        for r, gu in zip(rows, gus):
            gate = jnp.minimum(gu[:, :D_FF], SWIGLU_LIMIT)
            lin = jnp.clip(gu[:, D_FF:], -SWIGLU_LIMIT, SWIGLU_LIMIT)
            act = (lin + 1.0) * (gate * jax.nn.sigmoid(SWIGLU_ALPHA * gate))
            ys_ref[r, :] = _pack_bf16_pairs(_dot(act.astype(BF16), wd_b[...]) + bd_ref[0])

    nsteps = MOE_BLOCK // EXPERT_STEP
    for n in range(nsteps + 1):
        lo, hi = (n - 1) * EXPERT_STEP, n * EXPERT_STEP
        cond = (valid == 0) if n == 0 else ((valid > lo) & (valid <= hi))
        cuts = list(range(0, hi, EXPERT_SUB)) + [hi]

        @pl.when(cond)
        def _(hi=hi, cuts=cuts):
            if hi:
                compute([slice(a, b) for a, b in zip(cuts[:-1], cuts[1:])])
            if hi < MOE_BLOCK:
                ys_ref[hi:, :] = jnp.zeros((MOE_BLOCK - hi, D_MODEL // 2), jnp.uint32)


def _experts(block_e, n_used, slot, nxt, nvalid, xs, wgu, bgu, wd, bd):
    cap = xs.shape[0]
    nblk = cap // MOE_BLOCK
    return pl.pallas_call(
        _experts_kernel,
        grid_spec=pltpu.PrefetchScalarGridSpec(
            num_scalar_prefetch=5,
            grid=(nblk,),
            in_specs=[pl.BlockSpec((MOE_BLOCK, 512), lambda i, be, nu, *_: (jnp.minimum(i, nu[0] - 1), 0)),
                      pl.BlockSpec(memory_space=pl.ANY),
                      pl.BlockSpec((1, 1, 2 * D_FF), lambda i, be, *_: (be[i], 0, 0)),
                      pl.BlockSpec(memory_space=pl.ANY),
                      pl.BlockSpec((1, 1, D_MODEL), lambda i, be, *_: (be[i], 0, 0))],
            out_specs=pl.BlockSpec((MOE_BLOCK, D_MODEL // 2), lambda i, *_: (i, 0)),
            scratch_shapes=[pltpu.VMEM((2, D_MODEL, 2 * D_FF), F32), pltpu.VMEM((2, D_FF, D_MODEL), F32),
                            pltpu.VMEM((D_MODEL, 2 * D_FF), BF16), pltpu.VMEM((D_FF, D_MODEL), BF16),
                            pltpu.SemaphoreType.DMA((2, 2))]),
        out_shape=jax.ShapeDtypeStruct((cap, D_MODEL // 2), jnp.uint32),
        compiler_params=pltpu.CompilerParams(dimension_semantics=("arbitrary",),
                                             vmem_limit_bytes=VMEM_LIMIT),
        name="experts",
    )(block_e, n_used, slot, nxt, nvalid, xs, wgu, bgu, wd, bd)


def _combine_kernel(bg_ref, bo_ref, nb_ref, sg_ref, so_ref, ns_ref, ys_ref, ri_ref, rw_ref, x1_ref, gate_ref, g_ref,
                    o_ref, ybuf, sem):
    plan = (bg_ref, bo_ref, nb_ref, sg_ref, so_ref, ns_ref)
    nt = pl.num_programs(1)
    tile = pl.program_id(0) * nt + pl.program_id(1)
    ntiles = pl.num_programs(0) * nt
    slot = tile % 2
    tm = x1_ref.shape[1]

    def run_copy(sl):
        def make_copy(g, o, rows):
            return pltpu.make_async_copy(ys_ref.at[pl.ds(g, rows)], ybuf.at[sl, pl.ds(o, rows)], sem.at[sl])
        return make_copy

    @pl.when(tile == 0)
    def _():
        ybuf[...] = jnp.zeros_like(ybuf)
        _start_listed_copies(plan, tile, run_copy(slot))

    @pl.when(tile + 1 < ntiles)
    def _():
        _start_listed_copies(plan, tile + 1, run_copy(1 - slot))

    _wait_pieces(nb_ref[tile], ns_ref[tile], run_copy(slot))

    posf = ri_ref[0].astype(F32)
    rw = rw_ref[0]
    cb_cols = 256
    pos = [posf[:, TOP_K + k:TOP_K + k + 1] for k in range(TOP_K)]
    pos_blk = [jnp.floor(p * (1.0 / cb_cols)) for p in pos]
    pos_off = [p - cb_cols * b for p, b in zip(pos, pos_blk)]
    wk = [rw[:, k:k + 1].astype(BF16) for k in range(TOP_K)]
    col = lax.broadcasted_iota(jnp.int32, (tm, cb_cols), 1).astype(F32).astype(BF16)
    zero = jnp.zeros((), BF16)
    f_lo = jnp.zeros((tm, D_MODEL // 2), F32)
    f_hi = jnp.zeros((tm, D_MODEL // 2), F32)
    for cb in range(SORTED_ROWS // cb_cols):
        wp = functools.reduce(jnp.add, [
            jnp.where(col == jnp.where(pos_blk[k] == float(cb), pos_off[k], -1.0).astype(BF16), wk[k], zero)
            for k in range(TOP_K)])
        y_lo, y_hi = _unpack_bf16_pairs(ybuf[slot, cb * cb_cols:(cb + 1) * cb_cols, :])
        f_lo = f_lo + _dot(wp, y_lo)
        f_hi = f_hi + _dot(wp, y_hi)
    f = jnp.concatenate([f_lo, f_hi], axis=1)
    o_ref[0] = x1_ref[0] + gate_ref[0] * _rms(f, g_ref[...])


def _combine(plan, ys, ri, rw, x1, gate, g):
    b, s, d = x1.shape
    tm = TM_OUT
    tok = lambda bb, i, *_: (bb, i, 0)
    return pl.pallas_call(
        _combine_kernel,
        grid_spec=pltpu.PrefetchScalarGridSpec(
            num_scalar_prefetch=6,
            grid=(b, s // tm),
            in_specs=[pl.BlockSpec(memory_space=pl.ANY),
                      pl.BlockSpec((1, tm, LANES), tok), pl.BlockSpec((1, tm, LANES), tok),
                      pl.BlockSpec((1, tm, d), tok),
                      pl.BlockSpec((1, 1, d), lambda bb, i, *_: (bb, 0, 0)),
                      pl.BlockSpec((1, d), lambda bb, i, *_: (0, 0))],
            out_specs=pl.BlockSpec((1, tm, d), tok),
            scratch_shapes=[pltpu.VMEM((2, SORTED_ROWS, d // 2), jnp.uint32), pltpu.SemaphoreType.DMA((2,))]),
        out_shape=jax.ShapeDtypeStruct((b, s, d), F32),
        compiler_params=pltpu.CompilerParams(dimension_semantics=("arbitrary", "arbitrary"),
                                             vmem_limit_bytes=VMEM_LIMIT),
        name="combine",
    )(*plan, ys, ri, rw, x1, gate, g)


def _rope_angles(rows, rot_dim):
    row = jnp.broadcast_to(jnp.arange(rows, dtype=F32)[:, None], (rows, GRID_W)).reshape(-1)
    col = jnp.broadcast_to(jnp.arange(GRID_W, dtype=F32)[None, :], (rows, GRID_W)).reshape(-1)
    quarter = rot_dim // 4
    inv_freq = ROPE_BASE ** (-jnp.arange(quarter, dtype=F32) / quarter)
    ang = jnp.concatenate([row[:, None] * inv_freq, col[:, None] * inv_freq], axis=-1)
    return jnp.cos(ang), jnp.sin(ang)


def _rope_tables(s):
    ca, sa = _rope_angles(s // GRID_W, HEAD_DIM)
    cb, sb = _rope_angles(s // GRID_W, MLA_ROPE)
    one = lambda w: jnp.ones((s, w), F32)
    zero = lambda w: jnp.zeros((s, w), F32)
    cos_a = jnp.tile(jnp.concatenate([ca, ca], 1), (1, 2))
    sin_a = jnp.tile(jnp.concatenate([-sa, sa], 1), (1, 2))
    cos_q = jnp.concatenate([one(64), cb, cb, one(32)], 1)
    sin_q = jnp.concatenate([zero(64), -sb, sb, zero(32)], 1)
    cos_k = jnp.concatenate([cb, cb, one(96)], 1)
    sin_k = jnp.concatenate([-sb, sb, zero(96)], 1)
    return cos_a, sin_a, cos_q, sin_q, cos_k, sin_k


def _piece_lists(slot_start, tile_start, counts, first_off, piece, width):
    ne = counts.shape[1]
    cum = jnp.cumsum(counts, axis=1)
    p = jnp.arange(width, dtype=jnp.int32)
    owner = jnp.minimum(jnp.sum((cum[:, None, :] <= p[None, :, None]).astype(jnp.int32), axis=2), ne - 1)
    sel = owner[:, :, None] == jnp.arange(ne, dtype=jnp.int32)[None, None, :]
    pick = lambda a: jnp.sum(jnp.where(sel, a[:, None, :], 0), axis=2)
    off = pick(first_off) + (p[None, :] - pick(cum - counts)) * piece
    as_list = lambda a: a.reshape(-1).astype(jnp.int32)
    return as_list(pick(slot_start) + off), as_list(pick(tile_start) + off), cum[:, -1].astype(jnp.int32)


def _identity_tables(n):
    one, zero = jnp.ones((n, LANES), F32), jnp.zeros((n, LANES), F32)
    return one, zero, one, zero, one, zero


def _inproj_weights(w_in, g_q_a, w_uq, g_kv_a, w_ukv):
    d = w_in.shape[0]
    wq, wk, wv = w_in[:, :512], w_in[:, 512:640], w_in[:, 640:768]
    wcq, wckv, wkr = w_in[:, 768:1152], w_in[:, 1152:1408], w_in[:, 1408:1440]
    wkr_p = jnp.concatenate([wkr, jnp.zeros((d, LANES - MLA_ROPE), F32)], 1)
    wlat = jnp.concatenate([wcq, wckv, wkr_p], 1).astype(BF16)
    wwin = jnp.concatenate([wq, wk, wv], 1).astype(BF16)
    wuq = jnp.pad(w_uq.reshape(MLA_Q_LORA, MLA_HEADS, MLA_NOPE + MLA_ROPE), ((0, 0), (0, 0), (0, 32)))
    wuq = wuq.reshape(MLA_Q_LORA, MLA_HEADS * LANES).astype(BF16)
    wukv = w_ukv.reshape(MLA_KV_LORA, MLA_HEADS, MLA_NOPE + MLA_V)
    wkk = jnp.pad(wukv[:, :, :MLA_NOPE], ((0, 0), (0, 0), (0, 64))).reshape(MLA_KV_LORA, MLA_HEADS * LANES)
    wvv = wukv[:, :, MLA_NOPE:].reshape(MLA_KV_LORA, MLA_HEADS * MLA_V)
    return (wlat, wwin, g_q_a.reshape(1, -1), wuq, g_kv_a.reshape(1, -1), wkk.astype(BF16), wvv.T.astype(BF16))


def kernel(x, c, ctx, c_ctx, w_ada, b_ada, g_mix_pre, g_mix_post, w_in, sink, g_q_a, w_uq, g_kv_a, w_ukv, w_o,
           g_ffn_pre, g_ffn_post, w_router, b_router, w_gate_up, b_gate_up, w_down, b_down):
    b, s, d = x.shape
    cl = ctx.shape[1]
    t = b * s

    cc = jnp.zeros((8, d), F32).at[:b].set(c).at[b].set(c_ctx)
    mod_all = _adaln(cc, w_ada[0], b_ada[0].reshape(1, -1))
    mod = [mod_all[:b, i * d:(i + 1) * d].reshape(b, 1, d) for i in range(6)]
    mod_c = [jnp.broadcast_to(mod_all[b, i * d:(i + 1) * d].reshape(1, 1, d), (b, 1, d)) for i in range(2)]

    wts = _inproj_weights(w_in[0], g_q_a[0], w_uq[0], g_kv_a[0], w_ukv[0])
    gpre = g_mix_pre[0].reshape(1, d)
    qa, ka, vat, q, k, vt = _inproj(x, mod[0], mod[1], gpre, _rope_tables(s), wts, TM_IN)
    _, kac, vatc, _, kc, vct = _inproj(ctx, mod_c[0], mod_c[1], gpre, _identity_tables(cl), wts, cl)

    sinkv = jnp.repeat(sink[0] * LOG2_E, BLOCK).reshape(A_KV_HEADS, 1, -1)
    out_a = _gqa(sinkv, qa, ka, vat, kac, vatc)
    out_b = _mla(q, k, vt, kc, vct)

    wo = w_o[0].astype(BF16)
    wr = jnp.pad(w_router[0], ((0, 0), (0, LANES - N_EXPERTS)))
    wr_hi, wr_lo = _split_bf16(wr)
    br = jnp.concatenate([b_router[0], jnp.full((LANES - N_EXPERTS,), NEG_INF, F32)]).reshape(1, LANES)
    ii = jnp.arange(TM_OUT)
    tri = (ii[:, None] > ii[None, :]).astype(BF16)
    jj = jnp.arange(LANES)
    upper = (jj[:, None] < jj[None, :]).astype(BF16)
    x1, h2, ri, rw, cnt = _outproj(out_a, out_b, x, wo[:512], wo[512:], g_mix_post[0].reshape(1, d), mod[2],
                                   g_ffn_pre[0].reshape(1, d), mod[3], mod[4],
                                   jnp.concatenate([wr_hi, wr_lo], axis=1), wr_hi, br, tri, upper)

    cnt = cnt[:, 0, :N_EXPERTS].astype(jnp.int32)
    nt = cnt.shape[0]
    rows = (cnt + RUN_ALIGN - 1) // RUN_ALIGN * RUN_ALIGN
    tot = jnp.sum(rows, axis=0)
    carry = jnp.cumsum(rows, axis=0) - rows
    padded = (tot + MOE_BLOCK - 1) // MOE_BLOCK * MOE_BLOCK
    pad_end = jnp.cumsum(padded)
    pad_start = pad_end - padded
    nblk = -(-(t * TOP_K + nt * N_EXPERTS * (RUN_ALIGN - 1)) // MOE_BLOCK) + N_EXPERTS
    cap = nblk * MOE_BLOCK
    blk_row = jnp.arange(nblk, dtype=jnp.int32) * MOE_BLOCK
    block_e = jnp.minimum(jnp.sum((pad_end[None, :] <= blk_row[:, None]).astype(jnp.int32), axis=1), N_EXPERTS - 1)
    n_used = (pad_end[-1] // MOE_BLOCK).astype(jnp.int32).reshape(1)
    toff = jnp.cumsum(rows, axis=1) - rows
    gstart = pad_start[None, :] + carry
    n_big = rows // BIG_COPY
    n_small = (rows - n_big * BIG_COPY) // RUN_ALIGN
    big_g, big_o, tot_big = _piece_lists(gstart, toff, n_big, jnp.zeros_like(rows), BIG_COPY, MAX_BIG)
    small_g, small_o, tot_small = _piece_lists(gstart, toff, n_small, n_big * BIG_COPY, RUN_ALIGN, MAX_SMALL)
    plan = (big_g, big_o, tot_big, small_g, small_o, tot_small)
    zstart = jnp.concatenate([pad_start + tot, pad_end[-1:]]).astype(jnp.int32)
    zrows = jnp.concatenate([padded - tot, cap - pad_end[-1:]]).astype(jnp.int32)
    block_e = block_e.astype(jnp.int32)
    run_idx = jnp.cumsum(jnp.concatenate([jnp.zeros((1,), jnp.int32),
                                          (block_e[1:] != block_e[:-1]).astype(jnp.int32)]))
    eid = jnp.arange(N_EXPERTS, dtype=jnp.int32)
    later_used = (tot[None, :] > 0) & (eid[None, :] > eid[:, None])
    next_e = jnp.min(jnp.where(later_used, eid[None, :], N_EXPERTS), axis=1)
    next_e = jnp.where(next_e < N_EXPERTS, next_e, -1).astype(jnp.int32)
    nxt = jnp.sum(jnp.where(block_e[:, None] == eid[None, :], next_e[None, :], 0), axis=1).astype(jnp.int32)

    xs = _dispatch(plan, zstart, zrows, ri.reshape(t, LANES), h2.reshape(t, d), cap)
    run_end = jnp.sum(jnp.where(block_e[:, None] == eid[None, :], (pad_start + tot)[None, :], 0), axis=1)
    nvalid = jnp.where(blk_row < pad_end[-1], jnp.clip(run_end - blk_row, 0, MOE_BLOCK), 0).astype(jnp.int32)
    ys = _experts(block_e, n_used, (run_idx % 2).astype(jnp.int32), nxt, nvalid, xs, w_gate_up[0],
                  b_gate_up[0].reshape(N_EXPERTS, 1, -1), w_down[0], b_down[0].reshape(N_EXPERTS, 1, -1))
    return _combine(plan, ys, ri, rw, x1, mod[5], g_ffn_post[0].reshape(1, d))
```

```python
import functools

import jax
import jax.numpy as jnp
import numpy as np
from jax import lax
from jax.experimental import pallas as pl
from jax.experimental.pallas import tpu as pltpu

F32 = jnp.float32
BF16 = jnp.bfloat16

D_MODEL = 1024
GRID_W = 64
HEAD_DIM = 64
A_HEADS = 8
A_KV_HEADS = 2
BLOCK = 128
A_SCALE = HEAD_DIM ** -0.5
MLA_HEADS = 8
MLA_NOPE = 64
MLA_ROPE = 32
MLA_V = 64
MLA_Q_LORA = 384
MLA_KV_LORA = 256
MLA_SCALE = (MLA_NOPE + MLA_ROPE) ** -0.5
LOG2_E = 1.4426950408889634
N_EXPERTS = 32
TOP_K = 4
D_FF = 1024
SWIGLU_LIMIT = 7.0
SWIGLU_ALPHA = 1.702
ROPE_BASE = 10000.0
EPS = 1e-6
NEG_INF = -1e30

LANES = 128
VMEM_LIMIT = 56 * 1024 * 1024

TM_IN = 512
GQA_QB = 8
TQ_MLA = 256
TK_MLA = 512
MLA_GROUP = 2
VT_ROWS = 80
TM_OUT = 512
MOE_BLOCK = 512
EXPERT_SUB = 256
EXPERT_STEP = 128
RUN_ALIGN = 8
BIG_COPY = 32
SORTED_ROWS = TM_OUT * TOP_K + N_EXPERTS * RUN_ALIGN
LOOP_UNROLL = 8
MAX_BIG = SORTED_ROWS // BIG_COPY
MAX_SMALL = N_EXPERTS * (BIG_COPY // RUN_ALIGN - 1)


def _dot(a, b):
    return jnp.dot(a, b, preferred_element_type=F32)


def _dot_nt(a, b):
    return lax.dot_general(a, b, (((1,), (1,)), ((), ())), preferred_element_type=F32)


def _split_bf16(x):
    hi = x.astype(BF16)
    lo = (x - hi.astype(F32)).astype(BF16)
    return hi, lo


def _rms(x, g):
    return x * lax.rsqrt(jnp.mean(x * x, axis=-1, keepdims=True) + EPS) * g


def _rope(x, cos, sin_signed, first_half, half):
    n = x.shape[-1]
    partner = jnp.where(first_half, pltpu.roll(x, n - half, 1), pltpu.roll(x, half, 1))
    return x * cos + partner * sin_signed


def _adaln_kernel(c_ref, w_ref, b_ref, o_ref):
    c = c_ref[...]
    s = c * jax.nn.sigmoid(c)
    s_hi, s_lo = _split_bf16(s)
    w_hi, w_lo = _split_bf16(w_ref[...])
    o_ref[...] = _dot(s_hi, w_hi) + _dot(s_hi, w_lo) + _dot(s_lo, w_hi) + b_ref[...]


def _adaln(cc, w, b):
    n = w.shape[1]
    tn = 1024
    return pl.pallas_call(
        _adaln_kernel,
        grid=(n // tn,),
        in_specs=[pl.BlockSpec((8, D_MODEL), lambda j: (0, 0)),
                  pl.BlockSpec((D_MODEL, tn), lambda j: (0, j)),
                  pl.BlockSpec((1, tn), lambda j: (0, j))],
        out_specs=pl.BlockSpec((8, tn), lambda j: (0, j)),
        out_shape=jax.ShapeDtypeStruct((8, n), F32),
        compiler_params=pltpu.CompilerParams(dimension_semantics=("arbitrary",),
                                             vmem_limit_bytes=VMEM_LIMIT),
        name="adaln",
    )(cc, w, b)


def _inproj_kernel(with_q, x_ref, shift_ref, scale_ref, gpre_ref, ca_ref, sa_ref, cq_ref, sq_ref,
                   wlat_ref, wwin_ref, gq_ref, wuq_ref, gkv_ref, wk_ref, wv_ref, *out_refs):
    if with_q:
        qa_ref, ka_ref, vat_ref, q_ref, k_ref, vt_ref = out_refs
    else:
        ka_ref, vat_ref, k_ref, vt_ref = out_refs
    x = x_ref[0]
    tm = x.shape[0]
    h = _rms(x, gpre_ref[...]) * (1.0 + scale_ref[0]) + shift_ref[0]
    hb = h.astype(BF16)
    lat = _dot(hb, wlat_ref[...])
    p = _dot(hb, wwin_ref[...])
    c_ka = A_HEADS * HEAD_DIM if with_q else 0
    c_va = c_ka + LANES
    c_ckv, c_kr = MLA_Q_LORA, MLA_Q_LORA + MLA_KV_LORA
    lane = lax.broadcasted_iota(jnp.int32, (tm, LANES), 1)

    first_a = (lane & 32) == 0
    ca, sa = ca_ref[...], sa_ref[...]
    if with_q:
        for j in range(A_HEADS * HEAD_DIM // LANES):
            sl = slice(j * LANES, (j + 1) * LANES)
            qa_ref[0, :, sl] = (_rope(p[:, sl], ca, sa, first_a, 32) * (A_SCALE * LOG2_E)).astype(BF16)
    ka_ref[0] = _rope(p[:, c_ka:c_va], ca, sa, first_a, 32).astype(BF16)
    vat = p[:, c_va:c_va + LANES].T
    ones_a = (lax.broadcasted_iota(jnp.int32, (VT_ROWS - HEAD_DIM, BLOCK), 0) == 0).astype(BF16)
    for j in range(tm // BLOCK):
        for hk in range(A_KV_HEADS):
            vat_ref[0, j, hk, 0:HEAD_DIM, :] = vat[hk * HEAD_DIM:(hk + 1) * HEAD_DIM,
                                                   j * BLOCK:(j + 1) * BLOCK].astype(BF16)
            vat_ref[0, j, hk, HEAD_DIM:VT_ROWS, :] = ones_a

    cq, sq = cq_ref[...], sq_ref[...]
    rope_mla = lambda t: t * cq + pltpu.roll(t, LANES // 2, 1) * sq
    if with_q:
        cqn = _rms(lat[:, 0:c_ckv], gq_ref[...]).astype(BF16)
        qf = _dot(cqn, wuq_ref[...])
        for hh in range(MLA_HEADS):
            sl = slice(hh * LANES, (hh + 1) * LANES)
            q_ref[0, :, sl] = (rope_mla(qf[:, sl]) * (MLA_SCALE * LOG2_E)).astype(BF16)

    ckvn = _rms(lat[:, c_ckv:c_kr], gkv_ref[...]).astype(BF16)
    kr = rope_mla(lat[:, c_kr:c_kr + LANES])
    kn = _dot(ckvn, wk_ref[...])
    for hh in range(MLA_HEADS):
        sl = slice(hh * LANES, (hh + 1) * LANES)
        k_ref[0, :, sl] = (kn[:, sl] + kr).astype(BF16)
    vt = _dot_nt(wv_ref[...], ckvn).astype(BF16)
    ones_row = (lax.broadcasted_iota(jnp.int32, (VT_ROWS - MLA_V, tm), 0) == 0).astype(BF16)
    for hh in range(MLA_HEADS):
        vt_ref[0, 0, hh, 0:MLA_V, :] = vt[hh * MLA_V:(hh + 1) * MLA_V, :]
        vt_ref[0, 0, hh, MLA_V:VT_ROWS, :] = ones_row


def _inproj(x, shift, scale, gpre, tabs, wts, tm, with_q):
    bx, n, d = x.shape
    wlat, wwin, gq, wuq, gkv, wk, wv = wts
    if not with_q:
        wwin = wwin[:, A_HEADS * HEAD_DIM:]
    wts = (wlat, wwin, gq, wuq, gkv, wk, wv)
    tok = lambda i, b: (b, i, 0)
    vec = lambda i, b: (b, 0, 0)
    tab = lambda i, b: (i, 0)
    cst = lambda i, b: (0, 0)
    in_specs = [pl.BlockSpec((1, tm, d), tok), pl.BlockSpec((1, 1, d), vec), pl.BlockSpec((1, 1, d), vec),
                pl.BlockSpec((1, d), cst)]
    in_specs += [pl.BlockSpec((tm, LANES), tab)] * len(tabs)
    in_specs += [pl.BlockSpec(w.shape, cst) for w in wts]
    row_out = lambda w: (pl.BlockSpec((1, tm, w), tok), jax.ShapeDtypeStruct((bx, n, w), BF16))
    win_outs = [row_out(LANES),
                (pl.BlockSpec((1, tm // BLOCK, A_KV_HEADS, VT_ROWS, BLOCK), lambda i, b: (b, i, 0, 0, 0)),
                 jax.ShapeDtypeStruct((bx, n // BLOCK, A_KV_HEADS, VT_ROWS, BLOCK), BF16))]
    mla_outs = [row_out(MLA_HEADS * LANES),
                (pl.BlockSpec((1, 1, MLA_HEADS, VT_ROWS, tm), lambda i, b: (b, i, 0, 0, 0)),
                 jax.ShapeDtypeStruct((bx, n // tm, MLA_HEADS, VT_ROWS, tm), BF16))]
    if with_q:
        outs = [row_out(A_HEADS * HEAD_DIM)] + win_outs + [row_out(MLA_HEADS * LANES)] + mla_outs
    else:
        outs = win_outs + mla_outs
    return pl.pallas_call(
        functools.partial(_inproj_kernel, with_q),
        grid=(n // tm, bx),
        in_specs=in_specs,
        out_specs=[o[0] for o in outs],
        out_shape=[o[1] for o in outs],
        compiler_params=pltpu.CompilerParams(dimension_semantics=("arbitrary", "arbitrary"),
                                             vmem_limit_bytes=VMEM_LIMIT),
        name="inproj" if with_q else "inproj_ctx",
    )(x, shift, scale, gpre, *tabs, *wts)


def _gqa_kernel(sink_ref, q_ref, *refs):
    nwin = GQA_QB + 2
    k_refs, kx_ref = refs[:nwin], refs[nwin]
    v_refs, vx_ref = refs[nwin + 1:2 * nwin + 1], refs[2 * nwin + 1]
    o_ref = refs[2 * nwin + 2]
    n = pl.program_id(1)
    nsteps = pl.num_programs(1)
    nctx = kx_ref.shape[1] // BLOCK
    group = A_HEADS // A_KV_HEADS
    key = lax.broadcasted_iota(jnp.int32, (BLOCK, group * BLOCK), 0)
    qry = lax.broadcasted_iota(jnp.int32, (BLOCK, group * BLOCK), 1) & (BLOCK - 1)
    sts = {}

    def scores(qb):
        kcat = jnp.concatenate([k_refs[qb + j][0] for j in range(3)] + [kx_ref[0]], axis=0)
        for hk in range(A_KV_HEADS):
            qg = jnp.concatenate([q_ref[0, qb * BLOCK:(qb + 1) * BLOCK, hh * HEAD_DIM:(hh + 1) * HEAD_DIM]
                                  for hh in range(hk * group, (hk + 1) * group)], axis=0)
            k_hk = kcat[:, hk * HEAD_DIM:(hk + 1) * HEAD_DIM]
            sts[qb, hk] = _dot_nt(k_hk, qg)

    def finish(qb):
        vis_prev = key >= qry
        vis_next = key <= qry
        if qb == 0:
            vis_prev = key >= qry + jnp.where(n > 0, 0, 2 * BLOCK)
        if qb == GQA_QB - 1:
            vis_next = key <= qry - jnp.where(n < nsteps - 1, 0, 2 * BLOCK)
        outs = []
        for hk in range(A_KV_HEADS):
            st = sts[qb, hk]
            pieces = [jnp.where(vis_prev, st[0:BLOCK], NEG_INF), st[BLOCK:2 * BLOCK],
                      jnp.where(vis_next, st[2 * BLOCK:3 * BLOCK], NEG_INF)]
            pieces += [st[(3 + j) * BLOCK:(4 + j) * BLOCK] for j in range(nctx)]
            sink = sink_ref[hk]
            m = jnp.maximum(functools.reduce(jnp.maximum, [jnp.max(p, axis=0, keepdims=True) for p in pieces]),
                            sink)
            ps = [jnp.exp2(p - m).astype(BF16) for p in pieces]
            vts = [v_refs[qb + j][0, 0, hk] for j in range(3)] + [vx_ref[0, j, hk] for j in range(nctx)]
            pv = functools.reduce(jnp.add, [_dot(vt, p) for vt, p in zip(vts, ps)])
            l = pv[HEAD_DIM:HEAD_DIM + 1, :] + jnp.exp2(sink - m)
            o = pv[0:HEAD_DIM, :] * (1.0 / l)
            outs += [o[:, g * BLOCK:(g + 1) * BLOCK] for g in range(group)]
        for pair in range(A_HEADS // 2):
            both = jnp.concatenate([outs[2 * pair], outs[2 * pair + 1]], axis=0)
            o_ref[0, qb * BLOCK:(qb + 1) * BLOCK, pair * LANES:(pair + 1) * LANES] = both.T.astype(BF16)

    for qb in range(GQA_QB):
        scores(qb)
    for qb in range(GQA_QB):
        finish(qb)


def _gqa(sinkv, qa, ka, vat, kac, vatc):
    b, s, _ = qa.shape
    nb = s // BLOCK
    c = kac.shape[1]
    tq = GQA_QB * BLOCK
    cur = lambda bb, n: (bb, n, 0)
    win = lambda j: (lambda bb, n: (bb, jnp.clip(n * GQA_QB + j - 1, 0, nb - 1), 0))
    vwin = lambda j: (lambda bb, n: (bb, jnp.clip(n * GQA_QB + j - 1, 0, nb - 1), 0, 0, 0))
    vblk = (1, 1, A_KV_HEADS, VT_ROWS, BLOCK)
    in_specs = [pl.BlockSpec(sinkv.shape, lambda bb, n: (0, 0, 0)), pl.BlockSpec((1, tq, A_HEADS * HEAD_DIM), cur)]
    in_specs += [pl.BlockSpec((1, BLOCK, LANES), win(j)) for j in range(GQA_QB + 2)]
    in_specs += [pl.BlockSpec((1, c, LANES), lambda bb, n: (bb, 0, 0))]
    in_specs += [pl.BlockSpec(vblk, vwin(j)) for j in range(GQA_QB + 2)]
    in_specs += [pl.BlockSpec((1, c // BLOCK, A_KV_HEADS, VT_ROWS, BLOCK), lambda bb, n: (bb, 0, 0, 0, 0))]
    return pl.pallas_call(
        _gqa_kernel,
        grid=(b, nb // GQA_QB),
        in_specs=in_specs,
        out_specs=pl.BlockSpec((1, tq, 512), cur),
        out_shape=jax.ShapeDtypeStruct((b, s, 512), BF16),
        compiler_params=pltpu.CompilerParams(dimension_semantics=("arbitrary", "arbitrary"),
                                             vmem_limit_bytes=VMEM_LIMIT),
        name="gqa",
    )(sinkv, qa, *([ka] * (GQA_QB + 2)), kac, *([vat] * (GQA_QB + 2)), vatc)


def _mla_kernel(q_ref, k_ref, vt_ref, kc_ref, vct_ref, o_ref, m_ref, acc_ref, sa_ref, sb_ref, cma_ref, cmb_ref):
    nk = k_ref.shape[1] // TK_MLA

    sa_ref, sb_ref = (sa_ref, cma_ref), (sb_ref, cmb_ref)

    def scores_into(bufs, k_blk, heads=range(MLA_HEADS)):
        buf, cm = bufs
        n = k_blk.shape[0]
        for hh in heads:
            hsl = slice(hh * LANES, (hh + 1) * LANES)
            st = _dot_nt(k_blk[:, hsl], q_ref[0, :, hsl])
            buf[hh, 0:n, :] = st
            cm[hh, 0:1, :] = jnp.max(st, axis=0, keepdims=True)

    def consume(bufs, vt_blk, first, heads=range(MLA_HEADS)):
        buf, cm = bufs
        n = vt_blk.shape[2]
        for hh in heads:
            st = buf[hh, 0:n, :]
            cmax = cm[hh, 0:1, :]
            if first:
                m_new = cmax
            else:
                m_old = m_ref[hh, 0:1, :]
                m_new = jnp.maximum(m_old, cmax)
                alpha = jnp.exp2(m_old - m_new)
            p = jnp.exp2(st - m_new).astype(BF16)
            pv = _dot(vt_blk[hh], p)
            acc_ref[hh] = pv if first else alpha * acc_ref[hh] + pv
            m_ref[hh, 0:1, :] = m_new

    def k_chunk(c):
        return k_ref[0, pl.ds(pl.multiple_of(c * TK_MLA, TK_MLA), TK_MLA), :]

    def overlapped(buf_w, k_blk, buf_r, vt_blk, first):
        for g in range(0, MLA_HEADS, MLA_GROUP):
            hs = range(g, g + MLA_GROUP)
            scores_into(buf_w, k_blk, hs)
            consume(buf_r, vt_blk, first, hs)

    scores_into(sa_ref, k_chunk(0))
    overlapped(sb_ref, k_chunk(1), sa_ref, vt_ref[0, 0], True)

    def body(i, _):
        c = 2 * i + 1
        overlapped(sa_ref, k_chunk(c + 1), sb_ref, vt_ref[0, c], False)
        overlapped(sb_ref, k_chunk(c + 2), sa_ref, vt_ref[0, c + 1], False)
        return 0

    lax.fori_loop(0, (nk - 2) // 2, body, 0)
    overlapped(sa_ref, kc_ref[0], sb_ref, vt_ref[0, nk - 1], False)
    consume(sa_ref, vct_ref[0, 0], False)

    outs = [acc_ref[hh, 0:MLA_V, :] * (1.0 / acc_ref[hh, MLA_V:MLA_V + 1, :]) for hh in range(MLA_HEADS)]
    o_ref[0] = jnp.concatenate(outs, axis=0).T.astype(BF16)


def _mla(q, k, vt, kc, vct):
    b, s, _ = q.shape
    c = kc.shape[1]
    nk = s // TK_MLA
    assert vt.shape == (b, nk, MLA_HEADS, VT_ROWS, TK_MLA) and nk % 2 == 0
    return pl.pallas_call(
        _mla_kernel,
        grid=(b, s // TQ_MLA),
        in_specs=[pl.BlockSpec((1, TQ_MLA, 1024), lambda bb, i: (bb, i, 0)),
                  pl.BlockSpec((1, s, 1024), lambda bb, i: (bb, 0, 0)),
                  pl.BlockSpec((1, nk, MLA_HEADS, VT_ROWS, TK_MLA), lambda bb, i: (bb, 0, 0, 0, 0)),
                  pl.BlockSpec((1, c, 1024), lambda bb, i: (bb, 0, 0)),
                  pl.BlockSpec((1, 1, MLA_HEADS, VT_ROWS, c), lambda bb, i: (bb, 0, 0, 0, 0))],
        out_specs=pl.BlockSpec((1, TQ_MLA, 512), lambda bb, i: (bb, i, 0)),
        out_shape=jax.ShapeDtypeStruct((b, s, 512), BF16),
        scratch_shapes=[pltpu.VMEM((MLA_HEADS, 8, TQ_MLA), F32),
                        pltpu.VMEM((MLA_HEADS, VT_ROWS, TQ_MLA), F32),
                        pltpu.VMEM((MLA_HEADS, TK_MLA, TQ_MLA), F32), pltpu.VMEM((MLA_HEADS, TK_MLA, TQ_MLA), F32),
                        pltpu.VMEM((MLA_HEADS, 8, TQ_MLA), F32), pltpu.VMEM((MLA_HEADS, 8, TQ_MLA), F32)],
        compiler_params=pltpu.CompilerParams(dimension_semantics=("arbitrary", "arbitrary"),
                                             vmem_limit_bytes=VMEM_LIMIT),
        name="mla",
    )(q, k, vt, kc, vct)


def _outproj_kernel(oa_ref, ob_ref, x_ref, woa_ref, wob_ref, gpost_ref, gate_ref, gffn_ref, shift_ref, scale_ref,
                    wr_pair_ref, wr_hi_ref, br_ref, tri_ref, upper_ref,
                    x1_ref, h2_ref, ri_ref, rw_ref, cnt_ref):
    tm = x_ref.shape[1]
    nsub = 2
    sub = tm // nsub
    rows = [slice(j * sub, (j + 1) * sub) for j in range(nsub)]
    ys = [_dot(oa_ref[0, r, :], woa_ref[...]) + _dot(ob_ref[0, r, :], wob_ref[...]) for r in rows]
    sub_lane_f = lax.broadcasted_iota(jnp.int32, (sub, LANES), 1).astype(F32)
    idx_parts, w_parts = [], []
    for r, y in zip(rows, ys):
        x1 = x_ref[0, r, :] + gate_ref[0] * _rms(y, gpost_ref[...])
        x1_ref[0, r, :] = x1
        h2 = _rms(x1, gffn_ref[...]) * (1.0 + scale_ref[0]) + shift_ref[0]
        h2_ref[0, r, :] = h2.astype(BF16)
        h_hi, h_lo = _split_bf16(h2)
        both = _dot(h_hi, wr_pair_ref[...])
        cur = both[:, :LANES] + both[:, LANES:] + _dot(h_lo, wr_hi_ref[...]) + br_ref[...]
        tops, idxs = [], []
        for _ in range(TOP_K):
            mk = jnp.max(cur, axis=-1, keepdims=True)
            ik = jnp.min(jnp.where(cur == mk, sub_lane_f, float(LANES)), axis=-1, keepdims=True)
            tops.append(mk)
            idxs.append(ik)
            cur = jnp.where(sub_lane_f == ik, -jnp.inf, cur)
        es = [jnp.exp(t - tops[0]) for t in tops]
        inv = 1.0 / functools.reduce(jnp.add, es)
        idx_parts.append(idxs)
        w_parts.append([e * inv for e in es])

    hot_parts = [[(sub_lane_f == ik) for ik in idxs] for idxs in idx_parts]
    onehot = jnp.concatenate([functools.reduce(jnp.add, [o.astype(F32) for o in hots]) for hots in hot_parts],
                             axis=0)
    cnt = jnp.sum(onehot, axis=0, keepdims=True)
    prefix = _dot(tri_ref[...], onehot.astype(BF16))
    chunks = jnp.floor((cnt + (RUN_ALIGN - 1.0)) * (1.0 / RUN_ALIGN))
    run_start = _dot(jnp.broadcast_to(chunks, (8, LANES)).astype(BF16), upper_ref[...])[0:1, :] * RUN_ALIGN
    base = prefix + run_start
    sub_lane = lax.broadcasted_iota(jnp.int32, (sub, LANES), 1)
    for r, hots, idxs, wts in zip(rows, hot_parts, idx_parts, w_parts):
        ri = jnp.zeros((sub, LANES), jnp.int32)
        rw = jnp.zeros((sub, LANES), F32)
        for k in range(TOP_K):
            pos = jnp.sum(jnp.where(hots[k], base[r, :], 0.0), axis=-1, keepdims=True).astype(jnp.int32)
            ri = jnp.where(sub_lane == k, idxs[k].astype(jnp.int32), ri)
            ri = jnp.where(sub_lane == TOP_K + k, pos, ri)
            rw = jnp.where(sub_lane == k, wts[k], rw)
        ri_ref[0, r, :] = ri
        rw_ref[0, r, :] = rw
    cnt_ref[0] = cnt


def _outproj(oa, ob, x, woa, wob, gpost, gate, gffn, shift, scale, wrh, wrl, br, tri, upper):
    b, s, d = x.shape
    tm = TM_OUT
    nt = s // tm
    tok = lambda bb, i: (bb, i, 0)
    vec = lambda bb, i: (bb, 0, 0)
    cst = lambda bb, i: (0, 0)
    return pl.pallas_call(
        _outproj_kernel,
        grid=(b, nt),
        in_specs=[pl.BlockSpec((1, tm, 512), tok), pl.BlockSpec((1, tm, 512), tok), pl.BlockSpec((1, tm, d), tok),
                  pl.BlockSpec((512, d), cst), pl.BlockSpec((512, d), cst), pl.BlockSpec((1, d), cst),
                  pl.BlockSpec((1, 1, d), vec), pl.BlockSpec((1, d), cst),
                  pl.BlockSpec((1, 1, d), vec), pl.BlockSpec((1, 1, d), vec),
                  pl.BlockSpec((d, 2 * LANES), cst), pl.BlockSpec((d, LANES), cst), pl.BlockSpec((1, LANES), cst),
                  pl.BlockSpec((tm, tm), cst), pl.BlockSpec((LANES, LANES), cst)],
        out_specs=[pl.BlockSpec((1, tm, d), tok), pl.BlockSpec((1, tm, d), tok),
                   pl.BlockSpec((1, tm, LANES), tok), pl.BlockSpec((1, tm, LANES), tok),
                   pl.BlockSpec((1, 1, LANES), lambda bb, i: (bb * nt + i, 0, 0))],
        out_shape=[jax.ShapeDtypeStruct((b, s, d), F32), jax.ShapeDtypeStruct((b, s, d), BF16),
                   jax.ShapeDtypeStruct((b, s, LANES), jnp.int32), jax.ShapeDtypeStruct((b, s, LANES), F32),
                   jax.ShapeDtypeStruct((b * nt, 1, LANES), F32)],
        compiler_params=pltpu.CompilerParams(dimension_semantics=("arbitrary", "arbitrary"),
                                             vmem_limit_bytes=VMEM_LIMIT),
        name="outproj",
    )(oa, ob, x, woa, wob, gpost, gate, gffn, shift, scale, wrh, wrl, br, tri, upper)


def _start_pieces(g, o, rows, make_copy):
    n_big = lax.shift_right_logical(rows, BIG_COPY.bit_length() - 1)
    n_small = lax.shift_right_logical(rows, RUN_ALIGN.bit_length() - 1) & (BIG_COPY // RUN_ALIGN - 1)

    def big(c, _):
        off = c * BIG_COPY
        make_copy(pl.multiple_of(g + off, RUN_ALIGN), pl.multiple_of(o + off, RUN_ALIGN), BIG_COPY).start()
        return 0

    def small(c, _):
        off = n_big * BIG_COPY + c * RUN_ALIGN
        make_copy(pl.multiple_of(g + off, RUN_ALIGN), pl.multiple_of(o + off, RUN_ALIGN), RUN_ALIGN).start()
        return 0

    lax.fori_loop(0, n_big, big, 0)
    lax.fori_loop(0, n_small, small, 0)
    return n_big, n_small


def _unrolled_loop(n, step):
    shift = LOOP_UNROLL.bit_length() - 1
    trips = lax.shift_right_logical(n, shift)

    def many(t, _):
        for u in range(LOOP_UNROLL):
            step(t * LOOP_UNROLL + u)
        return 0

    def one(i, _):
        step(i)
        return 0

    lax.fori_loop(0, trips, many, 0)
    lax.fori_loop(trips * LOOP_UNROLL, n, one, 0)


def _wait_pieces(n_big, n_small, make_copy):
    _unrolled_loop(n_big, lambda i: make_copy(0, 0, BIG_COPY).wait())
    _unrolled_loop(n_small, lambda i: make_copy(0, 0, RUN_ALIGN).wait())


def _start_listed_copies(plan, tile, make_copy):
    big_g, big_o, n_big, small_g, small_o, n_small = plan

    def start_list(g_ref, o_ref, n, width, rows):
        def start(p):
            j = tile * width + p
            make_copy(pl.multiple_of(g_ref[j], RUN_ALIGN), pl.multiple_of(o_ref[j], RUN_ALIGN), rows).start()

        _unrolled_loop(n, start)

    start_list(big_g, big_o, n_big[tile], MAX_BIG, BIG_COPY)
    start_list(small_g, small_o, n_small[tile], MAX_SMALL, RUN_ALIGN)


def _dispatch_kernel(bg_ref, bo_ref, nb_ref, sg_ref, so_ref, ns_ref, zs_ref, zr_ref, ri_ref, h_ref, xs_ref,
                     sorted_ref, zero_ref, sem):
    plan = (bg_ref, bo_ref, nb_ref, sg_ref, so_ref, ns_ref)
    tile = pl.program_id(0)
    last = pl.num_programs(0) - 1
    slot = tile % 2
    tm = h_ref.shape[0]
    post = ri_ref[...].astype(F32).T
    h = h_ref[...]
    rb_rows = 256
    pos_blk = [jnp.floor(post[TOP_K + k:TOP_K + k + 1, :] * (1.0 / rb_rows)) for k in range(TOP_K)]
    pos_off = [post[TOP_K + k:TOP_K + k + 1, :] - rb_rows * pos_blk[k] for k in range(TOP_K)]
    row = lax.broadcasted_iota(jnp.int32, (rb_rows, tm), 0).astype(F32).astype(BF16)
    one, zero = jnp.ones((), BF16), jnp.zeros((), BF16)

    for rb in range(SORTED_ROWS // rb_rows):
        perm = functools.reduce(jnp.add, [
            jnp.where(row == jnp.where(pos_blk[k] == float(rb), pos_off[k], -1.0).astype(BF16), one, zero)
            for k in range(TOP_K)])
        xr = _dot(perm, h)
        lo = lax.bitcast_convert_type(xr[:, :512], jnp.uint32)
        hi = lax.bitcast_convert_type(xr[:, 512:], jnp.uint32)
        sorted_ref[slot, rb * rb_rows:(rb + 1) * rb_rows, :] = (lo >> 16) | (hi & jnp.uint32(0xFFFF0000))

    def run_copy(sl):
        def make_copy(g, o, rows):
            return pltpu.make_async_copy(sorted_ref.at[sl, pl.ds(o, rows)], xs_ref.at[pl.ds(g, rows)], sem.at[sl])
        return make_copy

    _start_listed_copies(plan, tile, run_copy(slot))

    @pl.when(tile > 0)
    def _():
        _wait_pieces(nb_ref[tile - 1], ns_ref[tile - 1], run_copy(1 - slot))

    @pl.when(tile == last)
    def _():
        _wait_pieces(nb_ref[tile], ns_ref[tile], run_copy(slot))
        zero_ref[...] = jnp.zeros_like(zero_ref)

        def zero_copy(g, o, rows):
            return pltpu.make_async_copy(zero_ref.at[pl.ds(0, rows)], xs_ref.at[pl.ds(g, rows)], sem.at[2])

        def per_expert(e, totals):
            nb, ns = _start_pieces(zs_ref[e], 0, zr_ref[e], zero_copy)
            return totals[0] + nb, totals[1] + ns

        zb, zs = lax.fori_loop(0, zs_ref.shape[0], per_expert, (0, 0))
        _wait_pieces(zb, zs, zero_copy)


def _dispatch(plan, zstart, zrows, ri, h2, cap):
    t, d = h2.shape
    tm = TM_OUT
    return pl.pallas_call(
        _dispatch_kernel,
        grid_spec=pltpu.PrefetchScalarGridSpec(
            num_scalar_prefetch=8,
            grid=(t // tm,),
            in_specs=[pl.BlockSpec((tm, LANES), lambda i, *_: (i, 0)),
                      pl.BlockSpec((tm, d), lambda i, *_: (i, 0))],
            out_specs=pl.BlockSpec(memory_space=pl.ANY),
            scratch_shapes=[pltpu.VMEM((2, SORTED_ROWS, d // 2), jnp.uint32),
                            pltpu.VMEM((BIG_COPY, d // 2), jnp.uint32),
                            pltpu.SemaphoreType.DMA((3,))]),
        out_shape=jax.ShapeDtypeStruct((cap, d // 2), jnp.uint32),
        compiler_params=pltpu.CompilerParams(dimension_semantics=("arbitrary",),
                                             vmem_limit_bytes=VMEM_LIMIT),
        name="dispatch",
    )(*plan, zstart, zrows, ri, h2)


def _pack_bf16_pairs(x):
    n = x.shape[1] // 2
    bits = lax.bitcast_convert_type(x.astype(BF16).astype(F32), jnp.uint32)
    return (bits[:, :n] >> 16) | (bits[:, n:] & jnp.uint32(0xFFFF0000))


def _unpack_bf16_pairs(w):
    lo = lax.bitcast_convert_type(w << 16, F32).astype(BF16)
    hi = lax.bitcast_convert_type(w & jnp.uint32(0xFFFF0000), F32).astype(BF16)
    return lo, hi


def _experts_kernel(be_ref, nu_ref, slot_ref, nxt_ref, nv_ref, xs_ref, wgu_hbm, bgu_ref, wd_hbm, bd_ref, ys_ref,
                    wgu_f, wd_f, wgu_b, wd_b, sem):
    del nu_ref
    i = pl.program_id(0)
    e = be_ref[i]
    slot = slot_ref[i]
    valid = nv_ref[i]
    used = valid > 0
    run_start = used & ((i == 0) | (be_ref[jnp.maximum(i - 1, 0)] != e))

    def weight_copies(ex, sl):
        return (pltpu.make_async_copy(wgu_hbm.at[ex], wgu_f.at[sl], sem.at[0, sl]),
                pltpu.make_async_copy(wd_hbm.at[ex], wd_f.at[sl], sem.at[1, sl]))

    @pl.when(i == 0)
    def _():
        for cp in weight_copies(e, slot):
            cp.start()

    @pl.when(run_start)
    def _():
        for cp in weight_copies(e, slot):
            cp.wait()
        nxt = nxt_ref[i]

        @pl.when(nxt >= 0)
        def _():
            for cp in weight_copies(nxt, 1 - slot):
                cp.start()

        wgu_b[...] = wgu_f[slot].astype(BF16)
        wd_b[...] = wd_f[slot].astype(BF16)

    def compute(rows):
        gus = []
        for r in rows:
            x_lo, x_hi = _unpack_bf16_pairs(xs_ref[r, :])
            gus.append(_dot(x_lo, wgu_b[0:512, :]) + _dot(x_hi, wgu_b[512:, :]) + bgu_ref[0])
        for r, gu in zip(rows, gus):
            gate = jnp.minimum(gu[:, :D_FF], SWIGLU_LIMIT)
            lin = jnp.clip(gu[:, D_FF:], -SWIGLU_LIMIT, SWIGLU_LIMIT)
            act = (lin + 1.0) * (gate * jax.nn.sigmoid(SWIGLU_ALPHA * gate))
            ys_ref[r, :] = _pack_bf16_pairs(_dot(act.astype(BF16), wd_b[...]) + bd_ref[0])

    nsteps = MOE_BLOCK // EXPERT_STEP
    for n in range(nsteps + 1):
        lo, hi = (n - 1) * EXPERT_STEP, n * EXPERT_STEP
        cond = (valid == 0) if n == 0 else ((valid > lo) & (valid <= hi))
        cuts = list(range(0, hi, EXPERT_SUB)) + [hi]

        @pl.when(cond)
        def _(hi=hi, cuts=cuts):
            if hi:
                compute([slice(a, b) for a, b in zip(cuts[:-1], cuts[1:])])
            if hi < MOE_BLOCK:
                ys_ref[hi:, :] = jnp.zeros((MOE_BLOCK - hi, D_MODEL // 2), jnp.uint32)


def _experts(block_e, n_used, slot, nxt, nvalid, xs, wgu, bgu, wd, bd):
    cap = xs.shape[0]
    nblk = cap // MOE_BLOCK
    return pl.pallas_call(
        _experts_kernel,
        grid_spec=pltpu.PrefetchScalarGridSpec(
            num_scalar_prefetch=5,
            grid=(nblk,),
            in_specs=[pl.BlockSpec((MOE_BLOCK, 512), lambda i, be, nu, *_: (jnp.minimum(i, nu[0] - 1), 0)),
                      pl.BlockSpec(memory_space=pl.ANY),
                      pl.BlockSpec((1, 1, 2 * D_FF), lambda i, be, *_: (be[i], 0, 0)),
                      pl.BlockSpec(memory_space=pl.ANY),
                      pl.BlockSpec((1, 1, D_MODEL), lambda i, be, *_: (be[i], 0, 0))],
            out_specs=pl.BlockSpec((MOE_BLOCK, D_MODEL // 2), lambda i, *_: (i, 0)),
            scratch_shapes=[pltpu.VMEM((2, D_MODEL, 2 * D_FF), F32), pltpu.VMEM((2, D_FF, D_MODEL), F32),
                            pltpu.VMEM((D_MODEL, 2 * D_FF), BF16), pltpu.VMEM((D_FF, D_MODEL), BF16),
                            pltpu.SemaphoreType.DMA((2, 2))]),
        out_shape=jax.ShapeDtypeStruct((cap, D_MODEL // 2), jnp.uint32),
        compiler_params=pltpu.CompilerParams(dimension_semantics=("arbitrary",),
                                             vmem_limit_bytes=VMEM_LIMIT),
        name="experts",
    )(block_e, n_used, slot, nxt, nvalid, xs, wgu, bgu, wd, bd)


def _combine_kernel(bg_ref, bo_ref, nb_ref, sg_ref, so_ref, ns_ref, ys_ref, ri_ref, rw_ref, x1_ref, gate_ref, g_ref,
                    o_ref, ybuf, sem):
    plan = (bg_ref, bo_ref, nb_ref, sg_ref, so_ref, ns_ref)
    nt = pl.num_programs(1)
    tile = pl.program_id(0) * nt + pl.program_id(1)
    ntiles = pl.num_programs(0) * nt
    slot = tile % 2
    tm = x1_ref.shape[1]

    def run_copy(sl):
        def make_copy(g, o, rows):
            return pltpu.make_async_copy(ys_ref.at[pl.ds(g, rows)], ybuf.at[sl, pl.ds(o, rows)], sem.at[sl])
        return make_copy

    @pl.when(tile == 0)
    def _():
        ybuf[...] = jnp.zeros_like(ybuf)
        _start_listed_copies(plan, tile, run_copy(slot))

    @pl.when(tile + 1 < ntiles)
    def _():
        _start_listed_copies(plan, tile + 1, run_copy(1 - slot))

    _wait_pieces(nb_ref[tile], ns_ref[tile], run_copy(slot))

    posf = ri_ref[0].astype(F32)
    rw = rw_ref[0]
    cb_cols = 256
    pos = [posf[:, TOP_K + k:TOP_K + k + 1] for k in range(TOP_K)]
    pos_blk = [jnp.floor(p * (1.0 / cb_cols)) for p in pos]
    pos_off = [p - cb_cols * b for p, b in zip(pos, pos_blk)]
    wk = [rw[:, k:k + 1].astype(BF16) for k in range(TOP_K)]
    col = lax.broadcasted_iota(jnp.int32, (tm, cb_cols), 1).astype(F32).astype(BF16)
    zero = jnp.zeros((), BF16)
    f_lo = jnp.zeros((tm, D_MODEL // 2), F32)
    f_hi = jnp.zeros((tm, D_MODEL // 2), F32)
    for cb in range(SORTED_ROWS // cb_cols):
        wp = functools.reduce(jnp.add, [
            jnp.where(col == jnp.where(pos_blk[k] == float(cb), pos_off[k], -1.0).astype(BF16), wk[k], zero)
            for k in range(TOP_K)])
        y_lo, y_hi = _unpack_bf16_pairs(ybuf[slot, cb * cb_cols:(cb + 1) * cb_cols, :])
        f_lo = f_lo + _dot(wp, y_lo)
        f_hi = f_hi + _dot(wp, y_hi)
    f = jnp.concatenate([f_lo, f_hi], axis=1)
    o_ref[0] = x1_ref[0] + gate_ref[0] * _rms(f, g_ref[...])


def _combine(plan, ys, ri, rw, x1, gate, g):
    b, s, d = x1.shape
    tm = TM_OUT
    tok = lambda bb, i, *_: (bb, i, 0)
    return pl.pallas_call(
        _combine_kernel,
        grid_spec=pltpu.PrefetchScalarGridSpec(
            num_scalar_prefetch=6,
            grid=(b, s // tm),
            in_specs=[pl.BlockSpec(memory_space=pl.ANY),
                      pl.BlockSpec((1, tm, LANES), tok), pl.BlockSpec((1, tm, LANES), tok),
                      pl.BlockSpec((1, tm, d), tok),
                      pl.BlockSpec((1, 1, d), lambda bb, i, *_: (bb, 0, 0)),
                      pl.BlockSpec((1, d), lambda bb, i, *_: (0, 0))],
            out_specs=pl.BlockSpec((1, tm, d), tok),
            scratch_shapes=[pltpu.VMEM((2, SORTED_ROWS, d // 2), jnp.uint32), pltpu.SemaphoreType.DMA((2,))]),
        out_shape=jax.ShapeDtypeStruct((b, s, d), F32),
        compiler_params=pltpu.CompilerParams(dimension_semantics=("arbitrary", "arbitrary"),
                                             vmem_limit_bytes=VMEM_LIMIT),
        name="combine",
    )(*plan, ys, ri, rw, x1, gate, g)


def _rope_angles(rows, rot_dim):
    row = np.repeat(np.arange(rows, dtype=np.float64), GRID_W)
    col = np.tile(np.arange(GRID_W, dtype=np.float64), rows)
    quarter = rot_dim // 4
    inv_freq = ROPE_BASE ** (-np.arange(quarter, dtype=np.float64) / quarter)
    ang = np.concatenate([row[:, None] * inv_freq, col[:, None] * inv_freq], axis=-1)
    return np.cos(ang), np.sin(ang)


def _mla_lane_of():
    j = np.arange(MLA_NOPE + MLA_ROPE)
    r = j - MLA_NOPE
    half = MLA_ROPE // 2
    return np.where(j < 48, 16 + j, np.where(j < MLA_NOPE, 32 + j, np.where(r < half, r, 48 + r)))


def _to_mla_lanes(w, dims):
    lane_of = _mla_lane_of()[dims]
    src = np.full((LANES,), -1)
    src[lane_of] = np.arange(len(dims))
    picked = jnp.take(w, jnp.asarray(np.maximum(src, 0)), axis=-1)
    return jnp.where(jnp.asarray(src >= 0), picked, 0.0)


def _rope_tables(s):
    ca, sa = _rope_angles(s // GRID_W, HEAD_DIM)
    cb, sb = _rope_angles(s // GRID_W, MLA_ROPE)
    cos_a = np.tile(np.concatenate([ca, ca], 1), (1, 2))
    sin_a = np.tile(np.concatenate([-sa, sa], 1), (1, 2))
    cos_q = np.concatenate([cb, np.ones((s, 48)), cb, np.ones((s, 48))], 1)
    sin_q = np.concatenate([-sb, np.zeros((s, 48)), sb, np.zeros((s, 48))], 1)
    return tuple(jnp.asarray(t, F32) for t in (cos_a, sin_a, cos_q, sin_q))


def _piece_lists(slot_start, tile_start, counts, first_off, piece, width):
    ne = counts.shape[1]
    cum = jnp.cumsum(counts, axis=1)
    p = jnp.arange(width, dtype=jnp.int32)
    owner = jnp.minimum(jnp.sum((cum[:, None, :] <= p[None, :, None]).astype(jnp.int32), axis=2), ne - 1)
    sel = owner[:, :, None] == jnp.arange(ne, dtype=jnp.int32)[None, None, :]
    pick = lambda a: jnp.sum(jnp.where(sel, a[:, None, :], 0), axis=2)
    off = pick(first_off) + (p[None, :] - pick(cum - counts)) * piece
    as_list = lambda a: a.reshape(-1).astype(jnp.int32)
    return as_list(pick(slot_start) + off), as_list(pick(tile_start) + off), cum[:, -1].astype(jnp.int32)


def _identity_tables(n):
    one, zero = jnp.ones((n, LANES), F32), jnp.zeros((n, LANES), F32)
    return one, zero, one, zero


def _inproj_weights(w_in, g_q_a, w_uq, g_kv_a, w_ukv):
    d = w_in.shape[0]
    wq, wk, wv = w_in[:, :512], w_in[:, 512:640], w_in[:, 640:768]
    wcq, wckv, wkr = w_in[:, 768:1152], w_in[:, 1152:1408], w_in[:, 1408:1440]
    nope, rope = np.arange(MLA_NOPE), MLA_NOPE + np.arange(MLA_ROPE)
    wkr_p = _to_mla_lanes(wkr, rope)
    wlat = jnp.concatenate([wcq, wckv, wkr_p], 1).astype(BF16)
    wwin = jnp.concatenate([wq, wk, wv], 1).astype(BF16)
    wuq = _to_mla_lanes(w_uq.reshape(MLA_Q_LORA, MLA_HEADS, MLA_NOPE + MLA_ROPE), np.arange(MLA_NOPE + MLA_ROPE))
    wuq = wuq.reshape(MLA_Q_LORA, MLA_HEADS * LANES).astype(BF16)
    wukv = w_ukv.reshape(MLA_KV_LORA, MLA_HEADS, MLA_NOPE + MLA_V)
    wkk = _to_mla_lanes(wukv[:, :, :MLA_NOPE], nope).reshape(MLA_KV_LORA, MLA_HEADS * LANES)
    wvv = wukv[:, :, MLA_NOPE:].reshape(MLA_KV_LORA, MLA_HEADS * MLA_V)
    return (wlat, wwin, g_q_a.reshape(1, -1), wuq, g_kv_a.reshape(1, -1), wkk.astype(BF16), wvv.T.astype(BF16))


def kernel(x, c, ctx, c_ctx, w_ada, b_ada, g_mix_pre, g_mix_post, w_in, sink, g_q_a, w_uq, g_kv_a, w_ukv, w_o,
           g_ffn_pre, g_ffn_post, w_router, b_router, w_gate_up, b_gate_up, w_down, b_down):
    b, s, d = x.shape
    cl = ctx.shape[1]
    t = b * s

    cc = jnp.zeros((8, d), F32).at[:b].set(c).at[b].set(c_ctx)
    mod_all = _adaln(cc, w_ada[0], b_ada[0].reshape(1, -1))
    mod = [mod_all[:b, i * d:(i + 1) * d].reshape(b, 1, d) for i in range(6)]
    mod_c = [jnp.broadcast_to(mod_all[b, i * d:(i + 1) * d].reshape(1, 1, d), (b, 1, d)) for i in range(2)]

    wts = _inproj_weights(w_in[0], g_q_a[0], w_uq[0], g_kv_a[0], w_ukv[0])
    gpre = g_mix_pre[0].reshape(1, d)
    qa, ka, vat, q, k, vt = _inproj(x, mod[0], mod[1], gpre, _rope_tables(s), wts, TM_IN, True)
    kac, vatc, kc, vct = _inproj(ctx, mod_c[0], mod_c[1], gpre, _identity_tables(cl), wts, cl, False)

    sinkv = jnp.repeat(sink[0] * LOG2_E, BLOCK).reshape(A_KV_HEADS, 1, -1)
    out_a = _gqa(sinkv, qa, ka, vat, kac, vatc)
    out_b = _mla(q, k, vt, kc, vct)

    wo = w_o[0].astype(BF16)
    wr = jnp.pad(w_router[0], ((0, 0), (0, LANES - N_EXPERTS)))
    wr_hi, wr_lo = _split_bf16(wr)
    br = jnp.concatenate([b_router[0], jnp.full((LANES - N_EXPERTS,), NEG_INF, F32)]).reshape(1, LANES)
    ii = jnp.arange(TM_OUT)
    tri = (ii[:, None] > ii[None, :]).astype(BF16)
    jj = jnp.arange(LANES)
    upper = (jj[:, None] < jj[None, :]).astype(BF16)
    x1, h2, ri, rw, cnt = _outproj(out_a, out_b, x, wo[:512], wo[512:], g_mix_post[0].reshape(1, d), mod[2],
                                   g_ffn_pre[0].reshape(1, d), mod[3], mod[4],
                                   jnp.concatenate([wr_hi, wr_lo], axis=1), wr_hi, br, tri, upper)

    cnt = cnt[:, 0, :N_EXPERTS].astype(jnp.int32)
    nt = cnt.shape[0]
    rows = (cnt + RUN_ALIGN - 1) // RUN_ALIGN * RUN_ALIGN
    tot = jnp.sum(rows, axis=0)
    carry = jnp.cumsum(rows, axis=0) - rows
    padded = (tot + MOE_BLOCK - 1) // MOE_BLOCK * MOE_BLOCK
    pad_end = jnp.cumsum(padded)
    pad_start = pad_end - padded
    nblk = -(-(t * TOP_K + nt * N_EXPERTS * (RUN_ALIGN - 1)) // MOE_BLOCK) + N_EXPERTS
    cap = nblk * MOE_BLOCK
    blk_row = jnp.arange(nblk, dtype=jnp.int32) * MOE_BLOCK
    block_e = jnp.minimum(jnp.sum((pad_end[None, :] <= blk_row[:, None]).astype(jnp.int32), axis=1), N_EXPERTS - 1)
    n_used = (pad_end[-1] // MOE_BLOCK).astype(jnp.int32).reshape(1)
    toff = jnp.cumsum(rows, axis=1) - rows
    gstart = pad_start[None, :] + carry
    n_big = rows // BIG_COPY
    n_small = (rows - n_big * BIG_COPY) // RUN_ALIGN
    big_g, big_o, tot_big = _piece_lists(gstart, toff, n_big, jnp.zeros_like(rows), BIG_COPY, MAX_BIG)
    small_g, small_o, tot_small = _piece_lists(gstart, toff, n_small, n_big * BIG_COPY, RUN_ALIGN, MAX_SMALL)
    plan = (big_g, big_o, tot_big, small_g, small_o, tot_small)
    zstart = jnp.concatenate([pad_start + tot, pad_end[-1:]]).astype(jnp.int32)
    zrows = jnp.concatenate([padded - tot, cap - pad_end[-1:]]).astype(jnp.int32)
    block_e = block_e.astype(jnp.int32)
    run_idx = jnp.cumsum(jnp.concatenate([jnp.zeros((1,), jnp.int32),
                                          (block_e[1:] != block_e[:-1]).astype(jnp.int32)]))
    eid = jnp.arange(N_EXPERTS, dtype=jnp.int32)
    later_used = (tot[None, :] > 0) & (eid[None, :] > eid[:, None])
    next_e = jnp.min(jnp.where(later_used, eid[None, :], N_EXPERTS), axis=1)
    next_e = jnp.where(next_e < N_EXPERTS, next_e, -1).astype(jnp.int32)
    nxt = jnp.sum(jnp.where(block_e[:, None] == eid[None, :], next_e[None, :], 0), axis=1).astype(jnp.int32)

    xs = _dispatch(plan, zstart, zrows, ri.reshape(t, LANES), h2.reshape(t, d), cap)
    run_end = jnp.sum(jnp.where(block_e[:, None] == eid[None, :], (pad_start + tot)[None, :], 0), axis=1)
    nvalid = jnp.where(blk_row < pad_end[-1], jnp.clip(run_end - blk_row, 0, MOE_BLOCK), 0).astype(jnp.int32)
    ys = _experts(block_e, n_used, (run_idx % 2).astype(jnp.int32), nxt, nvalid, xs, w_gate_up[0],
                  b_gate_up[0].reshape(N_EXPERTS, 1, -1), w_down[0], b_down[0].reshape(N_EXPERTS, 1, -1))
    return _combine(plan, ys, ri, rw, x1, mod[5], g_ffn_post[0].reshape(1, d))
```

```python
import functools

import jax
import jax.numpy as jnp
import numpy as np
from jax import lax
from jax.experimental import pallas as pl
from jax.experimental.pallas import tpu as pltpu

F32 = jnp.float32
BF16 = jnp.bfloat16

D_MODEL = 1024
GRID_W = 64
HEAD_DIM = 64
A_HEADS = 8
A_KV_HEADS = 2
BLOCK = 128
A_SCALE = HEAD_DIM ** -0.5
MLA_HEADS = 8
MLA_NOPE = 64
MLA_ROPE = 32
MLA_V = 64
MLA_Q_LORA = 384
MLA_KV_LORA = 256
MLA_SCALE = (MLA_NOPE + MLA_ROPE) ** -0.5
LOG2_E = 1.4426950408889634
N_EXPERTS = 32
TOP_K = 4
D_FF = 1024
SWIGLU_LIMIT = 7.0
SWIGLU_ALPHA = 1.702
ROPE_BASE = 10000.0
EPS = 1e-6
NEG_INF = -1e30

LANES = 128
VMEM_LIMIT = 56 * 1024 * 1024

TM_IN = 512
GQA_QB = 8
TQ_MLA = 256
TK_MLA = 512
MLA_GROUP = 2
VT_ROWS = 80
TM_OUT = 512
MOE_BLOCK = 512
EXPERT_SUB = 256
EXPERT_STEP = 128
RUN_ALIGN = 8
BIG_COPY = 32
SORTED_ROWS = TM_OUT * TOP_K + N_EXPERTS * RUN_ALIGN
LOOP_UNROLL = 8
MAX_BIG = SORTED_ROWS // BIG_COPY
MAX_SMALL = N_EXPERTS * (BIG_COPY // RUN_ALIGN - 1)


def _dot(a, b):
    return jnp.dot(a, b, preferred_element_type=F32)


def _dot_nt(a, b):
    return lax.dot_general(a, b, (((1,), (1,)), ((), ())), preferred_element_type=F32)


def _split_bf16(x):
    hi = x.astype(BF16)
    lo = (x - hi.astype(F32)).astype(BF16)
    return hi, lo


def _rms(x, g):
    return x * lax.rsqrt(jnp.mean(x * x, axis=-1, keepdims=True) + EPS) * g


def _rope(x, cos, sin_signed, first_half, half):
    n = x.shape[-1]
    partner = jnp.where(first_half, pltpu.roll(x, n - half, 1), pltpu.roll(x, half, 1))
    return x * cos + partner * sin_signed


def _adaln_kernel(c_ref, w_ref, b_ref, o_ref):
    c = c_ref[...]
    s = c * jax.nn.sigmoid(c)
    s_hi, s_lo = _split_bf16(s)
    w_hi, w_lo = _split_bf16(w_ref[...])
    o_ref[...] = _dot(s_hi, w_hi) + _dot(s_hi, w_lo) + _dot(s_lo, w_hi) + b_ref[...]


def _adaln(cc, w, b):
    n = w.shape[1]
    tn = 1024
    return pl.pallas_call(
        _adaln_kernel,
        grid=(n // tn,),
        in_specs=[pl.BlockSpec((8, D_MODEL), lambda j: (0, 0)),
                  pl.BlockSpec((D_MODEL, tn), lambda j: (0, j)),
                  pl.BlockSpec((1, tn), lambda j: (0, j))],
        out_specs=pl.BlockSpec((8, tn), lambda j: (0, j)),
        out_shape=jax.ShapeDtypeStruct((8, n), F32),
        compiler_params=pltpu.CompilerParams(dimension_semantics=("arbitrary",),
                                             vmem_limit_bytes=VMEM_LIMIT),
        name="adaln",
    )(cc, w, b)


def _inproj_kernel(with_q, x_ref, shift_ref, scale_ref, gpre_ref, ca_ref, sa_ref, cq_ref, sq_ref,
                   wlat_ref, wwin_ref, gq_ref, wuq_ref, gkv_ref, wk_ref, wv_ref, *out_refs):
    if with_q:
        qa_ref, ka_ref, vat_ref, q_ref, k_ref, vt_ref = out_refs
    else:
        ka_ref, vat_ref, k_ref, vt_ref = out_refs
    x = x_ref[0]
    tm = x.shape[0]
    h = _rms(x, gpre_ref[...]) * (1.0 + scale_ref[0]) + shift_ref[0]
    hb = h.astype(BF16)
    lat = _dot(hb, wlat_ref[...])
    p = _dot(hb, wwin_ref[...])
    c_ka = A_HEADS * HEAD_DIM if with_q else 0
    c_va = c_ka + LANES
    c_ckv, c_kr = MLA_Q_LORA, MLA_Q_LORA + MLA_KV_LORA
    lane = lax.broadcasted_iota(jnp.int32, (tm, LANES), 1)

    first_a = (lane & 32) == 0
    ca, sa = ca_ref[...], sa_ref[...]
    if with_q:
        for j in range(A_HEADS * HEAD_DIM // LANES):
            sl = slice(j * LANES, (j + 1) * LANES)
            qa_ref[0, :, sl] = (_rope(p[:, sl], ca, sa, first_a, 32) * (A_SCALE * LOG2_E)).astype(BF16)
    ka_ref[0] = _rope(p[:, c_ka:c_va], ca, sa, first_a, 32).astype(BF16)
    vat = p[:, c_va:c_va + LANES].T
    ones_a = (lax.broadcasted_iota(jnp.int32, (VT_ROWS - HEAD_DIM, BLOCK), 0) == 0).astype(BF16)
    for j in range(tm // BLOCK):
        for hk in range(A_KV_HEADS):
            vat_ref[0, j, hk, 0:HEAD_DIM, :] = vat[hk * HEAD_DIM:(hk + 1) * HEAD_DIM,
                                                   j * BLOCK:(j + 1) * BLOCK].astype(BF16)
            vat_ref[0, j, hk, HEAD_DIM:VT_ROWS, :] = ones_a

    cq, sq = cq_ref[...], sq_ref[...]
    rope_mla = lambda t: t * cq + pltpu.roll(t, LANES // 2, 1) * sq
    if with_q:
        cqn = _rms(lat[:, 0:c_ckv], gq_ref[...]).astype(BF16)
        qf = _dot(cqn, wuq_ref[...])
        for hh in range(MLA_HEADS):
            sl = slice(hh * LANES, (hh + 1) * LANES)
            q_ref[0, :, sl] = (rope_mla(qf[:, sl]) * (MLA_SCALE * LOG2_E)).astype(BF16)

    ckvn = _rms(lat[:, c_ckv:c_kr], gkv_ref[...]).astype(BF16)
    kr = rope_mla(lat[:, c_kr:c_kr + LANES])
    kn = _dot(ckvn, wk_ref[...])
    for hh in range(MLA_HEADS):
        sl = slice(hh * LANES, (hh + 1) * LANES)
        k_ref[0, :, sl] = (kn[:, sl] + kr).astype(BF16)
    vt = _dot_nt(wv_ref[...], ckvn).astype(BF16)
    ones_row = (lax.broadcasted_iota(jnp.int32, (VT_ROWS - MLA_V, tm), 0) == 0).astype(BF16)
    for hh in range(MLA_HEADS):
        vt_ref[0, 0, hh, 0:MLA_V, :] = vt[hh * MLA_V:(hh + 1) * MLA_V, :]
        vt_ref[0, 0, hh, MLA_V:VT_ROWS, :] = ones_row


def _inproj(x, mod, mod_row, gpre, tabs, wts, tm, with_q):
    bx, n, d = x.shape
    wlat, wwin, gq, wuq, gkv, wk, wv = wts
    if not with_q:
        wwin = wwin[:, A_HEADS * HEAD_DIM:]
    wts = (wlat, wwin, gq, wuq, gkv, wk, wv)
    tok = lambda i, b: (b, i, 0)
    vec = lambda col: (lambda i, b: (b if mod_row is None else mod_row, 0, col))
    tab = lambda i, b: (i, 0)
    cst = lambda i, b: (0, 0)
    in_specs = [pl.BlockSpec((1, tm, d), tok), pl.BlockSpec((1, 1, d), vec(0)), pl.BlockSpec((1, 1, d), vec(1)),
                pl.BlockSpec((1, d), cst)]
    in_specs += [pl.BlockSpec((tm, LANES), tab)] * len(tabs)
    in_specs += [pl.BlockSpec(w.shape, cst) for w in wts]
    row_out = lambda w: (pl.BlockSpec((1, tm, w), tok), jax.ShapeDtypeStruct((bx, n, w), BF16))
    win_outs = [row_out(LANES),
                (pl.BlockSpec((1, tm // BLOCK, A_KV_HEADS, VT_ROWS, BLOCK), lambda i, b: (b, i, 0, 0, 0)),
                 jax.ShapeDtypeStruct((bx, n // BLOCK, A_KV_HEADS, VT_ROWS, BLOCK), BF16))]
    mla_outs = [row_out(MLA_HEADS * LANES),
                (pl.BlockSpec((1, 1, MLA_HEADS, VT_ROWS, tm), lambda i, b: (b, i, 0, 0, 0)),
                 jax.ShapeDtypeStruct((bx, n // tm, MLA_HEADS, VT_ROWS, tm), BF16))]
    if with_q:
        outs = [row_out(A_HEADS * HEAD_DIM)] + win_outs + [row_out(MLA_HEADS * LANES)] + mla_outs
    else:
        outs = win_outs + mla_outs
    return pl.pallas_call(
        functools.partial(_inproj_kernel, with_q),
        grid=(n // tm, bx),
        in_specs=in_specs,
        out_specs=[o[0] for o in outs],
        out_shape=[o[1] for o in outs],
        compiler_params=pltpu.CompilerParams(dimension_semantics=("arbitrary", "arbitrary"),
                                             vmem_limit_bytes=VMEM_LIMIT),
        name="inproj" if with_q else "inproj_ctx",
    )(x, mod, mod, gpre, *tabs, *wts)


def _gqa_kernel(sink_ref, q_ref, *refs):
    nwin = GQA_QB + 2
    k_refs, kx_ref = refs[:nwin], refs[nwin]
    v_refs, vx_ref = refs[nwin + 1:2 * nwin + 1], refs[2 * nwin + 1]
    o_ref = refs[2 * nwin + 2]
    n = pl.program_id(1)
    nsteps = pl.num_programs(1)
    nctx = kx_ref.shape[1] // BLOCK
    group = A_HEADS // A_KV_HEADS
    key = lax.broadcasted_iota(jnp.int32, (BLOCK, group * BLOCK), 0)
    qry = lax.broadcasted_iota(jnp.int32, (BLOCK, group * BLOCK), 1) & (BLOCK - 1)
    sts = {}

    def scores(qb):
        kcat = jnp.concatenate([k_refs[qb + j][0] for j in range(3)] + [kx_ref[0]], axis=0)
        for hk in range(A_KV_HEADS):
            qg = jnp.concatenate([q_ref[0, qb * BLOCK:(qb + 1) * BLOCK, hh * HEAD_DIM:(hh + 1) * HEAD_DIM]
                                  for hh in range(hk * group, (hk + 1) * group)], axis=0)
            k_hk = kcat[:, hk * HEAD_DIM:(hk + 1) * HEAD_DIM]
            sts[qb, hk] = _dot_nt(k_hk, qg)

    def finish(qb):
        vis_prev = key >= qry
        vis_next = key <= qry
        if qb == 0:
            vis_prev = key >= qry + jnp.where(n > 0, 0, 2 * BLOCK)
        if qb == GQA_QB - 1:
            vis_next = key <= qry - jnp.where(n < nsteps - 1, 0, 2 * BLOCK)
        outs = []
        for hk in range(A_KV_HEADS):
            st = sts[qb, hk]
            pieces = [jnp.where(vis_prev, st[0:BLOCK], NEG_INF), st[BLOCK:2 * BLOCK],
                      jnp.where(vis_next, st[2 * BLOCK:3 * BLOCK], NEG_INF)]
            pieces += [st[(3 + j) * BLOCK:(4 + j) * BLOCK] for j in range(nctx)]
            sink = sink_ref[hk]
            m = jnp.maximum(functools.reduce(jnp.maximum, [jnp.max(p, axis=0, keepdims=True) for p in pieces]),
                            sink)
            ps = [jnp.exp2(p - m).astype(BF16) for p in pieces]
            vts = [v_refs[qb + j][0, 0, hk] for j in range(3)] + [vx_ref[0, j, hk] for j in range(nctx)]
            pv = functools.reduce(jnp.add, [_dot(vt, p) for vt, p in zip(vts, ps)])
            l = pv[HEAD_DIM:HEAD_DIM + 1, :] + jnp.exp2(sink - m)
            o = pv[0:HEAD_DIM, :] * (1.0 / l)
            outs += [o[:, g * BLOCK:(g + 1) * BLOCK] for g in range(group)]
        for pair in range(A_HEADS // 2):
            both = jnp.concatenate([outs[2 * pair], outs[2 * pair + 1]], axis=0)
            o_ref[0, qb * BLOCK:(qb + 1) * BLOCK, pair * LANES:(pair + 1) * LANES] = both.T.astype(BF16)

    for qb in range(GQA_QB):
        scores(qb)
    for qb in range(GQA_QB):
        finish(qb)


def _gqa(sinkv, qa, ka, vat, kac, vatc):
    b, s, _ = qa.shape
    nb = s // BLOCK
    c = kac.shape[1]
    tq = GQA_QB * BLOCK
    cur = lambda bb, n: (bb, n, 0)
    win = lambda j: (lambda bb, n: (bb, jnp.clip(n * GQA_QB + j - 1, 0, nb - 1), 0))
    vwin = lambda j: (lambda bb, n: (bb, jnp.clip(n * GQA_QB + j - 1, 0, nb - 1), 0, 0, 0))
    vblk = (1, 1, A_KV_HEADS, VT_ROWS, BLOCK)
    in_specs = [pl.BlockSpec(sinkv.shape, lambda bb, n: (0, 0, 0)), pl.BlockSpec((1, tq, A_HEADS * HEAD_DIM), cur)]
    in_specs += [pl.BlockSpec((1, BLOCK, LANES), win(j)) for j in range(GQA_QB + 2)]
    in_specs += [pl.BlockSpec((1, c, LANES), lambda bb, n: (bb, 0, 0))]
    in_specs += [pl.BlockSpec(vblk, vwin(j)) for j in range(GQA_QB + 2)]
    in_specs += [pl.BlockSpec((1, c // BLOCK, A_KV_HEADS, VT_ROWS, BLOCK), lambda bb, n: (bb, 0, 0, 0, 0))]
    return pl.pallas_call(
        _gqa_kernel,
        grid=(b, nb // GQA_QB),
        in_specs=in_specs,
        out_specs=pl.BlockSpec((1, tq, 512), cur),
        out_shape=jax.ShapeDtypeStruct((b, s, 512), BF16),
        compiler_params=pltpu.CompilerParams(dimension_semantics=("arbitrary", "arbitrary"),
                                             vmem_limit_bytes=VMEM_LIMIT),
        name="gqa",
    )(sinkv, qa, *([ka] * (GQA_QB + 2)), kac, *([vat] * (GQA_QB + 2)), vatc)


def _mla_kernel(q_ref, k_ref, vt_ref, kc_ref, vct_ref, o_ref, m_ref, acc_ref, sa_ref, sb_ref, cma_ref, cmb_ref):
    nk = k_ref.shape[1] // TK_MLA

    sa_ref, sb_ref = (sa_ref, cma_ref), (sb_ref, cmb_ref)

    def scores_into(bufs, k_blk, heads=range(MLA_HEADS)):
        buf, cm = bufs
        n = k_blk.shape[0]
        for hh in heads:
            hsl = slice(hh * LANES, (hh + 1) * LANES)
            st = _dot_nt(k_blk[:, hsl], q_ref[0, :, hsl])
            buf[hh, 0:n, :] = st
            cm[hh, 0:1, :] = jnp.max(st, axis=0, keepdims=True)

    def consume(bufs, vt_blk, first, heads=range(MLA_HEADS)):
        buf, cm = bufs
        n = vt_blk.shape[2]
        for hh in heads:
            st = buf[hh, 0:n, :]
            cmax = cm[hh, 0:1, :]
            if first:
                m_new = cmax
            else:
                m_old = m_ref[hh, 0:1, :]
                m_new = jnp.maximum(m_old, cmax)
                alpha = jnp.exp2(m_old - m_new)
            p = jnp.exp2(st - m_new).astype(BF16)
            pv = _dot(vt_blk[hh], p)
            acc_ref[hh] = pv if first else alpha * acc_ref[hh] + pv
            m_ref[hh, 0:1, :] = m_new

    def k_chunk(c):
        return k_ref[0, pl.ds(pl.multiple_of(c * TK_MLA, TK_MLA), TK_MLA), :]

    def overlapped(buf_w, k_blk, buf_r, vt_blk, first):
        for g in range(0, MLA_HEADS, MLA_GROUP):
            hs = range(g, g + MLA_GROUP)
            scores_into(buf_w, k_blk, hs)
            consume(buf_r, vt_blk, first, hs)

    scores_into(sa_ref, k_chunk(0))
    overlapped(sb_ref, k_chunk(1), sa_ref, vt_ref[0, 0], True)

    def body(i, _):
        c = 2 * i + 1
        overlapped(sa_ref, k_chunk(c + 1), sb_ref, vt_ref[0, c], False)
        overlapped(sb_ref, k_chunk(c + 2), sa_ref, vt_ref[0, c + 1], False)
        return 0

    lax.fori_loop(0, (nk - 2) // 2, body, 0)
    overlapped(sa_ref, kc_ref[0], sb_ref, vt_ref[0, nk - 1], False)
    consume(sa_ref, vct_ref[0, 0], False)

    outs = [acc_ref[hh, 0:MLA_V, :] * (1.0 / acc_ref[hh, MLA_V:MLA_V + 1, :]) for hh in range(MLA_HEADS)]
    o_ref[0] = jnp.concatenate(outs, axis=0).T.astype(BF16)


def _mla(q, k, vt, kc, vct):
    b, s, _ = q.shape
    c = kc.shape[1]
    nk = s // TK_MLA
    assert vt.shape == (b, nk, MLA_HEADS, VT_ROWS, TK_MLA) and nk % 2 == 0
    return pl.pallas_call(
        _mla_kernel,
        grid=(b, s // TQ_MLA),
        in_specs=[pl.BlockSpec((1, TQ_MLA, 1024), lambda bb, i: (bb, i, 0)),
                  pl.BlockSpec((1, s, 1024), lambda bb, i: (bb, 0, 0)),
                  pl.BlockSpec((1, nk, MLA_HEADS, VT_ROWS, TK_MLA), lambda bb, i: (bb, 0, 0, 0, 0)),
                  pl.BlockSpec((1, c, 1024), lambda bb, i: (bb, 0, 0)),
                  pl.BlockSpec((1, 1, MLA_HEADS, VT_ROWS, c), lambda bb, i: (bb, 0, 0, 0, 0))],
        out_specs=pl.BlockSpec((1, TQ_MLA, 512), lambda bb, i: (bb, i, 0)),
        out_shape=jax.ShapeDtypeStruct((b, s, 512), BF16),
        scratch_shapes=[pltpu.VMEM((MLA_HEADS, 8, TQ_MLA), F32),
                        pltpu.VMEM((MLA_HEADS, VT_ROWS, TQ_MLA), F32),
                        pltpu.VMEM((MLA_HEADS, TK_MLA, TQ_MLA), F32), pltpu.VMEM((MLA_HEADS, TK_MLA, TQ_MLA), F32),
                        pltpu.VMEM((MLA_HEADS, 8, TQ_MLA), F32), pltpu.VMEM((MLA_HEADS, 8, TQ_MLA), F32)],
        compiler_params=pltpu.CompilerParams(dimension_semantics=("arbitrary", "arbitrary"),
                                             vmem_limit_bytes=VMEM_LIMIT),
        name="mla",
    )(q, k, vt, kc, vct)


def _outproj_kernel(oa_ref, ob_ref, x_ref, woa_ref, wob_ref, gpost_ref, gate_ref, gffn_ref, shift_ref, scale_ref,
                    wr_pair_ref, wr_hi_ref, br_ref, tri_ref, upper_ref,
                    x1_ref, h2_ref, ri_ref, rw_ref, cnt_ref):
    tm = x_ref.shape[1]
    nsub = 2
    sub = tm // nsub
    rows = [slice(j * sub, (j + 1) * sub) for j in range(nsub)]
    ys = [_dot(oa_ref[0, r, :], woa_ref[...]) + _dot(ob_ref[0, r, :], wob_ref[...]) for r in rows]
    sub_lane_f = lax.broadcasted_iota(jnp.int32, (sub, LANES), 1).astype(F32)
    idx_parts, w_parts = [], []
    for r, y in zip(rows, ys):
        x1 = x_ref[0, r, :] + gate_ref[0] * _rms(y, gpost_ref[...])
        x1_ref[0, r, :] = x1
        h2 = _rms(x1, gffn_ref[...]) * (1.0 + scale_ref[0]) + shift_ref[0]
        h2_ref[0, r, :] = h2.astype(BF16)
        h_hi, h_lo = _split_bf16(h2)
        both = _dot(h_hi, wr_pair_ref[...])
        cur = both[:, :LANES] + both[:, LANES:] + _dot(h_lo, wr_hi_ref[...]) + br_ref[...]
        tops, idxs = [], []
        for _ in range(TOP_K):
            mk = jnp.max(cur, axis=-1, keepdims=True)
            ik = jnp.min(jnp.where(cur == mk, sub_lane_f, float(LANES)), axis=-1, keepdims=True)
            tops.append(mk)
            idxs.append(ik)
            cur = jnp.where(sub_lane_f == ik, -jnp.inf, cur)
        es = [jnp.exp(t - tops[0]) for t in tops]
        inv = 1.0 / functools.reduce(jnp.add, es)
        idx_parts.append(idxs)
        w_parts.append([e * inv for e in es])

    hot_parts = [[(sub_lane_f == ik) for ik in idxs] for idxs in idx_parts]
    onehot = jnp.concatenate([functools.reduce(jnp.add, [o.astype(F32) for o in hots]) for hots in hot_parts],
                             axis=0)
    cnt = jnp.sum(onehot, axis=0, keepdims=True)
    prefix = _dot(tri_ref[...], onehot.astype(BF16))
    chunks = jnp.floor((cnt + (RUN_ALIGN - 1.0)) * (1.0 / RUN_ALIGN))
    run_start = _dot(jnp.broadcast_to(chunks, (8, LANES)).astype(BF16), upper_ref[...])[0:1, :] * RUN_ALIGN
    base = prefix + run_start
    sub_lane = lax.broadcasted_iota(jnp.int32, (sub, LANES), 1)
    for r, hots, idxs, wts in zip(rows, hot_parts, idx_parts, w_parts):
        ri = jnp.zeros((sub, LANES), jnp.int32)
        rw = jnp.zeros((sub, LANES), F32)
        for k in range(TOP_K):
            pos = jnp.sum(jnp.where(hots[k], base[r, :], 0.0), axis=-1, keepdims=True).astype(jnp.int32)
            ri = jnp.where(sub_lane == k, idxs[k].astype(jnp.int32), ri)
            ri = jnp.where(sub_lane == TOP_K + k, pos, ri)
            rw = jnp.where(sub_lane == k, wts[k], rw)
        ri_ref[0, r, :] = ri
        rw_ref[0, r, :] = rw
    cnt_ref[0] = cnt


def _outproj(oa, ob, x, woa, wob, gpost, mod, gffn, wrh, wrl, br, tri, upper):
    b, s, d = x.shape
    tm = TM_OUT
    nt = s // tm
    tok = lambda bb, i: (bb, i, 0)
    vec = lambda col: (lambda bb, i: (bb, 0, col))
    cst = lambda bb, i: (0, 0)
    return pl.pallas_call(
        _outproj_kernel,
        grid=(b, nt),
        in_specs=[pl.BlockSpec((1, tm, 512), tok), pl.BlockSpec((1, tm, 512), tok), pl.BlockSpec((1, tm, d), tok),
                  pl.BlockSpec((512, d), cst), pl.BlockSpec((512, d), cst), pl.BlockSpec((1, d), cst),
                  pl.BlockSpec((1, 1, d), vec(2)), pl.BlockSpec((1, d), cst),
                  pl.BlockSpec((1, 1, d), vec(3)), pl.BlockSpec((1, 1, d), vec(4)),
                  pl.BlockSpec((d, 2 * LANES), cst), pl.BlockSpec((d, LANES), cst), pl.BlockSpec((1, LANES), cst),
                  pl.BlockSpec((tm, tm), cst), pl.BlockSpec((LANES, LANES), cst)],
        out_specs=[pl.BlockSpec((1, tm, d), tok), pl.BlockSpec((1, tm, d), tok),
                   pl.BlockSpec((1, tm, LANES), tok), pl.BlockSpec((1, tm, LANES), tok),
                   pl.BlockSpec((1, 1, LANES), lambda bb, i: (bb * nt + i, 0, 0))],
        out_shape=[jax.ShapeDtypeStruct((b, s, d), F32), jax.ShapeDtypeStruct((b, s, d), BF16),
                   jax.ShapeDtypeStruct((b, s, LANES), jnp.int32), jax.ShapeDtypeStruct((b, s, LANES), F32),
                   jax.ShapeDtypeStruct((b * nt, 1, LANES), F32)],
        compiler_params=pltpu.CompilerParams(dimension_semantics=("arbitrary", "arbitrary"),
                                             vmem_limit_bytes=VMEM_LIMIT),
        name="outproj",
    )(oa, ob, x, woa, wob, gpost, mod, gffn, mod, mod, wrh, wrl, br, tri, upper)


def _start_pieces(g, o, rows, make_copy):
    n_big = lax.shift_right_logical(rows, BIG_COPY.bit_length() - 1)
    n_small = lax.shift_right_logical(rows, RUN_ALIGN.bit_length() - 1) & (BIG_COPY // RUN_ALIGN - 1)

    def big(c, _):
        off = c * BIG_COPY
        make_copy(pl.multiple_of(g + off, RUN_ALIGN), pl.multiple_of(o + off, RUN_ALIGN), BIG_COPY).start()
        return 0

    def small(c, _):
        off = n_big * BIG_COPY + c * RUN_ALIGN
        make_copy(pl.multiple_of(g + off, RUN_ALIGN), pl.multiple_of(o + off, RUN_ALIGN), RUN_ALIGN).start()
        return 0

    lax.fori_loop(0, n_big, big, 0)
    lax.fori_loop(0, n_small, small, 0)
    return n_big, n_small


def _unrolled_loop(n, step):
    shift = LOOP_UNROLL.bit_length() - 1
    trips = lax.shift_right_logical(n, shift)

    def many(t, _):
        for u in range(LOOP_UNROLL):
            step(t * LOOP_UNROLL + u)
        return 0

    def one(i, _):
        step(i)
        return 0

    lax.fori_loop(0, trips, many, 0)
    lax.fori_loop(trips * LOOP_UNROLL, n, one, 0)


def _wait_pieces(n_big, n_small, make_copy):
    _unrolled_loop(n_big, lambda i: make_copy(0, 0, BIG_COPY).wait())
    _unrolled_loop(n_small, lambda i: make_copy(0, 0, RUN_ALIGN).wait())


def _start_listed_copies(plan, tile, make_copy):
    big_g, big_o, n_big, small_g, small_o, n_small = plan

    def start_list(g_ref, o_ref, n, width, rows):
        def start(p):
            j = tile * width + p
            make_copy(pl.multiple_of(g_ref[j], RUN_ALIGN), pl.multiple_of(o_ref[j], RUN_ALIGN), rows).start()

        _unrolled_loop(n, start)

    start_list(big_g, big_o, n_big[tile], MAX_BIG, BIG_COPY)
    start_list(small_g, small_o, n_small[tile], MAX_SMALL, RUN_ALIGN)


def _dispatch_kernel(bg_ref, bo_ref, nb_ref, sg_ref, so_ref, ns_ref, zs_ref, zr_ref, ri_ref, h_ref, xs_ref,
                     sorted_ref, zero_ref, sem):
    plan = (bg_ref, bo_ref, nb_ref, sg_ref, so_ref, ns_ref)
    tile = pl.program_id(0)
    last = pl.num_programs(0) - 1
    slot = tile % 2
    tm = h_ref.shape[0]
    post = ri_ref[...].astype(F32).T
    h = h_ref[...]
    rb_rows = 256
    pos_blk = [jnp.floor(post[TOP_K + k:TOP_K + k + 1, :] * (1.0 / rb_rows)) for k in range(TOP_K)]
    pos_off = [post[TOP_K + k:TOP_K + k + 1, :] - rb_rows * pos_blk[k] for k in range(TOP_K)]
    row = lax.broadcasted_iota(jnp.int32, (rb_rows, tm), 0).astype(F32).astype(BF16)
    one, zero = jnp.ones((), BF16), jnp.zeros((), BF16)

    for rb in range(SORTED_ROWS // rb_rows):
        perm = functools.reduce(jnp.add, [
            jnp.where(row == jnp.where(pos_blk[k] == float(rb), pos_off[k], -1.0).astype(BF16), one, zero)
            for k in range(TOP_K)])
        xr = _dot(perm, h)
        lo = lax.bitcast_convert_type(xr[:, :512], jnp.uint32)
        hi = lax.bitcast_convert_type(xr[:, 512:], jnp.uint32)
        sorted_ref[slot, rb * rb_rows:(rb + 1) * rb_rows, :] = (lo >> 16) | (hi & jnp.uint32(0xFFFF0000))

    def run_copy(sl):
        def make_copy(g, o, rows):
            return pltpu.make_async_copy(sorted_ref.at[sl, pl.ds(o, rows)], xs_ref.at[pl.ds(g, rows)], sem.at[sl])
        return make_copy

    _start_listed_copies(plan, tile, run_copy(slot))

    @pl.when(tile > 0)
    def _():
        _wait_pieces(nb_ref[tile - 1], ns_ref[tile - 1], run_copy(1 - slot))

    @pl.when(tile == last)
    def _():
        _wait_pieces(nb_ref[tile], ns_ref[tile], run_copy(slot))
        zero_ref[...] = jnp.zeros_like(zero_ref)

        def zero_copy(g, o, rows):
            return pltpu.make_async_copy(zero_ref.at[pl.ds(0, rows)], xs_ref.at[pl.ds(g, rows)], sem.at[2])

        def per_expert(e, totals):
            nb, ns = _start_pieces(zs_ref[e], 0, zr_ref[e], zero_copy)
            return totals[0] + nb, totals[1] + ns

        zb, zs = lax.fori_loop(0, zs_ref.shape[0], per_expert, (0, 0))
        _wait_pieces(zb, zs, zero_copy)


def _dispatch(plan, zstart, zrows, ri, h2, cap):
    t, d = h2.shape
    tm = TM_OUT
    return pl.pallas_call(
        _dispatch_kernel,
        grid_spec=pltpu.PrefetchScalarGridSpec(
            num_scalar_prefetch=8,
            grid=(t // tm,),
            in_specs=[pl.BlockSpec((tm, LANES), lambda i, *_: (i, 0)),
                      pl.BlockSpec((tm, d), lambda i, *_: (i, 0))],
            out_specs=pl.BlockSpec(memory_space=pl.ANY),
            scratch_shapes=[pltpu.VMEM((2, SORTED_ROWS, d // 2), jnp.uint32),
                            pltpu.VMEM((BIG_COPY, d // 2), jnp.uint32),
                            pltpu.SemaphoreType.DMA((3,))]),
        out_shape=jax.ShapeDtypeStruct((cap, d // 2), jnp.uint32),
        compiler_params=pltpu.CompilerParams(dimension_semantics=("arbitrary",),
                                             vmem_limit_bytes=VMEM_LIMIT),
        name="dispatch",
    )(*plan, zstart, zrows, ri, h2)


def _pack_bf16_pairs(x):
    n = x.shape[1] // 2
    bits = lax.bitcast_convert_type(x.astype(BF16).astype(F32), jnp.uint32)
    return (bits[:, :n] >> 16) | (bits[:, n:] & jnp.uint32(0xFFFF0000))


def _unpack_bf16_pairs(w):
    lo = lax.bitcast_convert_type(w << 16, F32).astype(BF16)
    hi = lax.bitcast_convert_type(w & jnp.uint32(0xFFFF0000), F32).astype(BF16)
    return lo, hi


def _experts_kernel(be_ref, nu_ref, slot_ref, nxt_ref, nv_ref, xs_ref, wgu_hbm, bgu_ref, wd_hbm, bd_ref, ys_ref,
                    wgu_f, wd_f, wgu_b, wd_b, sem):
    del nu_ref
    i = pl.program_id(0)
    e = be_ref[i]
    slot = slot_ref[i]
    valid = nv_ref[i]
    used = valid > 0
    run_start = used & ((i == 0) | (be_ref[jnp.maximum(i - 1, 0)] != e))

    def weight_copies(ex, sl):
        return (pltpu.make_async_copy(wgu_hbm.at[ex], wgu_f.at[sl], sem.at[0, sl]),
                pltpu.make_async_copy(wd_hbm.at[ex], wd_f.at[sl], sem.at[1, sl]))

    @pl.when(i == 0)
    def _():
        for cp in weight_copies(e, slot):
            cp.start()

    @pl.when(run_start)
    def _():
        for cp in weight_copies(e, slot):
            cp.wait()
        nxt = nxt_ref[i]

        @pl.when(nxt >= 0)
        def _():
            for cp in weight_copies(nxt, 1 - slot):
                cp.start()

        wgu_b[...] = wgu_f[slot].astype(BF16)
        wd_b[...] = wd_f[slot].astype(BF16)

    def compute(rows):
        gus = []
        for r in rows:
            x_lo, x_hi = _unpack_bf16_pairs(xs_ref[r, :])
            gus.append(_dot(x_lo, wgu_b[0:512, :]) + _dot(x_hi, wgu_b[512:, :]) + bgu_ref[0])
        for r, gu in zip(rows, gus):
            gate = jnp.minimum(gu[:, :D_FF], SWIGLU_LIMIT)
            lin = jnp.clip(gu[:, D_FF:], -SWIGLU_LIMIT, SWIGLU_LIMIT)
            act = (lin + 1.0) * (gate * jax.nn.sigmoid(SWIGLU_ALPHA * gate))
            ys_ref[r, :] = _pack_bf16_pairs(_dot(act.astype(BF16), wd_b[...]) + bd_ref[0])

    nsteps = MOE_BLOCK // EXPERT_STEP
    for n in range(nsteps + 1):
        lo, hi = (n - 1) * EXPERT_STEP, n * EXPERT_STEP
        cond = (valid == 0) if n == 0 else ((valid > lo) & (valid <= hi))
        cuts = list(range(0, hi, EXPERT_SUB)) + [hi]

        @pl.when(cond)
        def _(hi=hi, cuts=cuts):
            if hi:
                compute([slice(a, b) for a, b in zip(cuts[:-1], cuts[1:])])
            if hi < MOE_BLOCK:
                ys_ref[hi:, :] = jnp.zeros((MOE_BLOCK - hi, D_MODEL // 2), jnp.uint32)


def _experts(block_e, n_used, slot, nxt, nvalid, xs, wgu, bgu, wd, bd):
    cap = xs.shape[0]
    nblk = cap // MOE_BLOCK
    return pl.pallas_call(
        _experts_kernel,
        grid_spec=pltpu.PrefetchScalarGridSpec(
            num_scalar_prefetch=5,
            grid=(nblk,),
            in_specs=[pl.BlockSpec((MOE_BLOCK, 512), lambda i, be, nu, *_: (jnp.minimum(i, nu[0] - 1), 0)),
                      pl.BlockSpec(memory_space=pl.ANY),
                      pl.BlockSpec((1, 1, 2 * D_FF), lambda i, be, *_: (be[i], 0, 0)),
                      pl.BlockSpec(memory_space=pl.ANY),
                      pl.BlockSpec((1, 1, D_MODEL), lambda i, be, *_: (be[i], 0, 0))],
            out_specs=pl.BlockSpec((MOE_BLOCK, D_MODEL // 2), lambda i, *_: (i, 0)),
            scratch_shapes=[pltpu.VMEM((2, D_MODEL, 2 * D_FF), F32), pltpu.VMEM((2, D_FF, D_MODEL), F32),
                            pltpu.VMEM((D_MODEL, 2 * D_FF), BF16), pltpu.VMEM((D_FF, D_MODEL), BF16),
                            pltpu.SemaphoreType.DMA((2, 2))]),
        out_shape=jax.ShapeDtypeStruct((cap, D_MODEL // 2), jnp.uint32),
        compiler_params=pltpu.CompilerParams(dimension_semantics=("arbitrary",),
                                             vmem_limit_bytes=VMEM_LIMIT),
        name="experts",
    )(block_e, n_used, slot, nxt, nvalid, xs, wgu, bgu, wd, bd)


def _combine_kernel(bg_ref, bo_ref, nb_ref, sg_ref, so_ref, ns_ref, ys_ref, ri_ref, rw_ref, x1_ref, gate_ref, g_ref,
                    o_ref, ybuf, sem):
    plan = (bg_ref, bo_ref, nb_ref, sg_ref, so_ref, ns_ref)
    nt = pl.num_programs(1)
    tile = pl.program_id(0) * nt + pl.program_id(1)
    ntiles = pl.num_programs(0) * nt
    slot = tile % 2
    tm = x1_ref.shape[1]

    def run_copy(sl):
        def make_copy(g, o, rows):
            return pltpu.make_async_copy(ys_ref.at[pl.ds(g, rows)], ybuf.at[sl, pl.ds(o, rows)], sem.at[sl])
        return make_copy

    @pl.when(tile == 0)
    def _():
        ybuf[...] = jnp.zeros_like(ybuf)
        _start_listed_copies(plan, tile, run_copy(slot))

    @pl.when(tile + 1 < ntiles)
    def _():
        _start_listed_copies(plan, tile + 1, run_copy(1 - slot))

    _wait_pieces(nb_ref[tile], ns_ref[tile], run_copy(slot))

    posf = ri_ref[0].astype(F32)
    rw = rw_ref[0]
    cb_cols = 256
    pos = [posf[:, TOP_K + k:TOP_K + k + 1] for k in range(TOP_K)]
    pos_blk = [jnp.floor(p * (1.0 / cb_cols)) for p in pos]
    pos_off = [p - cb_cols * b for p, b in zip(pos, pos_blk)]
    wk = [rw[:, k:k + 1].astype(BF16) for k in range(TOP_K)]
    col = lax.broadcasted_iota(jnp.int32, (tm, cb_cols), 1).astype(F32).astype(BF16)
    zero = jnp.zeros((), BF16)
    f_lo = jnp.zeros((tm, D_MODEL // 2), F32)
    f_hi = jnp.zeros((tm, D_MODEL // 2), F32)
    for cb in range(SORTED_ROWS // cb_cols):
        wp = functools.reduce(jnp.add, [
            jnp.where(col == jnp.where(pos_blk[k] == float(cb), pos_off[k], -1.0).astype(BF16), wk[k], zero)
            for k in range(TOP_K)])
        y_lo, y_hi = _unpack_bf16_pairs(ybuf[slot, cb * cb_cols:(cb + 1) * cb_cols, :])
        f_lo = f_lo + _dot(wp, y_lo)
        f_hi = f_hi + _dot(wp, y_hi)
    f = jnp.concatenate([f_lo, f_hi], axis=1)
    o_ref[0] = x1_ref[0] + gate_ref[0] * _rms(f, g_ref[...])


def _combine(plan, ys, ri, rw, x1, mod, g):
    b, s, d = x1.shape
    tm = TM_OUT
    tok = lambda bb, i, *_: (bb, i, 0)
    return pl.pallas_call(
        _combine_kernel,
        grid_spec=pltpu.PrefetchScalarGridSpec(
            num_scalar_prefetch=6,
            grid=(b, s // tm),
            in_specs=[pl.BlockSpec(memory_space=pl.ANY),
                      pl.BlockSpec((1, tm, LANES), tok), pl.BlockSpec((1, tm, LANES), tok),
                      pl.BlockSpec((1, tm, d), tok),
                      pl.BlockSpec((1, 1, d), lambda bb, i, *_: (bb, 0, 5)),
                      pl.BlockSpec((1, d), lambda bb, i, *_: (0, 0))],
            out_specs=pl.BlockSpec((1, tm, d), tok),
            scratch_shapes=[pltpu.VMEM((2, SORTED_ROWS, d // 2), jnp.uint32), pltpu.SemaphoreType.DMA((2,))]),
        out_shape=jax.ShapeDtypeStruct((b, s, d), F32),
        compiler_params=pltpu.CompilerParams(dimension_semantics=("arbitrary", "arbitrary"),
                                             vmem_limit_bytes=VMEM_LIMIT),
        name="combine",
    )(*plan, ys, ri, rw, x1, mod, g)


def _rope_angles(rows, rot_dim):
    row = np.repeat(np.arange(rows, dtype=np.float64), GRID_W)
    col = np.tile(np.arange(GRID_W, dtype=np.float64), rows)
    quarter = rot_dim // 4
    inv_freq = ROPE_BASE ** (-np.arange(quarter, dtype=np.float64) / quarter)
    ang = np.concatenate([row[:, None] * inv_freq, col[:, None] * inv_freq], axis=-1)
    return np.cos(ang), np.sin(ang)


def _mla_lane_of():
    j = np.arange(MLA_NOPE + MLA_ROPE)
    r = j - MLA_NOPE
    half = MLA_ROPE // 2
    return np.where(j < 48, 16 + j, np.where(j < MLA_NOPE, 32 + j, np.where(r < half, r, 48 + r)))


def _to_mla_lanes(w, dims):
    lane_of = _mla_lane_of()[dims]
    src = np.full((LANES,), -1)
    src[lane_of] = np.arange(len(dims))
    picked = jnp.take(w, jnp.asarray(np.maximum(src, 0)), axis=-1)
    return jnp.where(jnp.asarray(src >= 0), picked, 0.0)


def _rope_tables(s):
    ca, sa = _rope_angles(s // GRID_W, HEAD_DIM)
    cb, sb = _rope_angles(s // GRID_W, MLA_ROPE)
    cos_a = np.tile(np.concatenate([ca, ca], 1), (1, 2))
    sin_a = np.tile(np.concatenate([-sa, sa], 1), (1, 2))
    cos_q = np.concatenate([cb, np.ones((s, 48)), cb, np.ones((s, 48))], 1)
    sin_q = np.concatenate([-sb, np.zeros((s, 48)), sb, np.zeros((s, 48))], 1)
    return tuple(jnp.asarray(t, F32) for t in (cos_a, sin_a, cos_q, sin_q))


def _piece_lists(slot_start, tile_start, counts, first_off, piece, width):
    ne = counts.shape[1]
    cum = jnp.cumsum(counts, axis=1)
    p = jnp.arange(width, dtype=jnp.int32)
    owner = jnp.minimum(jnp.sum((cum[:, None, :] <= p[None, :, None]).astype(jnp.int32), axis=2), ne - 1)
    sel = owner[:, :, None] == jnp.arange(ne, dtype=jnp.int32)[None, None, :]
    pick = lambda a: jnp.sum(jnp.where(sel, a[:, None, :], 0), axis=2)
    off = pick(first_off) + (p[None, :] - pick(cum - counts)) * piece
    as_list = lambda a: a.reshape(-1).astype(jnp.int32)
    return as_list(pick(slot_start) + off), as_list(pick(tile_start) + off), cum[:, -1].astype(jnp.int32)


def _identity_tables(n):
    one, zero = jnp.ones((n, LANES), F32), jnp.zeros((n, LANES), F32)
    return one, zero, one, zero


def _inproj_weights(w_in, g_q_a, w_uq, g_kv_a, w_ukv):
    d = w_in.shape[0]
    wq, wk, wv = w_in[:, :512], w_in[:, 512:640], w_in[:, 640:768]
    wcq, wckv, wkr = w_in[:, 768:1152], w_in[:, 1152:1408], w_in[:, 1408:1440]
    nope, rope = np.arange(MLA_NOPE), MLA_NOPE + np.arange(MLA_ROPE)
    wkr_p = _to_mla_lanes(wkr, rope)
    wlat = jnp.concatenate([wcq, wckv, wkr_p], 1).astype(BF16)
    wwin = jnp.concatenate([wq, wk, wv], 1).astype(BF16)
    wuq = _to_mla_lanes(w_uq.reshape(MLA_Q_LORA, MLA_HEADS, MLA_NOPE + MLA_ROPE), np.arange(MLA_NOPE + MLA_ROPE))
    wuq = wuq.reshape(MLA_Q_LORA, MLA_HEADS * LANES).astype(BF16)
    wukv = w_ukv.reshape(MLA_KV_LORA, MLA_HEADS, MLA_NOPE + MLA_V)
    wkk = _to_mla_lanes(wukv[:, :, :MLA_NOPE], nope).reshape(MLA_KV_LORA, MLA_HEADS * LANES)
    wvv = wukv[:, :, MLA_NOPE:].reshape(MLA_KV_LORA, MLA_HEADS * MLA_V)
    return (wlat, wwin, g_q_a.reshape(1, -1), wuq, g_kv_a.reshape(1, -1), wkk.astype(BF16), wvv.T.astype(BF16))


def kernel(x, c, ctx, c_ctx, w_ada, b_ada, g_mix_pre, g_mix_post, w_in, sink, g_q_a, w_uq, g_kv_a, w_ukv, w_o,
           g_ffn_pre, g_ffn_post, w_router, b_router, w_gate_up, b_gate_up, w_down, b_down):
    b, s, d = x.shape
    cl = ctx.shape[1]
    t = b * s

    cc = jnp.concatenate([c, c_ctx[None, :], jnp.zeros((8 - b - 1, d), F32)], axis=0)
    mod = _adaln(cc, w_ada[0], b_ada[0].reshape(1, -1)).reshape(8, 1, 6 * d)

    wts = _inproj_weights(w_in[0], g_q_a[0], w_uq[0], g_kv_a[0], w_ukv[0])
    gpre = g_mix_pre[0].reshape(1, d)
    qa, ka, vat, q, k, vt = _inproj(x, mod, None, gpre, _rope_tables(s), wts, TM_IN, True)
    kac, vatc, kc, vct = _inproj(ctx, mod, b, gpre, _identity_tables(cl), wts, cl, False)

    sinkv = jnp.repeat(sink[0] * LOG2_E, BLOCK).reshape(A_KV_HEADS, 1, -1)
    out_a = _gqa(sinkv, qa, ka, vat, kac, vatc)
    out_b = _mla(q, k, vt, kc, vct)

    wo = w_o[0].astype(BF16)
    wr = jnp.pad(w_router[0], ((0, 0), (0, LANES - N_EXPERTS)))
    wr_hi, wr_lo = _split_bf16(wr)
    br = jnp.concatenate([b_router[0], jnp.full((LANES - N_EXPERTS,), NEG_INF, F32)]).reshape(1, LANES)
    tri = jnp.asarray(np.tri(TM_OUT, k=-1), BF16)
    upper = jnp.asarray(np.tri(LANES, k=-1).T, BF16)
    x1, h2, ri, rw, cnt = _outproj(out_a, out_b, x, wo[:512], wo[512:], g_mix_post[0].reshape(1, d), mod,
                                   g_ffn_pre[0].reshape(1, d),
                                   jnp.concatenate([wr_hi, wr_lo], axis=1), wr_hi, br, tri, upper)

    cnt = cnt[:, 0, :N_EXPERTS].astype(jnp.int32)
    nt = cnt.shape[0]
    rows = (cnt + RUN_ALIGN - 1) // RUN_ALIGN * RUN_ALIGN
    tot = jnp.sum(rows, axis=0)
    carry = jnp.cumsum(rows, axis=0) - rows
    padded = (tot + MOE_BLOCK - 1) // MOE_BLOCK * MOE_BLOCK
    pad_end = jnp.cumsum(padded)
    pad_start = pad_end - padded
    nblk = -(-(t * TOP_K + nt * N_EXPERTS * (RUN_ALIGN - 1)) // MOE_BLOCK) + N_EXPERTS
    cap = nblk * MOE_BLOCK
    blk_row = jnp.arange(nblk, dtype=jnp.int32) * MOE_BLOCK
    block_e = jnp.minimum(jnp.sum((pad_end[None, :] <= blk_row[:, None]).astype(jnp.int32), axis=1), N_EXPERTS - 1)
    n_used = (pad_end[-1] // MOE_BLOCK).astype(jnp.int32).reshape(1)
    toff = jnp.cumsum(rows, axis=1) - rows
    gstart = pad_start[None, :] + carry
    n_big = rows // BIG_COPY
    n_small = (rows - n_big * BIG_COPY) // RUN_ALIGN
    big_g, big_o, tot_big = _piece_lists(gstart, toff, n_big, jnp.zeros_like(rows), BIG_COPY, MAX_BIG)
    small_g, small_o, tot_small = _piece_lists(gstart, toff, n_small, n_big * BIG_COPY, RUN_ALIGN, MAX_SMALL)
    plan = (big_g, big_o, tot_big, small_g, small_o, tot_small)
    zstart = jnp.concatenate([pad_start + tot, pad_end[-1:]]).astype(jnp.int32)
    zrows = jnp.concatenate([padded - tot, cap - pad_end[-1:]]).astype(jnp.int32)
    block_e = block_e.astype(jnp.int32)
    run_idx = jnp.cumsum(jnp.concatenate([jnp.zeros((1,), jnp.int32),
                                          (block_e[1:] != block_e[:-1]).astype(jnp.int32)]))
    eid = jnp.arange(N_EXPERTS, dtype=jnp.int32)
    later_used = (tot[None, :] > 0) & (eid[None, :] > eid[:, None])
    next_e = jnp.min(jnp.where(later_used, eid[None, :], N_EXPERTS), axis=1)
    next_e = jnp.where(next_e < N_EXPERTS, next_e, -1).astype(jnp.int32)
    nxt = jnp.sum(jnp.where(block_e[:, None] == eid[None, :], next_e[None, :], 0), axis=1).astype(jnp.int32)

    xs = _dispatch(plan, zstart, zrows, ri.reshape(t, LANES), h2.reshape(t, d), cap)
    run_end = jnp.sum(jnp.where(block_e[:, None] == eid[None, :], (pad_start + tot)[None, :], 0), axis=1)
    nvalid = jnp.where(blk_row < pad_end[-1], jnp.clip(run_end - blk_row, 0, MOE_BLOCK), 0).astype(jnp.int32)
    ys = _experts(block_e, n_used, (run_idx % 2).astype(jnp.int32), nxt, nvalid, xs, w_gate_up[0],
                  b_gate_up[0].reshape(N_EXPERTS, 1, -1), w_down[0], b_down[0].reshape(N_EXPERTS, 1, -1))
    return _combine(plan, ys, ri, rw, x1, mod, g_ffn_post[0].reshape(1, d))
```

```python
import functools

import jax
import jax.numpy as jnp
import numpy as np
from jax import lax
from jax.experimental import pallas as pl
from jax.experimental.pallas import tpu as pltpu

F32 = jnp.float32
BF16 = jnp.bfloat16

D_MODEL = 1024
GRID_W = 64
HEAD_DIM = 64
A_HEADS = 8
A_KV_HEADS = 2
BLOCK = 128
A_SCALE = HEAD_DIM ** -0.5
MLA_HEADS = 8
MLA_NOPE = 64
MLA_ROPE = 32
MLA_V = 64
MLA_Q_LORA = 384
MLA_KV_LORA = 256
MLA_SCALE = (MLA_NOPE + MLA_ROPE) ** -0.5
LOG2_E = 1.4426950408889634
N_EXPERTS = 32
TOP_K = 4
D_FF = 1024
SWIGLU_LIMIT = 7.0
SWIGLU_ALPHA = 1.702
ROPE_BASE = 10000.0
EPS = 1e-6
NEG_INF = -1e30

LANES = 128
VMEM_LIMIT = 56 * 1024 * 1024

TM_IN = 512
GQA_QB = 8
TQ_MLA = 256
TK_MLA = 512
MLA_GROUP = 2
VT_ROWS = 80
TM_OUT = 512
MOE_BLOCK = 512
EXPERT_SUB = 256
EXPERT_STEP = 128
RUN_ALIGN = 8
BIG_COPY = 32
SORTED_ROWS = TM_OUT * TOP_K + N_EXPERTS * RUN_ALIGN
LOOP_UNROLL = 8
MAX_BIG = SORTED_ROWS // BIG_COPY
MAX_SMALL = N_EXPERTS * (BIG_COPY // RUN_ALIGN - 1)


def _dot(a, b):
    return jnp.dot(a, b, preferred_element_type=F32)


def _dot_nt(a, b):
    return lax.dot_general(a, b, (((1,), (1,)), ((), ())), preferred_element_type=F32)


def _split_bf16(x):
    hi = x.astype(BF16)
    lo = (x - hi.astype(F32)).astype(BF16)
    return hi, lo


def _rms(x, g):
    return x * lax.rsqrt(jnp.mean(x * x, axis=-1, keepdims=True) + EPS) * g


def _rope(x, cos, sin_signed, first_half, half):
    n = x.shape[-1]
    partner = jnp.where(first_half, pltpu.roll(x, n - half, 1), pltpu.roll(x, half, 1))
    return x * cos + partner * sin_signed


def _adaln_kernel(c_ref, w_ref, b_ref, o_ref):
    c = c_ref[...]
    s = c * jax.nn.sigmoid(c)
    s_hi, s_lo = _split_bf16(s)
    w_hi, w_lo = _split_bf16(w_ref[...])
    o_ref[...] = _dot(s_hi, w_hi) + _dot(s_hi, w_lo) + _dot(s_lo, w_hi) + b_ref[...]


def _adaln(cc, w, b):
    n = w.shape[1]
    tn = 1024
    return pl.pallas_call(
        _adaln_kernel,
        grid=(n // tn,),
        in_specs=[pl.BlockSpec((8, D_MODEL), lambda j: (0, 0)),
                  pl.BlockSpec((D_MODEL, tn), lambda j: (0, j)),
                  pl.BlockSpec((1, tn), lambda j: (0, j))],
        out_specs=pl.BlockSpec((8, tn), lambda j: (0, j)),
        out_shape=jax.ShapeDtypeStruct((8, n), F32),
        compiler_params=pltpu.CompilerParams(dimension_semantics=("arbitrary",),
                                             vmem_limit_bytes=VMEM_LIMIT),
        name="adaln",
    )(cc, w, b)


def _inproj_kernel(with_q, x_ref, shift_ref, scale_ref, gpre_ref, ca_ref, sa_ref, cq_ref, sq_ref,
                   wlat_ref, wwin_ref, gq_ref, wuq_ref, gkv_ref, wk_ref, wv_ref, *out_refs):
    if with_q:
        qa_ref, ka_ref, vat_ref, q_ref, k_ref, vt_ref = out_refs
    else:
        ka_ref, vat_ref, k_ref, vt_ref = out_refs
    x = x_ref[0]
    tm = x.shape[0]
    h = _rms(x, gpre_ref[...]) * (1.0 + scale_ref[0]) + shift_ref[0]
    hb = h.astype(BF16)
    lat = _dot(hb, wlat_ref[...])
    p = _dot(hb, wwin_ref[...])
    c_ka = A_HEADS * HEAD_DIM if with_q else 0
    c_va = c_ka + LANES
    c_ckv, c_kr = MLA_Q_LORA, MLA_Q_LORA + MLA_KV_LORA
    lane = lax.broadcasted_iota(jnp.int32, (tm, LANES), 1)

    first_a = (lane & 32) == 0
    ca, sa = ca_ref[...], sa_ref[...]
    if with_q:
        for j in range(A_HEADS * HEAD_DIM // LANES):
            sl = slice(j * LANES, (j + 1) * LANES)
            qa_ref[0, :, sl] = (_rope(p[:, sl], ca, sa, first_a, 32) * (A_SCALE * LOG2_E)).astype(BF16)
    ka_ref[0] = _rope(p[:, c_ka:c_va], ca, sa, first_a, 32).astype(BF16)
    vat = p[:, c_va:c_va + LANES].T
    ones_a = (lax.broadcasted_iota(jnp.int32, (VT_ROWS - HEAD_DIM, BLOCK), 0) == 0).astype(BF16)
    for j in range(tm // BLOCK):
        for hk in range(A_KV_HEADS):
            vat_ref[0, j, hk, 0:HEAD_DIM, :] = vat[hk * HEAD_DIM:(hk + 1) * HEAD_DIM,
                                                   j * BLOCK:(j + 1) * BLOCK].astype(BF16)
            vat_ref[0, j, hk, HEAD_DIM:VT_ROWS, :] = ones_a

    cq, sq = cq_ref[...], sq_ref[...]
    rope_mla = lambda t: t * cq + pltpu.roll(t, LANES // 2, 1) * sq
    if with_q:
        cqn = _rms(lat[:, 0:c_ckv], gq_ref[...]).astype(BF16)
        qf = _dot(cqn, wuq_ref[...])
        for hh in range(MLA_HEADS):
            sl = slice(hh * LANES, (hh + 1) * LANES)
            q_ref[0, :, sl] = (rope_mla(qf[:, sl]) * (MLA_SCALE * LOG2_E)).astype(BF16)

    ckvn = _rms(lat[:, c_ckv:c_kr], gkv_ref[...]).astype(BF16)
    kr = rope_mla(lat[:, c_kr:c_kr + LANES])
    kn = _dot(ckvn, wk_ref[...])
    for hh in range(MLA_HEADS):
        sl = slice(hh * LANES, (hh + 1) * LANES)
        k_ref[0, :, sl] = (kn[:, sl] + kr).astype(BF16)
    vt = _dot_nt(wv_ref[...], ckvn).astype(BF16)
    ones_row = (lax.broadcasted_iota(jnp.int32, (VT_ROWS - MLA_V, tm), 0) == 0).astype(BF16)
    for hh in range(MLA_HEADS):
        vt_ref[0, 0, hh, 0:MLA_V, :] = vt[hh * MLA_V:(hh + 1) * MLA_V, :]
        vt_ref[0, 0, hh, MLA_V:VT_ROWS, :] = ones_row


def _inproj(x, mod, mod_row, gpre, tabs, wts, tm, with_q):
    bx, n, d = x.shape
    wlat, wwin, gq, wuq, gkv, wk, wv = wts
    if not with_q:
        wwin = wwin[:, A_HEADS * HEAD_DIM:]
    wts = (wlat, wwin, gq, wuq, gkv, wk, wv)
    tok = lambda i, b: (b, i, 0)
    vec = lambda col: (lambda i, b: (b if mod_row is None else mod_row, 0, col))
    tab = lambda i, b: (i, 0)
    cst = lambda i, b: (0, 0)
    in_specs = [pl.BlockSpec((1, tm, d), tok), pl.BlockSpec((1, 1, d), vec(0)), pl.BlockSpec((1, 1, d), vec(1)),
                pl.BlockSpec((1, d), cst)]
    in_specs += [pl.BlockSpec((tm, LANES), tab)] * len(tabs)
    in_specs += [pl.BlockSpec(w.shape, cst) for w in wts]
    row_out = lambda w: (pl.BlockSpec((1, tm, w), tok), jax.ShapeDtypeStruct((bx, n, w), BF16))
    win_outs = [row_out(LANES),
                (pl.BlockSpec((1, tm // BLOCK, A_KV_HEADS, VT_ROWS, BLOCK), lambda i, b: (b, i, 0, 0, 0)),
                 jax.ShapeDtypeStruct((bx, n // BLOCK, A_KV_HEADS, VT_ROWS, BLOCK), BF16))]
    mla_outs = [row_out(MLA_HEADS * LANES),
                (pl.BlockSpec((1, 1, MLA_HEADS, VT_ROWS, tm), lambda i, b: (b, i, 0, 0, 0)),
                 jax.ShapeDtypeStruct((bx, n // tm, MLA_HEADS, VT_ROWS, tm), BF16))]
    if with_q:
        outs = [row_out(A_HEADS * HEAD_DIM)] + win_outs + [row_out(MLA_HEADS * LANES)] + mla_outs
    else:
        outs = win_outs + mla_outs
    return pl.pallas_call(
        functools.partial(_inproj_kernel, with_q),
        grid=(n // tm, bx),
        in_specs=in_specs,
        out_specs=[o[0] for o in outs],
        out_shape=[o[1] for o in outs],
        compiler_params=pltpu.CompilerParams(dimension_semantics=("arbitrary", "arbitrary"),
                                             vmem_limit_bytes=VMEM_LIMIT),
        name="inproj" if with_q else "inproj_ctx",
    )(x, mod, mod, gpre, *tabs, *wts)


def _gqa_kernel(sink_ref, q_ref, *refs):
    nwin = GQA_QB + 2
    k_refs, kx_ref = refs[:nwin], refs[nwin]
    v_refs, vx_ref = refs[nwin + 1:2 * nwin + 1], refs[2 * nwin + 1]
    o_ref = refs[2 * nwin + 2]
    n = pl.program_id(1)
    nsteps = pl.num_programs(1)
    nctx = kx_ref.shape[1] // BLOCK
    group = A_HEADS // A_KV_HEADS
    key = lax.broadcasted_iota(jnp.int32, (BLOCK, group * BLOCK), 0)
    qry = lax.broadcasted_iota(jnp.int32, (BLOCK, group * BLOCK), 1) & (BLOCK - 1)
    sts = {}

    def scores(qb):
        kcat = jnp.concatenate([k_refs[qb + j][0] for j in range(3)] + [kx_ref[0]], axis=0)
        for hk in range(A_KV_HEADS):
            qg = jnp.concatenate([q_ref[0, qb * BLOCK:(qb + 1) * BLOCK, hh * HEAD_DIM:(hh + 1) * HEAD_DIM]
                                  for hh in range(hk * group, (hk + 1) * group)], axis=0)
            k_hk = kcat[:, hk * HEAD_DIM:(hk + 1) * HEAD_DIM]
            sts[qb, hk] = _dot_nt(k_hk, qg)

    def finish(qb):
        vis_prev = key >= qry
        vis_next = key <= qry
        if qb == 0:
            vis_prev = key >= qry + jnp.where(n > 0, 0, 2 * BLOCK)
        if qb == GQA_QB - 1:
            vis_next = key <= qry - jnp.where(n < nsteps - 1, 0, 2 * BLOCK)
        outs = []
        for hk in range(A_KV_HEADS):
            st = sts[qb, hk]
            pieces = [jnp.where(vis_prev, st[0:BLOCK], NEG_INF), st[BLOCK:2 * BLOCK],
                      jnp.where(vis_next, st[2 * BLOCK:3 * BLOCK], NEG_INF)]
            pieces += [st[(3 + j) * BLOCK:(4 + j) * BLOCK] for j in range(nctx)]
            sink = sink_ref[hk]
            m = jnp.maximum(functools.reduce(jnp.maximum, [jnp.max(p, axis=0, keepdims=True) for p in pieces]),
                            sink)
            ps = [jnp.exp2(p - m).astype(BF16) for p in pieces]
            vts = [v_refs[qb + j][0, 0, hk] for j in range(3)] + [vx_ref[0, j, hk] for j in range(nctx)]
            pv = functools.reduce(jnp.add, [_dot(vt, p) for vt, p in zip(vts, ps)])
            l = pv[HEAD_DIM:HEAD_DIM + 1, :] + jnp.exp2(sink - m)
            o = pv[0:HEAD_DIM, :] * (1.0 / l)
            outs += [o[:, g * BLOCK:(g + 1) * BLOCK] for g in range(group)]
        for pair in range(A_HEADS // 2):
            both = jnp.concatenate([outs[2 * pair], outs[2 * pair + 1]], axis=0)
            o_ref[0, qb * BLOCK:(qb + 1) * BLOCK, pair * LANES:(pair + 1) * LANES] = both.T.astype(BF16)

    for qb in range(GQA_QB):
        scores(qb)
    for qb in range(GQA_QB):
        finish(qb)


def _gqa(sinkv, qa, ka, vat, kac, vatc):
    b, s, _ = qa.shape
    nb = s // BLOCK
    c = kac.shape[1]
    tq = GQA_QB * BLOCK
    cur = lambda bb, n: (bb, n, 0)
    win = lambda j: (lambda bb, n: (bb, jnp.clip(n * GQA_QB + j - 1, 0, nb - 1), 0))
    vwin = lambda j: (lambda bb, n: (bb, jnp.clip(n * GQA_QB + j - 1, 0, nb - 1), 0, 0, 0))
    vblk = (1, 1, A_KV_HEADS, VT_ROWS, BLOCK)
    in_specs = [pl.BlockSpec(sinkv.shape, lambda bb, n: (0, 0, 0)), pl.BlockSpec((1, tq, A_HEADS * HEAD_DIM), cur)]
    in_specs += [pl.BlockSpec((1, BLOCK, LANES), win(j)) for j in range(GQA_QB + 2)]
    in_specs += [pl.BlockSpec((1, c, LANES), lambda bb, n: (bb, 0, 0))]
    in_specs += [pl.BlockSpec(vblk, vwin(j)) for j in range(GQA_QB + 2)]
    in_specs += [pl.BlockSpec((1, c // BLOCK, A_KV_HEADS, VT_ROWS, BLOCK), lambda bb, n: (bb, 0, 0, 0, 0))]
    return pl.pallas_call(
        _gqa_kernel,
        grid=(b, nb // GQA_QB),
        in_specs=in_specs,
        out_specs=pl.BlockSpec((1, tq, 512), cur),
        out_shape=jax.ShapeDtypeStruct((b, s, 512), BF16),
        compiler_params=pltpu.CompilerParams(dimension_semantics=("arbitrary", "arbitrary"),
                                             vmem_limit_bytes=VMEM_LIMIT),
        name="gqa",
    )(sinkv, qa, *([ka] * (GQA_QB + 2)), kac, *([vat] * (GQA_QB + 2)), vatc)


def _mla_kernel(q_ref, k_ref, vt_ref, kc_ref, vct_ref, o_ref, m_ref, acc_ref, sa_ref, sb_ref, cma_ref, cmb_ref):
    nk = k_ref.shape[1] // TK_MLA

    sa_ref, sb_ref = (sa_ref, cma_ref), (sb_ref, cmb_ref)

    def scores_into(bufs, k_blk, heads=range(MLA_HEADS)):
        buf, cm = bufs
        n = k_blk.shape[0]
        for hh in heads:
            hsl = slice(hh * LANES, (hh + 1) * LANES)
            st = _dot_nt(k_blk[:, hsl], q_ref[0, :, hsl])
            buf[hh, 0:n, :] = st
            cm[hh, 0:1, :] = jnp.max(st, axis=0, keepdims=True)

    def consume(bufs, vt_blk, first, heads=range(MLA_HEADS)):
        buf, cm = bufs
        n = vt_blk.shape[2]
        for hh in heads:
            st = buf[hh, 0:n, :]
            cmax = cm[hh, 0:1, :]
            if first:
                m_new = cmax
            else:
                m_old = m_ref[hh, 0:1, :]
                m_new = jnp.maximum(m_old, cmax)
                alpha = jnp.exp2(m_old - m_new)
            p = jnp.exp2(st - m_new).astype(BF16)
            pv = _dot(vt_blk[hh], p)
            acc_ref[hh] = pv if first else alpha * acc_ref[hh] + pv
            m_ref[hh, 0:1, :] = m_new

    def k_chunk(c):
        return k_ref[0, pl.ds(pl.multiple_of(c * TK_MLA, TK_MLA), TK_MLA), :]

    def overlapped(buf_w, k_blk, buf_r, vt_blk, first):
        for g in range(0, MLA_HEADS, MLA_GROUP):
            hs = range(g, g + MLA_GROUP)
            scores_into(buf_w, k_blk, hs)
            consume(buf_r, vt_blk, first, hs)

    scores_into(sa_ref, k_chunk(0))
    overlapped(sb_ref, k_chunk(1), sa_ref, vt_ref[0, 0], True)

    def body(i, _):
        c = 2 * i + 1
        overlapped(sa_ref, k_chunk(c + 1), sb_ref, vt_ref[0, c], False)
        overlapped(sb_ref, k_chunk(c + 2), sa_ref, vt_ref[0, c + 1], False)
        return 0

    lax.fori_loop(0, (nk - 2) // 2, body, 0)
    overlapped(sa_ref, kc_ref[0], sb_ref, vt_ref[0, nk - 1], False)
    consume(sa_ref, vct_ref[0, 0], False)

    outs = [acc_ref[hh, 0:MLA_V, :] * (1.0 / acc_ref[hh, MLA_V:MLA_V + 1, :]) for hh in range(MLA_HEADS)]
    o_ref[0] = jnp.concatenate(outs, axis=0).T.astype(BF16)


def _mla(q, k, vt, kc, vct):
    b, s, _ = q.shape
    c = kc.shape[1]
    nk = s // TK_MLA
    assert vt.shape == (b, nk, MLA_HEADS, VT_ROWS, TK_MLA) and nk % 2 == 0
    return pl.pallas_call(
        _mla_kernel,
        grid=(b, s // TQ_MLA),
        in_specs=[pl.BlockSpec((1, TQ_MLA, 1024), lambda bb, i: (bb, i, 0)),
                  pl.BlockSpec((1, s, 1024), lambda bb, i: (bb, 0, 0)),
                  pl.BlockSpec((1, nk, MLA_HEADS, VT_ROWS, TK_MLA), lambda bb, i: (bb, 0, 0, 0, 0)),
                  pl.BlockSpec((1, c, 1024), lambda bb, i: (bb, 0, 0)),
                  pl.BlockSpec((1, 1, MLA_HEADS, VT_ROWS, c), lambda bb, i: (bb, 0, 0, 0, 0))],
        out_specs=pl.BlockSpec((1, TQ_MLA, 512), lambda bb, i: (bb, i, 0)),
        out_shape=jax.ShapeDtypeStruct((b, s, 512), BF16),
        scratch_shapes=[pltpu.VMEM((MLA_HEADS, 8, TQ_MLA), F32),
                        pltpu.VMEM((MLA_HEADS, VT_ROWS, TQ_MLA), F32),
                        pltpu.VMEM((MLA_HEADS, TK_MLA, TQ_MLA), F32), pltpu.VMEM((MLA_HEADS, TK_MLA, TQ_MLA), F32),
                        pltpu.VMEM((MLA_HEADS, 8, TQ_MLA), F32), pltpu.VMEM((MLA_HEADS, 8, TQ_MLA), F32)],
        compiler_params=pltpu.CompilerParams(dimension_semantics=("arbitrary", "arbitrary"),
                                             vmem_limit_bytes=VMEM_LIMIT),
        name="mla",
    )(q, k, vt, kc, vct)


def _outproj_kernel(oa_ref, ob_ref, x_ref, woa_ref, wob_ref, gpost_ref, gate_ref, gffn_ref, shift_ref, scale_ref,
                    wr_pair_ref, wr_hi_ref, br_ref, tri_ref, upper_ref,
                    x1_ref, h2_ref, ri_ref, rw_ref, cnt_ref):
    tm = x_ref.shape[1]
    nsub = 2
    sub = tm // nsub
    rows = [slice(j * sub, (j + 1) * sub) for j in range(nsub)]
    ys = [_dot(oa_ref[0, r, :], woa_ref[...]) + _dot(ob_ref[0, r, :], wob_ref[...]) for r in rows]
    sub_lane_f = lax.broadcasted_iota(jnp.int32, (sub, LANES), 1).astype(F32)
    idx_parts, w_parts = [], []
    for r, y in zip(rows, ys):
        x1 = x_ref[0, r, :] + gate_ref[0] * _rms(y, gpost_ref[...])
        x1_ref[0, r, :] = x1
        h2 = _rms(x1, gffn_ref[...]) * (1.0 + scale_ref[0]) + shift_ref[0]
        h2_ref[0, r, :] = h2.astype(BF16)
        h_hi, h_lo = _split_bf16(h2)
        both = _dot(h_hi, wr_pair_ref[...])
        cur = both[:, :LANES] + both[:, LANES:] + _dot(h_lo, wr_hi_ref[...]) + br_ref[...]
        tops, idxs = [], []
        for _ in range(TOP_K):
            mk = jnp.max(cur, axis=-1, keepdims=True)
            ik = jnp.min(jnp.where(cur == mk, sub_lane_f, float(LANES)), axis=-1, keepdims=True)
            tops.append(mk)
            idxs.append(ik)
            cur = jnp.where(sub_lane_f == ik, -jnp.inf, cur)
        es = [jnp.exp(t - tops[0]) for t in tops]
        inv = 1.0 / functools.reduce(jnp.add, es)
        idx_parts.append(idxs)
        w_parts.append([e * inv for e in es])

    hot_parts = [[(sub_lane_f == ik) for ik in idxs] for idxs in idx_parts]
    onehot = jnp.concatenate([functools.reduce(jnp.add, [o.astype(F32) for o in hots]) for hots in hot_parts],
                             axis=0)
    cnt = jnp.sum(onehot, axis=0, keepdims=True)
    prefix = _dot(tri_ref[...], onehot.astype(BF16))
    chunks = jnp.floor((cnt + (RUN_ALIGN - 1.0)) * (1.0 / RUN_ALIGN))
    run_start = _dot(jnp.broadcast_to(chunks, (8, LANES)).astype(BF16), upper_ref[...])[0:1, :] * RUN_ALIGN
    base = prefix + run_start
    sub_lane = lax.broadcasted_iota(jnp.int32, (sub, LANES), 1)
    for r, hots, idxs, wts in zip(rows, hot_parts, idx_parts, w_parts):
        ri = jnp.zeros((sub, LANES), jnp.int32)
        rw = jnp.zeros((sub, LANES), F32)
        for k in range(TOP_K):
            pos = jnp.sum(jnp.where(hots[k], base[r, :], 0.0), axis=-1, keepdims=True).astype(jnp.int32)
            ri = jnp.where(sub_lane == k, idxs[k].astype(jnp.int32), ri)
            ri = jnp.where(sub_lane == TOP_K + k, pos, ri)
            rw = jnp.where(sub_lane == k, wts[k], rw)
        ri_ref[0, r, :] = ri
        rw_ref[0, r, :] = rw
    cnt_ref[0] = cnt


def _outproj(oa, ob, x, woa, wob, gpost, mod, gffn, wrh, wrl, br, tri, upper):
    b, s, d = x.shape
    tm = TM_OUT
    nt = s // tm
    tok = lambda bb, i: (bb, i, 0)
    vec = lambda col: (lambda bb, i: (bb, 0, col))
    cst = lambda bb, i: (0, 0)
    return pl.pallas_call(
        _outproj_kernel,
        grid=(b, nt),
        in_specs=[pl.BlockSpec((1, tm, 512), tok), pl.BlockSpec((1, tm, 512), tok), pl.BlockSpec((1, tm, d), tok),
                  pl.BlockSpec((512, d), cst), pl.BlockSpec((512, d), cst), pl.BlockSpec((1, d), cst),
                  pl.BlockSpec((1, 1, d), vec(2)), pl.BlockSpec((1, d), cst),
                  pl.BlockSpec((1, 1, d), vec(3)), pl.BlockSpec((1, 1, d), vec(4)),
                  pl.BlockSpec((d, 2 * LANES), cst), pl.BlockSpec((d, LANES), cst), pl.BlockSpec((1, LANES), cst),
                  pl.BlockSpec((tm, tm), cst), pl.BlockSpec((LANES, LANES), cst)],
        out_specs=[pl.BlockSpec((1, tm, d), tok), pl.BlockSpec((1, tm, d), tok),
                   pl.BlockSpec((1, tm, LANES), tok), pl.BlockSpec((1, tm, LANES), tok),
                   pl.BlockSpec((1, 1, LANES), lambda bb, i: (bb * nt + i, 0, 0))],
        out_shape=[jax.ShapeDtypeStruct((b, s, d), F32), jax.ShapeDtypeStruct((b, s, d), BF16),
                   jax.ShapeDtypeStruct((b, s, LANES), jnp.int32), jax.ShapeDtypeStruct((b, s, LANES), F32),
                   jax.ShapeDtypeStruct((b * nt, 1, LANES), F32)],
        compiler_params=pltpu.CompilerParams(dimension_semantics=("arbitrary", "arbitrary"),
                                             vmem_limit_bytes=VMEM_LIMIT),
        name="outproj",
    )(oa, ob, x, woa, wob, gpost, mod, gffn, mod, mod, wrh, wrl, br, tri, upper)


def _start_pieces(g, o, rows, make_copy):
    n_big = lax.shift_right_logical(rows, BIG_COPY.bit_length() - 1)
    n_small = lax.shift_right_logical(rows, RUN_ALIGN.bit_length() - 1) & (BIG_COPY // RUN_ALIGN - 1)

    def big(c, _):
        off = c * BIG_COPY
        make_copy(pl.multiple_of(g + off, RUN_ALIGN), pl.multiple_of(o + off, RUN_ALIGN), BIG_COPY).start()
        return 0

    def small(c, _):
        off = n_big * BIG_COPY + c * RUN_ALIGN
        make_copy(pl.multiple_of(g + off, RUN_ALIGN), pl.multiple_of(o + off, RUN_ALIGN), RUN_ALIGN).start()
        return 0

    lax.fori_loop(0, n_big, big, 0)
    lax.fori_loop(0, n_small, small, 0)
    return n_big, n_small


def _unrolled_loop(n, step):
    shift = LOOP_UNROLL.bit_length() - 1
    trips = lax.shift_right_logical(n, shift)

    def many(t, _):
        for u in range(LOOP_UNROLL):
            step(t * LOOP_UNROLL + u)
        return 0

    def one(i, _):
        step(i)
        return 0

    lax.fori_loop(0, trips, many, 0)
    lax.fori_loop(trips * LOOP_UNROLL, n, one, 0)


def _wait_pieces(n_big, n_small, make_copy):
    _unrolled_loop(n_big, lambda i: make_copy(0, 0, BIG_COPY).wait())
    _unrolled_loop(n_small, lambda i: make_copy(0, 0, RUN_ALIGN).wait())


def _start_listed_copies(plan, tile, make_copy):
    big_g, big_o, n_big, small_g, small_o, n_small = plan

    def start_list(g_ref, o_ref, n, width, rows):
        def start(p):
            j = tile * width + p
            make_copy(pl.multiple_of(g_ref[j], RUN_ALIGN), pl.multiple_of(o_ref[j], RUN_ALIGN), rows).start()

        _unrolled_loop(n, start)

    start_list(big_g, big_o, n_big[tile], MAX_BIG, BIG_COPY)
    start_list(small_g, small_o, n_small[tile], MAX_SMALL, RUN_ALIGN)


def _dispatch_kernel(bg_ref, bo_ref, nb_ref, sg_ref, so_ref, ns_ref, zs_ref, zr_ref, ri_ref, h_ref, xs_ref,
                     sorted_ref, zero_ref, sem):
    plan = (bg_ref, bo_ref, nb_ref, sg_ref, so_ref, ns_ref)
    tile = pl.program_id(0)
    last = pl.num_programs(0) - 1
    slot = tile % 2
    tm = h_ref.shape[0]
    post = ri_ref[...].astype(F32).T
    h = h_ref[...]
    rb_rows = 256
    pos_blk = [jnp.floor(post[TOP_K + k:TOP_K + k + 1, :] * (1.0 / rb_rows)) for k in range(TOP_K)]
    pos_off = [post[TOP_K + k:TOP_K + k + 1, :] - rb_rows * pos_blk[k] for k in range(TOP_K)]
    row = lax.broadcasted_iota(jnp.int32, (rb_rows, tm), 0).astype(F32).astype(BF16)
    one, zero = jnp.ones((), BF16), jnp.zeros((), BF16)

    for rb in range(SORTED_ROWS // rb_rows):
        perm = functools.reduce(jnp.add, [
            jnp.where(row == jnp.where(pos_blk[k] == float(rb), pos_off[k], -1.0).astype(BF16), one, zero)
            for k in range(TOP_K)])
        xr = _dot(perm, h)
        lo = lax.bitcast_convert_type(xr[:, :512], jnp.uint32)
        hi = lax.bitcast_convert_type(xr[:, 512:], jnp.uint32)
        sorted_ref[slot, rb * rb_rows:(rb + 1) * rb_rows, :] = (lo >> 16) | (hi & jnp.uint32(0xFFFF0000))

    def run_copy(sl):
        def make_copy(g, o, rows):
            return pltpu.make_async_copy(sorted_ref.at[sl, pl.ds(o, rows)], xs_ref.at[pl.ds(g, rows)], sem.at[sl])
        return make_copy

    _start_listed_copies(plan, tile, run_copy(slot))

    @pl.when(tile > 0)
    def _():
        _wait_pieces(nb_ref[tile - 1], ns_ref[tile - 1], run_copy(1 - slot))

    def zero_copy(g, o, rows):
        return pltpu.make_async_copy(zero_ref.at[pl.ds(0, rows)], xs_ref.at[pl.ds(g, rows)], sem.at[2])

    @pl.when(tile == 0)
    def _():
        zero_ref[...] = jnp.zeros_like(zero_ref)

        def start_zero(e, carry):
            _start_pieces(zs_ref[e], 0, zr_ref[e], zero_copy)
            return carry

        lax.fori_loop(0, zs_ref.shape[0], start_zero, 0)

    @pl.when(tile == last)
    def _():
        _wait_pieces(nb_ref[tile], ns_ref[tile], run_copy(slot))

        def count(e, totals):
            rows = zr_ref[e]
            n_big = lax.shift_right_logical(rows, BIG_COPY.bit_length() - 1)
            n_small = lax.shift_right_logical(rows, RUN_ALIGN.bit_length() - 1) & (BIG_COPY // RUN_ALIGN - 1)
            return totals[0] + n_big, totals[1] + n_small

        zb, zs = lax.fori_loop(0, zs_ref.shape[0], count, (0, 0))
        _wait_pieces(zb, zs, zero_copy)


def _dispatch(plan, zstart, zrows, ri, h2, cap):
    t, d = h2.shape
    tm = TM_OUT
    return pl.pallas_call(
        _dispatch_kernel,
        grid_spec=pltpu.PrefetchScalarGridSpec(
            num_scalar_prefetch=8,
            grid=(t // tm,),
            in_specs=[pl.BlockSpec((tm, LANES), lambda i, *_: (i, 0)),
                      pl.BlockSpec((tm, d), lambda i, *_: (i, 0))],
            out_specs=pl.BlockSpec(memory_space=pl.ANY),
            scratch_shapes=[pltpu.VMEM((2, SORTED_ROWS, d // 2), jnp.uint32),
                            pltpu.VMEM((BIG_COPY, d // 2), jnp.uint32),
                            pltpu.SemaphoreType.DMA((3,))]),
        out_shape=jax.ShapeDtypeStruct((cap, d // 2), jnp.uint32),
        compiler_params=pltpu.CompilerParams(dimension_semantics=("arbitrary",),
                                             vmem_limit_bytes=VMEM_LIMIT),
        name="dispatch",
    )(*plan, zstart, zrows, ri, h2)


def _pack_bf16_pairs(x):
    n = x.shape[1] // 2
    bits = lax.bitcast_convert_type(x.astype(BF16).astype(F32), jnp.uint32)
    return (bits[:, :n] >> 16) | (bits[:, n:] & jnp.uint32(0xFFFF0000))


def _unpack_bf16_pairs(w):
    lo = lax.bitcast_convert_type(w << 16, F32).astype(BF16)
    hi = lax.bitcast_convert_type(w & jnp.uint32(0xFFFF0000), F32).astype(BF16)
    return lo, hi


def _experts_kernel(be_ref, nu_ref, slot_ref, nxt_ref, nv_ref, xs_ref, wgu_hbm, bgu_ref, wd_hbm, bd_ref, ys_ref,
                    wgu_f, wd_f, wgu_b, wd_b, sem):
    del nu_ref
    i = pl.program_id(0)
    e = be_ref[i]
    slot = slot_ref[i]
    valid = nv_ref[i]
    used = valid > 0
    run_start = used & ((i == 0) | (be_ref[jnp.maximum(i - 1, 0)] != e))

    def weight_copies(ex, sl):
        return (pltpu.make_async_copy(wgu_hbm.at[ex], wgu_f.at[sl], sem.at[0, sl]),
                pltpu.make_async_copy(wd_hbm.at[ex], wd_f.at[sl], sem.at[1, sl]))

    @pl.when(i == 0)
    def _():
        for cp in weight_copies(e, slot):
            cp.start()

    @pl.when(run_start)
    def _():
        for cp in weight_copies(e, slot):
            cp.wait()
        nxt = nxt_ref[i]

        @pl.when(nxt >= 0)
        def _():
            for cp in weight_copies(nxt, 1 - slot):
                cp.start()

        wgu_b[...] = wgu_f[slot].astype(BF16)
        wd_b[...] = wd_f[slot].astype(BF16)

    def compute(rows):
        gus = []
        for r in rows:
            x_lo, x_hi = _unpack_bf16_pairs(xs_ref[r, :])
            gus.append(_dot(x_lo, wgu_b[0:512, :]) + _dot(x_hi, wgu_b[512:, :]) + bgu_ref[0])
        for r, gu in zip(rows, gus):
            gate = jnp.minimum(gu[:, :D_FF], SWIGLU_LIMIT)
            lin = jnp.clip(gu[:, D_FF:], -SWIGLU_LIMIT, SWIGLU_LIMIT)
            act = (lin + 1.0) * (gate * jax.nn.sigmoid(SWIGLU_ALPHA * gate))
            ys_ref[r, :] = _pack_bf16_pairs(_dot(act.astype(BF16), wd_b[...]) + bd_ref[0])

    nsteps = MOE_BLOCK // EXPERT_STEP
    for n in range(nsteps + 1):
        lo, hi = (n - 1) * EXPERT_STEP, n * EXPERT_STEP
        cond = (valid == 0) if n == 0 else ((valid > lo) & (valid <= hi))
        cuts = list(range(0, hi, EXPERT_SUB)) + [hi]

        @pl.when(cond)
        def _(hi=hi, cuts=cuts):
            if hi:
                compute([slice(a, b) for a, b in zip(cuts[:-1], cuts[1:])])
            if hi < MOE_BLOCK:
                ys_ref[hi:, :] = jnp.zeros((MOE_BLOCK - hi, D_MODEL // 2), jnp.uint32)


def _experts(block_e, n_used, slot, nxt, nvalid, xs, wgu, bgu, wd, bd):
    cap = xs.shape[0]
    nblk = cap // MOE_BLOCK
    return pl.pallas_call(
        _experts_kernel,
        grid_spec=pltpu.PrefetchScalarGridSpec(
            num_scalar_prefetch=5,
            grid=(nblk,),
            in_specs=[pl.BlockSpec((MOE_BLOCK, 512), lambda i, be, nu, *_: (jnp.minimum(i, nu[0] - 1), 0)),
                      pl.BlockSpec(memory_space=pl.ANY),
                      pl.BlockSpec((1, 1, 2 * D_FF), lambda i, be, *_: (be[i], 0, 0)),
                      pl.BlockSpec(memory_space=pl.ANY),
                      pl.BlockSpec((1, 1, D_MODEL), lambda i, be, *_: (be[i], 0, 0))],
            out_specs=pl.BlockSpec((MOE_BLOCK, D_MODEL // 2), lambda i, *_: (i, 0)),
            scratch_shapes=[pltpu.VMEM((2, D_MODEL, 2 * D_FF), F32), pltpu.VMEM((2, D_FF, D_MODEL), F32),
                            pltpu.VMEM((D_MODEL, 2 * D_FF), BF16), pltpu.VMEM((D_FF, D_MODEL), BF16),
                            pltpu.SemaphoreType.DMA((2, 2))]),
        out_shape=jax.ShapeDtypeStruct((cap, D_MODEL // 2), jnp.uint32),
        compiler_params=pltpu.CompilerParams(dimension_semantics=("arbitrary",),
                                             vmem_limit_bytes=VMEM_LIMIT),
        name="experts",
    )(block_e, n_used, slot, nxt, nvalid, xs, wgu, bgu, wd, bd)


def _combine_kernel(bg_ref, bo_ref, nb_ref, sg_ref, so_ref, ns_ref, ys_ref, ri_ref, rw_ref, x1_ref, gate_ref, g_ref,
                    o_ref, ybuf, sem):
    plan = (bg_ref, bo_ref, nb_ref, sg_ref, so_ref, ns_ref)
    nt = pl.num_programs(1)
    tile = pl.program_id(0) * nt + pl.program_id(1)
    ntiles = pl.num_programs(0) * nt
    slot = tile % 2
    tm = x1_ref.shape[1]

    def run_copy(sl):
        def make_copy(g, o, rows):
            return pltpu.make_async_copy(ys_ref.at[pl.ds(g, rows)], ybuf.at[sl, pl.ds(o, rows)], sem.at[sl])
        return make_copy

    @pl.when(tile == 0)
    def _():
        ybuf[...] = jnp.zeros_like(ybuf)
        _start_listed_copies(plan, tile, run_copy(slot))

    @pl.when(tile + 1 < ntiles)
    def _():
        _start_listed_copies(plan, tile + 1, run_copy(1 - slot))

    _wait_pieces(nb_ref[tile], ns_ref[tile], run_copy(slot))

    posf = ri_ref[0].astype(F32)
    rw = rw_ref[0]
    cb_cols = 256
    pos = [posf[:, TOP_K + k:TOP_K + k + 1] for k in range(TOP_K)]
    pos_blk = [jnp.floor(p * (1.0 / cb_cols)) for p in pos]
    pos_off = [p - cb_cols * b for p, b in zip(pos, pos_blk)]
    wk = [rw[:, k:k + 1].astype(BF16) for k in range(TOP_K)]
    col = lax.broadcasted_iota(jnp.int32, (tm, cb_cols), 1).astype(F32).astype(BF16)
    zero = jnp.zeros((), BF16)
    f_lo = jnp.zeros((tm, D_MODEL // 2), F32)
    f_hi = jnp.zeros((tm, D_MODEL // 2), F32)
    for cb in range(SORTED_ROWS // cb_cols):
        wp = functools.reduce(jnp.add, [
            jnp.where(col == jnp.where(pos_blk[k] == float(cb), pos_off[k], -1.0).astype(BF16), wk[k], zero)
            for k in range(TOP_K)])
        y_lo, y_hi = _unpack_bf16_pairs(ybuf[slot, cb * cb_cols:(cb + 1) * cb_cols, :])
        f_lo = f_lo + _dot(wp, y_lo)
        f_hi = f_hi + _dot(wp, y_hi)
    f = jnp.concatenate([f_lo, f_hi], axis=1)
    o_ref[0] = x1_ref[0] + gate_ref[0] * _rms(f, g_ref[...])


def _combine(plan, ys, ri, rw, x1, mod, g):
    b, s, d = x1.shape
    tm = TM_OUT
    tok = lambda bb, i, *_: (bb, i, 0)
    return pl.pallas_call(
        _combine_kernel,
        grid_spec=pltpu.PrefetchScalarGridSpec(
            num_scalar_prefetch=6,
            grid=(b, s // tm),
            in_specs=[pl.BlockSpec(memory_space=pl.ANY),
                      pl.BlockSpec((1, tm, LANES), tok), pl.BlockSpec((1, tm, LANES), tok),
                      pl.BlockSpec((1, tm, d), tok),
                      pl.BlockSpec((1, 1, d), lambda bb, i, *_: (bb, 0, 5)),
                      pl.BlockSpec((1, d), lambda bb, i, *_: (0, 0))],
            out_specs=pl.BlockSpec((1, tm, d), tok),
            scratch_shapes=[pltpu.VMEM((2, SORTED_ROWS, d // 2), jnp.uint32), pltpu.SemaphoreType.DMA((2,))]),
        out_shape=jax.ShapeDtypeStruct((b, s, d), F32),
        compiler_params=pltpu.CompilerParams(dimension_semantics=("arbitrary", "arbitrary"),
                                             vmem_limit_bytes=VMEM_LIMIT),
        name="combine",
    )(*plan, ys, ri, rw, x1, mod, g)


def _rope_angles(rows, rot_dim):
    row = np.repeat(np.arange(rows, dtype=np.float64), GRID_W)
    col = np.tile(np.arange(GRID_W, dtype=np.float64), rows)
    quarter = rot_dim // 4
    inv_freq = ROPE_BASE ** (-np.arange(quarter, dtype=np.float64) / quarter)
    ang = np.concatenate([row[:, None] * inv_freq, col[:, None] * inv_freq], axis=-1)
    return np.cos(ang), np.sin(ang)


def _mla_lane_of():
    j = np.arange(MLA_NOPE + MLA_ROPE)
    r = j - MLA_NOPE
    half = MLA_ROPE // 2
    return np.where(j < 48, 16 + j, np.where(j < MLA_NOPE, 32 + j, np.where(r < half, r, 48 + r)))


def _to_mla_lanes(w, dims):
    lane_of = _mla_lane_of()[dims]
    src = np.full((LANES,), -1)
    src[lane_of] = np.arange(len(dims))
    picked = jnp.take(w, jnp.asarray(np.maximum(src, 0)), axis=-1)
    return jnp.where(jnp.asarray(src >= 0), picked, 0.0)


def _rope_tables(s):
    ca, sa = _rope_angles(s // GRID_W, HEAD_DIM)
    cb, sb = _rope_angles(s // GRID_W, MLA_ROPE)
    cos_a = np.tile(np.concatenate([ca, ca], 1), (1, 2))
    sin_a = np.tile(np.concatenate([-sa, sa], 1), (1, 2))
    cos_q = np.concatenate([cb, np.ones((s, 48)), cb, np.ones((s, 48))], 1)
    sin_q = np.concatenate([-sb, np.zeros((s, 48)), sb, np.zeros((s, 48))], 1)
    return tuple(jnp.asarray(t, F32) for t in (cos_a, sin_a, cos_q, sin_q))


def _piece_lists(slot_start, tile_start, counts, first_off, piece, width):
    ne = counts.shape[1]
    cum = jnp.cumsum(counts, axis=1)
    p = jnp.arange(width, dtype=jnp.int32)
    owner = jnp.minimum(jnp.sum((cum[:, None, :] <= p[None, :, None]).astype(jnp.int32), axis=2), ne - 1)
    sel = owner[:, :, None] == jnp.arange(ne, dtype=jnp.int32)[None, None, :]
    pick = lambda a: jnp.sum(jnp.where(sel, a[:, None, :], 0), axis=2)
    off = pick(first_off) + (p[None, :] - pick(cum - counts)) * piece
    as_list = lambda a: a.reshape(-1).astype(jnp.int32)
    return as_list(pick(slot_start) + off), as_list(pick(tile_start) + off), cum[:, -1].astype(jnp.int32)


def _identity_tables(n):
    one, zero = jnp.ones((n, LANES), F32), jnp.zeros((n, LANES), F32)
    return one, zero, one, zero


def _inproj_weights(w_in, g_q_a, w_uq, g_kv_a, w_ukv):
    d = w_in.shape[0]
    wq, wk, wv = w_in[:, :512], w_in[:, 512:640], w_in[:, 640:768]
    wcq, wckv, wkr = w_in[:, 768:1152], w_in[:, 1152:1408], w_in[:, 1408:1440]
    nope, rope = np.arange(MLA_NOPE), MLA_NOPE + np.arange(MLA_ROPE)
    wkr_p = _to_mla_lanes(wkr, rope)
    wlat = jnp.concatenate([wcq, wckv, wkr_p], 1).astype(BF16)
    wwin = jnp.concatenate([wq, wk, wv], 1).astype(BF16)
    wuq = _to_mla_lanes(w_uq.reshape(MLA_Q_LORA, MLA_HEADS, MLA_NOPE + MLA_ROPE), np.arange(MLA_NOPE + MLA_ROPE))
    wuq = wuq.reshape(MLA_Q_LORA, MLA_HEADS * LANES).astype(BF16)
    wukv = w_ukv.reshape(MLA_KV_LORA, MLA_HEADS, MLA_NOPE + MLA_V)
    wkk = _to_mla_lanes(wukv[:, :, :MLA_NOPE], nope).reshape(MLA_KV_LORA, MLA_HEADS * LANES)
    wvv = wukv[:, :, MLA_NOPE:].reshape(MLA_KV_LORA, MLA_HEADS * MLA_V)
    return (wlat, wwin, g_q_a.reshape(1, -1), wuq, g_kv_a.reshape(1, -1), wkk.astype(BF16), wvv.T.astype(BF16))


def kernel(x, c, ctx, c_ctx, w_ada, b_ada, g_mix_pre, g_mix_post, w_in, sink, g_q_a, w_uq, g_kv_a, w_ukv, w_o,
           g_ffn_pre, g_ffn_post, w_router, b_router, w_gate_up, b_gate_up, w_down, b_down):
    b, s, d = x.shape
    cl = ctx.shape[1]
    t = b * s

    cc = jnp.concatenate([c, c_ctx[None, :], jnp.zeros((8 - b - 1, d), F32)], axis=0)
    mod = _adaln(cc, w_ada[0], b_ada[0].reshape(1, -1)).reshape(8, 1, 6 * d)

    wts = _inproj_weights(w_in[0], g_q_a[0], w_uq[0], g_kv_a[0], w_ukv[0])
    gpre = g_mix_pre[0].reshape(1, d)
    qa, ka, vat, q, k, vt = _inproj(x, mod, None, gpre, _rope_tables(s), wts, TM_IN, True)
    kac, vatc, kc, vct = _inproj(ctx, mod, b, gpre, _identity_tables(cl), wts, cl, False)

    sinkv = jnp.repeat(sink[0] * LOG2_E, BLOCK).reshape(A_KV_HEADS, 1, -1)
    out_a = _gqa(sinkv, qa, ka, vat, kac, vatc)
    out_b = _mla(q, k, vt, kc, vct)

    wo = w_o[0].astype(BF16)
    wr = jnp.pad(w_router[0], ((0, 0), (0, LANES - N_EXPERTS)))
    wr_hi, wr_lo = _split_bf16(wr)
    br = jnp.concatenate([b_router[0], jnp.full((LANES - N_EXPERTS,), NEG_INF, F32)]).reshape(1, LANES)
    tri = jnp.asarray(np.tri(TM_OUT, k=-1), BF16)
    upper = jnp.asarray(np.tri(LANES, k=-1).T, BF16)
    x1, h2, ri, rw, cnt = _outproj(out_a, out_b, x, wo[:512], wo[512:], g_mix_post[0].reshape(1, d), mod,
                                   g_ffn_pre[0].reshape(1, d),
                                   jnp.concatenate([wr_hi, wr_lo], axis=1), wr_hi, br, tri, upper)

    cnt = cnt[:, 0, :N_EXPERTS].astype(jnp.int32)
    nt = cnt.shape[0]
    rows = (cnt + RUN_ALIGN - 1) // RUN_ALIGN * RUN_ALIGN
    tot = jnp.sum(rows, axis=0)
    carry = jnp.cumsum(rows, axis=0) - rows
    padded = (tot + MOE_BLOCK - 1) // MOE_BLOCK * MOE_BLOCK
    pad_end = jnp.cumsum(padded)
    pad_start = pad_end - padded
    nblk = -(-(t * TOP_K + nt * N_EXPERTS * (RUN_ALIGN - 1)) // MOE_BLOCK) + N_EXPERTS
    cap = nblk * MOE_BLOCK
    blk_row = jnp.arange(nblk, dtype=jnp.int32) * MOE_BLOCK
    block_e = jnp.minimum(jnp.sum((pad_end[None, :] <= blk_row[:, None]).astype(jnp.int32), axis=1), N_EXPERTS - 1)
    n_used = (pad_end[-1] // MOE_BLOCK).astype(jnp.int32).reshape(1)
    toff = jnp.cumsum(rows, axis=1) - rows
    gstart = pad_start[None, :] + carry
    n_big = rows // BIG_COPY
    n_small = (rows - n_big * BIG_COPY) // RUN_ALIGN
    big_g, big_o, tot_big = _piece_lists(gstart, toff, n_big, jnp.zeros_like(rows), BIG_COPY, MAX_BIG)
    small_g, small_o, tot_small = _piece_lists(gstart, toff, n_small, n_big * BIG_COPY, RUN_ALIGN, MAX_SMALL)
    plan = (big_g, big_o, tot_big, small_g, small_o, tot_small)
    zstart = jnp.concatenate([pad_start + tot, pad_end[-1:]]).astype(jnp.int32)
    zrows = jnp.concatenate([padded - tot, cap - pad_end[-1:]]).astype(jnp.int32)
    block_e = block_e.astype(jnp.int32)
    run_idx = jnp.cumsum(jnp.concatenate([jnp.zeros((1,), jnp.int32),
                                          (block_e[1:] != block_e[:-1]).astype(jnp.int32)]))
    eid = jnp.arange(N_EXPERTS, dtype=jnp.int32)
    later_used = (tot[None, :] > 0) & (eid[None, :] > eid[:, None])
    next_e = jnp.min(jnp.where(later_used, eid[None, :], N_EXPERTS), axis=1)
    next_e = jnp.where(next_e < N_EXPERTS, next_e, -1).astype(jnp.int32)
    nxt = jnp.sum(jnp.where(block_e[:, None] == eid[None, :], next_e[None, :], 0), axis=1).astype(jnp.int32)

    xs = _dispatch(plan, zstart, zrows, ri.reshape(t, LANES), h2.reshape(t, d), cap)
    run_end = jnp.sum(jnp.where(block_e[:, None] == eid[None, :], (pad_start + tot)[None, :], 0), axis=1)
    nvalid = jnp.where(blk_row < pad_end[-1], jnp.clip(run_end - blk_row, 0, MOE_BLOCK), 0).astype(jnp.int32)
    ys = _experts(block_e, n_used, (run_idx % 2).astype(jnp.int32), nxt, nvalid, xs, w_gate_up[0],
                  b_gate_up[0].reshape(N_EXPERTS, 1, -1), w_down[0], b_down[0].reshape(N_EXPERTS, 1, -1))
    return _combine(plan, ys, ri, rw, x1, mod, g_ffn_post[0].reshape(1, d))
```

```python
import functools

import jax
import jax.numpy as jnp
import numpy as np
from jax import lax
from jax.experimental import pallas as pl
from jax.experimental.pallas import tpu as pltpu

F32 = jnp.float32
BF16 = jnp.bfloat16

D_MODEL = 1024
GRID_W = 64
HEAD_DIM = 64
A_HEADS = 8
A_KV_HEADS = 2
BLOCK = 128
A_SCALE = HEAD_DIM ** -0.5
MLA_HEADS = 8
MLA_NOPE = 64
MLA_ROPE = 32
MLA_V = 64
MLA_Q_LORA = 384
MLA_KV_LORA = 256
MLA_SCALE = (MLA_NOPE + MLA_ROPE) ** -0.5
LOG2_E = 1.4426950408889634
N_EXPERTS = 32
TOP_K = 4
D_FF = 1024
SWIGLU_LIMIT = 7.0
SWIGLU_ALPHA = 1.702
ROPE_BASE = 10000.0
EPS = 1e-6
NEG_INF = -1e30

LANES = 128
VMEM_LIMIT = 56 * 1024 * 1024

TM_IN = 512
GQA_QB = 8
TQ_MLA = 256
TK_MLA = 512
MLA_GROUP = 2
VT_ROWS = 80
TM_OUT = 512
MOE_BLOCK = 512
EXPERT_SUB = 256
EXPERT_STEP = 128
RUN_ALIGN = 8
BIG_COPY = 32
SORTED_ROWS = TM_OUT * TOP_K + N_EXPERTS * RUN_ALIGN
LOOP_UNROLL = 8
MAX_BIG = SORTED_ROWS // BIG_COPY
MAX_SMALL = N_EXPERTS * (BIG_COPY // RUN_ALIGN - 1)


def _dot(a, b):
    return jnp.dot(a, b, preferred_element_type=F32)


def _dot_nt(a, b):
    return lax.dot_general(a, b, (((1,), (1,)), ((), ())), preferred_element_type=F32)


def _split_bf16(x):
    hi = x.astype(BF16)
    lo = (x - hi.astype(F32)).astype(BF16)
    return hi, lo


def _rms(x, g):
    return x * lax.rsqrt(jnp.mean(x * x, axis=-1, keepdims=True) + EPS) * g


def _rope(x, cos, sin_signed, first_half, half):
    n = x.shape[-1]
    partner = jnp.where(first_half, pltpu.roll(x, n - half, 1), pltpu.roll(x, half, 1))
    return x * cos + partner * sin_signed


def _adaln_kernel(c_ref, w_ref, b_ref, o_ref):
    c = c_ref[...]
    s = c * jax.nn.sigmoid(c)
    s_hi, s_lo = _split_bf16(s)
    w_hi, w_lo = _split_bf16(w_ref[...])
    o_ref[...] = _dot(s_hi, w_hi) + _dot(s_hi, w_lo) + _dot(s_lo, w_hi) + b_ref[...]


def _adaln(cc, w, b):
    n = w.shape[1]
    tn = 1024
    return pl.pallas_call(
        _adaln_kernel,
        grid=(n // tn,),
        in_specs=[pl.BlockSpec((8, D_MODEL), lambda j: (0, 0)),
                  pl.BlockSpec((D_MODEL, tn), lambda j: (0, j)),
                  pl.BlockSpec((1, tn), lambda j: (0, j))],
        out_specs=pl.BlockSpec((8, tn), lambda j: (0, j)),
        out_shape=jax.ShapeDtypeStruct((8, n), F32),
        compiler_params=pltpu.CompilerParams(dimension_semantics=("arbitrary",),
                                             vmem_limit_bytes=VMEM_LIMIT),
        name="adaln",
    )(cc, w, b)


def _inproj_kernel(with_q, x_ref, shift_ref, scale_ref, gpre_ref, ca_ref, sa_ref, cq_ref, sq_ref,
                   wlat_ref, wwin_ref, gq_ref, wuq_ref, gkv_ref, wk_ref, wv_ref, *out_refs):
    if with_q:
        qa_ref, ka_ref, vat_ref, q_ref, k_ref, vt_ref = out_refs
    else:
        ka_ref, vat_ref, k_ref, vt_ref = out_refs
    x = x_ref[0]
    tm = x.shape[0]
    h = _rms(x, gpre_ref[...]) * (1.0 + scale_ref[0]) + shift_ref[0]
    hb = h.astype(BF16)
    lat = _dot(hb, wlat_ref[...])
    p = _dot(hb, wwin_ref[...])
    c_ka = A_HEADS * HEAD_DIM if with_q else 0
    c_va = c_ka + LANES
    c_ckv, c_kr = MLA_Q_LORA, MLA_Q_LORA + MLA_KV_LORA
    lane = lax.broadcasted_iota(jnp.int32, (tm, LANES), 1)

    first_a = (lane & 32) == 0
    ca, sa = ca_ref[...], sa_ref[...]
    if with_q:
        for j in range(A_HEADS * HEAD_DIM // LANES):
            sl = slice(j * LANES, (j + 1) * LANES)
            qa_ref[0, :, sl] = (_rope(p[:, sl], ca, sa, first_a, 32) * (A_SCALE * LOG2_E)).astype(BF16)
    ka_ref[0] = _rope(p[:, c_ka:c_va], ca, sa, first_a, 32).astype(BF16)
    vat = p[:, c_va:c_va + LANES].T
    ones_a = (lax.broadcasted_iota(jnp.int32, (VT_ROWS - HEAD_DIM, BLOCK), 0) == 0).astype(BF16)
    for j in range(tm // BLOCK):
        for hk in range(A_KV_HEADS):
            vat_ref[0, j, hk, 0:HEAD_DIM, :] = vat[hk * HEAD_DIM:(hk + 1) * HEAD_DIM,
                                                   j * BLOCK:(j + 1) * BLOCK].astype(BF16)
            vat_ref[0, j, hk, HEAD_DIM:VT_ROWS, :] = ones_a

    cq, sq = cq_ref[...], sq_ref[...]
    rope_mla = lambda t: t * cq + pltpu.roll(t, LANES // 2, 1) * sq
    if with_q:
        cqn = _rms(lat[:, 0:c_ckv], gq_ref[...]).astype(BF16)
        qf = _dot(cqn, wuq_ref[...])
        for hh in range(MLA_HEADS):
            sl = slice(hh * LANES, (hh + 1) * LANES)
            q_ref[0, :, sl] = (rope_mla(qf[:, sl]) * (MLA_SCALE * LOG2_E)).astype(BF16)

    ckvn = _rms(lat[:, c_ckv:c_kr], gkv_ref[...]).astype(BF16)
    kr = rope_mla(lat[:, c_kr:c_kr + LANES])
    kn = _dot(ckvn, wk_ref[...])
    for hh in range(MLA_HEADS):
        sl = slice(hh * LANES, (hh + 1) * LANES)
        k_ref[0, :, sl] = (kn[:, sl] + kr).astype(BF16)
    vt = _dot_nt(wv_ref[...], ckvn).astype(BF16)
    ones_row = (lax.broadcasted_iota(jnp.int32, (VT_ROWS - MLA_V, tm), 0) == 0).astype(BF16)
    for hh in range(MLA_HEADS):
        vt_ref[0, 0, hh, 0:MLA_V, :] = vt[hh * MLA_V:(hh + 1) * MLA_V, :]
        vt_ref[0, 0, hh, MLA_V:VT_ROWS, :] = ones_row


def _inproj(x, mod, mod_row, gpre, tabs, wts, tm, with_q):
    bx, n, d = x.shape
    wlat, wwin, gq, wuq, gkv, wk, wv = wts
    if not with_q:
        wwin = wwin[:, A_HEADS * HEAD_DIM:]
    wts = (wlat, wwin, gq, wuq, gkv, wk, wv)
    tok = lambda i, b: (b, i, 0)
    vec = lambda col: (lambda i, b: (b if mod_row is None else mod_row, 0, col))
    tab = lambda i, b: (i, 0)
    cst = lambda i, b: (0, 0)
    in_specs = [pl.BlockSpec((1, tm, d), tok), pl.BlockSpec((1, 1, d), vec(0)), pl.BlockSpec((1, 1, d), vec(1)),
                pl.BlockSpec((1, d), cst)]
    in_specs += [pl.BlockSpec((tm, LANES), tab)] * len(tabs)
    in_specs += [pl.BlockSpec(w.shape, cst) for w in wts]
    row_out = lambda w: (pl.BlockSpec((1, tm, w), tok), jax.ShapeDtypeStruct((bx, n, w), BF16))
    win_outs = [row_out(LANES),
                (pl.BlockSpec((1, tm // BLOCK, A_KV_HEADS, VT_ROWS, BLOCK), lambda i, b: (b, i, 0, 0, 0)),
                 jax.ShapeDtypeStruct((bx, n // BLOCK, A_KV_HEADS, VT_ROWS, BLOCK), BF16))]
    mla_outs = [row_out(MLA_HEADS * LANES),
                (pl.BlockSpec((1, 1, MLA_HEADS, VT_ROWS, tm), lambda i, b: (b, i, 0, 0, 0)),
                 jax.ShapeDtypeStruct((bx, n // tm, MLA_HEADS, VT_ROWS, tm), BF16))]
    if with_q:
        outs = [row_out(A_HEADS * HEAD_DIM)] + win_outs + [row_out(MLA_HEADS * LANES)] + mla_outs
    else:
        outs = win_outs + mla_outs
    return pl.pallas_call(
        functools.partial(_inproj_kernel, with_q),
        grid=(n // tm, bx),
        in_specs=in_specs,
        out_specs=[o[0] for o in outs],
        out_shape=[o[1] for o in outs],
        compiler_params=pltpu.CompilerParams(dimension_semantics=("arbitrary", "arbitrary"),
                                             vmem_limit_bytes=VMEM_LIMIT),
        name="inproj" if with_q else "inproj_ctx",
    )(x, mod, mod, gpre, *tabs, *wts)


def _gqa_kernel(sink_ref, q_ref, *refs):
    nwin = GQA_QB + 2
    k_refs, kx_ref = refs[:nwin], refs[nwin]
    v_refs, vx_ref = refs[nwin + 1:2 * nwin + 1], refs[2 * nwin + 1]
    o_ref = refs[2 * nwin + 2]
    n = pl.program_id(1)
    nsteps = pl.num_programs(1)
    nctx = kx_ref.shape[1] // BLOCK
    group = A_HEADS // A_KV_HEADS
    key = lax.broadcasted_iota(jnp.int32, (BLOCK, group * BLOCK), 0)
    qry = lax.broadcasted_iota(jnp.int32, (BLOCK, group * BLOCK), 1) & (BLOCK - 1)
    sts = {}

    def scores(qb):
        kcat = jnp.concatenate([k_refs[qb + j][0] for j in range(3)] + [kx_ref[0]], axis=0)
        for hk in range(A_KV_HEADS):
            qg = jnp.concatenate([q_ref[0, qb * BLOCK:(qb + 1) * BLOCK, hh * HEAD_DIM:(hh + 1) * HEAD_DIM]
                                  for hh in range(hk * group, (hk + 1) * group)], axis=0)
            k_hk = kcat[:, hk * HEAD_DIM:(hk + 1) * HEAD_DIM]
            sts[qb, hk] = _dot_nt(k_hk, qg)

    def finish(qb):
        vis_prev = key >= qry
        vis_next = key <= qry
        if qb == 0:
            vis_prev = key >= qry + jnp.where(n > 0, 0, 2 * BLOCK)
        if qb == GQA_QB - 1:
            vis_next = key <= qry - jnp.where(n < nsteps - 1, 0, 2 * BLOCK)
        outs = []
        for hk in range(A_KV_HEADS):
            st = sts[qb, hk]
            pieces = [jnp.where(vis_prev, st[0:BLOCK], NEG_INF), st[BLOCK:2 * BLOCK],
                      jnp.where(vis_next, st[2 * BLOCK:3 * BLOCK], NEG_INF)]
            pieces += [st[(3 + j) * BLOCK:(4 + j) * BLOCK] for j in range(nctx)]
            sink = sink_ref[hk]
            m = jnp.maximum(functools.reduce(jnp.maximum, [jnp.max(p, axis=0, keepdims=True) for p in pieces]),
                            sink)
            ps = [jnp.exp2(p - m).astype(BF16) for p in pieces]
            vts = [v_refs[qb + j][0, 0, hk] for j in range(3)] + [vx_ref[0, j, hk] for j in range(nctx)]
            pv = functools.reduce(jnp.add, [_dot(vt, p) for vt, p in zip(vts, ps)])
            l = pv[HEAD_DIM:HEAD_DIM + 1, :] + jnp.exp2(sink - m)
            o = pv[0:HEAD_DIM, :] * (1.0 / l)
            outs += [o[:, g * BLOCK:(g + 1) * BLOCK] for g in range(group)]
        for pair in range(A_HEADS // 2):
            both = jnp.concatenate([outs[2 * pair], outs[2 * pair + 1]], axis=0)
            o_ref[0, qb * BLOCK:(qb + 1) * BLOCK, pair * LANES:(pair + 1) * LANES] = both.T.astype(BF16)

    for qb in range(GQA_QB):
        scores(qb)
    for qb in range(GQA_QB):
        finish(qb)


def _gqa(sinkv, qa, ka, vat, kac, vatc):
    b, s, _ = qa.shape
    nb = s // BLOCK
    c = kac.shape[1]
    tq = GQA_QB * BLOCK
    cur = lambda bb, n: (bb, n, 0)
    win = lambda j: (lambda bb, n: (bb, jnp.clip(n * GQA_QB + j - 1, 0, nb - 1), 0))
    vwin = lambda j: (lambda bb, n: (bb, jnp.clip(n * GQA_QB + j - 1, 0, nb - 1), 0, 0, 0))
    vblk = (1, 1, A_KV_HEADS, VT_ROWS, BLOCK)
    in_specs = [pl.BlockSpec(sinkv.shape, lambda bb, n: (0, 0, 0)), pl.BlockSpec((1, tq, A_HEADS * HEAD_DIM), cur)]
    in_specs += [pl.BlockSpec((1, BLOCK, LANES), win(j)) for j in range(GQA_QB + 2)]
    in_specs += [pl.BlockSpec((1, c, LANES), lambda bb, n: (bb, 0, 0))]
    in_specs += [pl.BlockSpec(vblk, vwin(j)) for j in range(GQA_QB + 2)]
    in_specs += [pl.BlockSpec((1, c // BLOCK, A_KV_HEADS, VT_ROWS, BLOCK), lambda bb, n: (bb, 0, 0, 0, 0))]
    return pl.pallas_call(
        _gqa_kernel,
        grid=(b, nb // GQA_QB),
        in_specs=in_specs,
        out_specs=pl.BlockSpec((1, tq, 512), cur),
        out_shape=jax.ShapeDtypeStruct((b, s, 512), BF16),
        compiler_params=pltpu.CompilerParams(dimension_semantics=("arbitrary", "arbitrary"),
                                             vmem_limit_bytes=VMEM_LIMIT),
        name="gqa",
    )(sinkv, qa, *([ka] * (GQA_QB + 2)), kac, *([vat] * (GQA_QB + 2)), vatc)


def _mla_kernel(q_ref, k_ref, vt_ref, kc_ref, vct_ref, o_ref, m_ref, acc_ref, sa_ref, sb_ref, cma_ref, cmb_ref):
    nk = k_ref.shape[1] // TK_MLA

    sa_ref, sb_ref = (sa_ref, cma_ref), (sb_ref, cmb_ref)

    def scores_into(bufs, k_blk, heads=range(MLA_HEADS)):
        buf, cm = bufs
        n = k_blk.shape[0]
        for hh in heads:
            hsl = slice(hh * LANES, (hh + 1) * LANES)
            st = _dot_nt(k_blk[:, hsl], q_ref[0, :, hsl])
            buf[hh, 0:n, :] = st
            cm[hh, 0:1, :] = jnp.max(st, axis=0, keepdims=True)

    def consume(bufs, vt_blk, first, heads=range(MLA_HEADS)):
        buf, cm = bufs
        n = vt_blk.shape[2]
        for hh in heads:
            st = buf[hh, 0:n, :]
            cmax = cm[hh, 0:1, :]
            if first:
                m_new = cmax
            else:
                m_old = m_ref[hh, 0:1, :]
                m_new = jnp.maximum(m_old, cmax)
                alpha = jnp.exp2(m_old - m_new)
            p = jnp.exp2(st - m_new).astype(BF16)
            pv = _dot(vt_blk[hh], p)
            acc_ref[hh] = pv if first else alpha * acc_ref[hh] + pv
            m_ref[hh, 0:1, :] = m_new

    def k_chunk(c):
        return k_ref[0, pl.ds(pl.multiple_of(c * TK_MLA, TK_MLA), TK_MLA), :]

    def overlapped(buf_w, k_blk, buf_r, vt_blk, first):
        for g in range(0, MLA_HEADS, MLA_GROUP):
            hs = range(g, g + MLA_GROUP)
            scores_into(buf_w, k_blk, hs)
            consume(buf_r, vt_blk, first, hs)

    scores_into(sa_ref, k_chunk(0))
    overlapped(sb_ref, k_chunk(1), sa_ref, vt_ref[0, 0], True)

    def body(i, _):
        c = 2 * i + 1
        overlapped(sa_ref, k_chunk(c + 1), sb_ref, vt_ref[0, c], False)
        overlapped(sb_ref, k_chunk(c + 2), sa_ref, vt_ref[0, c + 1], False)
        return 0

    lax.fori_loop(0, (nk - 2) // 2, body, 0)
    overlapped(sa_ref, kc_ref[0], sb_ref, vt_ref[0, nk - 1], False)
    consume(sa_ref, vct_ref[0, 0], False)

    outs = [acc_ref[hh, 0:MLA_V, :] * (1.0 / acc_ref[hh, MLA_V:MLA_V + 1, :]) for hh in range(MLA_HEADS)]
    o_ref[0] = jnp.concatenate(outs, axis=0).T.astype(BF16)


def _mla(q, k, vt, kc, vct):
    b, s, _ = q.shape
    c = kc.shape[1]
    nk = s // TK_MLA
    assert vt.shape == (b, nk, MLA_HEADS, VT_ROWS, TK_MLA) and nk % 2 == 0
    return pl.pallas_call(
        _mla_kernel,
        grid=(b, s // TQ_MLA),
        in_specs=[pl.BlockSpec((1, TQ_MLA, 1024), lambda bb, i: (bb, i, 0)),
                  pl.BlockSpec((1, s, 1024), lambda bb, i: (bb, 0, 0)),
                  pl.BlockSpec((1, nk, MLA_HEADS, VT_ROWS, TK_MLA), lambda bb, i: (bb, 0, 0, 0, 0)),
                  pl.BlockSpec((1, c, 1024), lambda bb, i: (bb, 0, 0)),
                  pl.BlockSpec((1, 1, MLA_HEADS, VT_ROWS, c), lambda bb, i: (bb, 0, 0, 0, 0))],
        out_specs=pl.BlockSpec((1, TQ_MLA, 512), lambda bb, i: (bb, i, 0)),
        out_shape=jax.ShapeDtypeStruct((b, s, 512), BF16),
        scratch_shapes=[pltpu.VMEM((MLA_HEADS, 8, TQ_MLA), F32),
                        pltpu.VMEM((MLA_HEADS, VT_ROWS, TQ_MLA), F32),
                        pltpu.VMEM((MLA_HEADS, TK_MLA, TQ_MLA), F32), pltpu.VMEM((MLA_HEADS, TK_MLA, TQ_MLA), F32),
                        pltpu.VMEM((MLA_HEADS, 8, TQ_MLA), F32), pltpu.VMEM((MLA_HEADS, 8, TQ_MLA), F32)],
        compiler_params=pltpu.CompilerParams(dimension_semantics=("arbitrary", "arbitrary"),
                                             vmem_limit_bytes=VMEM_LIMIT),
        name="mla",
    )(q, k, vt, kc, vct)


def _outproj_kernel(oa_ref, ob_ref, x_ref, woa_ref, wob_ref, gpost_ref, gate_ref, gffn_ref, shift_ref, scale_ref,
                    wr_pair_ref, wr_hi_ref, br_ref, tri_ref, upper_ref,
                    x1_ref, h2_ref, ri_ref, rw_ref, cnt_ref):
    tm = x_ref.shape[1]
    nsub = 2
    sub = tm // nsub
    rows = [slice(j * sub, (j + 1) * sub) for j in range(nsub)]
    ys = [_dot(oa_ref[0, r, :], woa_ref[...]) + _dot(ob_ref[0, r, :], wob_ref[...]) for r in rows]
    sub_lane_f = lax.broadcasted_iota(jnp.int32, (sub, LANES), 1).astype(F32)
    idx_parts, w_parts = [], []
    for r, y in zip(rows, ys):
        x1 = x_ref[0, r, :] + gate_ref[0] * _rms(y, gpost_ref[...])
        x1_ref[0, r, :] = x1
        h2 = _rms(x1, gffn_ref[...]) * (1.0 + scale_ref[0]) + shift_ref[0]
        h2_ref[0, r, :] = h2.astype(BF16)
        h_hi, h_lo = _split_bf16(h2)
        both = _dot(h_hi, wr_pair_ref[...])
        cur = both[:, :LANES] + both[:, LANES:] + _dot(h_lo, wr_hi_ref[...]) + br_ref[...]
        tops, idxs = [], []
        for _ in range(TOP_K):
            mk = jnp.max(cur, axis=-1, keepdims=True)
            ik = jnp.min(jnp.where(cur == mk, sub_lane_f, float(LANES)), axis=-1, keepdims=True)
            tops.append(mk)
            idxs.append(ik)
            cur = jnp.where(sub_lane_f == ik, -jnp.inf, cur)
        es = [jnp.exp(t - tops[0]) for t in tops]
        inv = 1.0 / functools.reduce(jnp.add, es)
        idx_parts.append(idxs)
        w_parts.append([e * inv for e in es])

    hot_parts = [[(sub_lane_f == ik) for ik in idxs] for idxs in idx_parts]
    onehot = jnp.concatenate([functools.reduce(jnp.add, [o.astype(F32) for o in hots]) for hots in hot_parts],
                             axis=0)
    cnt = jnp.sum(onehot, axis=0, keepdims=True)
    prefix = _dot(tri_ref[...], onehot.astype(BF16))
    chunks = jnp.floor((cnt + (RUN_ALIGN - 1.0)) * (1.0 / RUN_ALIGN))
    run_start = _dot(jnp.broadcast_to(chunks, (8, LANES)).astype(BF16), upper_ref[...])[0:1, :] * RUN_ALIGN
    base = prefix + run_start
    sub_lane = lax.broadcasted_iota(jnp.int32, (sub, LANES), 1)
    for r, hots, idxs, wts in zip(rows, hot_parts, idx_parts, w_parts):
        ri = jnp.zeros((sub, LANES), jnp.int32)
        rw = jnp.zeros((sub, LANES), F32)
        for k in range(TOP_K):
            pos = jnp.sum(jnp.where(hots[k], base[r, :], 0.0), axis=-1, keepdims=True).astype(jnp.int32)
            ri = jnp.where(sub_lane == k, idxs[k].astype(jnp.int32), ri)
            ri = jnp.where(sub_lane == TOP_K + k, pos, ri)
            rw = jnp.where(sub_lane == k, wts[k], rw)
        ri_ref[0, r, :] = ri
        rw_ref[0, r, :] = rw
    cnt_ref[0] = cnt


def _outproj(oa, ob, x, woa, wob, gpost, mod, gffn, wrh, wrl, br, tri, upper):
    b, s, d = x.shape
    tm = TM_OUT
    nt = s // tm
    tok = lambda bb, i: (bb, i, 0)
    vec = lambda col: (lambda bb, i: (bb, 0, col))
    cst = lambda bb, i: (0, 0)
    return pl.pallas_call(
        _outproj_kernel,
        grid=(b, nt),
        in_specs=[pl.BlockSpec((1, tm, 512), tok), pl.BlockSpec((1, tm, 512), tok), pl.BlockSpec((1, tm, d), tok),
                  pl.BlockSpec((512, d), cst), pl.BlockSpec((512, d), cst), pl.BlockSpec((1, d), cst),
                  pl.BlockSpec((1, 1, d), vec(2)), pl.BlockSpec((1, d), cst),
                  pl.BlockSpec((1, 1, d), vec(3)), pl.BlockSpec((1, 1, d), vec(4)),
                  pl.BlockSpec((d, 2 * LANES), cst), pl.BlockSpec((d, LANES), cst), pl.BlockSpec((1, LANES), cst),
                  pl.BlockSpec((tm, tm), cst), pl.BlockSpec((LANES, LANES), cst)],
        out_specs=[pl.BlockSpec((1, tm, d), tok), pl.BlockSpec((1, tm, d), tok),
                   pl.BlockSpec((1, tm, LANES), tok), pl.BlockSpec((1, tm, LANES), tok),
                   pl.BlockSpec((1, 1, LANES), lambda bb, i: (bb * nt + i, 0, 0))],
        out_shape=[jax.ShapeDtypeStruct((b, s, d), F32), jax.ShapeDtypeStruct((b, s, d), BF16),
                   jax.ShapeDtypeStruct((b, s, LANES), jnp.int32), jax.ShapeDtypeStruct((b, s, LANES), F32),
                   jax.ShapeDtypeStruct((b * nt, 1, LANES), F32)],
        compiler_params=pltpu.CompilerParams(dimension_semantics=("arbitrary", "arbitrary"),
                                             vmem_limit_bytes=VMEM_LIMIT),
        name="outproj",
    )(oa, ob, x, woa, wob, gpost, mod, gffn, mod, mod, wrh, wrl, br, tri, upper)


def _start_pieces(g, o, rows, make_copy):
    n_big = lax.shift_right_logical(rows, BIG_COPY.bit_length() - 1)
    n_small = lax.shift_right_logical(rows, RUN_ALIGN.bit_length() - 1) & (BIG_COPY // RUN_ALIGN - 1)

    def big(c, _):
        off = c * BIG_COPY
        make_copy(pl.multiple_of(g + off, RUN_ALIGN), pl.multiple_of(o + off, RUN_ALIGN), BIG_COPY).start()
        return 0

    def small(c, _):
        off = n_big * BIG_COPY + c * RUN_ALIGN
        make_copy(pl.multiple_of(g + off, RUN_ALIGN), pl.multiple_of(o + off, RUN_ALIGN), RUN_ALIGN).start()
        return 0

    lax.fori_loop(0, n_big, big, 0)
    lax.fori_loop(0, n_small, small, 0)
    return n_big, n_small


def _unrolled_loop(n, step):
    shift = LOOP_UNROLL.bit_length() - 1
    trips = lax.shift_right_logical(n, shift)

    def many(t, _):
        for u in range(LOOP_UNROLL):
            step(t * LOOP_UNROLL + u)
        return 0

    def one(i, _):
        step(i)
        return 0

    lax.fori_loop(0, trips, many, 0)
    lax.fori_loop(trips * LOOP_UNROLL, n, one, 0)


def _wait_pieces(n_big, n_small, make_copy):
    _unrolled_loop(n_big, lambda i: make_copy(0, 0, BIG_COPY).wait())
    _unrolled_loop(n_small, lambda i: make_copy(0, 0, RUN_ALIGN).wait())


def _start_listed_copies(plan, tile, make_copy):
    big_g, big_o, n_big, small_g, small_o, n_small = plan

    def start_list(g_ref, o_ref, n, width, rows):
        def start(p):
            j = tile * width + p
            make_copy(pl.multiple_of(g_ref[j], RUN_ALIGN), pl.multiple_of(o_ref[j], RUN_ALIGN), rows).start()

        _unrolled_loop(n, start)

    start_list(big_g, big_o, n_big[tile], MAX_BIG, BIG_COPY)
    start_list(small_g, small_o, n_small[tile], MAX_SMALL, RUN_ALIGN)


def _dispatch_kernel(bg_ref, bo_ref, nb_ref, sg_ref, so_ref, ns_ref, zs_ref, zr_ref, ri_ref, h_ref, xs_ref,
                     sorted_ref, zero_ref, sem):
    plan = (bg_ref, bo_ref, nb_ref, sg_ref, so_ref, ns_ref)
    tile = pl.program_id(0)
    last = pl.num_programs(0) - 1
    slot = tile % 2
    tm = h_ref.shape[0]
    post = ri_ref[...].astype(F32).T
    h = h_ref[...]
    rb_rows = 256
    pos_blk = [jnp.floor(post[TOP_K + k:TOP_K + k + 1, :] * (1.0 / rb_rows)) for k in range(TOP_K)]
    pos_off = [post[TOP_K + k:TOP_K + k + 1, :] - rb_rows * pos_blk[k] for k in range(TOP_K)]
    row = lax.broadcasted_iota(jnp.int32, (rb_rows, tm), 0).astype(F32).astype(BF16)
    one, zero = jnp.ones((), BF16), jnp.zeros((), BF16)

    for rb in range(SORTED_ROWS // rb_rows):
        perm = functools.reduce(jnp.add, [
            jnp.where(row == jnp.where(pos_blk[k] == float(rb), pos_off[k], -1.0).astype(BF16), one, zero)
            for k in range(TOP_K)])
        xr = _dot(perm, h)
        lo = lax.bitcast_convert_type(xr[:, :512], jnp.uint32)
        hi = lax.bitcast_convert_type(xr[:, 512:], jnp.uint32)
        sorted_ref[slot, rb * rb_rows:(rb + 1) * rb_rows, :] = (lo >> 16) | (hi & jnp.uint32(0xFFFF0000))

    def run_copy(sl):
        def make_copy(g, o, rows):
            return pltpu.make_async_copy(sorted_ref.at[sl, pl.ds(o, rows)], xs_ref.at[pl.ds(g, rows)], sem.at[sl])
        return make_copy

    _start_listed_copies(plan, tile, run_copy(slot))

    @pl.when(tile > 0)
    def _():
        _wait_pieces(nb_ref[tile - 1], ns_ref[tile - 1], run_copy(1 - slot))

    def zero_copy(g, o, rows):
        return pltpu.make_async_copy(zero_ref.at[pl.ds(0, rows)], xs_ref.at[pl.ds(g, rows)], sem.at[2])

    @pl.when(tile == 0)
    def _():
        zero_ref[...] = jnp.zeros_like(zero_ref)

        def start_zero(e, carry):
            _start_pieces(zs_ref[e], 0, zr_ref[e], zero_copy)
            return carry

        lax.fori_loop(0, zs_ref.shape[0], start_zero, 0)

    @pl.when(tile == last)
    def _():
        _wait_pieces(nb_ref[tile], ns_ref[tile], run_copy(slot))

        def count(e, totals):
            rows = zr_ref[e]
            n_big = lax.shift_right_logical(rows, BIG_COPY.bit_length() - 1)
            n_small = lax.shift_right_logical(rows, RUN_ALIGN.bit_length() - 1) & (BIG_COPY // RUN_ALIGN - 1)
            return totals[0] + n_big, totals[1] + n_small

        zb, zs = lax.fori_loop(0, zs_ref.shape[0], count, (0, 0))
        _wait_pieces(zb, zs, zero_copy)


def _dispatch(plan, zstart, zrows, ri, h2, cap):
    t, d = h2.shape
    tm = TM_OUT
    return pl.pallas_call(
        _dispatch_kernel,
        grid_spec=pltpu.PrefetchScalarGridSpec(
            num_scalar_prefetch=8,
            grid=(t // tm,),
            in_specs=[pl.BlockSpec((tm, LANES), lambda i, *_: (i, 0)),
                      pl.BlockSpec((tm, d), lambda i, *_: (i, 0))],
            out_specs=pl.BlockSpec(memory_space=pl.ANY),
            scratch_shapes=[pltpu.VMEM((2, SORTED_ROWS, d // 2), jnp.uint32),
                            pltpu.VMEM((BIG_COPY, d // 2), jnp.uint32),
                            pltpu.SemaphoreType.DMA((3,))]),
        out_shape=jax.ShapeDtypeStruct((cap, d // 2), jnp.uint32),
        compiler_params=pltpu.CompilerParams(dimension_semantics=("arbitrary",),
                                             vmem_limit_bytes=VMEM_LIMIT),
        name="dispatch",
    )(*plan, zstart, zrows, ri, h2)


def _pack_bf16_pairs(x):
    n = x.shape[1] // 2
    bits = lax.bitcast_convert_type(x.astype(BF16).astype(F32), jnp.uint32)
    return (bits[:, :n] >> 16) | (bits[:, n:] & jnp.uint32(0xFFFF0000))


def _unpack_bf16_pairs(w):
    lo = lax.bitcast_convert_type(w << 16, F32).astype(BF16)
    hi = lax.bitcast_convert_type(w & jnp.uint32(0xFFFF0000), F32).astype(BF16)
    return lo, hi


def _experts_kernel(be_ref, nu_ref, slot_ref, nxt_ref, nv_ref, xs_ref, wgu_hbm, bgu_ref, wd_hbm, bd_ref, ys_ref,
                    wgu_f, wd_f, wgu_b, wd_b, sem):
    del nu_ref
    i = pl.program_id(0)
    e = be_ref[i]
    slot = slot_ref[i]
    valid = nv_ref[i]
    used = valid > 0
    run_start = used & ((i == 0) | (be_ref[jnp.maximum(i - 1, 0)] != e))

    def weight_copies(ex, sl):
        return (pltpu.make_async_copy(wgu_hbm.at[ex], wgu_f.at[sl], sem.at[0, sl]),
                pltpu.make_async_copy(wd_hbm.at[ex], wd_f.at[sl], sem.at[1, sl]))

    @pl.when(i == 0)
    def _():
        for cp in weight_copies(e, slot):
            cp.start()

    @pl.when(run_start)
    def _():
        for cp in weight_copies(e, slot):
            cp.wait()
        nxt = nxt_ref[i]

        @pl.when(nxt >= 0)
        def _():
            for cp in weight_copies(nxt, 1 - slot):
                cp.start()

    def compute(rows, fresh):
        if fresh:
            wgu_b[...] = wgu_f[slot].astype(BF16)
            wd_b[...] = wd_f[slot].astype(BF16)
        gus = []
        for r in rows:
            x_lo, x_hi = _unpack_bf16_pairs(xs_ref[r, :])
            gus.append(_dot(x_lo, wgu_b[0:512, :]) + _dot(x_hi, wgu_b[512:, :]) + bgu_ref[0])
        for r, gu in zip(rows, gus):
            gate = jnp.minimum(gu[:, :D_FF], SWIGLU_LIMIT)
            lin = jnp.clip(gu[:, D_FF:], -SWIGLU_LIMIT, SWIGLU_LIMIT)
            act = (lin + 1.0) * (gate * jax.nn.sigmoid(SWIGLU_ALPHA * gate))
            ys_ref[r, :] = _pack_bf16_pairs(_dot(act.astype(BF16), wd_b[...]) + bd_ref[0])

    nsteps = MOE_BLOCK // EXPERT_STEP
    for n in range(nsteps + 1):
        lo, hi = (n - 1) * EXPERT_STEP, n * EXPERT_STEP
        in_range = (valid == 0) if n == 0 else ((valid > lo) & (valid <= hi))
        cuts = list(range(0, hi, EXPERT_SUB)) + [hi]
        for fresh in ((False,) if n == 0 else (False, True)):
            cond = in_range & (run_start if fresh else jnp.logical_not(run_start))

            @pl.when(cond)
            def _(hi=hi, cuts=cuts, fresh=fresh):
                if hi:
                    compute([slice(a, b) for a, b in zip(cuts[:-1], cuts[1:])], fresh)
                if hi < MOE_BLOCK:
                    ys_ref[hi:, :] = jnp.zeros((MOE_BLOCK - hi, D_MODEL // 2), jnp.uint32)


def _experts(block_e, n_used, slot, nxt, nvalid, xs, wgu, bgu, wd, bd):
    cap = xs.shape[0]
    nblk = cap // MOE_BLOCK
    return pl.pallas_call(
        _experts_kernel,
        grid_spec=pltpu.PrefetchScalarGridSpec(
            num_scalar_prefetch=5,
            grid=(nblk,),
            in_specs=[pl.BlockSpec((MOE_BLOCK, 512), lambda i, be, nu, *_: (jnp.minimum(i, nu[0] - 1), 0)),
                      pl.BlockSpec(memory_space=pl.ANY),
                      pl.BlockSpec((1, 1, 2 * D_FF), lambda i, be, *_: (be[i], 0, 0)),
                      pl.BlockSpec(memory_space=pl.ANY),
                      pl.BlockSpec((1, 1, D_MODEL), lambda i, be, *_: (be[i], 0, 0))],
            out_specs=pl.BlockSpec((MOE_BLOCK, D_MODEL // 2), lambda i, *_: (i, 0)),
            scratch_shapes=[pltpu.VMEM((2, D_MODEL, 2 * D_FF), F32), pltpu.VMEM((2, D_FF, D_MODEL), F32),
                            pltpu.VMEM((D_MODEL, 2 * D_FF), BF16), pltpu.VMEM((D_FF, D_MODEL), BF16),
                            pltpu.SemaphoreType.DMA((2, 2))]),
        out_shape=jax.ShapeDtypeStruct((cap, D_MODEL // 2), jnp.uint32),
        compiler_params=pltpu.CompilerParams(dimension_semantics=("arbitrary",),
                                             vmem_limit_bytes=VMEM_LIMIT),
        name="experts",
    )(block_e, n_used, slot, nxt, nvalid, xs, wgu, bgu, wd, bd)


def _combine_kernel(bg_ref, bo_ref, nb_ref, sg_ref, so_ref, ns_ref, ys_ref, ri_ref, rw_ref, x1_ref, gate_ref, g_ref,
                    o_ref, ybuf, sem):
    plan = (bg_ref, bo_ref, nb_ref, sg_ref, so_ref, ns_ref)
    nt = pl.num_programs(1)
    tile = pl.program_id(0) * nt + pl.program_id(1)
    ntiles = pl.num_programs(0) * nt
    slot = tile % 2
    tm = x1_ref.shape[1]

    def run_copy(sl):
        def make_copy(g, o, rows):
            return pltpu.make_async_copy(ys_ref.at[pl.ds(g, rows)], ybuf.at[sl, pl.ds(o, rows)], sem.at[sl])
        return make_copy

    @pl.when(tile == 0)
    def _():
        ybuf[...] = jnp.zeros_like(ybuf)
        _start_listed_copies(plan, tile, run_copy(slot))

    @pl.when(tile + 1 < ntiles)
    def _():
        _start_listed_copies(plan, tile + 1, run_copy(1 - slot))

    _wait_pieces(nb_ref[tile], ns_ref[tile], run_copy(slot))

    posf = ri_ref[0].astype(F32)
    rw = rw_ref[0]
    cb_cols = 256
    pos = [posf[:, TOP_K + k:TOP_K + k + 1] for k in range(TOP_K)]
    pos_blk = [jnp.floor(p * (1.0 / cb_cols)) for p in pos]
    pos_off = [p - cb_cols * b for p, b in zip(pos, pos_blk)]
    wk = [rw[:, k:k + 1].astype(BF16) for k in range(TOP_K)]
    col = lax.broadcasted_iota(jnp.int32, (tm, cb_cols), 1).astype(F32).astype(BF16)
    zero = jnp.zeros((), BF16)
    f_lo = jnp.zeros((tm, D_MODEL // 2), F32)
    f_hi = jnp.zeros((tm, D_MODEL // 2), F32)
    for cb in range(SORTED_ROWS // cb_cols):
        wp = functools.reduce(jnp.add, [
            jnp.where(col == jnp.where(pos_blk[k] == float(cb), pos_off[k], -1.0).astype(BF16), wk[k], zero)
            for k in range(TOP_K)])
        y_lo, y_hi = _unpack_bf16_pairs(ybuf[slot, cb * cb_cols:(cb + 1) * cb_cols, :])
        f_lo = f_lo + _dot(wp, y_lo)
        f_hi = f_hi + _dot(wp, y_hi)
    f = jnp.concatenate([f_lo, f_hi], axis=1)
    o_ref[0] = x1_ref[0] + gate_ref[0] * _rms(f, g_ref[...])


def _combine(plan, ys, ri, rw, x1, mod, g):
    b, s, d = x1.shape
    tm = TM_OUT
    tok = lambda bb, i, *_: (bb, i, 0)
    return pl.pallas_call(
        _combine_kernel,
        grid_spec=pltpu.PrefetchScalarGridSpec(
            num_scalar_prefetch=6,
            grid=(b, s // tm),
            in_specs=[pl.BlockSpec(memory_space=pl.ANY),
                      pl.BlockSpec((1, tm, LANES), tok), pl.BlockSpec((1, tm, LANES), tok),
                      pl.BlockSpec((1, tm, d), tok),
                      pl.BlockSpec((1, 1, d), lambda bb, i, *_: (bb, 0, 5)),
                      pl.BlockSpec((1, d), lambda bb, i, *_: (0, 0))],
            out_specs=pl.BlockSpec((1, tm, d), tok),
            scratch_shapes=[pltpu.VMEM((2, SORTED_ROWS, d // 2), jnp.uint32), pltpu.SemaphoreType.DMA((2,))]),
        out_shape=jax.ShapeDtypeStruct((b, s, d), F32),
        compiler_params=pltpu.CompilerParams(dimension_semantics=("arbitrary", "arbitrary"),
                                             vmem_limit_bytes=VMEM_LIMIT),
        name="combine",
    )(*plan, ys, ri, rw, x1, mod, g)


def _rope_angles(rows, rot_dim):
    row = np.repeat(np.arange(rows, dtype=np.float64), GRID_W)
    col = np.tile(np.arange(GRID_W, dtype=np.float64), rows)
    quarter = rot_dim // 4
    inv_freq = ROPE_BASE ** (-np.arange(quarter, dtype=np.float64) / quarter)
    ang = np.concatenate([row[:, None] * inv_freq, col[:, None] * inv_freq], axis=-1)
    return np.cos(ang), np.sin(ang)


def _mla_lane_of():
    j = np.arange(MLA_NOPE + MLA_ROPE)
    r = j - MLA_NOPE
    half = MLA_ROPE // 2
    return np.where(j < 48, 16 + j, np.where(j < MLA_NOPE, 32 + j, np.where(r < half, r, 48 + r)))


def _to_mla_lanes(w, dims):
    lane_of = _mla_lane_of()[dims]
    src = np.full((LANES,), -1)
    src[lane_of] = np.arange(len(dims))
    picked = jnp.take(w, jnp.asarray(np.maximum(src, 0)), axis=-1)
    return jnp.where(jnp.asarray(src >= 0), picked, 0.0)


def _rope_tables(s):
    ca, sa = _rope_angles(s // GRID_W, HEAD_DIM)
    cb, sb = _rope_angles(s // GRID_W, MLA_ROPE)
    cos_a = np.tile(np.concatenate([ca, ca], 1), (1, 2))
    sin_a = np.tile(np.concatenate([-sa, sa], 1), (1, 2))
    cos_q = np.concatenate([cb, np.ones((s, 48)), cb, np.ones((s, 48))], 1)
    sin_q = np.concatenate([-sb, np.zeros((s, 48)), sb, np.zeros((s, 48))], 1)
    return tuple(jnp.asarray(t, F32) for t in (cos_a, sin_a, cos_q, sin_q))


def _piece_lists(slot_start, tile_start, counts, first_off, piece, width):
    ne = counts.shape[1]
    cum = jnp.cumsum(counts, axis=1)
    p = jnp.arange(width, dtype=jnp.int32)
    owner = jnp.minimum(jnp.sum((cum[:, None, :] <= p[None, :, None]).astype(jnp.int32), axis=2), ne - 1)
    sel = owner[:, :, None] == jnp.arange(ne, dtype=jnp.int32)[None, None, :]
    pick = lambda a: jnp.sum(jnp.where(sel, a[:, None, :], 0), axis=2)
    off = pick(first_off) + (p[None, :] - pick(cum - counts)) * piece
    as_list = lambda a: a.reshape(-1).astype(jnp.int32)
    return as_list(pick(slot_start) + off), as_list(pick(tile_start) + off), cum[:, -1].astype(jnp.int32)


def _identity_tables(n):
    one, zero = jnp.ones((n, LANES), F32), jnp.zeros((n, LANES), F32)
    return one, zero, one, zero


def _inproj_weights(w_in, g_q_a, w_uq, g_kv_a, w_ukv):
    d = w_in.shape[0]
    wq, wk, wv = w_in[:, :512], w_in[:, 512:640], w_in[:, 640:768]
    wcq, wckv, wkr = w_in[:, 768:1152], w_in[:, 1152:1408], w_in[:, 1408:1440]
    nope, rope = np.arange(MLA_NOPE), MLA_NOPE + np.arange(MLA_ROPE)
    wkr_p = _to_mla_lanes(wkr, rope)
    wlat = jnp.concatenate([wcq, wckv, wkr_p], 1).astype(BF16)
    wwin = jnp.concatenate([wq, wk, wv], 1).astype(BF16)
    wuq = _to_mla_lanes(w_uq.reshape(MLA_Q_LORA, MLA_HEADS, MLA_NOPE + MLA_ROPE), np.arange(MLA_NOPE + MLA_ROPE))
    wuq = wuq.reshape(MLA_Q_LORA, MLA_HEADS * LANES).astype(BF16)
    wukv = w_ukv.reshape(MLA_KV_LORA, MLA_HEADS, MLA_NOPE + MLA_V)
    wkk = _to_mla_lanes(wukv[:, :, :MLA_NOPE], nope).reshape(MLA_KV_LORA, MLA_HEADS * LANES)
    wvv = wukv[:, :, MLA_NOPE:].reshape(MLA_KV_LORA, MLA_HEADS * MLA_V)
    return (wlat, wwin, g_q_a.reshape(1, -1), wuq, g_kv_a.reshape(1, -1), wkk.astype(BF16), wvv.T.astype(BF16))


def kernel(x, c, ctx, c_ctx, w_ada, b_ada, g_mix_pre, g_mix_post, w_in, sink, g_q_a, w_uq, g_kv_a, w_ukv, w_o,
           g_ffn_pre, g_ffn_post, w_router, b_router, w_gate_up, b_gate_up, w_down, b_down):
    b, s, d = x.shape
    cl = ctx.shape[1]
    t = b * s

    cc = jnp.concatenate([c, c_ctx[None, :], jnp.zeros((8 - b - 1, d), F32)], axis=0)
    mod = _adaln(cc, w_ada[0], b_ada[0].reshape(1, -1)).reshape(8, 1, 6 * d)

    wts = _inproj_weights(w_in[0], g_q_a[0], w_uq[0], g_kv_a[0], w_ukv[0])
    gpre = g_mix_pre[0].reshape(1, d)
    qa, ka, vat, q, k, vt = _inproj(x, mod, None, gpre, _rope_tables(s), wts, TM_IN, True)
    kac, vatc, kc, vct = _inproj(ctx, mod, b, gpre, _identity_tables(cl), wts, cl, False)

    sinkv = jnp.repeat(sink[0] * LOG2_E, BLOCK).reshape(A_KV_HEADS, 1, -1)
    out_a = _gqa(sinkv, qa, ka, vat, kac, vatc)
    out_b = _mla(q, k, vt, kc, vct)

    wo = w_o[0].astype(BF16)
    wr = jnp.pad(w_router[0], ((0, 0), (0, LANES - N_EXPERTS)))
    wr_hi, wr_lo = _split_bf16(wr)
    br = jnp.concatenate([b_router[0], jnp.full((LANES - N_EXPERTS,), NEG_INF, F32)]).reshape(1, LANES)
    tri = jnp.asarray(np.tri(TM_OUT, k=-1), BF16)
    upper = jnp.asarray(np.tri(LANES, k=-1).T, BF16)
    x1, h2, ri, rw, cnt = _outproj(out_a, out_b, x, wo[:512], wo[512:], g_mix_post[0].reshape(1, d), mod,
                                   g_ffn_pre[0].reshape(1, d),
                                   jnp.concatenate([wr_hi, wr_lo], axis=1), wr_hi, br, tri, upper)

    cnt = cnt[:, 0, :N_EXPERTS].astype(jnp.int32)
    nt = cnt.shape[0]
    rows = (cnt + RUN_ALIGN - 1) // RUN_ALIGN * RUN_ALIGN
    tot = jnp.sum(rows, axis=0)
    carry = jnp.cumsum(rows, axis=0) - rows
    padded = (tot + MOE_BLOCK - 1) // MOE_BLOCK * MOE_BLOCK
    pad_end = jnp.cumsum(padded)
    pad_start = pad_end - padded
    nblk = -(-(t * TOP_K + nt * N_EXPERTS * (RUN_ALIGN - 1)) // MOE_BLOCK) + N_EXPERTS
    cap = nblk * MOE_BLOCK
    blk_row = jnp.arange(nblk, dtype=jnp.int32) * MOE_BLOCK
    block_e = jnp.minimum(jnp.sum((pad_end[None, :] <= blk_row[:, None]).astype(jnp.int32), axis=1), N_EXPERTS - 1)
    n_used = (pad_end[-1] // MOE_BLOCK).astype(jnp.int32).reshape(1)
    toff = jnp.cumsum(rows, axis=1) - rows
    gstart = pad_start[None, :] + carry
    n_big = rows // BIG_COPY
    n_small = (rows - n_big * BIG_COPY) // RUN_ALIGN
    big_g, big_o, tot_big = _piece_lists(gstart, toff, n_big, jnp.zeros_like(rows), BIG_COPY, MAX_BIG)
    small_g, small_o, tot_small = _piece_lists(gstart, toff, n_small, n_big * BIG_COPY, RUN_ALIGN, MAX_SMALL)
    plan = (big_g, big_o, tot_big, small_g, small_o, tot_small)
    zstart = jnp.concatenate([pad_start + tot, pad_end[-1:]]).astype(jnp.int32)
    zrows = jnp.concatenate([padded - tot, cap - pad_end[-1:]]).astype(jnp.int32)
    block_e = block_e.astype(jnp.int32)
    run_idx = jnp.cumsum(jnp.concatenate([jnp.zeros((1,), jnp.int32),
                                          (block_e[1:] != block_e[:-1]).astype(jnp.int32)]))
    eid = jnp.arange(N_EXPERTS, dtype=jnp.int32)
    later_used = (tot[None, :] > 0) & (eid[None, :] > eid[:, None])
    next_e = jnp.min(jnp.where(later_used, eid[None, :], N_EXPERTS), axis=1)
    next_e = jnp.where(next_e < N_EXPERTS, next_e, -1).astype(jnp.int32)
    nxt = jnp.sum(jnp.where(block_e[:, None] == eid[None, :], next_e[None, :], 0), axis=1).astype(jnp.int32)

    xs = _dispatch(plan, zstart, zrows, ri.reshape(t, LANES), h2.reshape(t, d), cap)
    run_end = jnp.sum(jnp.where(block_e[:, None] == eid[None, :], (pad_start + tot)[None, :], 0), axis=1)
    nvalid = jnp.where(blk_row < pad_end[-1], jnp.clip(run_end - blk_row, 0, MOE_BLOCK), 0).astype(jnp.int32)
    ys = _experts(block_e, n_used, (run_idx % 2).astype(jnp.int32), nxt, nvalid, xs, w_gate_up[0],
                  b_gate_up[0].reshape(N_EXPERTS, 1, -1), w_down[0], b_down[0].reshape(N_EXPERTS, 1, -1))
    return _combine(plan, ys, ri, rw, x1, mod, g_ffn_post[0].reshape(1, d))
```

```python
import functools

import jax
import jax.numpy as jnp
import numpy as np
from jax import lax
from jax.experimental import pallas as pl
from jax.experimental.pallas import tpu as pltpu

F32 = jnp.float32
BF16 = jnp.bfloat16

D_MODEL = 1024
GRID_W = 64
HEAD_DIM = 64
A_HEADS = 8
A_KV_HEADS = 2
BLOCK = 128
A_SCALE = HEAD_DIM ** -0.5
MLA_HEADS = 8
MLA_NOPE = 64
MLA_ROPE = 32
MLA_V = 64
MLA_Q_LORA = 384
MLA_KV_LORA = 256
MLA_SCALE = (MLA_NOPE + MLA_ROPE) ** -0.5
LOG2_E = 1.4426950408889634
N_EXPERTS = 32
TOP_K = 4
D_FF = 1024
SWIGLU_LIMIT = 7.0
SWIGLU_ALPHA = 1.702
ROPE_BASE = 10000.0
EPS = 1e-6
NEG_INF = -1e30

LANES = 128
VMEM_LIMIT = 56 * 1024 * 1024

TM_IN = 512
GQA_QB = 8
TQ_MLA = 256
TK_MLA = 512
MLA_GROUP = 2
VT_ROWS = 80
TM_OUT = 512
OUT_TILES = 2
MOE_BLOCK = 512
EXPERT_SUB = 256
EXPERT_STEP = 128
RUN_ALIGN = 8
BIG_COPY = 32
SORTED_ROWS = TM_OUT * TOP_K + N_EXPERTS * RUN_ALIGN
LOOP_UNROLL = 8
MAX_BIG = SORTED_ROWS // BIG_COPY
MAX_SMALL = N_EXPERTS * (BIG_COPY // RUN_ALIGN - 1)


def _dot(a, b):
    return jnp.dot(a, b, preferred_element_type=F32)


def _dot_nt(a, b):
    return lax.dot_general(a, b, (((1,), (1,)), ((), ())), preferred_element_type=F32)


def _split_bf16(x):
    hi = x.astype(BF16)
    lo = (x - hi.astype(F32)).astype(BF16)
    return hi, lo


def _rms(x, g):
    return x * lax.rsqrt(jnp.mean(x * x, axis=-1, keepdims=True) + EPS) * g


def _rope(x, cos, sin_signed, first_half, half):
    n = x.shape[-1]
    partner = jnp.where(first_half, pltpu.roll(x, n - half, 1), pltpu.roll(x, half, 1))
    return x * cos + partner * sin_signed


def _adaln_kernel(c_ref, w_ref, b_ref, o_ref):
    c = c_ref[...]
    s = c * jax.nn.sigmoid(c)
    s_hi, s_lo = _split_bf16(s)
    w_hi, w_lo = _split_bf16(w_ref[...])
    o_ref[...] = _dot(s_hi, w_hi) + _dot(s_hi, w_lo) + _dot(s_lo, w_hi) + b_ref[...]


def _adaln(cc, w, b):
    n = w.shape[1]
    tn = 1024
    return pl.pallas_call(
        _adaln_kernel,
        grid=(n // tn,),
        in_specs=[pl.BlockSpec((8, D_MODEL), lambda j: (0, 0)),
                  pl.BlockSpec((D_MODEL, tn), lambda j: (0, j)),
                  pl.BlockSpec((1, tn), lambda j: (0, j))],
        out_specs=pl.BlockSpec((8, tn), lambda j: (0, j)),
        out_shape=jax.ShapeDtypeStruct((8, n), F32),
        compiler_params=pltpu.CompilerParams(dimension_semantics=("arbitrary",),
                                             vmem_limit_bytes=VMEM_LIMIT),
        name="adaln",
    )(cc, w, b)


def _inproj_kernel(with_q, x_ref, shift_ref, scale_ref, gpre_ref, ca_ref, sa_ref, cq_ref, sq_ref,
                   wlat_ref, wwin_ref, gq_ref, wuq_ref, gkv_ref, wk_ref, wv_ref, *out_refs):
    if with_q:
        qa_ref, ka_ref, vat_ref, q_ref, k_ref, vt_ref = out_refs
    else:
        ka_ref, vat_ref, k_ref, vt_ref = out_refs
    x = x_ref[0]
    tm = x.shape[0]
    h = _rms(x, gpre_ref[...]) * (1.0 + scale_ref[0]) + shift_ref[0]
    hb = h.astype(BF16)
    lat = _dot(hb, wlat_ref[...])
    p = _dot(hb, wwin_ref[...])
    c_ka = A_HEADS * HEAD_DIM if with_q else 0
    c_va = c_ka + LANES
    c_ckv, c_kr = MLA_Q_LORA, MLA_Q_LORA + MLA_KV_LORA
    lane = lax.broadcasted_iota(jnp.int32, (tm, LANES), 1)

    first_a = (lane & 32) == 0
    ca, sa = ca_ref[...], sa_ref[...]
    if with_q:
        for j in range(A_HEADS * HEAD_DIM // LANES):
            sl = slice(j * LANES, (j + 1) * LANES)
            qa_ref[0, :, sl] = (_rope(p[:, sl], ca, sa, first_a, 32) * (A_SCALE * LOG2_E)).astype(BF16)
    ka_ref[0] = _rope(p[:, c_ka:c_va], ca, sa, first_a, 32).astype(BF16)
    vat = p[:, c_va:c_va + LANES].T
    ones_a = (lax.broadcasted_iota(jnp.int32, (VT_ROWS - HEAD_DIM, BLOCK), 0) == 0).astype(BF16)
    for j in range(tm // BLOCK):
        for hk in range(A_KV_HEADS):
            vat_ref[0, j, hk, 0:HEAD_DIM, :] = vat[hk * HEAD_DIM:(hk + 1) * HEAD_DIM,
                                                   j * BLOCK:(j + 1) * BLOCK].astype(BF16)
            vat_ref[0, j, hk, HEAD_DIM:VT_ROWS, :] = ones_a

    cq, sq = cq_ref[...], sq_ref[...]
    rope_mla = lambda t: t * cq + pltpu.roll(t, LANES // 2, 1) * sq
    if with_q:
        cqn = _rms(lat[:, 0:c_ckv], gq_ref[...]).astype(BF16)
        qf = _dot(cqn, wuq_ref[...])
        for hh in range(MLA_HEADS):
            sl = slice(hh * LANES, (hh + 1) * LANES)
            q_ref[0, :, sl] = (rope_mla(qf[:, sl]) * (MLA_SCALE * LOG2_E)).astype(BF16)

    ckvn = _rms(lat[:, c_ckv:c_kr], gkv_ref[...]).astype(BF16)
    kr = rope_mla(lat[:, c_kr:c_kr + LANES])
    kn = _dot(ckvn, wk_ref[...])
    for hh in range(MLA_HEADS):
        sl = slice(hh * LANES, (hh + 1) * LANES)
        k_ref[0, :, sl] = (kn[:, sl] + kr).astype(BF16)
    vt = _dot_nt(wv_ref[...], ckvn).astype(BF16)
    ones_row = (lax.broadcasted_iota(jnp.int32, (VT_ROWS - MLA_V, tm), 0) == 0).astype(BF16)
    for hh in range(MLA_HEADS):
        vt_ref[0, 0, hh, 0:MLA_V, :] = vt[hh * MLA_V:(hh + 1) * MLA_V, :]
        vt_ref[0, 0, hh, MLA_V:VT_ROWS, :] = ones_row


def _inproj(x, mod, mod_row, gpre, tabs, wts, tm, with_q):
    bx, n, d = x.shape
    wlat, wwin, gq, wuq, gkv, wk, wv = wts
    if not with_q:
        wwin = wwin[:, A_HEADS * HEAD_DIM:]
    wts = (wlat, wwin, gq, wuq, gkv, wk, wv)
    tok = lambda i, b: (b, i, 0)
    vec = lambda col: (lambda i, b: (b if mod_row is None else mod_row, 0, col))
    tab = lambda i, b: (i, 0)
    cst = lambda i, b: (0, 0)
    in_specs = [pl.BlockSpec((1, tm, d), tok), pl.BlockSpec((1, 1, d), vec(0)), pl.BlockSpec((1, 1, d), vec(1)),
                pl.BlockSpec((1, d), cst)]
    in_specs += [pl.BlockSpec((tm, LANES), tab)] * len(tabs)
    in_specs += [pl.BlockSpec(w.shape, cst) for w in wts]
    row_out = lambda w: (pl.BlockSpec((1, tm, w), tok), jax.ShapeDtypeStruct((bx, n, w), BF16))
    win_outs = [row_out(LANES),
                (pl.BlockSpec((1, tm // BLOCK, A_KV_HEADS, VT_ROWS, BLOCK), lambda i, b: (b, i, 0, 0, 0)),
                 jax.ShapeDtypeStruct((bx, n // BLOCK, A_KV_HEADS, VT_ROWS, BLOCK), BF16))]
    mla_outs = [row_out(MLA_HEADS * LANES),
                (pl.BlockSpec((1, 1, MLA_HEADS, VT_ROWS, tm), lambda i, b: (b, i, 0, 0, 0)),
                 jax.ShapeDtypeStruct((bx, n // tm, MLA_HEADS, VT_ROWS, tm), BF16))]
    if with_q:
        outs = [row_out(A_HEADS * HEAD_DIM)] + win_outs + [row_out(MLA_HEADS * LANES)] + mla_outs
    else:
        outs = win_outs + mla_outs
    return pl.pallas_call(
        functools.partial(_inproj_kernel, with_q),
        grid=(n // tm, bx),
        in_specs=in_specs,
        out_specs=[o[0] for o in outs],
        out_shape=[o[1] for o in outs],
        compiler_params=pltpu.CompilerParams(dimension_semantics=("arbitrary", "arbitrary"),
                                             vmem_limit_bytes=VMEM_LIMIT),
        name="inproj" if with_q else "inproj_ctx",
    )(x, mod, mod, gpre, *tabs, *wts)


def _gqa_kernel(sink_ref, q_ref, *refs):
    nwin = GQA_QB + 2
    k_refs, kx_ref = refs[:nwin], refs[nwin]
    v_refs, vx_ref = refs[nwin + 1:2 * nwin + 1], refs[2 * nwin + 1]
    o_ref = refs[2 * nwin + 2]
    n = pl.program_id(1)
    nsteps = pl.num_programs(1)
    nctx = kx_ref.shape[1] // BLOCK
    group = A_HEADS // A_KV_HEADS
    key = lax.broadcasted_iota(jnp.int32, (BLOCK, group * BLOCK), 0)
    qry = lax.broadcasted_iota(jnp.int32, (BLOCK, group * BLOCK), 1) & (BLOCK - 1)
    sts = {}

    def scores(qb):
        kcat = jnp.concatenate([k_refs[qb + j][0] for j in range(3)] + [kx_ref[0]], axis=0)
        for hk in range(A_KV_HEADS):
            qg = jnp.concatenate([q_ref[0, qb * BLOCK:(qb + 1) * BLOCK, hh * HEAD_DIM:(hh + 1) * HEAD_DIM]
                                  for hh in range(hk * group, (hk + 1) * group)], axis=0)
            k_hk = kcat[:, hk * HEAD_DIM:(hk + 1) * HEAD_DIM]
            sts[qb, hk] = _dot_nt(k_hk, qg)

    def finish(qb):
        vis_prev = key >= qry
        vis_next = key <= qry
        if qb == 0:
            vis_prev = key >= qry + jnp.where(n > 0, 0, 2 * BLOCK)
        if qb == GQA_QB - 1:
            vis_next = key <= qry - jnp.where(n < nsteps - 1, 0, 2 * BLOCK)
        outs = []
        for hk in range(A_KV_HEADS):
            st = sts[qb, hk]
            pieces = [jnp.where(vis_prev, st[0:BLOCK], NEG_INF), st[BLOCK:2 * BLOCK],
                      jnp.where(vis_next, st[2 * BLOCK:3 * BLOCK], NEG_INF)]
            pieces += [st[(3 + j) * BLOCK:(4 + j) * BLOCK] for j in range(nctx)]
            sink = sink_ref[hk]
            m = jnp.maximum(functools.reduce(jnp.maximum, [jnp.max(p, axis=0, keepdims=True) for p in pieces]),
                            sink)
            ps = [jnp.exp2(p - m).astype(BF16) for p in pieces]
            vts = [v_refs[qb + j][0, 0, hk] for j in range(3)] + [vx_ref[0, j, hk] for j in range(nctx)]
            pv = functools.reduce(jnp.add, [_dot(vt, p) for vt, p in zip(vts, ps)])
            l = pv[HEAD_DIM:HEAD_DIM + 1, :] + jnp.exp2(sink - m)
            o = pv[0:HEAD_DIM, :] * (1.0 / l)
            outs += [o[:, g * BLOCK:(g + 1) * BLOCK] for g in range(group)]
        for pair in range(A_HEADS // 2):
            both = jnp.concatenate([outs[2 * pair], outs[2 * pair + 1]], axis=0)
            o_ref[0, qb * BLOCK:(qb + 1) * BLOCK, pair * LANES:(pair + 1) * LANES] = both.T.astype(BF16)

    for qb in range(GQA_QB):
        scores(qb)
    for qb in range(GQA_QB):
        finish(qb)


def _gqa(sinkv, qa, ka, vat, kac, vatc):
    b, s, _ = qa.shape
    nb = s // BLOCK
    c = kac.shape[1]
    tq = GQA_QB * BLOCK
    cur = lambda bb, n: (bb, n, 0)
    win = lambda j: (lambda bb, n: (bb, jnp.clip(n * GQA_QB + j - 1, 0, nb - 1), 0))
    vwin = lambda j: (lambda bb, n: (bb, jnp.clip(n * GQA_QB + j - 1, 0, nb - 1), 0, 0, 0))
    vblk = (1, 1, A_KV_HEADS, VT_ROWS, BLOCK)
    in_specs = [pl.BlockSpec(sinkv.shape, lambda bb, n: (0, 0, 0)), pl.BlockSpec((1, tq, A_HEADS * HEAD_DIM), cur)]
    in_specs += [pl.BlockSpec((1, BLOCK, LANES), win(j)) for j in range(GQA_QB + 2)]
    in_specs += [pl.BlockSpec((1, c, LANES), lambda bb, n: (bb, 0, 0))]
    in_specs += [pl.BlockSpec(vblk, vwin(j)) for j in range(GQA_QB + 2)]
    in_specs += [pl.BlockSpec((1, c // BLOCK, A_KV_HEADS, VT_ROWS, BLOCK), lambda bb, n: (bb, 0, 0, 0, 0))]
    return pl.pallas_call(
        _gqa_kernel,
        grid=(b, nb // GQA_QB),
        in_specs=in_specs,
        out_specs=pl.BlockSpec((1, tq, 512), cur),
        out_shape=jax.ShapeDtypeStruct((b, s, 512), BF16),
        compiler_params=pltpu.CompilerParams(dimension_semantics=("arbitrary", "arbitrary"),
                                             vmem_limit_bytes=VMEM_LIMIT),
        name="gqa",
    )(sinkv, qa, *([ka] * (GQA_QB + 2)), kac, *([vat] * (GQA_QB + 2)), vatc)


def _mla_kernel(q_ref, k_ref, vt_ref, kc_ref, vct_ref, o_ref, m_ref, acc_ref, sa_ref, sb_ref, cma_ref, cmb_ref):
    nk = k_ref.shape[1] // TK_MLA

    sa_ref, sb_ref = (sa_ref, cma_ref), (sb_ref, cmb_ref)

    def scores_into(bufs, k_blk, heads=range(MLA_HEADS)):
        buf, cm = bufs
        n = k_blk.shape[0]
        for hh in heads:
            hsl = slice(hh * LANES, (hh + 1) * LANES)
            st = _dot_nt(k_blk[:, hsl], q_ref[0, :, hsl])
            buf[hh, 0:n, :] = st
            cm[hh, 0:1, :] = jnp.max(st, axis=0, keepdims=True)

    def consume(bufs, vt_blk, first, heads=range(MLA_HEADS)):
        buf, cm = bufs
        n = vt_blk.shape[2]
        for hh in heads:
            st = buf[hh, 0:n, :]
            cmax = cm[hh, 0:1, :]
            if first:
                m_new = cmax
            else:
                m_old = m_ref[hh, 0:1, :]
                m_new = jnp.maximum(m_old, cmax)
                alpha = jnp.exp2(m_old - m_new)
            p = jnp.exp2(st - m_new).astype(BF16)
            pv = _dot(vt_blk[hh], p)
            acc_ref[hh] = pv if first else alpha * acc_ref[hh] + pv
            m_ref[hh, 0:1, :] = m_new

    def k_chunk(c):
        return k_ref[0, pl.ds(pl.multiple_of(c * TK_MLA, TK_MLA), TK_MLA), :]

    def overlapped(buf_w, k_blk, buf_r, vt_blk, first):
        for g in range(0, MLA_HEADS, MLA_GROUP):
            hs = range(g, g + MLA_GROUP)
            scores_into(buf_w, k_blk, hs)
            consume(buf_r, vt_blk, first, hs)

    scores_into(sa_ref, k_chunk(0))
    overlapped(sb_ref, k_chunk(1), sa_ref, vt_ref[0, 0], True)

    def body(i, _):
        c = 2 * i + 1
        overlapped(sa_ref, k_chunk(c + 1), sb_ref, vt_ref[0, c], False)
        overlapped(sb_ref, k_chunk(c + 2), sa_ref, vt_ref[0, c + 1], False)
        return 0

    lax.fori_loop(0, (nk - 2) // 2, body, 0)
    overlapped(sa_ref, kc_ref[0], sb_ref, vt_ref[0, nk - 1], False)
    consume(sa_ref, vct_ref[0, 0], False)

    outs = [acc_ref[hh, 0:MLA_V, :] * (1.0 / acc_ref[hh, MLA_V:MLA_V + 1, :]) for hh in range(MLA_HEADS)]
    o_ref[0] = jnp.concatenate(outs, axis=0).T.astype(BF16)


def _mla(q, k, vt, kc, vct):
    b, s, _ = q.shape
    c = kc.shape[1]
    nk = s // TK_MLA
    assert vt.shape == (b, nk, MLA_HEADS, VT_ROWS, TK_MLA) and nk % 2 == 0
    return pl.pallas_call(
        _mla_kernel,
        grid=(b, s // TQ_MLA),
        in_specs=[pl.BlockSpec((1, TQ_MLA, 1024), lambda bb, i: (bb, i, 0)),
                  pl.BlockSpec((1, s, 1024), lambda bb, i: (bb, 0, 0)),
                  pl.BlockSpec((1, nk, MLA_HEADS, VT_ROWS, TK_MLA), lambda bb, i: (bb, 0, 0, 0, 0)),
                  pl.BlockSpec((1, c, 1024), lambda bb, i: (bb, 0, 0)),
                  pl.BlockSpec((1, 1, MLA_HEADS, VT_ROWS, c), lambda bb, i: (bb, 0, 0, 0, 0))],
        out_specs=pl.BlockSpec((1, TQ_MLA, 512), lambda bb, i: (bb, i, 0)),
        out_shape=jax.ShapeDtypeStruct((b, s, 512), BF16),
        scratch_shapes=[pltpu.VMEM((MLA_HEADS, 8, TQ_MLA), F32),
                        pltpu.VMEM((MLA_HEADS, VT_ROWS, TQ_MLA), F32),
                        pltpu.VMEM((MLA_HEADS, TK_MLA, TQ_MLA), F32), pltpu.VMEM((MLA_HEADS, TK_MLA, TQ_MLA), F32),
                        pltpu.VMEM((MLA_HEADS, 8, TQ_MLA), F32), pltpu.VMEM((MLA_HEADS, 8, TQ_MLA), F32)],
        compiler_params=pltpu.CompilerParams(dimension_semantics=("arbitrary", "arbitrary"),
                                             vmem_limit_bytes=VMEM_LIMIT),
        name="mla",
    )(q, k, vt, kc, vct)


def _outproj_kernel(oa_ref, ob_ref, x_ref, woa_ref, wob_ref, gpost_ref, gate_ref, gffn_ref, shift_ref, scale_ref,
                    wr_pair_ref, wr_hi_ref, br_ref, tri_ref, upper_ref,
                    x1_ref, h2_ref, ri_ref, rw_ref, cnt_ref):
    tm = TM_OUT
    nsub = 2
    sub = tm // nsub
    rows = [slice(j * sub, (j + 1) * sub) for j in range(OUT_TILES * nsub)]
    ys = [_dot(oa_ref[0, r, :], woa_ref[...]) + _dot(ob_ref[0, r, :], wob_ref[...]) for r in rows]
    sub_lane_f = lax.broadcasted_iota(jnp.int32, (sub, LANES), 1).astype(F32)
    idx_parts, w_parts = [], []
    for r, y in zip(rows, ys):
        x1 = x_ref[0, r, :] + gate_ref[0] * _rms(y, gpost_ref[...])
        x1_ref[0, r, :] = x1
        h2 = _rms(x1, gffn_ref[...]) * (1.0 + scale_ref[0]) + shift_ref[0]
        h2_ref[0, r, :] = h2.astype(BF16)
        h_hi, h_lo = _split_bf16(h2)
        both = _dot(h_hi, wr_pair_ref[...])
        cur = both[:, :LANES] + both[:, LANES:] + _dot(h_lo, wr_hi_ref[...]) + br_ref[...]
        tops, idxs = [], []
        for _ in range(TOP_K):
            mk = jnp.max(cur, axis=-1, keepdims=True)
            ik = jnp.min(jnp.where(cur == mk, sub_lane_f, float(LANES)), axis=-1, keepdims=True)
            tops.append(mk)
            idxs.append(ik)
            cur = jnp.where(sub_lane_f == ik, -jnp.inf, cur)
        es = [jnp.exp(t - tops[0]) for t in tops]
        inv = 1.0 / functools.reduce(jnp.add, es)
        idx_parts.append(idxs)
        w_parts.append([e * inv for e in es])

    hot_parts = [[(sub_lane_f == ik) for ik in idxs] for idxs in idx_parts]
    sub_lane = lax.broadcasted_iota(jnp.int32, (sub, LANES), 1)
    for ti in range(OUT_TILES):
        groups = range(ti * nsub, (ti + 1) * nsub)
        onehot = jnp.concatenate([functools.reduce(jnp.add, [o.astype(F32) for o in hot_parts[g]]) for g in groups],
                                 axis=0)
        cnt = jnp.sum(onehot, axis=0, keepdims=True)
        prefix = _dot(tri_ref[...], onehot.astype(BF16))
        chunks = jnp.floor((cnt + (RUN_ALIGN - 1.0)) * (1.0 / RUN_ALIGN))
        run_start = _dot(jnp.broadcast_to(chunks, (8, LANES)).astype(BF16), upper_ref[...])[0:1, :] * RUN_ALIGN
        base = prefix + run_start
        for j, g in enumerate(groups):
            ri = jnp.zeros((sub, LANES), jnp.int32)
            rw = jnp.zeros((sub, LANES), F32)
            for k in range(TOP_K):
                pos = jnp.sum(jnp.where(hot_parts[g][k], base[j * sub:(j + 1) * sub, :], 0.0), axis=-1,
                              keepdims=True).astype(jnp.int32)
                ri = jnp.where(sub_lane == k, idx_parts[g][k].astype(jnp.int32), ri)
                ri = jnp.where(sub_lane == TOP_K + k, pos, ri)
                rw = jnp.where(sub_lane == k, w_parts[g][k], rw)
            ri_ref[0, rows[g], :] = ri
            rw_ref[0, rows[g], :] = rw
        cnt_ref[ti] = cnt


def _outproj(oa, ob, x, woa, wob, gpost, mod, gffn, wrh, wrl, br, tri, upper):
    b, s, d = x.shape
    tm = TM_OUT * OUT_TILES
    nt = s // tm
    tok = lambda bb, i: (bb, i, 0)
    vec = lambda col: (lambda bb, i: (bb, 0, col))
    cst = lambda bb, i: (0, 0)
    return pl.pallas_call(
        _outproj_kernel,
        grid=(b, nt),
        in_specs=[pl.BlockSpec((1, tm, 512), tok), pl.BlockSpec((1, tm, 512), tok), pl.BlockSpec((1, tm, d), tok),
                  pl.BlockSpec((512, d), cst), pl.BlockSpec((512, d), cst), pl.BlockSpec((1, d), cst),
                  pl.BlockSpec((1, 1, d), vec(2)), pl.BlockSpec((1, d), cst),
                  pl.BlockSpec((1, 1, d), vec(3)), pl.BlockSpec((1, 1, d), vec(4)),
                  pl.BlockSpec((d, 2 * LANES), cst), pl.BlockSpec((d, LANES), cst), pl.BlockSpec((1, LANES), cst),
                  pl.BlockSpec((TM_OUT, TM_OUT), cst), pl.BlockSpec((LANES, LANES), cst)],
        out_specs=[pl.BlockSpec((1, tm, d), tok), pl.BlockSpec((1, tm, d), tok),
                   pl.BlockSpec((1, tm, LANES), tok), pl.BlockSpec((1, tm, LANES), tok),
                   pl.BlockSpec((OUT_TILES, 1, LANES), lambda bb, i: (bb * nt + i, 0, 0))],
        out_shape=[jax.ShapeDtypeStruct((b, s, d), F32), jax.ShapeDtypeStruct((b, s, d), BF16),
                   jax.ShapeDtypeStruct((b, s, LANES), jnp.int32), jax.ShapeDtypeStruct((b, s, LANES), F32),
                   jax.ShapeDtypeStruct((b * nt * OUT_TILES, 1, LANES), F32)],
        compiler_params=pltpu.CompilerParams(dimension_semantics=("arbitrary", "arbitrary"),
                                             vmem_limit_bytes=VMEM_LIMIT),
        name="outproj",
    )(oa, ob, x, woa, wob, gpost, mod, gffn, mod, mod, wrh, wrl, br, tri, upper)


def _start_pieces(g, o, rows, make_copy):
    n_big = lax.shift_right_logical(rows, BIG_COPY.bit_length() - 1)
    n_small = lax.shift_right_logical(rows, RUN_ALIGN.bit_length() - 1) & (BIG_COPY // RUN_ALIGN - 1)

    def big(c, _):
        off = c * BIG_COPY
        make_copy(pl.multiple_of(g + off, RUN_ALIGN), pl.multiple_of(o + off, RUN_ALIGN), BIG_COPY).start()
        return 0

    def small(c, _):
        off = n_big * BIG_COPY + c * RUN_ALIGN
        make_copy(pl.multiple_of(g + off, RUN_ALIGN), pl.multiple_of(o + off, RUN_ALIGN), RUN_ALIGN).start()
        return 0

    lax.fori_loop(0, n_big, big, 0)
    lax.fori_loop(0, n_small, small, 0)
    return n_big, n_small


def _unrolled_loop(n, step):
    shift = LOOP_UNROLL.bit_length() - 1
    trips = lax.shift_right_logical(n, shift)

    def many(t, _):
        for u in range(LOOP_UNROLL):
            step(t * LOOP_UNROLL + u)
        return 0

    def one(i, _):
        step(i)
        return 0

    lax.fori_loop(0, trips, many, 0)
    lax.fori_loop(trips * LOOP_UNROLL, n, one, 0)


def _wait_pieces(n_big, n_small, make_copy):
    _unrolled_loop(n_big, lambda i: make_copy(0, 0, BIG_COPY).wait())
    _unrolled_loop(n_small, lambda i: make_copy(0, 0, RUN_ALIGN).wait())


def _start_listed_copies(plan, tile, make_copy):
    big_g, big_o, n_big, small_g, small_o, n_small = plan

    def start_list(g_ref, o_ref, n, width, rows):
        def start(p):
            j = tile * width + p
            make_copy(pl.multiple_of(g_ref[j], RUN_ALIGN), pl.multiple_of(o_ref[j], RUN_ALIGN), rows).start()

        _unrolled_loop(n, start)

    start_list(big_g, big_o, n_big[tile], MAX_BIG, BIG_COPY)
    start_list(small_g, small_o, n_small[tile], MAX_SMALL, RUN_ALIGN)


def _dispatch_kernel(bg_ref, bo_ref, nb_ref, sg_ref, so_ref, ns_ref, zs_ref, zr_ref, ri_ref, h_ref, xs_ref,
                     sorted_ref, zero_ref, sem):
    plan = (bg_ref, bo_ref, nb_ref, sg_ref, so_ref, ns_ref)
    tile = pl.program_id(0)
    last = pl.num_programs(0) - 1
    slot = tile % 2
    tm = h_ref.shape[0]
    post = ri_ref[...].astype(F32).T
    h = h_ref[...]
    rb_rows = 256
    pos_blk = [jnp.floor(post[TOP_K + k:TOP_K + k + 1, :] * (1.0 / rb_rows)) for k in range(TOP_K)]
    pos_off = [post[TOP_K + k:TOP_K + k + 1, :] - rb_rows * pos_blk[k] for k in range(TOP_K)]
    row = lax.broadcasted_iota(jnp.int32, (rb_rows, tm), 0).astype(F32).astype(BF16)
    one, zero = jnp.ones((), BF16), jnp.zeros((), BF16)

    for rb in range(SORTED_ROWS // rb_rows):
        perm = functools.reduce(jnp.add, [
            jnp.where(row == jnp.where(pos_blk[k] == float(rb), pos_off[k], -1.0).astype(BF16), one, zero)
            for k in range(TOP_K)])
        xr = _dot(perm, h)
        lo = lax.bitcast_convert_type(xr[:, :512], jnp.uint32)
        hi = lax.bitcast_convert_type(xr[:, 512:], jnp.uint32)
        sorted_ref[slot, rb * rb_rows:(rb + 1) * rb_rows, :] = (lo >> 16) | (hi & jnp.uint32(0xFFFF0000))

    def run_copy(sl):
        def make_copy(g, o, rows):
            return pltpu.make_async_copy(sorted_ref.at[sl, pl.ds(o, rows)], xs_ref.at[pl.ds(g, rows)], sem.at[sl])
        return make_copy

    _start_listed_copies(plan, tile, run_copy(slot))

    @pl.when(tile > 0)
    def _():
        _wait_pieces(nb_ref[tile - 1], ns_ref[tile - 1], run_copy(1 - slot))

    def zero_copy(g, o, rows):
        return pltpu.make_async_copy(zero_ref.at[pl.ds(0, rows)], xs_ref.at[pl.ds(g, rows)], sem.at[2])

    @pl.when(tile == 0)
    def _():
        zero_ref[...] = jnp.zeros_like(zero_ref)

        def start_zero(e, carry):
            _start_pieces(zs_ref[e], 0, zr_ref[e], zero_copy)
            return carry

        lax.fori_loop(0, zs_ref.shape[0], start_zero, 0)

    @pl.when(tile == last)
    def _():
        _wait_pieces(nb_ref[tile], ns_ref[tile], run_copy(slot))

        def count(e, totals):
            rows = zr_ref[e]
            n_big = lax.shift_right_logical(rows, BIG_COPY.bit_length() - 1)
            n_small = lax.shift_right_logical(rows, RUN_ALIGN.bit_length() - 1) & (BIG_COPY // RUN_ALIGN - 1)
            return totals[0] + n_big, totals[1] + n_small

        zb, zs = lax.fori_loop(0, zs_ref.shape[0], count, (0, 0))
        _wait_pieces(zb, zs, zero_copy)


def _dispatch(plan, zstart, zrows, ri, h2, cap):
    t, d = h2.shape
    tm = TM_OUT
    return pl.pallas_call(
        _dispatch_kernel,
        grid_spec=pltpu.PrefetchScalarGridSpec(
            num_scalar_prefetch=8,
            grid=(t // tm,),
            in_specs=[pl.BlockSpec((tm, LANES), lambda i, *_: (i, 0)),
                      pl.BlockSpec((tm, d), lambda i, *_: (i, 0))],
            out_specs=pl.BlockSpec(memory_space=pl.ANY),
            scratch_shapes=[pltpu.VMEM((2, SORTED_ROWS, d // 2), jnp.uint32),
                            pltpu.VMEM((BIG_COPY, d // 2), jnp.uint32),
                            pltpu.SemaphoreType.DMA((3,))]),
        out_shape=jax.ShapeDtypeStruct((cap, d // 2), jnp.uint32),
        compiler_params=pltpu.CompilerParams(dimension_semantics=("arbitrary",),
                                             vmem_limit_bytes=VMEM_LIMIT),
        name="dispatch",
    )(*plan, zstart, zrows, ri, h2)


def _pack_bf16_pairs(x):
    n = x.shape[1] // 2
    bits = lax.bitcast_convert_type(x.astype(BF16).astype(F32), jnp.uint32)
    return (bits[:, :n] >> 16) | (bits[:, n:] & jnp.uint32(0xFFFF0000))


def _unpack_bf16_pairs(w):
    lo = lax.bitcast_convert_type(w << 16, F32).astype(BF16)
    hi = lax.bitcast_convert_type(w & jnp.uint32(0xFFFF0000), F32).astype(BF16)
    return lo, hi


def _experts_kernel(be_ref, nu_ref, slot_ref, nxt_ref, nv_ref, xs_ref, wgu_hbm, bgu_ref, wd_hbm, bd_ref, ys_ref,
                    wgu_f, wd_f, wgu_b, wd_b, sem):
    del nu_ref
    i = pl.program_id(0)
    e = be_ref[i]
    slot = slot_ref[i]
    valid = nv_ref[i]
    used = valid > 0
    run_start = used & ((i == 0) | (be_ref[jnp.maximum(i - 1, 0)] != e))

    def weight_copies(ex, sl):
        return (pltpu.make_async_copy(wgu_hbm.at[ex], wgu_f.at[sl], sem.at[0, sl]),
                pltpu.make_async_copy(wd_hbm.at[ex], wd_f.at[sl], sem.at[1, sl]))

    @pl.when(i == 0)
    def _():
        for cp in weight_copies(e, slot):
            cp.start()

    @pl.when(run_start)
    def _():
        for cp in weight_copies(e, slot):
            cp.wait()
        nxt = nxt_ref[i]

        @pl.when(nxt >= 0)
        def _():
            for cp in weight_copies(nxt, 1 - slot):
                cp.start()

    def compute(rows, fresh):
        if fresh:
            wgu_b[...] = wgu_f[slot].astype(BF16)
            wd_b[...] = wd_f[slot].astype(BF16)
        gus = []
        for r in rows:
            x_lo, x_hi = _unpack_bf16_pairs(xs_ref[r, :])
            gus.append(_dot(x_lo, wgu_b[0:512, :]) + _dot(x_hi, wgu_b[512:, :]) + bgu_ref[0])
        for r, gu in zip(rows, gus):
            gate = jnp.minimum(gu[:, :D_FF], SWIGLU_LIMIT)
            lin = jnp.clip(gu[:, D_FF:], -SWIGLU_LIMIT, SWIGLU_LIMIT)
            act = (lin + 1.0) * (gate * jax.nn.sigmoid(SWIGLU_ALPHA * gate))
            ys_ref[r, :] = _pack_bf16_pairs(_dot(act.astype(BF16), wd_b[...]) + bd_ref[0])

    nsteps = MOE_BLOCK // EXPERT_STEP
    for n in range(nsteps + 1):
        lo, hi = (n - 1) * EXPERT_STEP, n * EXPERT_STEP
        in_range = (valid == 0) if n == 0 else ((valid > lo) & (valid <= hi))
        cuts = list(range(0, hi, EXPERT_SUB)) + [hi]
        for fresh in ((False,) if n == 0 else (False, True)):
            cond = in_range & (run_start if fresh else jnp.logical_not(run_start))

            @pl.when(cond)
            def _(hi=hi, cuts=cuts, fresh=fresh):
                if hi:
                    compute([slice(a, b) for a, b in zip(cuts[:-1], cuts[1:])], fresh)
                if hi < MOE_BLOCK:
                    ys_ref[hi:, :] = jnp.zeros((MOE_BLOCK - hi, D_MODEL // 2), jnp.uint32)


def _experts(block_e, n_used, slot, nxt, nvalid, xs, wgu, bgu, wd, bd):
    cap = xs.shape[0]
    nblk = cap // MOE_BLOCK
    return pl.pallas_call(
        _experts_kernel,
        grid_spec=pltpu.PrefetchScalarGridSpec(
            num_scalar_prefetch=5,
            grid=(nblk,),
            in_specs=[pl.BlockSpec((MOE_BLOCK, 512), lambda i, be, nu, *_: (jnp.minimum(i, nu[0] - 1), 0)),
                      pl.BlockSpec(memory_space=pl.ANY),
                      pl.BlockSpec((1, 1, 2 * D_FF), lambda i, be, *_: (be[i], 0, 0)),
                      pl.BlockSpec(memory_space=pl.ANY),
                      pl.BlockSpec((1, 1, D_MODEL), lambda i, be, *_: (be[i], 0, 0))],
            out_specs=pl.BlockSpec((MOE_BLOCK, D_MODEL // 2), lambda i, *_: (i, 0)),
            scratch_shapes=[pltpu.VMEM((2, D_MODEL, 2 * D_FF), F32), pltpu.VMEM((2, D_FF, D_MODEL), F32),
                            pltpu.VMEM((D_MODEL, 2 * D_FF), BF16), pltpu.VMEM((D_FF, D_MODEL), BF16),
                            pltpu.SemaphoreType.DMA((2, 2))]),
        out_shape=jax.ShapeDtypeStruct((cap, D_MODEL // 2), jnp.uint32),
        compiler_params=pltpu.CompilerParams(dimension_semantics=("arbitrary",),
                                             vmem_limit_bytes=VMEM_LIMIT),
        name="experts",
    )(block_e, n_used, slot, nxt, nvalid, xs, wgu, bgu, wd, bd)


def _combine_kernel(bg_ref, bo_ref, nb_ref, sg_ref, so_ref, ns_ref, ys_ref, ri_ref, rw_ref, x1_ref, gate_ref, g_ref,
                    o_ref, ybuf, sem):
    plan = (bg_ref, bo_ref, nb_ref, sg_ref, so_ref, ns_ref)
    nt = pl.num_programs(1)
    tile = pl.program_id(0) * nt + pl.program_id(1)
    ntiles = pl.num_programs(0) * nt
    slot = tile % 2
    tm = x1_ref.shape[1]

    def run_copy(sl):
        def make_copy(g, o, rows):
            return pltpu.make_async_copy(ys_ref.at[pl.ds(g, rows)], ybuf.at[sl, pl.ds(o, rows)], sem.at[sl])
        return make_copy

    @pl.when(tile == 0)
    def _():
        ybuf[...] = jnp.zeros_like(ybuf)
        _start_listed_copies(plan, tile, run_copy(slot))

    @pl.when(tile + 1 < ntiles)
    def _():
        _start_listed_copies(plan, tile + 1, run_copy(1 - slot))

    _wait_pieces(nb_ref[tile], ns_ref[tile], run_copy(slot))

    posf = ri_ref[0].astype(F32)
    rw = rw_ref[0]
    cb_cols = 256
    pos = [posf[:, TOP_K + k:TOP_K + k + 1] for k in range(TOP_K)]
    pos_blk = [jnp.floor(p * (1.0 / cb_cols)) for p in pos]
    pos_off = [p - cb_cols * b for p, b in zip(pos, pos_blk)]
    wk = [rw[:, k:k + 1].astype(BF16) for k in range(TOP_K)]
    col = lax.broadcasted_iota(jnp.int32, (tm, cb_cols), 1).astype(F32).astype(BF16)
    zero = jnp.zeros((), BF16)
    f_lo = jnp.zeros((tm, D_MODEL // 2), F32)
    f_hi = jnp.zeros((tm, D_MODEL // 2), F32)
    for cb in range(SORTED_ROWS // cb_cols):
        wp = functools.reduce(jnp.add, [
            jnp.where(col == jnp.where(pos_blk[k] == float(cb), pos_off[k], -1.0).astype(BF16), wk[k], zero)
            for k in range(TOP_K)])
        y_lo, y_hi = _unpack_bf16_pairs(ybuf[slot, cb * cb_cols:(cb + 1) * cb_cols, :])
        f_lo = f_lo + _dot(wp, y_lo)
        f_hi = f_hi + _dot(wp, y_hi)
    f = jnp.concatenate([f_lo, f_hi], axis=1)
    o_ref[0] = x1_ref[0] + gate_ref[0] * _rms(f, g_ref[...])


def _combine(plan, ys, ri, rw, x1, mod, g):
    b, s, d = x1.shape
    tm = TM_OUT
    tok = lambda bb, i, *_: (bb, i, 0)
    return pl.pallas_call(
        _combine_kernel,
        grid_spec=pltpu.PrefetchScalarGridSpec(
            num_scalar_prefetch=6,
            grid=(b, s // tm),
            in_specs=[pl.BlockSpec(memory_space=pl.ANY),
                      pl.BlockSpec((1, tm, LANES), tok), pl.BlockSpec((1, tm, LANES), tok),
                      pl.BlockSpec((1, tm, d), tok),
                      pl.BlockSpec((1, 1, d), lambda bb, i, *_: (bb, 0, 5)),
                      pl.BlockSpec((1, d), lambda bb, i, *_: (0, 0))],
            out_specs=pl.BlockSpec((1, tm, d), tok),
            scratch_shapes=[pltpu.VMEM((2, SORTED_ROWS, d // 2), jnp.uint32), pltpu.SemaphoreType.DMA((2,))]),
        out_shape=jax.ShapeDtypeStruct((b, s, d), F32),
        compiler_params=pltpu.CompilerParams(dimension_semantics=("arbitrary", "arbitrary"),
                                             vmem_limit_bytes=VMEM_LIMIT),
        name="combine",
    )(*plan, ys, ri, rw, x1, mod, g)


def _rope_angles(rows, rot_dim):
    row = np.repeat(np.arange(rows, dtype=np.float64), GRID_W)
    col = np.tile(np.arange(GRID_W, dtype=np.float64), rows)
    quarter = rot_dim // 4
    inv_freq = ROPE_BASE ** (-np.arange(quarter, dtype=np.float64) / quarter)
    ang = np.concatenate([row[:, None] * inv_freq, col[:, None] * inv_freq], axis=-1)
    return np.cos(ang), np.sin(ang)


def _mla_lane_of():
    j = np.arange(MLA_NOPE + MLA_ROPE)
    r = j - MLA_NOPE
    half = MLA_ROPE // 2
    return np.where(j < 48, 16 + j, np.where(j < MLA_NOPE, 32 + j, np.where(r < half, r, 48 + r)))


def _to_mla_lanes(w, dims):
    lane_of = _mla_lane_of()[dims]
    src = np.full((LANES,), -1)
    src[lane_of] = np.arange(len(dims))
    picked = jnp.take(w, jnp.asarray(np.maximum(src, 0)), axis=-1)
    return jnp.where(jnp.asarray(src >= 0), picked, 0.0)


def _rope_tables(s):
    ca, sa = _rope_angles(s // GRID_W, HEAD_DIM)
    cb, sb = _rope_angles(s // GRID_W, MLA_ROPE)
    cos_a = np.tile(np.concatenate([ca, ca], 1), (1, 2))
    sin_a = np.tile(np.concatenate([-sa, sa], 1), (1, 2))
    cos_q = np.concatenate([cb, np.ones((s, 48)), cb, np.ones((s, 48))], 1)
    sin_q = np.concatenate([-sb, np.zeros((s, 48)), sb, np.zeros((s, 48))], 1)
    return tuple(jnp.asarray(t, F32) for t in (cos_a, sin_a, cos_q, sin_q))


def _piece_lists(slot_start, tile_start, counts, first_off, piece, width):
    ne = counts.shape[1]
    cum = jnp.cumsum(counts, axis=1)
    p = jnp.arange(width, dtype=jnp.int32)
    owner = jnp.minimum(jnp.sum((cum[:, None, :] <= p[None, :, None]).astype(jnp.int32), axis=2), ne - 1)
    sel = owner[:, :, None] == jnp.arange(ne, dtype=jnp.int32)[None, None, :]
    pick = lambda a: jnp.sum(jnp.where(sel, a[:, None, :], 0), axis=2)
    off = pick(first_off) + (p[None, :] - pick(cum - counts)) * piece
    as_list = lambda a: a.reshape(-1).astype(jnp.int32)
    return as_list(pick(slot_start) + off), as_list(pick(tile_start) + off), cum[:, -1].astype(jnp.int32)


def _identity_tables(n):
    one, zero = jnp.ones((n, LANES), F32), jnp.zeros((n, LANES), F32)
    return one, zero, one, zero


def _inproj_weights(w_in, g_q_a, w_uq, g_kv_a, w_ukv):
    d = w_in.shape[0]
    wq, wk, wv = w_in[:, :512], w_in[:, 512:640], w_in[:, 640:768]
    wcq, wckv, wkr = w_in[:, 768:1152], w_in[:, 1152:1408], w_in[:, 1408:1440]
    nope, rope = np.arange(MLA_NOPE), MLA_NOPE + np.arange(MLA_ROPE)
    wkr_p = _to_mla_lanes(wkr, rope)
    wlat = jnp.concatenate([wcq, wckv, wkr_p], 1).astype(BF16)
    wwin = jnp.concatenate([wq, wk, wv], 1).astype(BF16)
    wuq = _to_mla_lanes(w_uq.reshape(MLA_Q_LORA, MLA_HEADS, MLA_NOPE + MLA_ROPE), np.arange(MLA_NOPE + MLA_ROPE))
    wuq = wuq.reshape(MLA_Q_LORA, MLA_HEADS * LANES).astype(BF16)
    wukv = w_ukv.reshape(MLA_KV_LORA, MLA_HEADS, MLA_NOPE + MLA_V)
    wkk = _to_mla_lanes(wukv[:, :, :MLA_NOPE], nope).reshape(MLA_KV_LORA, MLA_HEADS * LANES)
    wvv = wukv[:, :, MLA_NOPE:].reshape(MLA_KV_LORA, MLA_HEADS * MLA_V)
    return (wlat, wwin, g_q_a.reshape(1, -1), wuq, g_kv_a.reshape(1, -1), wkk.astype(BF16), wvv.T.astype(BF16))


def kernel(x, c, ctx, c_ctx, w_ada, b_ada, g_mix_pre, g_mix_post, w_in, sink, g_q_a, w_uq, g_kv_a, w_ukv, w_o,
           g_ffn_pre, g_ffn_post, w_router, b_router, w_gate_up, b_gate_up, w_down, b_down):
    b, s, d = x.shape
    cl = ctx.shape[1]
    t = b * s

    cc = jnp.concatenate([c, c_ctx[None, :], jnp.zeros((8 - b - 1, d), F32)], axis=0)
    mod = _adaln(cc, w_ada[0], b_ada[0].reshape(1, -1)).reshape(8, 1, 6 * d)

    wts = _inproj_weights(w_in[0], g_q_a[0], w_uq[0], g_kv_a[0], w_ukv[0])
    gpre = g_mix_pre[0].reshape(1, d)
    qa, ka, vat, q, k, vt = _inproj(x, mod, None, gpre, _rope_tables(s), wts, TM_IN, True)
    kac, vatc, kc, vct = _inproj(ctx, mod, b, gpre, _identity_tables(cl), wts, cl, False)

    sinkv = jnp.repeat(sink[0] * LOG2_E, BLOCK).reshape(A_KV_HEADS, 1, -1)
    out_a = _gqa(sinkv, qa, ka, vat, kac, vatc)
    out_b = _mla(q, k, vt, kc, vct)

    wo = w_o[0].astype(BF16)
    wr = jnp.pad(w_router[0], ((0, 0), (0, LANES - N_EXPERTS)))
    wr_hi, wr_lo = _split_bf16(wr)
    br = jnp.concatenate([b_router[0], jnp.full((LANES - N_EXPERTS,), NEG_INF, F32)]).reshape(1, LANES)
    tri = jnp.asarray(np.tri(TM_OUT, k=-1), BF16)
    upper = jnp.asarray(np.tri(LANES, k=-1).T, BF16)
    x1, h2, ri, rw, cnt = _outproj(out_a, out_b, x, wo[:512], wo[512:], g_mix_post[0].reshape(1, d), mod,
                                   g_ffn_pre[0].reshape(1, d),
                                   jnp.concatenate([wr_hi, wr_lo], axis=1), wr_hi, br, tri, upper)

    cnt = cnt[:, 0, :N_EXPERTS].astype(jnp.int32)
    nt = cnt.shape[0]
    rows = (cnt + RUN_ALIGN - 1) // RUN_ALIGN * RUN_ALIGN
    tot = jnp.sum(rows, axis=0)
    carry = jnp.cumsum(rows, axis=0) - rows
    padded = (tot + MOE_BLOCK - 1) // MOE_BLOCK * MOE_BLOCK
    pad_end = jnp.cumsum(padded)
    pad_start = pad_end - padded
    nblk = -(-(t * TOP_K + nt * N_EXPERTS * (RUN_ALIGN - 1)) // MOE_BLOCK) + N_EXPERTS
    cap = nblk * MOE_BLOCK
    blk_row = jnp.arange(nblk, dtype=jnp.int32) * MOE_BLOCK
    block_e = jnp.minimum(jnp.sum((pad_end[None, :] <= blk_row[:, None]).astype(jnp.int32), axis=1), N_EXPERTS - 1)
    n_used = (pad_end[-1] // MOE_BLOCK).astype(jnp.int32).reshape(1)
    toff = jnp.cumsum(rows, axis=1) - rows
    gstart = pad_start[None, :] + carry
    n_big = rows // BIG_COPY
    n_small = (rows - n_big * BIG_COPY) // RUN_ALIGN
    big_g, big_o, tot_big = _piece_lists(gstart, toff, n_big, jnp.zeros_like(rows), BIG_COPY, MAX_BIG)
    small_g, small_o, tot_small = _piece_lists(gstart, toff, n_small, n_big * BIG_COPY, RUN_ALIGN, MAX_SMALL)
    plan = (big_g, big_o, tot_big, small_g, small_o, tot_small)
    zstart = jnp.concatenate([pad_start + tot, pad_end[-1:]]).astype(jnp.int32)
    zrows = jnp.concatenate([padded - tot, cap - pad_end[-1:]]).astype(jnp.int32)
    block_e = block_e.astype(jnp.int32)
    run_idx = jnp.cumsum(jnp.concatenate([jnp.zeros((1,), jnp.int32),
                                          (block_e[1:] != block_e[:-1]).astype(jnp.int32)]))
    eid = jnp.arange(N_EXPERTS, dtype=jnp.int32)
    later_used = (tot[None, :] > 0) & (eid[None, :] > eid[:, None])
    next_e = jnp.min(jnp.where(later_used, eid[None, :], N_EXPERTS), axis=1)
    next_e = jnp.where(next_e < N_EXPERTS, next_e, -1).astype(jnp.int32)
    nxt = jnp.sum(jnp.where(block_e[:, None] == eid[None, :], next_e[None, :], 0), axis=1).astype(jnp.int32)

    xs = _dispatch(plan, zstart, zrows, ri.reshape(t, LANES), h2.reshape(t, d), cap)
    run_end = jnp.sum(jnp.where(block_e[:, None] == eid[None, :], (pad_start + tot)[None, :], 0), axis=1)
    nvalid = jnp.where(blk_row < pad_end[-1], jnp.clip(run_end - blk_row, 0, MOE_BLOCK), 0).astype(jnp.int32)
    ys = _experts(block_e, n_used, (run_idx % 2).astype(jnp.int32), nxt, nvalid, xs, w_gate_up[0],
                  b_gate_up[0].reshape(N_EXPERTS, 1, -1), w_down[0], b_down[0].reshape(N_EXPERTS, 1, -1))
    return _combine(plan, ys, ri, rw, x1, mod, g_ffn_post[0].reshape(1, d))
```

```python
import functools

import jax
import jax.numpy as jnp
import numpy as np
from jax import lax
from jax.experimental import pallas as pl
from jax.experimental.pallas import tpu as pltpu

F32 = jnp.float32
BF16 = jnp.bfloat16

D_MODEL = 1024
GRID_W = 64
HEAD_DIM = 64
A_HEADS = 8
A_KV_HEADS = 2
BLOCK = 128
A_SCALE = HEAD_DIM ** -0.5
MLA_HEADS = 8
MLA_NOPE = 64
MLA_ROPE = 32
MLA_V = 64
MLA_Q_LORA = 384
MLA_KV_LORA = 256
MLA_SCALE = (MLA_NOPE + MLA_ROPE) ** -0.5
LOG2_E = 1.4426950408889634
N_EXPERTS = 32
TOP_K = 4
D_FF = 1024
SWIGLU_LIMIT = 7.0
SWIGLU_ALPHA = 1.702
ROPE_BASE = 10000.0
EPS = 1e-6
NEG_INF = -1e30

LANES = 128
VMEM_LIMIT = 56 * 1024 * 1024

TM_IN = 512
GQA_QB = 8
TQ_MLA = 256
TK_MLA = 512
MLA_GROUP = 2
VT_ROWS = 80
TM_OUT = 512
OUT_TILES = 2
MOE_BLOCK = 512
EXPERT_SUB = 256
EXPERT_STEP = 128
RUN_ALIGN = 8
BIG_COPY = 32
SORTED_ROWS = TM_OUT * TOP_K + N_EXPERTS * RUN_ALIGN
LOOP_UNROLL = 8
MAX_BIG = SORTED_ROWS // BIG_COPY
MAX_SMALL = N_EXPERTS * (BIG_COPY // RUN_ALIGN - 1)


def _dot(a, b):
    return jnp.dot(a, b, preferred_element_type=F32)


def _dot_nt(a, b):
    return lax.dot_general(a, b, (((1,), (1,)), ((), ())), preferred_element_type=F32)


def _split_bf16(x):
    hi = x.astype(BF16)
    lo = (x - hi.astype(F32)).astype(BF16)
    return hi, lo


def _rms(x, g):
    return x * lax.rsqrt(jnp.mean(x * x, axis=-1, keepdims=True) + EPS) * g


def _rope(x, cos, sin_signed, first_half, half):
    n = x.shape[-1]
    partner = jnp.where(first_half, pltpu.roll(x, n - half, 1), pltpu.roll(x, half, 1))
    return x * cos + partner * sin_signed


def _adaln_kernel(c_ref, w_ref, b_ref, o_ref):
    c = c_ref[...]
    s = c * jax.nn.sigmoid(c)
    s_hi, s_lo = _split_bf16(s)
    w_hi, w_lo = _split_bf16(w_ref[...])
    o_ref[...] = _dot(s_hi, w_hi) + _dot(s_hi, w_lo) + _dot(s_lo, w_hi) + b_ref[...]


def _adaln(cc, w, b):
    n = w.shape[1]
    tn = 1024
    return pl.pallas_call(
        _adaln_kernel,
        grid=(n // tn,),
        in_specs=[pl.BlockSpec((8, D_MODEL), lambda j: (0, 0)),
                  pl.BlockSpec((D_MODEL, tn), lambda j: (0, j)),
                  pl.BlockSpec((1, tn), lambda j: (0, j))],
        out_specs=pl.BlockSpec((8, tn), lambda j: (0, j)),
        out_shape=jax.ShapeDtypeStruct((8, n), F32),
        compiler_params=pltpu.CompilerParams(dimension_semantics=("arbitrary",),
                                             vmem_limit_bytes=VMEM_LIMIT),
        name="adaln",
    )(cc, w, b)


def _inproj_kernel(with_q, x_ref, shift_ref, scale_ref, gpre_ref, ca_ref, sa_ref, cq_ref, sq_ref,
                   wlat_ref, wwin_ref, gq_ref, wuq_ref, gkv_ref, wk_ref, wv_ref, *out_refs):
    if with_q:
        qa_ref, ka_ref, vat_ref, q_ref, k_ref, vt_ref = out_refs
    else:
        ka_ref, vat_ref, k_ref, vt_ref = out_refs
    x = x_ref[0]
    tm = x.shape[0]
    h = _rms(x, gpre_ref[...]) * (1.0 + scale_ref[0]) + shift_ref[0]
    hb = h.astype(BF16)
    lat = _dot(hb, wlat_ref[...])
    p = _dot(hb, wwin_ref[...])
    c_ka = A_HEADS * HEAD_DIM if with_q else 0
    c_va = c_ka + LANES
    c_ckv, c_kr = MLA_Q_LORA, MLA_Q_LORA + MLA_KV_LORA
    lane = lax.broadcasted_iota(jnp.int32, (tm, LANES), 1)

    first_a = (lane & 32) == 0
    ca, sa = ca_ref[...], sa_ref[...]
    if with_q:
        for j in range(A_HEADS * HEAD_DIM // LANES):
            sl = slice(j * LANES, (j + 1) * LANES)
            qa_ref[0, :, sl] = (_rope(p[:, sl], ca, sa, first_a, 32) * (A_SCALE * LOG2_E)).astype(BF16)
    ka_ref[0] = _rope(p[:, c_ka:c_va], ca, sa, first_a, 32).astype(BF16)
    vat = p[:, c_va:c_va + LANES].T
    ones_a = (lax.broadcasted_iota(jnp.int32, (VT_ROWS - HEAD_DIM, BLOCK), 0) == 0).astype(BF16)
    for j in range(tm // BLOCK):
        for hk in range(A_KV_HEADS):
            vat_ref[0, j, hk, 0:HEAD_DIM, :] = vat[hk * HEAD_DIM:(hk + 1) * HEAD_DIM,
                                                   j * BLOCK:(j + 1) * BLOCK].astype(BF16)
            vat_ref[0, j, hk, HEAD_DIM:VT_ROWS, :] = ones_a

    cq, sq = cq_ref[...], sq_ref[...]
    rope_mla = lambda t: t * cq + pltpu.roll(t, LANES // 2, 1) * sq
    if with_q:
        cqn = _rms(lat[:, 0:c_ckv], gq_ref[...]).astype(BF16)
        qf = _dot(cqn, wuq_ref[...])
        for hh in range(MLA_HEADS):
            sl = slice(hh * LANES, (hh + 1) * LANES)
            q_ref[0, :, sl] = (rope_mla(qf[:, sl]) * (MLA_SCALE * LOG2_E)).astype(BF16)

    ckvn = _rms(lat[:, c_ckv:c_kr], gkv_ref[...]).astype(BF16)
    kr = rope_mla(lat[:, c_kr:c_kr + LANES])
    kn = _dot(ckvn, wk_ref[...])
    for hh in range(MLA_HEADS):
        sl = slice(hh * LANES, (hh + 1) * LANES)
        k_ref[0, :, sl] = (kn[:, sl] + kr).astype(BF16)
    vt = _dot_nt(wv_ref[...], ckvn).astype(BF16)
    ones_row = (lax.broadcasted_iota(jnp.int32, (VT_ROWS - MLA_V, tm), 0) == 0).astype(BF16)
    for hh in range(MLA_HEADS):
        vt_ref[0, 0, hh, 0:MLA_V, :] = vt[hh * MLA_V:(hh + 1) * MLA_V, :]
        vt_ref[0, 0, hh, MLA_V:VT_ROWS, :] = ones_row


def _inproj(x, mod, mod_row, gpre, tabs, wts, tm, with_q):
    bx, n, d = x.shape
    wlat, wwin, gq, wuq, gkv, wk, wv = wts
    if not with_q:
        wwin = wwin[:, A_HEADS * HEAD_DIM:]
    wts = (wlat, wwin, gq, wuq, gkv, wk, wv)
    tok = lambda i, b: (b, i, 0)
    vec = lambda col: (lambda i, b: (b if mod_row is None else mod_row, 0, col))
    tab = lambda i, b: (i, 0)
    cst = lambda i, b: (0, 0)
    in_specs = [pl.BlockSpec((1, tm, d), tok), pl.BlockSpec((1, 1, d), vec(0)), pl.BlockSpec((1, 1, d), vec(1)),
                pl.BlockSpec((1, d), cst)]
    in_specs += [pl.BlockSpec((tm, LANES), tab)] * len(tabs)
    in_specs += [pl.BlockSpec(w.shape, cst) for w in wts]
    row_out = lambda w: (pl.BlockSpec((1, tm, w), tok), jax.ShapeDtypeStruct((bx, n, w), BF16))
    win_outs = [row_out(LANES),
                (pl.BlockSpec((1, tm // BLOCK, A_KV_HEADS, VT_ROWS, BLOCK), lambda i, b: (b, i, 0, 0, 0)),
                 jax.ShapeDtypeStruct((bx, n // BLOCK, A_KV_HEADS, VT_ROWS, BLOCK), BF16))]
    mla_outs = [row_out(MLA_HEADS * LANES),
                (pl.BlockSpec((1, 1, MLA_HEADS, VT_ROWS, tm), lambda i, b: (b, i, 0, 0, 0)),
                 jax.ShapeDtypeStruct((bx, n // tm, MLA_HEADS, VT_ROWS, tm), BF16))]
    if with_q:
        outs = [row_out(A_HEADS * HEAD_DIM)] + win_outs + [row_out(MLA_HEADS * LANES)] + mla_outs
    else:
        outs = win_outs + mla_outs
    return pl.pallas_call(
        functools.partial(_inproj_kernel, with_q),
        grid=(n // tm, bx),
        in_specs=in_specs,
        out_specs=[o[0] for o in outs],
        out_shape=[o[1] for o in outs],
        compiler_params=pltpu.CompilerParams(dimension_semantics=("arbitrary", "arbitrary"),
                                             vmem_limit_bytes=VMEM_LIMIT),
        name="inproj" if with_q else "inproj_ctx",
    )(x, mod, mod, gpre, *tabs, *wts)


def _gqa_kernel(sink_ref, q_ref, *refs):
    nwin = GQA_QB + 2
    k_refs, kx_ref = refs[:nwin], refs[nwin]
    v_refs, vx_ref = refs[nwin + 1:2 * nwin + 1], refs[2 * nwin + 1]
    o_ref = refs[2 * nwin + 2]
    n = pl.program_id(1)
    nsteps = pl.num_programs(1)
    nctx = kx_ref.shape[1] // BLOCK
    group = A_HEADS // A_KV_HEADS
    key = lax.broadcasted_iota(jnp.int32, (BLOCK, group * BLOCK), 0)
    qry = lax.broadcasted_iota(jnp.int32, (BLOCK, group * BLOCK), 1) & (BLOCK - 1)
    sts = {}

    def scores(qb):
        kcat = jnp.concatenate([k_refs[qb + j][0] for j in range(3)] + [kx_ref[0]], axis=0)
        for hk in range(A_KV_HEADS):
            qg = jnp.concatenate([q_ref[0, qb * BLOCK:(qb + 1) * BLOCK, hh * HEAD_DIM:(hh + 1) * HEAD_DIM]
                                  for hh in range(hk * group, (hk + 1) * group)], axis=0)
            k_hk = kcat[:, hk * HEAD_DIM:(hk + 1) * HEAD_DIM]
            sts[qb, hk] = _dot_nt(k_hk, qg)

    def finish(qb):
        vis_prev = key >= qry
        vis_next = key <= qry
        if qb == 0:
            vis_prev = key >= qry + jnp.where(n > 0, 0, 2 * BLOCK)
        if qb == GQA_QB - 1:
            vis_next = key <= qry - jnp.where(n < nsteps - 1, 0, 2 * BLOCK)
        outs = []
        for hk in range(A_KV_HEADS):
            st = sts[qb, hk]
            pieces = [jnp.where(vis_prev, st[0:BLOCK], NEG_INF), st[BLOCK:2 * BLOCK],
                      jnp.where(vis_next, st[2 * BLOCK:3 * BLOCK], NEG_INF)]
            pieces += [st[(3 + j) * BLOCK:(4 + j) * BLOCK] for j in range(nctx)]
            sink = sink_ref[hk]
            m = jnp.maximum(functools.reduce(jnp.maximum, [jnp.max(p, axis=0, keepdims=True) for p in pieces]),
                            sink)
            ps = [jnp.exp2(p - m).astype(BF16) for p in pieces]
            vts = [v_refs[qb + j][0, 0, hk] for j in range(3)] + [vx_ref[0, j, hk] for j in range(nctx)]
            pv = functools.reduce(jnp.add, [_dot(vt, p) for vt, p in zip(vts, ps)])
            l = pv[HEAD_DIM:HEAD_DIM + 1, :] + jnp.exp2(sink - m)
            o = pv[0:HEAD_DIM, :] * (1.0 / l)
            outs += [o[:, g * BLOCK:(g + 1) * BLOCK] for g in range(group)]
        for pair in range(A_HEADS // 2):
            both = jnp.concatenate([outs[2 * pair], outs[2 * pair + 1]], axis=0)
            o_ref[0, qb * BLOCK:(qb + 1) * BLOCK, pair * LANES:(pair + 1) * LANES] = both.T.astype(BF16)

    for qb in range(GQA_QB):
        scores(qb)
    for qb in range(GQA_QB):
        finish(qb)


def _gqa(sinkv, qa, ka, vat, kac, vatc):
    b, s, _ = qa.shape
    nb = s // BLOCK
    c = kac.shape[1]
    tq = GQA_QB * BLOCK
    cur = lambda bb, n: (bb, n, 0)
    win = lambda j: (lambda bb, n: (bb, jnp.clip(n * GQA_QB + j - 1, 0, nb - 1), 0))
    vwin = lambda j: (lambda bb, n: (bb, jnp.clip(n * GQA_QB + j - 1, 0, nb - 1), 0, 0, 0))
    vblk = (1, 1, A_KV_HEADS, VT_ROWS, BLOCK)
    in_specs = [pl.BlockSpec(sinkv.shape, lambda bb, n: (0, 0, 0)), pl.BlockSpec((1, tq, A_HEADS * HEAD_DIM), cur)]
    in_specs += [pl.BlockSpec((1, BLOCK, LANES), win(j)) for j in range(GQA_QB + 2)]
    in_specs += [pl.BlockSpec((1, c, LANES), lambda bb, n: (bb, 0, 0))]
    in_specs += [pl.BlockSpec(vblk, vwin(j)) for j in range(GQA_QB + 2)]
    in_specs += [pl.BlockSpec((1, c // BLOCK, A_KV_HEADS, VT_ROWS, BLOCK), lambda bb, n: (bb, 0, 0, 0, 0))]
    return pl.pallas_call(
        _gqa_kernel,
        grid=(b, nb // GQA_QB),
        in_specs=in_specs,
        out_specs=pl.BlockSpec((1, tq, 512), cur),
        out_shape=jax.ShapeDtypeStruct((b, s, 512), BF16),
        compiler_params=pltpu.CompilerParams(dimension_semantics=("arbitrary", "arbitrary"),
                                             vmem_limit_bytes=VMEM_LIMIT),
        name="gqa",
    )(sinkv, qa, *([ka] * (GQA_QB + 2)), kac, *([vat] * (GQA_QB + 2)), vatc)


def _mla_kernel(q_ref, k_ref, vt_ref, kc_ref, vct_ref, o_ref, m_ref, acc_ref, sa_ref, sb_ref, cma_ref, cmb_ref):
    nk = k_ref.shape[1] // TK_MLA

    sa_ref, sb_ref = (sa_ref, cma_ref), (sb_ref, cmb_ref)

    def scores_into(bufs, k_blk, heads=range(MLA_HEADS)):
        buf, cm = bufs
        n = k_blk.shape[0]
        for hh in heads:
            hsl = slice(hh * LANES, (hh + 1) * LANES)
            st = _dot_nt(k_blk[:, hsl], q_ref[0, :, hsl])
            buf[hh, 0:n, :] = st
            cm[hh, 0:1, :] = jnp.max(st, axis=0, keepdims=True)

    def consume(bufs, vt_blk, first, heads=range(MLA_HEADS)):
        buf, cm = bufs
        n = vt_blk.shape[2]
        for hh in heads:
            st = buf[hh, 0:n, :]
            cmax = cm[hh, 0:1, :]
            if first:
                m_new = cmax
            else:
                m_old = m_ref[hh, 0:1, :]
                m_new = jnp.maximum(m_old, cmax)
                alpha = jnp.exp2(m_old - m_new)
            p = jnp.exp2(st - m_new).astype(BF16)
            pv = _dot(vt_blk[hh], p)
            acc_ref[hh] = pv if first else alpha * acc_ref[hh] + pv
            m_ref[hh, 0:1, :] = m_new

    def k_chunk(c):
        return k_ref[0, pl.ds(pl.multiple_of(c * TK_MLA, TK_MLA), TK_MLA), :]

    def overlapped(buf_w, k_blk, buf_r, vt_blk, first):
        for g in range(0, MLA_HEADS, MLA_GROUP):
            hs = range(g, g + MLA_GROUP)
            scores_into(buf_w, k_blk, hs)
            consume(buf_r, vt_blk, first, hs)

    scores_into(sa_ref, k_chunk(0))
    overlapped(sb_ref, k_chunk(1), sa_ref, vt_ref[0, 0], True)

    def body(i, _):
        c = 2 * i + 1
        overlapped(sa_ref, k_chunk(c + 1), sb_ref, vt_ref[0, c], False)
        overlapped(sb_ref, k_chunk(c + 2), sa_ref, vt_ref[0, c + 1], False)
        return 0

    lax.fori_loop(0, (nk - 2) // 2, body, 0)
    overlapped(sa_ref, kc_ref[0], sb_ref, vt_ref[0, nk - 1], False)
    consume(sa_ref, vct_ref[0, 0], False)

    outs = [acc_ref[hh, 0:MLA_V, :] * (1.0 / acc_ref[hh, MLA_V:MLA_V + 1, :]) for hh in range(MLA_HEADS)]
    o_ref[0] = jnp.concatenate(outs, axis=0).T.astype(BF16)


def _mla(q, k, vt, kc, vct):
    b, s, _ = q.shape
    c = kc.shape[1]
    nk = s // TK_MLA
    assert vt.shape == (b, nk, MLA_HEADS, VT_ROWS, TK_MLA) and nk % 2 == 0
    return pl.pallas_call(
        _mla_kernel,
        grid=(b, s // TQ_MLA),
        in_specs=[pl.BlockSpec((1, TQ_MLA, 1024), lambda bb, i: (bb, i, 0)),
                  pl.BlockSpec((1, s, 1024), lambda bb, i: (bb, 0, 0)),
                  pl.BlockSpec((1, nk, MLA_HEADS, VT_ROWS, TK_MLA), lambda bb, i: (bb, 0, 0, 0, 0)),
                  pl.BlockSpec((1, c, 1024), lambda bb, i: (bb, 0, 0)),
                  pl.BlockSpec((1, 1, MLA_HEADS, VT_ROWS, c), lambda bb, i: (bb, 0, 0, 0, 0))],
        out_specs=pl.BlockSpec((1, TQ_MLA, 512), lambda bb, i: (bb, i, 0)),
        out_shape=jax.ShapeDtypeStruct((b, s, 512), BF16),
        scratch_shapes=[pltpu.VMEM((MLA_HEADS, 8, TQ_MLA), F32),
                        pltpu.VMEM((MLA_HEADS, VT_ROWS, TQ_MLA), F32),
                        pltpu.VMEM((MLA_HEADS, TK_MLA, TQ_MLA), F32), pltpu.VMEM((MLA_HEADS, TK_MLA, TQ_MLA), F32),
                        pltpu.VMEM((MLA_HEADS, 8, TQ_MLA), F32), pltpu.VMEM((MLA_HEADS, 8, TQ_MLA), F32)],
        compiler_params=pltpu.CompilerParams(dimension_semantics=("arbitrary", "arbitrary"),
                                             vmem_limit_bytes=VMEM_LIMIT),
        name="mla",
    )(q, k, vt, kc, vct)


def _outproj_kernel(oa_ref, ob_ref, x_ref, woa_ref, wob_ref, gpost_ref, gate_ref, gffn_ref, shift_ref, scale_ref,
                    wr_pair_ref, wr_hi_ref, br_ref, tri_ref, upper_ref,
                    x1_ref, h2_ref, ri_ref, rw_ref, cnt_ref):
    tm = TM_OUT
    nsub = 2
    sub = tm // nsub
    rows = [slice(j * sub, (j + 1) * sub) for j in range(OUT_TILES * nsub)]
    ys = [_dot(oa_ref[0, r, :], woa_ref[...]) + _dot(ob_ref[0, r, :], wob_ref[...]) for r in rows]
    sub_lane_f = lax.broadcasted_iota(jnp.int32, (sub, LANES), 1).astype(F32)
    idx_parts, w_parts = [], []
    for r, y in zip(rows, ys):
        x1 = x_ref[0, r, :] + gate_ref[0] * _rms(y, gpost_ref[...])
        x1_ref[0, r, :] = x1
        h2 = _rms(x1, gffn_ref[...]) * (1.0 + scale_ref[0]) + shift_ref[0]
        h2_ref[0, r, :] = h2.astype(BF16)
        h_hi, h_lo = _split_bf16(h2)
        both = _dot(h_hi, wr_pair_ref[...])
        cur = both[:, :LANES] + both[:, LANES:] + _dot(h_lo, wr_hi_ref[...]) + br_ref[...]
        tops, idxs = [], []
        for _ in range(TOP_K):
            mk = jnp.max(cur, axis=-1, keepdims=True)
            ik = jnp.min(jnp.where(cur == mk, sub_lane_f, float(LANES)), axis=-1, keepdims=True)
            tops.append(mk)
            idxs.append(ik)
            cur = jnp.where(sub_lane_f == ik, -jnp.inf, cur)
        es = [jnp.exp(t - tops[0]) for t in tops]
        inv = 1.0 / functools.reduce(jnp.add, es)
        idx_parts.append(idxs)
        w_parts.append([e * inv for e in es])

    hot_parts = [[(sub_lane_f == ik) for ik in idxs] for idxs in idx_parts]
    sub_lane = lax.broadcasted_iota(jnp.int32, (sub, LANES), 1)
    for ti in range(OUT_TILES):
        groups = range(ti * nsub, (ti + 1) * nsub)
        onehot = jnp.concatenate([functools.reduce(jnp.add, [o.astype(F32) for o in hot_parts[g]]) for g in groups],
                                 axis=0)
        cnt = jnp.sum(onehot, axis=0, keepdims=True)
        prefix = _dot(tri_ref[...], onehot.astype(BF16))
        chunks = jnp.floor((cnt + (RUN_ALIGN - 1.0)) * (1.0 / RUN_ALIGN))
        run_start = _dot(jnp.broadcast_to(chunks, (8, LANES)).astype(BF16), upper_ref[...])[0:1, :] * RUN_ALIGN
        base = prefix + run_start
        for j, g in enumerate(groups):
            ri = jnp.zeros((sub, LANES), jnp.int32)
            rw = jnp.zeros((sub, LANES), F32)
            for k in range(TOP_K):
                pos = jnp.sum(jnp.where(hot_parts[g][k], base[j * sub:(j + 1) * sub, :], 0.0), axis=-1,
                              keepdims=True).astype(jnp.int32)
                ri = jnp.where(sub_lane == k, idx_parts[g][k].astype(jnp.int32), ri)
                ri = jnp.where(sub_lane == TOP_K + k, pos, ri)
                rw = jnp.where(sub_lane == k, w_parts[g][k], rw)
            ri_ref[0, rows[g], :] = ri
            rw_ref[0, rows[g], :] = rw
        cnt_ref[ti] = cnt


def _outproj(oa, ob, x, woa, wob, gpost, mod, gffn, wrh, wrl, br, tri, upper):
    b, s, d = x.shape
    tm = TM_OUT * OUT_TILES
    nt = s // tm
    tok = lambda bb, i: (bb, i, 0)
    vec = lambda col: (lambda bb, i: (bb, 0, col))
    cst = lambda bb, i: (0, 0)
    return pl.pallas_call(
        _outproj_kernel,
        grid=(b, nt),
        in_specs=[pl.BlockSpec((1, tm, 512), tok), pl.BlockSpec((1, tm, 512), tok), pl.BlockSpec((1, tm, d), tok),
                  pl.BlockSpec((512, d), cst), pl.BlockSpec((512, d), cst), pl.BlockSpec((1, d), cst),
                  pl.BlockSpec((1, 1, d), vec(2)), pl.BlockSpec((1, d), cst),
                  pl.BlockSpec((1, 1, d), vec(3)), pl.BlockSpec((1, 1, d), vec(4)),
                  pl.BlockSpec((d, 2 * LANES), cst), pl.BlockSpec((d, LANES), cst), pl.BlockSpec((1, LANES), cst),
                  pl.BlockSpec((TM_OUT, TM_OUT), cst), pl.BlockSpec((LANES, LANES), cst)],
        out_specs=[pl.BlockSpec((1, tm, d), tok), pl.BlockSpec((1, tm, d), tok),
                   pl.BlockSpec((1, tm, LANES), tok), pl.BlockSpec((1, tm, LANES), tok),
                   pl.BlockSpec((OUT_TILES, 1, LANES), lambda bb, i: (bb * nt + i, 0, 0))],
        out_shape=[jax.ShapeDtypeStruct((b, s, d), F32), jax.ShapeDtypeStruct((b, s, d), BF16),
                   jax.ShapeDtypeStruct((b, s, LANES), jnp.int32), jax.ShapeDtypeStruct((b, s, LANES), F32),
                   jax.ShapeDtypeStruct((b * nt * OUT_TILES, 1, LANES), F32)],
        compiler_params=pltpu.CompilerParams(dimension_semantics=("arbitrary", "arbitrary"),
                                             vmem_limit_bytes=VMEM_LIMIT),
        name="outproj",
    )(oa, ob, x, woa, wob, gpost, mod, gffn, mod, mod, wrh, wrl, br, tri, upper)


def _start_pieces(g, o, rows, make_copy):
    n_big = lax.shift_right_logical(rows, BIG_COPY.bit_length() - 1)
    n_small = lax.shift_right_logical(rows, RUN_ALIGN.bit_length() - 1) & (BIG_COPY // RUN_ALIGN - 1)

    def big(c, _):
        off = c * BIG_COPY
        make_copy(pl.multiple_of(g + off, RUN_ALIGN), pl.multiple_of(o + off, RUN_ALIGN), BIG_COPY).start()
        return 0

    def small(c, _):
        off = n_big * BIG_COPY + c * RUN_ALIGN
        make_copy(pl.multiple_of(g + off, RUN_ALIGN), pl.multiple_of(o + off, RUN_ALIGN), RUN_ALIGN).start()
        return 0

    lax.fori_loop(0, n_big, big, 0)
    lax.fori_loop(0, n_small, small, 0)
    return n_big, n_small


def _unrolled_loop(n, step):
    shift = LOOP_UNROLL.bit_length() - 1
    trips = lax.shift_right_logical(n, shift)

    def many(t, _):
        for u in range(LOOP_UNROLL):
            step(t * LOOP_UNROLL + u, u)
        return 0

    def one(i, _):
        step(i, 0)
        return 0

    lax.fori_loop(0, trips, many, 0)
    lax.fori_loop(trips * LOOP_UNROLL, n, one, 0)


def _wait_pieces(n_big, n_small, make_copy):
    _unrolled_loop(n_big, lambda i, u: make_copy(0, 0, BIG_COPY).wait())
    _unrolled_loop(n_small, lambda i, u: make_copy(0, 0, RUN_ALIGN).wait())


def _start_listed_copies(plan, tile, make_copy):
    big_g, big_o, n_big, small_g, small_o, n_small = plan

    def start_list(g_ref, o_ref, n, width, rows):
        def start(p, u):
            j = tile * width + p
            make_copy(pl.multiple_of(g_ref[j], RUN_ALIGN), pl.multiple_of(o_ref[j], RUN_ALIGN),
                      rows).start(priority=u % 2)

        _unrolled_loop(n, start)

    start_list(big_g, big_o, n_big[tile], MAX_BIG, BIG_COPY)
    start_list(small_g, small_o, n_small[tile], MAX_SMALL, RUN_ALIGN)


def _dispatch_kernel(bg_ref, bo_ref, nb_ref, sg_ref, so_ref, ns_ref, zs_ref, zr_ref, ri_ref, h_ref, xs_ref,
                     sorted_ref, zero_ref, sem):
    plan = (bg_ref, bo_ref, nb_ref, sg_ref, so_ref, ns_ref)
    tile = pl.program_id(0)
    last = pl.num_programs(0) - 1
    slot = tile % 2
    tm = h_ref.shape[0]
    post = ri_ref[...].astype(F32).T
    h = h_ref[...]
    rb_rows = 256
    pos_blk = [jnp.floor(post[TOP_K + k:TOP_K + k + 1, :] * (1.0 / rb_rows)) for k in range(TOP_K)]
    pos_off = [post[TOP_K + k:TOP_K + k + 1, :] - rb_rows * pos_blk[k] for k in range(TOP_K)]
    row = lax.broadcasted_iota(jnp.int32, (rb_rows, tm), 0).astype(F32).astype(BF16)
    one, zero = jnp.ones((), BF16), jnp.zeros((), BF16)

    for rb in range(SORTED_ROWS // rb_rows):
        perm = functools.reduce(jnp.add, [
            jnp.where(row == jnp.where(pos_blk[k] == float(rb), pos_off[k], -1.0).astype(BF16), one, zero)
            for k in range(TOP_K)])
        xr = _dot(perm, h)
        lo = lax.bitcast_convert_type(xr[:, :512], jnp.uint32)
        hi = lax.bitcast_convert_type(xr[:, 512:], jnp.uint32)
        sorted_ref[slot, rb * rb_rows:(rb + 1) * rb_rows, :] = (lo >> 16) | (hi & jnp.uint32(0xFFFF0000))

    def run_copy(sl):
        def make_copy(g, o, rows):
            return pltpu.make_async_copy(sorted_ref.at[sl, pl.ds(o, rows)], xs_ref.at[pl.ds(g, rows)], sem.at[sl])
        return make_copy

    _start_listed_copies(plan, tile, run_copy(slot))

    @pl.when(tile > 0)
    def _():
        _wait_pieces(nb_ref[tile - 1], ns_ref[tile - 1], run_copy(1 - slot))

    def zero_copy(g, o, rows):
        return pltpu.make_async_copy(zero_ref.at[pl.ds(0, rows)], xs_ref.at[pl.ds(g, rows)], sem.at[2])

    @pl.when(tile == 0)
    def _():
        zero_ref[...] = jnp.zeros_like(zero_ref)

        def start_zero(e, carry):
            _start_pieces(zs_ref[e], 0, zr_ref[e], zero_copy)
            return carry

        lax.fori_loop(0, zs_ref.shape[0], start_zero, 0)

    @pl.when(tile == last)
    def _():
        _wait_pieces(nb_ref[tile], ns_ref[tile], run_copy(slot))

        def count(e, totals):
            rows = zr_ref[e]
            n_big = lax.shift_right_logical(rows, BIG_COPY.bit_length() - 1)
            n_small = lax.shift_right_logical(rows, RUN_ALIGN.bit_length() - 1) & (BIG_COPY // RUN_ALIGN - 1)
            return totals[0] + n_big, totals[1] + n_small

        zb, zs = lax.fori_loop(0, zs_ref.shape[0], count, (0, 0))
        _wait_pieces(zb, zs, zero_copy)


def _dispatch(plan, zstart, zrows, ri, h2, cap):
    t, d = h2.shape
    tm = TM_OUT
    return pl.pallas_call(
        _dispatch_kernel,
        grid_spec=pltpu.PrefetchScalarGridSpec(
            num_scalar_prefetch=8,
            grid=(t // tm,),
            in_specs=[pl.BlockSpec((tm, LANES), lambda i, *_: (i, 0)),
                      pl.BlockSpec((tm, d), lambda i, *_: (i, 0))],
            out_specs=pl.BlockSpec(memory_space=pl.ANY),
            scratch_shapes=[pltpu.VMEM((2, SORTED_ROWS, d // 2), jnp.uint32),
                            pltpu.VMEM((BIG_COPY, d // 2), jnp.uint32),
                            pltpu.SemaphoreType.DMA((3,))]),
        out_shape=jax.ShapeDtypeStruct((cap, d // 2), jnp.uint32),
        compiler_params=pltpu.CompilerParams(dimension_semantics=("arbitrary",),
                                             vmem_limit_bytes=VMEM_LIMIT),
        name="dispatch",
    )(*plan, zstart, zrows, ri, h2)


def _pack_bf16_pairs(x):
    n = x.shape[1] // 2
    bits = lax.bitcast_convert_type(x.astype(BF16).astype(F32), jnp.uint32)
    return (bits[:, :n] >> 16) | (bits[:, n:] & jnp.uint32(0xFFFF0000))


def _unpack_bf16_pairs(w):
    lo = lax.bitcast_convert_type(w << 16, F32).astype(BF16)
    hi = lax.bitcast_convert_type(w & jnp.uint32(0xFFFF0000), F32).astype(BF16)
    return lo, hi


def _experts_kernel(be_ref, nu_ref, slot_ref, nxt_ref, nv_ref, xs_ref, wgu_hbm, bgu_ref, wd_hbm, bd_ref, ys_ref,
                    wgu_f, wd_f, wgu_b, wd_b, sem):
    del nu_ref
    i = pl.program_id(0)
    e = be_ref[i]
    slot = slot_ref[i]
    valid = nv_ref[i]
    used = valid > 0
    run_start = used & ((i == 0) | (be_ref[jnp.maximum(i - 1, 0)] != e))

    def weight_copies(ex, sl):
        return (pltpu.make_async_copy(wgu_hbm.at[ex], wgu_f.at[sl], sem.at[0, sl]),
                pltpu.make_async_copy(wd_hbm.at[ex], wd_f.at[sl], sem.at[1, sl]))

    @pl.when(i == 0)
    def _():
        for cp in weight_copies(e, slot):
            cp.start()

    @pl.when(run_start)
    def _():
        for cp in weight_copies(e, slot):
            cp.wait()
        nxt = nxt_ref[i]

        @pl.when(nxt >= 0)
        def _():
            for cp in weight_copies(nxt, 1 - slot):
                cp.start()

    def compute(rows, fresh):
        if fresh:
            wgu_b[...] = wgu_f[slot].astype(BF16)
            wd_b[...] = wd_f[slot].astype(BF16)
        gus = []
        for r in rows:
            x_lo, x_hi = _unpack_bf16_pairs(xs_ref[r, :])
            gus.append(_dot(x_lo, wgu_b[0:512, :]) + _dot(x_hi, wgu_b[512:, :]) + bgu_ref[0])
        for r, gu in zip(rows, gus):
            gate = jnp.minimum(gu[:, :D_FF], SWIGLU_LIMIT)
            lin = jnp.clip(gu[:, D_FF:], -SWIGLU_LIMIT, SWIGLU_LIMIT)
            act = (lin + 1.0) * (gate * jax.nn.sigmoid(SWIGLU_ALPHA * gate))
            ys_ref[r, :] = _pack_bf16_pairs(_dot(act.astype(BF16), wd_b[...]) + bd_ref[0])

    nsteps = MOE_BLOCK // EXPERT_STEP
    for n in range(nsteps + 1):
        lo, hi = (n - 1) * EXPERT_STEP, n * EXPERT_STEP
        in_range = (valid == 0) if n == 0 else ((valid > lo) & (valid <= hi))
        cuts = list(range(0, hi, EXPERT_SUB)) + [hi]
        for fresh in ((False,) if n == 0 else (False, True)):
            cond = in_range & (run_start if fresh else jnp.logical_not(run_start))

            @pl.when(cond)
            def _(hi=hi, cuts=cuts, fresh=fresh):
                if hi:
                    compute([slice(a, b) for a, b in zip(cuts[:-1], cuts[1:])], fresh)
                if hi < MOE_BLOCK:
                    ys_ref[hi:, :] = jnp.zeros((MOE_BLOCK - hi, D_MODEL // 2), jnp.uint32)


def _experts(block_e, n_used, slot, nxt, nvalid, xs, wgu, bgu, wd, bd):
    cap = xs.shape[0]
    nblk = cap // MOE_BLOCK
    return pl.pallas_call(
        _experts_kernel,
        grid_spec=pltpu.PrefetchScalarGridSpec(
            num_scalar_prefetch=5,
            grid=(nblk,),
            in_specs=[pl.BlockSpec((MOE_BLOCK, 512), lambda i, be, nu, *_: (jnp.minimum(i, nu[0] - 1), 0)),
                      pl.BlockSpec(memory_space=pl.ANY),
                      pl.BlockSpec((1, 1, 2 * D_FF), lambda i, be, *_: (be[i], 0, 0)),
                      pl.BlockSpec(memory_space=pl.ANY),
                      pl.BlockSpec((1, 1, D_MODEL), lambda i, be, *_: (be[i], 0, 0))],
            out_specs=pl.BlockSpec((MOE_BLOCK, D_MODEL // 2), lambda i, *_: (i, 0)),
            scratch_shapes=[pltpu.VMEM((2, D_MODEL, 2 * D_FF), F32), pltpu.VMEM((2, D_FF, D_MODEL), F32),
                            pltpu.VMEM((D_MODEL, 2 * D_FF), BF16), pltpu.VMEM((D_FF, D_MODEL), BF16),
                            pltpu.SemaphoreType.DMA((2, 2))]),
        out_shape=jax.ShapeDtypeStruct((cap, D_MODEL // 2), jnp.uint32),
        compiler_params=pltpu.CompilerParams(dimension_semantics=("arbitrary",),
                                             vmem_limit_bytes=VMEM_LIMIT),
        name="experts",
    )(block_e, n_used, slot, nxt, nvalid, xs, wgu, bgu, wd, bd)


def _combine_kernel(bg_ref, bo_ref, nb_ref, sg_ref, so_ref, ns_ref, ys_ref, ri_ref, rw_ref, x1_ref, gate_ref, g_ref,
                    o_ref, ybuf, sem):
    plan = (bg_ref, bo_ref, nb_ref, sg_ref, so_ref, ns_ref)
    nt = pl.num_programs(1)
    tile = pl.program_id(0) * nt + pl.program_id(1)
    ntiles = pl.num_programs(0) * nt
    slot = tile % 2
    tm = x1_ref.shape[1]

    def run_copy(sl):
        def make_copy(g, o, rows):
            return pltpu.make_async_copy(ys_ref.at[pl.ds(g, rows)], ybuf.at[sl, pl.ds(o, rows)], sem.at[sl])
        return make_copy

    @pl.when(tile == 0)
    def _():
        ybuf[...] = jnp.zeros_like(ybuf)
        _start_listed_copies(plan, tile, run_copy(slot))

    @pl.when(tile + 1 < ntiles)
    def _():
        _start_listed_copies(plan, tile + 1, run_copy(1 - slot))

    _wait_pieces(nb_ref[tile], ns_ref[tile], run_copy(slot))

    posf = ri_ref[0].astype(F32)
    rw = rw_ref[0]
    cb_cols = 256
    pos = [posf[:, TOP_K + k:TOP_K + k + 1] for k in range(TOP_K)]
    pos_blk = [jnp.floor(p * (1.0 / cb_cols)) for p in pos]
    pos_off = [p - cb_cols * b for p, b in zip(pos, pos_blk)]
    wk = [rw[:, k:k + 1].astype(BF16) for k in range(TOP_K)]
    col = lax.broadcasted_iota(jnp.int32, (tm, cb_cols), 1).astype(F32).astype(BF16)
    zero = jnp.zeros((), BF16)
    f_lo = jnp.zeros((tm, D_MODEL // 2), F32)
    f_hi = jnp.zeros((tm, D_MODEL // 2), F32)
    for cb in range(SORTED_ROWS // cb_cols):
        wp = functools.reduce(jnp.add, [
            jnp.where(col == jnp.where(pos_blk[k] == float(cb), pos_off[k], -1.0).astype(BF16), wk[k], zero)
            for k in range(TOP_K)])
        y_lo, y_hi = _unpack_bf16_pairs(ybuf[slot, cb * cb_cols:(cb + 1) * cb_cols, :])
        f_lo = f_lo + _dot(wp, y_lo)
        f_hi = f_hi + _dot(wp, y_hi)
    f = jnp.concatenate([f_lo, f_hi], axis=1)
    o_ref[0] = x1_ref[0] + gate_ref[0] * _rms(f, g_ref[...])


def _combine(plan, ys, ri, rw, x1, mod, g):
    b, s, d = x1.shape
    tm = TM_OUT
    tok = lambda bb, i, *_: (bb, i, 0)
    return pl.pallas_call(
        _combine_kernel,
        grid_spec=pltpu.PrefetchScalarGridSpec(
            num_scalar_prefetch=6,
            grid=(b, s // tm),
            in_specs=[pl.BlockSpec(memory_space=pl.ANY),
                      pl.BlockSpec((1, tm, LANES), tok), pl.BlockSpec((1, tm, LANES), tok),
                      pl.BlockSpec((1, tm, d), tok),
                      pl.BlockSpec((1, 1, d), lambda bb, i, *_: (bb, 0, 5)),
                      pl.BlockSpec((1, d), lambda bb, i, *_: (0, 0))],
            out_specs=pl.BlockSpec((1, tm, d), tok),
            scratch_shapes=[pltpu.VMEM((2, SORTED_ROWS, d // 2), jnp.uint32), pltpu.SemaphoreType.DMA((2,))]),
        out_shape=jax.ShapeDtypeStruct((b, s, d), F32),
        compiler_params=pltpu.CompilerParams(dimension_semantics=("arbitrary", "arbitrary"),
                                             vmem_limit_bytes=VMEM_LIMIT),
        name="combine",
    )(*plan, ys, ri, rw, x1, mod, g)


def _rope_angles(rows, rot_dim):
    row = np.repeat(np.arange(rows, dtype=np.float64), GRID_W)
    col = np.tile(np.arange(GRID_W, dtype=np.float64), rows)
    quarter = rot_dim // 4
    inv_freq = ROPE_BASE ** (-np.arange(quarter, dtype=np.float64) / quarter)
    ang = np.concatenate([row[:, None] * inv_freq, col[:, None] * inv_freq], axis=-1)
    return np.cos(ang), np.sin(ang)


def _mla_lane_of():
    j = np.arange(MLA_NOPE + MLA_ROPE)
    r = j - MLA_NOPE
    half = MLA_ROPE // 2
    return np.where(j < 48, 16 + j, np.where(j < MLA_NOPE, 32 + j, np.where(r < half, r, 48 + r)))


def _to_mla_lanes(w, dims):
    lane_of = _mla_lane_of()[dims]
    src = np.full((LANES,), -1)
    src[lane_of] = np.arange(len(dims))
    picked = jnp.take(w, jnp.asarray(np.maximum(src, 0)), axis=-1)
    return jnp.where(jnp.asarray(src >= 0), picked, 0.0)


def _rope_tables(s):
    ca, sa = _rope_angles(s // GRID_W, HEAD_DIM)
    cb, sb = _rope_angles(s // GRID_W, MLA_ROPE)
    cos_a = np.tile(np.concatenate([ca, ca], 1), (1, 2))
    sin_a = np.tile(np.concatenate([-sa, sa], 1), (1, 2))
    cos_q = np.concatenate([cb, np.ones((s, 48)), cb, np.ones((s, 48))], 1)
    sin_q = np.concatenate([-sb, np.zeros((s, 48)), sb, np.zeros((s, 48))], 1)
    return tuple(jnp.asarray(t, F32) for t in (cos_a, sin_a, cos_q, sin_q))


def _piece_lists(slot_start, tile_start, counts, first_off, piece, width):
    ne = counts.shape[1]
    cum = jnp.cumsum(counts, axis=1)
    p = jnp.arange(width, dtype=jnp.int32)
    owner = jnp.minimum(jnp.sum((cum[:, None, :] <= p[None, :, None]).astype(jnp.int32), axis=2), ne - 1)
    sel = owner[:, :, None] == jnp.arange(ne, dtype=jnp.int32)[None, None, :]
    pick = lambda a: jnp.sum(jnp.where(sel, a[:, None, :], 0), axis=2)
    off = pick(first_off) + (p[None, :] - pick(cum - counts)) * piece
    as_list = lambda a: a.reshape(-1).astype(jnp.int32)
    return as_list(pick(slot_start) + off), as_list(pick(tile_start) + off), cum[:, -1].astype(jnp.int32)


def _identity_tables(n):
    one, zero = jnp.ones((n, LANES), F32), jnp.zeros((n, LANES), F32)
    return one, zero, one, zero


def _inproj_weights(w_in, g_q_a, w_uq, g_kv_a, w_ukv):
    d = w_in.shape[0]
    wq, wk, wv = w_in[:, :512], w_in[:, 512:640], w_in[:, 640:768]
    wcq, wckv, wkr = w_in[:, 768:1152], w_in[:, 1152:1408], w_in[:, 1408:1440]
    nope, rope = np.arange(MLA_NOPE), MLA_NOPE + np.arange(MLA_ROPE)
    wkr_p = _to_mla_lanes(wkr, rope)
    wlat = jnp.concatenate([wcq, wckv, wkr_p], 1).astype(BF16)
    wwin = jnp.concatenate([wq, wk, wv], 1).astype(BF16)
    wuq = _to_mla_lanes(w_uq.reshape(MLA_Q_LORA, MLA_HEADS, MLA_NOPE + MLA_ROPE), np.arange(MLA_NOPE + MLA_ROPE))
    wuq = wuq.reshape(MLA_Q_LORA, MLA_HEADS * LANES).astype(BF16)
    wukv = w_ukv.reshape(MLA_KV_LORA, MLA_HEADS, MLA_NOPE + MLA_V)
    wkk = _to_mla_lanes(wukv[:, :, :MLA_NOPE], nope).reshape(MLA_KV_LORA, MLA_HEADS * LANES)
    wvv = wukv[:, :, MLA_NOPE:].reshape(MLA_KV_LORA, MLA_HEADS * MLA_V)
    return (wlat, wwin, g_q_a.reshape(1, -1), wuq, g_kv_a.reshape(1, -1), wkk.astype(BF16), wvv.T.astype(BF16))


def kernel(x, c, ctx, c_ctx, w_ada, b_ada, g_mix_pre, g_mix_post, w_in, sink, g_q_a, w_uq, g_kv_a, w_ukv, w_o,
           g_ffn_pre, g_ffn_post, w_router, b_router, w_gate_up, b_gate_up, w_down, b_down):
    b, s, d = x.shape
    cl = ctx.shape[1]
    t = b * s

    cc = jnp.concatenate([c, c_ctx[None, :], jnp.zeros((8 - b - 1, d), F32)], axis=0)
    mod = _adaln(cc, w_ada[0], b_ada[0].reshape(1, -1)).reshape(8, 1, 6 * d)

    wts = _inproj_weights(w_in[0], g_q_a[0], w_uq[0], g_kv_a[0], w_ukv[0])
    gpre = g_mix_pre[0].reshape(1, d)
    qa, ka, vat, q, k, vt = _inproj(x, mod, None, gpre, _rope_tables(s), wts, TM_IN, True)
    kac, vatc, kc, vct = _inproj(ctx, mod, b, gpre, _identity_tables(cl), wts, cl, False)

    sinkv = jnp.repeat(sink[0] * LOG2_E, BLOCK).reshape(A_KV_HEADS, 1, -1)
    out_a = _gqa(sinkv, qa, ka, vat, kac, vatc)
    out_b = _mla(q, k, vt, kc, vct)

    wo = w_o[0].astype(BF16)
    wr = jnp.pad(w_router[0], ((0, 0), (0, LANES - N_EXPERTS)))
    wr_hi, wr_lo = _split_bf16(wr)
    br = jnp.concatenate([b_router[0], jnp.full((LANES - N_EXPERTS,), NEG_INF, F32)]).reshape(1, LANES)
    tri = jnp.asarray(np.tri(TM_OUT, k=-1), BF16)
    upper = jnp.asarray(np.tri(LANES, k=-1).T, BF16)
    x1, h2, ri, rw, cnt = _outproj(out_a, out_b, x, wo[:512], wo[512:], g_mix_post[0].reshape(1, d), mod,
                                   g_ffn_pre[0].reshape(1, d),
                                   jnp.concatenate([wr_hi, wr_lo], axis=1), wr_hi, br, tri, upper)

    cnt = cnt[:, 0, :N_EXPERTS].astype(jnp.int32)
    nt = cnt.shape[0]
    rows = (cnt + RUN_ALIGN - 1) // RUN_ALIGN * RUN_ALIGN
    tot = jnp.sum(rows, axis=0)
    carry = jnp.cumsum(rows, axis=0) - rows
    padded = (tot + MOE_BLOCK - 1) // MOE_BLOCK * MOE_BLOCK
    pad_end = jnp.cumsum(padded)
    pad_start = pad_end - padded
    nblk = -(-(t * TOP_K + nt * N_EXPERTS * (RUN_ALIGN - 1)) // MOE_BLOCK) + N_EXPERTS
    cap = nblk * MOE_BLOCK
    blk_row = jnp.arange(nblk, dtype=jnp.int32) * MOE_BLOCK
    block_e = jnp.minimum(jnp.sum((pad_end[None, :] <= blk_row[:, None]).astype(jnp.int32), axis=1), N_EXPERTS - 1)
    n_used = (pad_end[-1] // MOE_BLOCK).astype(jnp.int32).reshape(1)
    toff = jnp.cumsum(rows, axis=1) - rows
    gstart = pad_start[None, :] + carry
    n_big = rows // BIG_COPY
    n_small = (rows - n_big * BIG_COPY) // RUN_ALIGN
    big_g, big_o, tot_big = _piece_lists(gstart, toff, n_big, jnp.zeros_like(rows), BIG_COPY, MAX_BIG)
    small_g, small_o, tot_small = _piece_lists(gstart, toff, n_small, n_big * BIG_COPY, RUN_ALIGN, MAX_SMALL)
    plan = (big_g, big_o, tot_big, small_g, small_o, tot_small)
    zstart = jnp.concatenate([pad_start + tot, pad_end[-1:]]).astype(jnp.int32)
    zrows = jnp.concatenate([padded - tot, cap - pad_end[-1:]]).astype(jnp.int32)
    block_e = block_e.astype(jnp.int32)
    run_idx = jnp.cumsum(jnp.concatenate([jnp.zeros((1,), jnp.int32),
                                          (block_e[1:] != block_e[:-1]).astype(jnp.int32)]))
    eid = jnp.arange(N_EXPERTS, dtype=jnp.int32)
    later_used = (tot[None, :] > 0) & (eid[None, :] > eid[:, None])
    next_e = jnp.min(jnp.where(later_used, eid[None, :], N_EXPERTS), axis=1)
    next_e = jnp.where(next_e < N_EXPERTS, next_e, -1).astype(jnp.int32)
    nxt = jnp.sum(jnp.where(block_e[:, None] == eid[None, :], next_e[None, :], 0), axis=1).astype(jnp.int32)

    xs = _dispatch(plan, zstart, zrows, ri.reshape(t, LANES), h2.reshape(t, d), cap)
    run_end = jnp.sum(jnp.where(block_e[:, None] == eid[None, :], (pad_start + tot)[None, :], 0), axis=1)
    nvalid = jnp.where(blk_row < pad_end[-1], jnp.clip(run_end - blk_row, 0, MOE_BLOCK), 0).astype(jnp.int32)
    ys = _experts(block_e, n_used, (run_idx % 2).astype(jnp.int32), nxt, nvalid, xs, w_gate_up[0],
                  b_gate_up[0].reshape(N_EXPERTS, 1, -1), w_down[0], b_down[0].reshape(N_EXPERTS, 1, -1))
    return _combine(plan, ys, ri, rw, x1, mod, g_ffn_post[0].reshape(1, d))
```
